```python
import jax
import jax.numpy as jnp
from jax import lax
import numpy as np

D_MODEL = 2048
BATCH = 8
SEQ = 2048
DEPTH = 2

CHUNK = 64
D_PL = 256
D_FF = 5632
EPS = 1e-6
N_EVEN = (DEPTH + 1) // 2
N_ODD = DEPTH // 2

GLA_HEADS = 4
GLA_DK = D_MODEL // 4
GLA_DV = D_MODEL // 2
GLA_HEAD_K = GLA_DK // GLA_HEADS
GLA_HEAD_V = GLA_DV // GLA_HEADS
GLA_GATE_RANK = 16
GLA_GATE_TAU = 16.0
CONV_CH = D_MODEL // 2
CONV_WIDTH = 31
AB_SPLITS = [GLA_DK, GLA_DK, GLA_DV, GLA_DV, GLA_GATE_RANK, CONV_CH, CONV_CH]
AB_IN = sum(AB_SPLITS)
AB_OUT = GLA_DV + CONV_CH
ATT_HEADS = 16
ATT_HEAD_DIM = D_MODEL // ATT_HEADS
LEFT_CHUNKS = 8
BAND = (LEFT_CHUNKS + 1) * CHUNK
REL_CLIP = 128

kernel_name = "hybrid_gla_conv_chunkattn_macaron"


def rms_norm(x, g):
    xf = x.astype(jnp.float32)
    y = xf * lax.rsqrt(jnp.mean(xf * xf, axis=-1, keepdims=True) + EPS)
    return (y * g.astype(jnp.float32)).astype(x.dtype)


def layer_norm(x, g, b):
    xf = x.astype(jnp.float32)
    mu = jnp.mean(xf, axis=-1, keepdims=True)
    var = jnp.mean(jnp.square(xf - mu), axis=-1, keepdims=True)
    y = (xf - mu) * lax.rsqrt(var + EPS) * g.astype(jnp.float32) + b.astype(jnp.float32)
    return y.astype(x.dtype)


def swiglu_ffn(h, w_gate, w_up, w_down):
    return (jax.nn.silu(h @ w_gate) * (h @ w_up)) @ w_down


def gla_chunked(q, k, v, log_a):
    B, T, H, dk = q.shape
    dv = v.shape[-1]
    n = T // CHUNK

    def to_chunks(a):
        return a.astype(jnp.float32).reshape(B, n, CHUNK, H, a.shape[-1]).transpose(1, 0, 3, 2, 4)

    qc, kc, vc, gc = to_chunks(q), to_chunks(k), to_chunks(v), to_chunks(log_a)
    causal = jnp.tril(jnp.ones((CHUNK, CHUNK), dtype=bool))

    def step(S, inp):
        qj, kj, vj, gj = inp
        b = jnp.cumsum(gj, axis=2)
        diff = b[:, :, :, None, :] - b[:, :, None, :, :]
        decay = jnp.exp(jnp.where(causal[:, :, None], diff, -jnp.inf))
        A = jnp.einsum('bhtd,bhsd,bhtsd->bhts', qj, kj, decay)
        o = (jnp.einsum('bhtd,bhdv->bhtv', qj * jnp.exp(b), S)
             + jnp.einsum('bhts,bhsv->bhtv', A, vj))
        b_last = b[:, :, -1:, :]
        S = (S * jnp.exp(b_last[:, :, 0, :])[..., None]
             + jnp.einsum('bhsd,bhsv->bhdv', kj * jnp.exp(b_last - b), vj))
        return S, o

    S0 = jnp.zeros((B, H, dk, dv), jnp.float32)
    _, o = lax.scan(step, S0, (qc, kc, vc, gc))
    return o.transpose(1, 0, 3, 2, 4).reshape(B, T, H, dv)


def mixer_gla_conv(h, w_in, gla_gate_w, gla_gate_b, gla_norm_g,
                   conv_dw, conv_dw_b, conv_ln_g, conv_ln_b, w_out):
    B, T, _ = h.shape
    z = h @ w_in
    idx = np.cumsum(AB_SPLITS)[:-1].tolist()
    q, k, v, r, gz, ca, cb = jnp.split(z, idx, axis=-1)

    log_a = jax.nn.log_sigmoid((gz @ gla_gate_w + gla_gate_b).astype(jnp.float32)) / GLA_GATE_TAU
    q = q.reshape(B, T, GLA_HEADS, GLA_HEAD_K) * (GLA_HEAD_K ** -0.5)
    k = k.reshape(B, T, GLA_HEADS, GLA_HEAD_K)
    v = v.reshape(B, T, GLA_HEADS, GLA_HEAD_V)
    log_a = log_a.reshape(B, T, GLA_HEADS, GLA_HEAD_K)
    o = gla_chunked(q, k, v, log_a)
    o = rms_norm(o, gla_norm_g).reshape(B, T, GLA_DV)
    a_out = (o * jax.nn.silu(r.astype(jnp.float32))).astype(h.dtype)

    u = ca * jax.nn.sigmoid(cb)
    rhs = conv_dw.astype(u.dtype).reshape(CONV_WIDTH, 1, CONV_CH)
    u = lax.conv_general_dilated(u, rhs, window_strides=(1,),
                                 padding=[(CONV_WIDTH - 1, 0)],
                                 dimension_numbers=('NWC', 'WIO', 'NWC'),
                                 feature_group_count=CONV_CH)
    u = u + conv_dw_b
    b_out = jax.nn.silu(layer_norm(u, conv_ln_g, conv_ln_b))

    return jnp.concatenate([a_out, b_out], axis=-1) @ w_out


def mixer_chunk_attention(h, w_qkv, rel_bias, w_o):
    B, T, D = h.shape
    n = T // CHUNK
    pad = LEFT_CHUNKS * CHUNK
    q, k, v = jnp.split(h @ w_qkv, 3, axis=-1)
    q = q.reshape(B, n, CHUNK, ATT_HEADS, ATT_HEAD_DIM) * (ATT_HEAD_DIM ** -0.5)
    kp = jnp.pad(k.reshape(B, T, ATT_HEADS, ATT_HEAD_DIM), ((0, 0), (pad, 0), (0, 0), (0, 0)))
    vp = jnp.pad(v.reshape(B, T, ATT_HEADS, ATT_HEAD_DIM), ((0, 0), (pad, 0), (0, 0), (0, 0)))

    t_pos = jnp.arange(CHUNK)
    s_pos = jnp.arange(BAND)
    rel = t_pos[:, None] - s_pos[None, :] + pad
    bias = rel_bias[:, jnp.clip(rel, -REL_CLIP, REL_CLIP) + REL_CLIP].astype(jnp.float32)

    def one_chunk(j):
        qj = lax.dynamic_index_in_dim(q, j, axis=1, keepdims=False)
        kj = lax.dynamic_slice_in_dim(kp, j * CHUNK, BAND, axis=1)
        vj = lax.dynamic_slice_in_dim(vp, j * CHUNK, BAND, axis=1)
        sc = jnp.einsum('bthd,bshd->bhts', qj, kj).astype(jnp.float32) + bias
        valid = (j * CHUNK - pad + s_pos) >= 0
        sc = jnp.where(valid[None, None, None, :], sc, -jnp.inf)
        pr = jax.nn.softmax(sc, axis=-1).astype(vj.dtype)
        return jnp.einsum('bhts,bshd->bthd', pr, vj)

    o = lax.map(one_chunk, jnp.arange(n))
    o = o.transpose(1, 0, 2, 3, 4).reshape(B, T, D)
    return o @ w_o


def _fwd_setup_inputs(seed: int = 0) -> dict:
    key = jax.random.key(seed)
    ks = jax.random.split(key, 24)
    f32 = jnp.float32

    def w(k, shape, fan_in):
        return jax.random.normal(k, shape, f32) * (fan_in ** -0.5)

    def gain(k, shape):
        return 1.0 + 0.05 * jax.random.normal(k, shape, f32)

    def small(k, shape, scale=0.01):
        return scale * jax.random.normal(k, shape, f32)

    return {
        "x": jax.random.normal(ks[0], (BATCH, SEQ, D_MODEL), f32),
        "p": jax.random.normal(ks[1], (DEPTH, BATCH, SEQ, D_PL), f32),
        "ffn_norm": gain(ks[2], (DEPTH, 2, D_MODEL)),
        "ffn_w_gate": w(ks[3], (DEPTH, 2, D_MODEL, D_FF), D_MODEL),
        "ffn_w_up": w(ks[4], (DEPTH, 2, D_MODEL, D_FF), D_MODEL),
        "ffn_w_down": w(ks[5], (DEPTH, 2, D_FF, D_MODEL), D_FF),
        "mix_norm": gain(ks[6], (DEPTH, D_MODEL)),
        "ab_w_in": w(ks[7], (N_EVEN, D_MODEL, AB_IN), D_MODEL),
        "gla_gate_w": w(ks[8], (N_EVEN, GLA_GATE_RANK, GLA_DK), GLA_GATE_RANK),
        "gla_gate_b": small(ks[9], (N_EVEN, GLA_DK), 0.1),
        "gla_norm_g": gain(ks[10], (N_EVEN, GLA_HEAD_V)),
        "conv_dw": w(ks[11], (N_EVEN, CONV_WIDTH, CONV_CH), CONV_WIDTH),
        "conv_dw_b": small(ks[12], (N_EVEN, CONV_CH)),
        "conv_ln_g": gain(ks[13], (N_EVEN, CONV_CH)),
        "conv_ln_b": small(ks[14], (N_EVEN, CONV_CH)),
        "ab_w_out": w(ks[15], (N_EVEN, AB_OUT, D_MODEL), AB_OUT),
        "att_w_qkv": w(ks[16], (N_ODD, D_MODEL, 3 * D_MODEL), D_MODEL),
        "att_rel_bias": small(ks[17], (N_ODD, ATT_HEADS, 2 * REL_CLIP + 1), 0.1),
        "att_w_o": w(ks[18], (N_ODD, D_MODEL, D_MODEL), D_MODEL),
        "pl_norm": gain(ks[19], (DEPTH, D_MODEL)),
        "pl_w_gate": w(ks[20], (DEPTH, D_MODEL, D_MODEL), D_MODEL),
        "pl_w_proj": w(ks[21], (DEPTH, D_PL, D_MODEL), D_PL),
        "final_norm": gain(ks[22], (D_MODEL,)),
    }


def _fwd_reference(x, p, ffn_norm, ffn_w_gate, ffn_w_up, ffn_w_down, mix_norm,
              ab_w_in, gla_gate_w, gla_gate_b, gla_norm_g,
              conv_dw, conv_dw_b, conv_ln_g, conv_ln_b, ab_w_out,
              att_w_qkv, att_rel_bias, att_w_o,
              pl_norm, pl_w_gate, pl_w_proj, final_norm):
    for i in range(DEPTH):
        x = x + 0.5 * swiglu_ffn(rms_norm(x, ffn_norm[i, 0]),
                                 ffn_w_gate[i, 0], ffn_w_up[i, 0], ffn_w_down[i, 0])
        h = rms_norm(x, mix_norm[i])
        e = i // 2
        if i % 2 == 0:
            x = x + mixer_gla_conv(h, ab_w_in[e], gla_gate_w[e], gla_gate_b[e], gla_norm_g[e],
                                   conv_dw[e], conv_dw_b[e], conv_ln_g[e], conv_ln_b[e], ab_w_out[e])
        else:
            x = x + mixer_chunk_attention(h, att_w_qkv[e], att_rel_bias[e], att_w_o[e])
        x = x + 0.5 * swiglu_ffn(rms_norm(x, ffn_norm[i, 1]),
                                 ffn_w_gate[i, 1], ffn_w_up[i, 1], ffn_w_down[i, 1])
        gate = jax.nn.sigmoid(rms_norm(x, pl_norm[i]) @ pl_w_gate[i])
        x = x + gate * (p[i] @ pl_w_proj[i])
    return rms_norm(x, final_norm)


import jax as _jax
import jax.numpy as _jnp

TWIN_FORMAT = 'train_step'
FWD_PARAMS = ['x', 'p', 'ffn_norm', 'ffn_w_gate', 'ffn_w_up', 'ffn_w_down', 'mix_norm', 'ab_w_in', 'gla_gate_w', 'gla_gate_b', 'gla_norm_g', 'conv_dw', 'conv_dw_b', 'conv_ln_g', 'conv_ln_b', 'ab_w_out', 'att_w_qkv', 'att_rel_bias', 'att_w_o', 'pl_norm', 'pl_w_gate', 'pl_w_proj', 'final_norm']
TWIN_WEIGHTS = ['ffn_norm', 'ffn_w_gate', 'ffn_w_up', 'ffn_w_down', 'mix_norm', 'ab_w_in', 'gla_gate_w', 'gla_gate_b', 'gla_norm_g', 'conv_dw', 'conv_dw_b', 'conv_ln_g', 'conv_ln_b', 'ab_w_out', 'att_w_qkv', 'att_rel_bias', 'att_w_o', 'pl_norm', 'pl_w_gate', 'pl_w_proj', 'final_norm']
TWIN_DIFF_INPUT = 'x'
TWIN_INPUTS = ['x', 'p', 'ffn_norm', 'ffn_w_gate', 'ffn_w_up', 'ffn_w_down', 'mix_norm', 'ab_w_in', 'gla_gate_w', 'gla_gate_b', 'gla_norm_g', 'conv_dw', 'conv_dw_b', 'conv_ln_g', 'conv_ln_b', 'ab_w_out', 'att_w_qkv', 'att_rel_bias', 'att_w_o', 'pl_norm', 'pl_w_gate', 'pl_w_proj', 'final_norm', 'loss_target', 'm_ffn_norm', 'm_ffn_w_gate', 'm_ffn_w_up', 'm_ffn_w_down', 'm_mix_norm', 'm_ab_w_in', 'm_gla_gate_w', 'm_gla_gate_b', 'm_gla_norm_g', 'm_conv_dw', 'm_conv_dw_b', 'm_conv_ln_g', 'm_conv_ln_b', 'm_ab_w_out', 'm_att_w_qkv', 'm_att_rel_bias', 'm_att_w_o', 'm_pl_norm', 'm_pl_w_gate', 'm_pl_w_proj', 'm_final_norm', 'v_ffn_norm', 'v_ffn_w_gate', 'v_ffn_w_up', 'v_ffn_w_down', 'v_mix_norm', 'v_ab_w_in', 'v_gla_gate_w', 'v_gla_gate_b', 'v_gla_norm_g', 'v_conv_dw', 'v_conv_dw_b', 'v_conv_ln_g', 'v_conv_ln_b', 'v_ab_w_out', 'v_att_w_qkv', 'v_att_rel_bias', 'v_att_w_o', 'v_pl_norm', 'v_pl_w_gate', 'v_pl_w_proj', 'v_final_norm']
TWIN_OUTPUTS = ['loss', 'grad_x', 'grad_ffn_norm', 'grad_ffn_w_gate', 'grad_ffn_w_up', 'grad_ffn_w_down', 'grad_mix_norm', 'grad_ab_w_in', 'grad_gla_gate_w', 'grad_gla_gate_b', 'grad_gla_norm_g', 'grad_conv_dw', 'grad_conv_dw_b', 'grad_conv_ln_g', 'grad_conv_ln_b', 'grad_ab_w_out', 'grad_att_w_qkv', 'grad_att_rel_bias', 'grad_att_w_o', 'grad_pl_norm', 'grad_pl_w_gate', 'grad_pl_w_proj', 'grad_final_norm', 'delta_ffn_norm', 'delta_ffn_w_gate', 'delta_ffn_w_up', 'delta_ffn_w_down', 'delta_mix_norm', 'delta_ab_w_in', 'delta_gla_gate_w', 'delta_gla_gate_b', 'delta_gla_norm_g', 'delta_conv_dw', 'delta_conv_dw_b', 'delta_conv_ln_g', 'delta_conv_ln_b', 'delta_ab_w_out', 'delta_att_w_qkv', 'delta_att_rel_bias', 'delta_att_w_o', 'delta_pl_norm', 'delta_pl_w_gate', 'delta_pl_w_proj', 'delta_final_norm', 'new_m_ffn_norm', 'new_m_ffn_w_gate', 'new_m_ffn_w_up', 'new_m_ffn_w_down', 'new_m_mix_norm', 'new_m_ab_w_in', 'new_m_gla_gate_w', 'new_m_gla_gate_b', 'new_m_gla_norm_g', 'new_m_conv_dw', 'new_m_conv_dw_b', 'new_m_conv_ln_g', 'new_m_conv_ln_b', 'new_m_ab_w_out', 'new_m_att_w_qkv', 'new_m_att_rel_bias', 'new_m_att_w_o', 'new_m_pl_norm', 'new_m_pl_w_gate', 'new_m_pl_w_proj', 'new_m_final_norm', 'new_v_ffn_norm', 'new_v_ffn_w_gate', 'new_v_ffn_w_up', 'new_v_ffn_w_down', 'new_v_mix_norm', 'new_v_ab_w_in', 'new_v_gla_gate_w', 'new_v_gla_gate_b', 'new_v_gla_norm_g', 'new_v_conv_dw', 'new_v_conv_dw_b', 'new_v_conv_ln_g', 'new_v_conv_ln_b', 'new_v_ab_w_out', 'new_v_att_w_qkv', 'new_v_att_rel_bias', 'new_v_att_w_o', 'new_v_pl_norm', 'new_v_pl_w_gate', 'new_v_pl_w_proj', 'new_v_final_norm']
TWIN_LEAF_KINDS = {'loss': 'loss', 'grad_x': 'grad_x', 'grad_ffn_norm': 'grad_w', 'grad_ffn_w_gate': 'grad_w', 'grad_ffn_w_up': 'grad_w', 'grad_ffn_w_down': 'grad_w', 'grad_mix_norm': 'grad_w', 'grad_ab_w_in': 'grad_w', 'grad_gla_gate_w': 'grad_w', 'grad_gla_gate_b': 'grad_w', 'grad_gla_norm_g': 'grad_w', 'grad_conv_dw': 'grad_w', 'grad_conv_dw_b': 'grad_w', 'grad_conv_ln_g': 'grad_w', 'grad_conv_ln_b': 'grad_w', 'grad_ab_w_out': 'grad_w', 'grad_att_w_qkv': 'grad_w', 'grad_att_rel_bias': 'grad_w', 'grad_att_w_o': 'grad_w', 'grad_pl_norm': 'grad_w', 'grad_pl_w_gate': 'grad_w', 'grad_pl_w_proj': 'grad_w', 'grad_final_norm': 'grad_w', 'delta_ffn_norm': 'delta_w', 'delta_ffn_w_gate': 'delta_w', 'delta_ffn_w_up': 'delta_w', 'delta_ffn_w_down': 'delta_w', 'delta_mix_norm': 'delta_w', 'delta_ab_w_in': 'delta_w', 'delta_gla_gate_w': 'delta_w', 'delta_gla_gate_b': 'delta_w', 'delta_gla_norm_g': 'delta_w', 'delta_conv_dw': 'delta_w', 'delta_conv_dw_b': 'delta_w', 'delta_conv_ln_g': 'delta_w', 'delta_conv_ln_b': 'delta_w', 'delta_ab_w_out': 'delta_w', 'delta_att_w_qkv': 'delta_w', 'delta_att_rel_bias': 'delta_w', 'delta_att_w_o': 'delta_w', 'delta_pl_norm': 'delta_w', 'delta_pl_w_gate': 'delta_w', 'delta_pl_w_proj': 'delta_w', 'delta_final_norm': 'delta_w', 'new_m_ffn_norm': 'new_m', 'new_m_ffn_w_gate': 'new_m', 'new_m_ffn_w_up': 'new_m', 'new_m_ffn_w_down': 'new_m', 'new_m_mix_norm': 'new_m', 'new_m_ab_w_in': 'new_m', 'new_m_gla_gate_w': 'new_m', 'new_m_gla_gate_b': 'new_m', 'new_m_gla_norm_g': 'new_m', 'new_m_conv_dw': 'new_m', 'new_m_conv_dw_b': 'new_m', 'new_m_conv_ln_g': 'new_m', 'new_m_conv_ln_b': 'new_m', 'new_m_ab_w_out': 'new_m', 'new_m_att_w_qkv': 'new_m', 'new_m_att_rel_bias': 'new_m', 'new_m_att_w_o': 'new_m', 'new_m_pl_norm': 'new_m', 'new_m_pl_w_gate': 'new_m', 'new_m_pl_w_proj': 'new_m', 'new_m_final_norm': 'new_m', 'new_v_ffn_norm': 'new_v', 'new_v_ffn_w_gate': 'new_v', 'new_v_ffn_w_up': 'new_v', 'new_v_ffn_w_down': 'new_v', 'new_v_mix_norm': 'new_v', 'new_v_ab_w_in': 'new_v', 'new_v_gla_gate_w': 'new_v', 'new_v_gla_gate_b': 'new_v', 'new_v_gla_norm_g': 'new_v', 'new_v_conv_dw': 'new_v', 'new_v_conv_dw_b': 'new_v', 'new_v_conv_ln_g': 'new_v', 'new_v_conv_ln_b': 'new_v', 'new_v_ab_w_out': 'new_v', 'new_v_att_w_qkv': 'new_v', 'new_v_att_rel_bias': 'new_v', 'new_v_att_w_o': 'new_v', 'new_v_pl_norm': 'new_v', 'new_v_pl_w_gate': 'new_v', 'new_v_pl_w_proj': 'new_v', 'new_v_final_norm': 'new_v'}


def _forward(args):
    return _fwd_reference(*[args[k] for k in FWD_PARAMS])


def _output_shape():
    out = _jax.eval_shape(lambda: _forward(_fwd_setup_inputs(0)))
    return out.shape, out.dtype

N_MICROBATCH = 1
ADAM_LR = 0.001
ADAM_B1 = 0.9
ADAM_B2 = 0.999
ADAM_EPS = 1e-08
ADAM_WD = 0.01
ADAM_STEP = 10
PER_EXAMPLE_BATCH_AXIS = {'x': 0, 'p': 1, 'loss_target': 0}
SHARED_INPUTS = []
_WEIGHT_DTYPES = {'ffn_norm': _jnp.float32, 'ffn_w_gate': _jnp.float32, 'ffn_w_up': _jnp.float32, 'ffn_w_down': _jnp.float32, 'mix_norm': _jnp.float32, 'ab_w_in': _jnp.float32, 'gla_gate_w': _jnp.float32, 'gla_gate_b': _jnp.float32, 'gla_norm_g': _jnp.float32, 'conv_dw': _jnp.float32, 'conv_dw_b': _jnp.float32, 'conv_ln_g': _jnp.float32, 'conv_ln_b': _jnp.float32, 'ab_w_out': _jnp.float32, 'att_w_qkv': _jnp.float32, 'att_rel_bias': _jnp.float32, 'att_w_o': _jnp.float32, 'pl_norm': _jnp.float32, 'pl_w_gate': _jnp.float32, 'pl_w_proj': _jnp.float32, 'final_norm': _jnp.float32}
MOMENT_SCALE = {'ffn_norm': 2.214781e-02, 'ffn_w_gate': 9.595641e-03, 'ffn_w_up': 9.296234e-03, 'ffn_w_down': 1.541197e-02, 'mix_norm': 3.555769e-02, 'ab_w_in': 3.126146e-02, 'gla_gate_w': 4.804265e-03, 'gla_gate_b': 2.028040e-02, 'gla_norm_g': 6.092543e-02, 'conv_dw': 3.118008e-02, 'conv_dw_b': 7.927118e-02, 'conv_ln_g': 4.305808e-02, 'conv_ln_b': 4.669483e-02, 'ab_w_out': 3.131434e-02, 'att_w_qkv': 6.815418e-03, 'att_rel_bias': 4.111964e-03, 'att_w_o': 8.047362e-03, 'pl_norm': 9.958598e-03, 'pl_w_gate': 9.715354e-03, 'pl_w_proj': 2.398135e-02, 'final_norm': 8.006441e+00}


def _to_microbatches(a, axis):
    t = _jnp.moveaxis(a, axis, 0)
    t = t.reshape((N_MICROBATCH, t.shape[0] // N_MICROBATCH) + t.shape[1:])
    return _jnp.moveaxis(t, 1, axis + 1)


def setup_inputs(seed: int = 0) -> dict:
    inp = _fwd_setup_inputs(seed)
    key = _jax.random.fold_in(_jax.random.key(seed), 7919)
    shape, _ = _output_shape()
    out = dict(inp)
    out["loss_target"] = _jax.random.normal(_jax.random.fold_in(key, 0), shape, _jnp.float32)
    for i, name in enumerate(TWIN_WEIGHTS):
        w = inp[name].astype(_jnp.float32)
        if MOMENT_SCALE is None:
            s = _jnp.sqrt(_jnp.mean(_jnp.square(w)) + 1e-30)
        else:
            s = MOMENT_SCALE[name]
        km, kv = _jax.random.split(_jax.random.fold_in(key, i + 1))
        out[name] = w
        out["m_" + name] = s * _jax.random.normal(km, w.shape, _jnp.float32)
        out["v_" + name] = (s * s) * _jax.random.uniform(kv, w.shape, _jnp.float32, 0.5, 1.5)
    if N_MICROBATCH > 1:
        for name, axis in PER_EXAMPLE_BATCH_AXIS.items():
            out[name] = _to_microbatches(out[name], axis)
    return {'x': out['x'], 'p': out['p'], 'ffn_norm': out['ffn_norm'], 'ffn_w_gate': out['ffn_w_gate'], 'ffn_w_up': out['ffn_w_up'], 'ffn_w_down': out['ffn_w_down'], 'mix_norm': out['mix_norm'], 'ab_w_in': out['ab_w_in'], 'gla_gate_w': out['gla_gate_w'], 'gla_gate_b': out['gla_gate_b'], 'gla_norm_g': out['gla_norm_g'], 'conv_dw': out['conv_dw'], 'conv_dw_b': out['conv_dw_b'], 'conv_ln_g': out['conv_ln_g'], 'conv_ln_b': out['conv_ln_b'], 'ab_w_out': out['ab_w_out'], 'att_w_qkv': out['att_w_qkv'], 'att_rel_bias': out['att_rel_bias'], 'att_w_o': out['att_w_o'], 'pl_norm': out['pl_norm'], 'pl_w_gate': out['pl_w_gate'], 'pl_w_proj': out['pl_w_proj'], 'final_norm': out['final_norm'], 'loss_target': out['loss_target'], 'm_ffn_norm': out['m_ffn_norm'], 'm_ffn_w_gate': out['m_ffn_w_gate'], 'm_ffn_w_up': out['m_ffn_w_up'], 'm_ffn_w_down': out['m_ffn_w_down'], 'm_mix_norm': out['m_mix_norm'], 'm_ab_w_in': out['m_ab_w_in'], 'm_gla_gate_w': out['m_gla_gate_w'], 'm_gla_gate_b': out['m_gla_gate_b'], 'm_gla_norm_g': out['m_gla_norm_g'], 'm_conv_dw': out['m_conv_dw'], 'm_conv_dw_b': out['m_conv_dw_b'], 'm_conv_ln_g': out['m_conv_ln_g'], 'm_conv_ln_b': out['m_conv_ln_b'], 'm_ab_w_out': out['m_ab_w_out'], 'm_att_w_qkv': out['m_att_w_qkv'], 'm_att_rel_bias': out['m_att_rel_bias'], 'm_att_w_o': out['m_att_w_o'], 'm_pl_norm': out['m_pl_norm'], 'm_pl_w_gate': out['m_pl_w_gate'], 'm_pl_w_proj': out['m_pl_w_proj'], 'm_final_norm': out['m_final_norm'], 'v_ffn_norm': out['v_ffn_norm'], 'v_ffn_w_gate': out['v_ffn_w_gate'], 'v_ffn_w_up': out['v_ffn_w_up'], 'v_ffn_w_down': out['v_ffn_w_down'], 'v_mix_norm': out['v_mix_norm'], 'v_ab_w_in': out['v_ab_w_in'], 'v_gla_gate_w': out['v_gla_gate_w'], 'v_gla_gate_b': out['v_gla_gate_b'], 'v_gla_norm_g': out['v_gla_norm_g'], 'v_conv_dw': out['v_conv_dw'], 'v_conv_dw_b': out['v_conv_dw_b'], 'v_conv_ln_g': out['v_conv_ln_g'], 'v_conv_ln_b': out['v_conv_ln_b'], 'v_ab_w_out': out['v_ab_w_out'], 'v_att_w_qkv': out['v_att_w_qkv'], 'v_att_rel_bias': out['v_att_rel_bias'], 'v_att_w_o': out['v_att_w_o'], 'v_pl_norm': out['v_pl_norm'], 'v_pl_w_gate': out['v_pl_w_gate'], 'v_pl_w_proj': out['v_pl_w_proj'], 'v_final_norm': out['v_final_norm']}


def _loss(weights, diff, rest, loss_target):
    with _jax.named_scope("forward"):
        args = {**rest, TWIN_DIFF_INPUT: diff, **{k: w.astype(_WEIGHT_DTYPES[k]) for k, w in weights.items()}}
        y = _forward(args)
    with _jax.named_scope("loss_head"):
        err = _jnp.square(y.astype(_jnp.float32) - loss_target)
        return 0.5 * _jnp.sum(_jnp.mean(err, axis=-1)) if err.ndim else 0.5 * err


def _adamw(w, g, m, v):
    m = ADAM_B1 * m + (1.0 - ADAM_B1) * g
    v = ADAM_B2 * v + (1.0 - ADAM_B2) * _jnp.square(g)
    m_hat = m / (1.0 - ADAM_B1 ** ADAM_STEP)
    v_hat = v / (1.0 - ADAM_B2 ** ADAM_STEP)
    delta = -ADAM_LR * (m_hat / (_jnp.sqrt(v_hat) + ADAM_EPS) + ADAM_WD * w)
    return delta, m, v


def reference(x, p, ffn_norm, ffn_w_gate, ffn_w_up, ffn_w_down, mix_norm, ab_w_in, gla_gate_w, gla_gate_b, gla_norm_g, conv_dw, conv_dw_b, conv_ln_g, conv_ln_b, ab_w_out, att_w_qkv, att_rel_bias, att_w_o, pl_norm, pl_w_gate, pl_w_proj, final_norm, loss_target, m_ffn_norm, m_ffn_w_gate, m_ffn_w_up, m_ffn_w_down, m_mix_norm, m_ab_w_in, m_gla_gate_w, m_gla_gate_b, m_gla_norm_g, m_conv_dw, m_conv_dw_b, m_conv_ln_g, m_conv_ln_b, m_ab_w_out, m_att_w_qkv, m_att_rel_bias, m_att_w_o, m_pl_norm, m_pl_w_gate, m_pl_w_proj, m_final_norm, v_ffn_norm, v_ffn_w_gate, v_ffn_w_up, v_ffn_w_down, v_mix_norm, v_ab_w_in, v_gla_gate_w, v_gla_gate_b, v_gla_norm_g, v_conv_dw, v_conv_dw_b, v_conv_ln_g, v_conv_ln_b, v_ab_w_out, v_att_w_qkv, v_att_rel_bias, v_att_w_o, v_pl_norm, v_pl_w_gate, v_pl_w_proj, v_final_norm):
    given = dict(x=x, p=p, ffn_norm=ffn_norm, ffn_w_gate=ffn_w_gate, ffn_w_up=ffn_w_up, ffn_w_down=ffn_w_down, mix_norm=mix_norm, ab_w_in=ab_w_in, gla_gate_w=gla_gate_w, gla_gate_b=gla_gate_b, gla_norm_g=gla_norm_g, conv_dw=conv_dw, conv_dw_b=conv_dw_b, conv_ln_g=conv_ln_g, conv_ln_b=conv_ln_b, ab_w_out=ab_w_out, att_w_qkv=att_w_qkv, att_rel_bias=att_rel_bias, att_w_o=att_w_o, pl_norm=pl_norm, pl_w_gate=pl_w_gate, pl_w_proj=pl_w_proj, final_norm=final_norm, loss_target=loss_target, m_ffn_norm=m_ffn_norm, m_ffn_w_gate=m_ffn_w_gate, m_ffn_w_up=m_ffn_w_up, m_ffn_w_down=m_ffn_w_down, m_mix_norm=m_mix_norm, m_ab_w_in=m_ab_w_in, m_gla_gate_w=m_gla_gate_w, m_gla_gate_b=m_gla_gate_b, m_gla_norm_g=m_gla_norm_g, m_conv_dw=m_conv_dw, m_conv_dw_b=m_conv_dw_b, m_conv_ln_g=m_conv_ln_g, m_conv_ln_b=m_conv_ln_b, m_ab_w_out=m_ab_w_out, m_att_w_qkv=m_att_w_qkv, m_att_rel_bias=m_att_rel_bias, m_att_w_o=m_att_w_o, m_pl_norm=m_pl_norm, m_pl_w_gate=m_pl_w_gate, m_pl_w_proj=m_pl_w_proj, m_final_norm=m_final_norm, v_ffn_norm=v_ffn_norm, v_ffn_w_gate=v_ffn_w_gate, v_ffn_w_up=v_ffn_w_up, v_ffn_w_down=v_ffn_w_down, v_mix_norm=v_mix_norm, v_ab_w_in=v_ab_w_in, v_gla_gate_w=v_gla_gate_w, v_gla_gate_b=v_gla_gate_b, v_gla_norm_g=v_gla_norm_g, v_conv_dw=v_conv_dw, v_conv_dw_b=v_conv_dw_b, v_conv_ln_g=v_conv_ln_g, v_conv_ln_b=v_conv_ln_b, v_ab_w_out=v_ab_w_out, v_att_w_qkv=v_att_w_qkv, v_att_rel_bias=v_att_rel_bias, v_att_w_o=v_att_w_o, v_pl_norm=v_pl_norm, v_pl_w_gate=v_pl_w_gate, v_pl_w_proj=v_pl_w_proj, v_final_norm=v_final_norm)
    weights = {n: given[n] for n in TWIN_WEIGHTS}
    shared = {n: given[n] for n in SHARED_INPUTS}
    per_example = {n: given[n] for n in ['x', 'p']}
    grad_fn = _jax.value_and_grad(_loss, argnums=(0, 1))

    def one_microbatch(ex, loss_target):
        ex = dict(ex)
        diff = ex.pop(TWIN_DIFF_INPUT)
        return grad_fn(weights, diff, {**shared, **ex}, loss_target)

    if N_MICROBATCH == 1:
        loss, (grad_w, grad_x) = one_microbatch(per_example, given["loss_target"])
    else:
        def body(carry, xs):
            loss_sum, grad_sum = carry
            l_k, (gw_k, gx_k) = one_microbatch(xs[0], xs[1])
            with _jax.named_scope("update"):
                return (loss_sum + l_k, _jax.tree.map(_jnp.add, grad_sum, gw_k)), gx_k

        init = (_jnp.zeros((), _jnp.float32), _jax.tree.map(_jnp.zeros_like, weights))
        (loss, grad_w), grad_x = _jax.lax.scan(body, init, (per_example, given["loss_target"]))
    with _jax.named_scope("update"):
        delta_w, new_m, new_v = {}, {}, {}
        for n in TWIN_WEIGHTS:
            delta_w[n], new_m[n], new_v[n] = _adamw(weights[n], grad_w[n], given["m_" + n], given["v_" + n])
    return (loss, grad_x, *[grad_w[n] for n in TWIN_WEIGHTS], *[delta_w[n] for n in TWIN_WEIGHTS],
            *[new_m[n] for n in TWIN_WEIGHTS], *[new_v[n] for n in TWIN_WEIGHTS])
```

```python
import math

import jax
import jax.numpy as jnp
from jax import lax
from jax.experimental import pallas as pl
from jax.experimental.pallas import tpu as pltpu

F32 = jnp.float32
BF = jnp.bfloat16
MESH = pl.DeviceIdType.MESH
HI = lax.Precision.HIGHEST
V7X_VMEM_LIMIT = 56 * 1024 * 1024
LANES = 128
EPS = 1e-6
NEG = -1e30

CHUNK = 64
LEFT_CHUNKS = 8
QB = 256
KW = QB + LEFT_CHUNKS * CHUNK
PADK = LEFT_CHUNKS * CHUNK
REL_CLIP = 128
REL_PAD = 384
ATT_H = 16
HD = 128
GLA_H = 4
HK = 128
HV = 256
GLA_TAU = 16.0
CONV_W = 31
CONV_PAD = 32
N_CHIPS = 4

ADAM_LR = 0.001
ADAM_B1 = 0.9
ADAM_B2 = 0.999
ADAM_EPS = 1e-08
ADAM_WD = 0.01
ADAM_STEP = 10

NN = ((1,), (0,))
NT = ((1,), (1,))
TN = ((0,), (0,))


def _dot(a, b, dims, prec=None):
    return lax.dot_general(a, b, (dims, ((), ())), preferred_element_type=F32, precision=prec)


def _bdot(a, b, dims):
    return _dot(a.astype(BF), b.astype(BF), dims)


def _cp(*sem):
    return pltpu.CompilerParams(dimension_semantics=sem if sem else None, vmem_limit_bytes=V7X_VMEM_LIMIT)


def _sds(shape, dtype):
    return jax.ShapeDtypeStruct(shape, dtype)


def _sigmoid(x):
    return 1.0 / (1.0 + jnp.exp(-x))


def _silu(x):
    return x * _sigmoid(x)


def _dsilu(x):
    s = _sigmoid(x)
    return s * (1.0 + x * (1.0 - s))


def _f(x):
    return x.astype(F32)


def _row_spec(item, tb):
    if isinstance(item, tuple) and item[0] == "cols":
        _, arr, width, blk = item
        return arr, pl.BlockSpec((tb, width), lambda i: (i, blk))
    if isinstance(item, tuple) and item[0] == "lead":
        _, arr, idx = item
        return arr, pl.BlockSpec((None, tb, arr.shape[2]), lambda i: (idx, i, 0))
    return item, pl.BlockSpec((tb, item.shape[1]), lambda i: (i, 0))


def rowwise(name, fn, rows, bcast, outs, accs=(), tb=256):
    pairs = [_row_spec(it, tb) for it in rows]
    arrs = [a for a, _ in pairs]
    n_rows = arrs[0].shape[0] if arrs[0].ndim == 2 else arrs[0].shape[1]
    tb = min(tb, n_rows)
    pairs = [_row_spec(it, tb) for it in rows]
    n_r, n_b, n_o = len(rows), len(bcast), len(outs)

    def body(*refs):
        r, b = refs[:n_r], refs[n_r:n_r + n_b]
        o, a = refs[n_r + n_b:n_r + n_b + n_o], refs[n_r + n_b + n_o:]
        res = fn(*[v[...] for v in r], *[v[...] for v in b])
        if not isinstance(res, tuple):
            res = (res,)
        for ref, val in zip(o, res[:n_o]):
            ref[...] = val.astype(ref.dtype)
        if a:
            @pl.when(pl.program_id(0) == 0)
            def _():
                for ref in a:
                    ref[...] = jnp.zeros_like(ref)
            for ref, val in zip(a, res[n_o:]):
                ref[...] += val

    in_specs = [s for _, s in pairs] + [pl.BlockSpec(v.shape, lambda i, nd=v.ndim: (0,) * nd) for v in bcast]
    out_specs = [pl.BlockSpec((tb, c), lambda i: (i, 0)) for c, _ in outs]
    out_specs += [pl.BlockSpec(s, lambda i: (0, 0)) for s in accs]
    out_shape = [_sds((n_rows, c), dt) for c, dt in outs] + [_sds(s, F32) for s in accs]
    res = pl.pallas_call(body, grid=(n_rows // tb,), in_specs=in_specs, out_specs=out_specs, out_shape=out_shape,
                         name=name, compiler_params=_cp("arbitrary"))(*arrs, *bcast)
    return res[0] if len(res) == 1 else tuple(res)


def mm(name, dims, grid, a, a_bs, a_im, b, b_bs, b_im, o_bs, o_im, out, scale=1.0, res=None, into=None):
    nk = grid[2]
    acc_shape = tuple(d for d in o_bs if d is not None)

    def body(*refs):
        a_ref, b_ref = refs[0], refs[1]
        pos = 2
        res_ref = None
        if res is not None:
            res_ref = refs[pos]
            pos += 1
        if into is not None:
            pos += 1
        o_ref = refs[pos]
        part = _bdot(a_ref[...], b_ref[...], dims)

        def finish(acc):
            v = acc * scale if scale != 1.0 else acc
            if res_ref is not None:
                v = v + _f(res_ref[...])
            o_ref[...] = v.astype(o_ref.dtype)

        if nk == 1:
            finish(part)
        else:
            acc_ref = refs[pos + 1]
            k = pl.program_id(2)

            @pl.when(k == 0)
            def _():
                acc_ref[...] = part

            @pl.when(k > 0)
            def _():
                acc_ref[...] += part

            @pl.when(k == nk - 1)
            def _():
                finish(acc_ref[...])

    operands = [a, b]
    in_specs = [pl.BlockSpec(a_bs, a_im), pl.BlockSpec(b_bs, b_im)]
    if res is not None:
        operands.append(res)
        in_specs.append(pl.BlockSpec(o_bs, o_im))
    aliases = {}
    if into is not None:
        aliases = {len(operands): 0}
        operands.append(into)
        in_specs.append(pl.BlockSpec(memory_space=pl.ANY))
        out = _sds(into.shape, into.dtype)
    scratch = [pltpu.VMEM(acc_shape, F32)] if nk > 1 else []
    return pl.pallas_call(body, grid=grid, in_specs=in_specs, out_specs=pl.BlockSpec(o_bs, o_im), out_shape=out,
                          scratch_shapes=scratch, input_output_aliases=aliases, name=name,
                          compiler_params=_cp("parallel", "parallel", "arbitrary"))(*operands)


def _mk(m, n, k):
    return (m, k)


def _mn(m, n, k):
    return (m, n)


def _km(m, n, k):
    return (k, m)


def _kn(m, n, k):
    return (k, n)


def _nk(m, n, k):
    return (n, k)


def mm_nn(name, a, b, out_dtype, tn, tk=512, scale=1.0, res=None):
    T, K = a.shape
    N = b.shape[1]
    tm, tk, tn = min(T, 1024), min(tk, K), min(tn, N)
    return mm(name, NN, (T // tm, N // tn, K // tk), a, (tm, tk), _mk, b, (tk, tn), _kn, (tm, tn), _mn,
              _sds((T, N), out_dtype), scale=scale, res=res)


def mm_nt(name, a, b, out_dtype, tn, tk=512, scale=1.0, res=None):
    T, K = a.shape
    N = b.shape[0]
    tm, tk, tn = min(T, 1024), min(tk, K), min(tn, N)
    return mm(name, NT, (T // tm, N // tn, K // tk), a, (tm, tk), _mk, b, (tn, tk), _nk, (tm, tn), _mn,
              _sds((T, N), out_dtype), scale=scale, res=res)


def mm_tn(name, a, b, out_dtype, tm, tn, scale=1.0):
    T, M = a.shape
    N = b.shape[1]
    tk, tm, tn = min(T, 1024), min(tm, M), min(tn, N)
    return mm(name, TN, (M // tm, N // tn, T // tk), a, (tk, tm), _km, b, (tk, tn), _kn, (tm, tn), _mn,
              _sds((M, N), out_dtype), scale=scale)


HBM_SPEC = pl.BlockSpec(memory_space=pl.ANY)


def _place():
    x, y, c = lax.axis_index("x"), lax.axis_index("y"), lax.axis_index("c")
    chips = [(1 - x, y), (x, 1 - y), (1 - x, 1 - y)]
    return x, y, c, chips


def allgather8(v, name):
    m_per, n = v.shape

    def body(x_ref, out_ref, send_sems, recv_sems, local_sem):
        x, y, c, chips = _place()
        me, sibling = (x, y, c), (x, y, 1 - c)

        def rows(px, py, pc):
            return out_ref.at[pl.ds((4 * px + 2 * py + pc) * m_per, m_per), :]

        def copy(k, block, to, src=None):
            return pltpu.make_async_remote_copy(
                src_ref=rows(*block) if src is None else src, dst_ref=rows(*block),
                send_sem=send_sems.at[k], recv_sem=recv_sems.at[k], device_id=to, device_id_type=MESH)

        mine = pltpu.make_async_copy(x_ref, rows(*me), local_sem)
        mine.start()
        first = [copy(0, me, sibling, src=x_ref)]
        first += [copy(1 + j, me, (*chip, c), src=x_ref) for j, chip in enumerate(chips)]
        for cp in first:
            cp.start()
        passed = [copy(4 + j, (*chip, c), sibling) for j, chip in enumerate(chips)]
        for j, chip in enumerate(chips):
            copy(1 + j, (*chip, c), me).wait_recv()
            passed[j].start()
        copy(0, sibling, me).wait_recv()
        for j, chip in enumerate(chips):
            copy(4 + j, (*chip, 1 - c), me).wait_recv()
        for cp in first + passed:
            cp.wait_send()
        mine.wait()

    return pl.pallas_call(
        body, out_shape=_sds((8 * m_per, n), v.dtype), in_specs=[pl.BlockSpec(memory_space=pltpu.VMEM)],
        out_specs=pl.BlockSpec(memory_space=pltpu.VMEM), name=name,
        scratch_shapes=[pltpu.SemaphoreType.DMA((7,)), pltpu.SemaphoreType.DMA((7,)), pltpu.SemaphoreType.DMA],
    )(v)


def gather4(sh, name):
    _, A, B = sh.shape

    def body(s_ref, o_ref, ssem, rsem, fssem, frsem, lsem):
        x, y, c, chips = _place()
        me = 2 * x + y
        sib = (x, y, 1 - c)
        local = pltpu.make_async_copy(s_ref, o_ref.at[me], lsem)
        local.start()

        def ici(j, src_chip, to):
            k = 2 * src_chip[0] + src_chip[1]
            return pltpu.make_async_remote_copy(src_ref=s_ref.at[c], dst_ref=o_ref.at[k, c], send_sem=ssem.at[j],
                                                recv_sem=rsem.at[j], device_id=to, device_id_type=MESH)

        def fwd(j, src_chip, half):
            k = 2 * src_chip[0] + src_chip[1]
            return pltpu.make_async_remote_copy(src_ref=o_ref.at[k, half], dst_ref=o_ref.at[k, half],
                                                send_sem=fssem.at[j], recv_sem=frsem.at[j], device_id=sib,
                                                device_id_type=MESH)

        sends = [ici(j, (x, y), (*chip, c)) for j, chip in enumerate(chips)]
        for cp in sends:
            cp.start()
        passed = [fwd(j, chip, c) for j, chip in enumerate(chips)]
        for j, chip in enumerate(chips):
            ici(j, chip, (x, y, c)).wait_recv()
            passed[j].start()
        for j, chip in enumerate(chips):
            fwd(j, chip, 1 - c).wait_recv()
        for cp in sends + passed:
            cp.wait_send()
        local.wait()

    return pl.pallas_call(
        body, out_shape=_sds((N_CHIPS, 2, A, B), sh.dtype), in_specs=[HBM_SPEC], out_specs=HBM_SPEC, name=name,
        scratch_shapes=[pltpu.SemaphoreType.DMA((3,)), pltpu.SemaphoreType.DMA((3,)), pltpu.SemaphoreType.DMA((3,)),
                        pltpu.SemaphoreType.DMA((3,)), pltpu.SemaphoreType.DMA],
    )(sh)


def rs_sibling(g, name):
    _, _, A, B = g.shape

    def body(g_ref, own_ref, got_ref, ssem, rsem, lsem):
        x, y, c, _ = _place()
        sib = (x, y, 1 - c)
        local = [pltpu.make_async_copy(g_ref.at[k, c], own_ref.at[k], lsem.at[k]) for k in range(N_CHIPS)]
        remote = [pltpu.make_async_remote_copy(src_ref=g_ref.at[k, 1 - c], dst_ref=got_ref.at[k], send_sem=ssem.at[k],
                                               recv_sem=rsem.at[k], device_id=sib, device_id_type=MESH)
                  for k in range(N_CHIPS)]
        for cp in remote + local:
            cp.start()
        for cp in remote:
            cp.wait_recv()
        for cp in remote:
            cp.wait_send()
        for cp in local:
            cp.wait()

    return pl.pallas_call(
        body, out_shape=(_sds((N_CHIPS, A, B), g.dtype), _sds((N_CHIPS, A, B), g.dtype)), in_specs=[HBM_SPEC],
        out_specs=(HBM_SPEC, HBM_SPEC), name=name,
        scratch_shapes=[pltpu.SemaphoreType.DMA((N_CHIPS,)), pltpu.SemaphoreType.DMA((N_CHIPS,)),
                        pltpu.SemaphoreType.DMA((N_CHIPS,))],
    )(g)


def rs_chips(h, name):
    _, A, B = h.shape

    def body(h_ref, own_ref, got_ref, ssem, rsem, lsem):
        x, y, c, chips = _place()
        local = pltpu.make_async_copy(h_ref.at[2 * x + y], own_ref, lsem)
        remote = [pltpu.make_async_remote_copy(src_ref=h_ref.at[2 * chip[0] + chip[1]], dst_ref=got_ref.at[j],
                                               send_sem=ssem.at[j], recv_sem=rsem.at[j], device_id=(*chip, c),
                                               device_id_type=MESH) for j, chip in enumerate(chips)]
        for cp in remote:
            cp.start()
        local.start()
        for cp in remote:
            cp.wait_recv()
        for cp in remote:
            cp.wait_send()
        local.wait()

    return pl.pallas_call(
        body, out_shape=(_sds((A, B), h.dtype), _sds((3, A, B), h.dtype)), in_specs=[HBM_SPEC],
        out_specs=(HBM_SPEC, HBM_SPEC), name=name,
        scratch_shapes=[pltpu.SemaphoreType.DMA((3,)), pltpu.SemaphoreType.DMA((3,)), pltpu.SemaphoreType.DMA],
    )(h)


def rs_share(r, name):
    A, B = r.shape

    def body(r_ref, o_ref, ssem, rsem, lsem):
        x, y, c, _ = _place()
        local = pltpu.make_async_copy(r_ref, o_ref.at[c], lsem)
        remote = pltpu.make_async_remote_copy(src_ref=r_ref, dst_ref=o_ref.at[c], send_sem=ssem, recv_sem=rsem,
                                              device_id=(x, y, 1 - c), device_id_type=MESH)
        remote.start()
        local.start()
        pltpu.make_async_remote_copy(src_ref=r_ref, dst_ref=o_ref.at[1 - c], send_sem=ssem, recv_sem=rsem,
                                     device_id=(x, y, 1 - c), device_id_type=MESH).wait_recv()
        remote.wait_send()
        local.wait()

    return pl.pallas_call(
        body, out_shape=_sds((2, A, B), r.dtype), in_specs=[HBM_SPEC], out_specs=HBM_SPEC, name=name,
        scratch_shapes=[pltpu.SemaphoreType.DMA, pltpu.SemaphoreType.DMA, pltpu.SemaphoreType.DMA],
    )(r)


def reduce_scatter(g, tag):
    _, _, A, B = g.shape
    own, got = rs_sibling(g, "rs_sibling_" + tag)
    h = rowwise("rs_add2_" + tag, lambda a, b: _f(a) + _f(b),
                [own.reshape(N_CHIPS * A, B), got.reshape(N_CHIPS * A, B)], [], [(B, BF)])
    own2, got2 = rs_chips(h.reshape(N_CHIPS, A, B), "rs_chips_" + tag)
    r = rowwise("rs_add4_" + tag, lambda a, b, c, d: ((_f(a) + _f(b)) + _f(c)) + _f(d),
                [own2, ("lead", got2, 0), ("lead", got2, 1), ("lead", got2, 2)], [], [(B, F32)])
    return rs_share(r, "rs_share_" + tag)


def _rms_parts(x):
    rstd = lax.rsqrt(jnp.mean(x * x, axis=-1, keepdims=True) + EPS)
    return x * rstd, rstd


def rms_fwd(x, g, name):
    return rowwise(name, lambda xv, gv: _rms_parts(xv)[0] * gv, [x], [g], [(x.shape[1], BF)])


def _rms_bwd_fn(x, dh, dres, g):
    xh, rstd = _rms_parts(x)
    dxh = _f(dh) * g
    dx = rstd * (dxh - xh * jnp.mean(dxh * xh, axis=-1, keepdims=True)) + dres
    return dx, jnp.sum(_f(dh) * xh, axis=0, keepdims=True)


def rms_bwd(x, g, dh, dres, name):
    D = x.shape[1]
    return rowwise(name, _rms_bwd_fn, [x, dh, dres], [g], [(D, F32)], [(1, D)])


def cast_bf(w2d, name):
    return rowwise(name, lambda v: v, [w2d], [], [(w2d.shape[1], BF)])


def ffn_fwd(xin, gain, WG, WU, WD, i, j):
    T, D = xin.shape
    fs = WG.shape[-1]
    F = N_CHIPS * fs
    tm, tk, tn = min(T, 1024), 512, 1024
    h = rms_fwd(xin, gain, "ffn_rms")

    def up(Wt, nm):
        return mm(nm, NN, (T // tm, N_CHIPS, D // tk), h, (tm, tk), _mk, Wt, (None, None, None, tk, fs),
                  lambda m, n, k: (n, i, j, k, 0), (tm, fs), _mn, _sds((T, F), BF))

    zg, zu = up(WG, "ffn_gate"), up(WU, "ffn_up")
    a = rowwise("ffn_swiglu", lambda g, u: _silu(_f(g)) * _f(u), [zg, zu], [], [(F, BF)])
    x2 = mm("ffn_down", NN, (T // tm, D // tn, N_CHIPS), a, (tm, fs), _mk, WD, (None, None, None, fs, tn),
            lambda m, n, k: (k, i, j, 0, n), (tm, tn), _mn, _sds((T, D), F32), scale=0.5, res=xin)
    return x2, (xin, h, zg, zu, a)


def _swiglu_bwd_fn(da, zg, zu):
    da, zg, zu = _f(da), _f(zg), _f(zu)
    return da * zu * _dsilu(zg), da * _silu(zg)


def ffn_bwd(dx2, saved, gain, WG, WU, WD, i, j, G_WG, G_WU, G_WD):
    xin, h, zg, zu, a = saved
    T, D = xin.shape
    fs = WG.shape[-1]
    F = N_CHIPS * fs
    tm, tk, tn, tkk = min(T, 1024), 512, 1024, min(T, 1024)
    da = mm("ffn_da", NT, (T // tm, N_CHIPS, D // tk), dx2, (tm, tk), _mk, WD, (None, None, None, fs, tk),
            lambda m, n, k: (n, i, j, 0, k), (tm, fs), _mn, _sds((T, F), BF), scale=0.5)
    G_WD = mm("ffn_dwd", TN, (N_CHIPS, D // tn, T // tkk), a, (tkk, fs), _km, dx2, (tkk, tn), _kn,
              (None, None, None, fs, tn), lambda m, n, k: (m, i, j, 0, n), None, scale=0.5, into=G_WD)
    dzg, dzu = rowwise("ffn_dswiglu", _swiglu_bwd_fn, [da, zg, zu], [], [(F, BF), (F, BF)])

    def dw(dz, G, nm):
        return mm(nm, TN, (D // tn, N_CHIPS, T // tkk), h, (tkk, tn), _km, dz, (tkk, fs), _kn,
                  (None, None, None, tn, fs), lambda m, n, k: (n, i, j, m, 0), None, into=G)

    G_WG, G_WU = dw(dzg, G_WG, "ffn_dwg"), dw(dzu, G_WU, "ffn_dwu")

    def dh_part(dz, Wt, nm, res):
        return mm(nm, NT, (T // tm, D // tn, N_CHIPS), dz, (tm, fs), _mk, Wt, (None, None, None, tn, fs),
                  lambda m, n, k: (k, i, j, n, 0), (tm, tn), _mn, _sds((T, D), F32), res=res)

    dh = dh_part(dzu, WU, "ffn_dh_u", dh_part(dzg, WG, "ffn_dh_g", None))
    dx, dgain = rms_bwd(xin, gain, dh, dx2, "ffn_rms_bwd")
    return dx, dgain, G_WG, G_WU, G_WD


def _gla_chunk(q_ref, k_ref, v_ref, u_ref, b_ref, rows):
    r = lax.broadcasted_iota(jnp.int32, (CHUNK, CHUNK), 0)
    c = lax.broadcasted_iota(jnp.int32, (CHUNK, CHUNK), 1)
    causal = c <= r
    u = u_ref[rows, :] + b_ref[...]
    g = (jnp.minimum(u, 0.0) - jnp.log(1.0 + jnp.exp(-jnp.abs(u)))) * (1.0 / GLA_TAU)
    b = _dot(causal.astype(F32), g, NN, HI)
    last = lax.broadcasted_iota(jnp.int32, (CHUNK, HK), 0) == CHUNK - 1
    blast = jnp.sum(jnp.where(last, b, 0.0), axis=0, keepdims=True)
    eb = jnp.exp(b)
    qb = q_ref[rows, :] * (HK ** -0.5) * eb
    k = k_ref[rows, :]
    kb = k * jnp.exp(-b)
    kl = k * jnp.exp(blast - b)
    A = jnp.where(causal, _bdot(qb, kb, NT), 0.0)
    return causal, u, b, blast, eb, qb, kb, kl, A


def _gla_in_specs(T):
    return [pl.BlockSpec((T, HK), lambda h: (0, h)), pl.BlockSpec((T, HK), lambda h: (0, GLA_H + h)),
            pl.BlockSpec((T, HV), lambda h: (0, GLA_H + h)), pl.BlockSpec((T, HK), lambda h: (0, h)),
            pl.BlockSpec((1, HK), lambda h: (0, h))]


def gla_fwd(zmain, ug, gate_b):
    T = zmain.shape[0]
    nC = T // CHUNK

    def body(q_ref, k_ref, v_ref, u_ref, b_ref, o_ref, s_ref, st_ref):
        st_ref[...] = jnp.zeros_like(st_ref)

        def step(n, carry):
            rows = pl.ds(pl.multiple_of(n * CHUNK, CHUNK), CHUNK)
            _, _, _, blast, _, qb, _, kl, A = _gla_chunk(q_ref, k_ref, v_ref, u_ref, b_ref, rows)
            v = v_ref[rows, :]
            ST = st_ref[...]
            s_ref[n] = ST
            o_ref[rows, :] = _bdot(qb, ST, NT) + _bdot(A, v, NN)
            st_ref[...] = ST * jnp.exp(blast) + _bdot(v, kl, TN)
            return carry

        lax.fori_loop(0, nC, step, 0)

    return pl.pallas_call(
        body, grid=(GLA_H,), in_specs=_gla_in_specs(T),
        out_specs=[pl.BlockSpec((T, HV), lambda h: (0, h)), pl.BlockSpec((nC, None, HV, HK), lambda h: (0, h, 0, 0))],
        out_shape=[_sds((T, GLA_H * HV), F32), _sds((nC, GLA_H, HV, HK), F32)],
        scratch_shapes=[pltpu.VMEM((HV, HK), F32)], name="gla_fwd", compiler_params=_cp("parallel"),
    )(zmain, zmain, zmain, ug, gate_b)


def gla_bwd(zmain, ug, gate_b, S, do):
    T = zmain.shape[0]
    nC = T // CHUNK

    def body(q_ref, k_ref, v_ref, u_ref, b_ref, s_ref, do_ref, dq_ref, dk_ref, dv_ref, du_ref, dgb_ref, dst_ref):
        dst_ref[...] = jnp.zeros_like(dst_ref)
        dgb_ref[...] = jnp.zeros_like(dgb_ref)

        def step(it, carry):
            n = nC - 1 - it
            rows = pl.ds(pl.multiple_of(n * CHUNK, CHUNK), CHUNK)
            causal, u, b, blast, eb, qb, kb, kl, A = _gla_chunk(q_ref, k_ref, v_ref, u_ref, b_ref, rows)
            v = v_ref[rows, :]
            dout = do_ref[rows, :]
            ST = s_ref[n]
            dST = dst_ref[...]
            elast = jnp.exp(blast)
            dA = jnp.where(causal, _bdot(dout, v, NT), 0.0)
            dv_ref[rows, :] = (_bdot(A, dout, TN) + _bdot(kl, dST, NT)).astype(dv_ref.dtype)
            dqb = _bdot(dout, ST, NN) + _bdot(dA, kb, NN)
            dkb = _bdot(dA, qb, TN)
            dkl = _bdot(v, dST, NN)
            ddec = jnp.sum(ST * dST, axis=0, keepdims=True)
            dst_ref[...] = dST * elast + _bdot(dout, qb, TN)
            dq_ref[rows, :] = (dqb * eb * (HK ** -0.5)).astype(dq_ref.dtype)
            dk_ref[rows, :] = (dkb * jnp.exp(-b) + dkl * jnp.exp(blast - b)).astype(dk_ref.dtype)
            db = dqb * qb - dkb * kb - dkl * kl
            dbl = jnp.sum(dkl * kl, axis=0, keepdims=True) + elast * ddec
            dg = _dot(jnp.logical_not(causal).astype(F32) + jnp.where(
                lax.broadcasted_iota(jnp.int32, (CHUNK, CHUNK), 0) == lax.broadcasted_iota(jnp.int32, (CHUNK, CHUNK), 1),
                1.0, 0.0), db, NN, HI) + dbl
            du = dg * (1.0 / GLA_TAU) / (1.0 + jnp.exp(u))
            du_ref[rows, :] = du
            dgb_ref[...] += jnp.sum(du, axis=0, keepdims=True)
            return carry

        lax.fori_loop(0, nC, step, 0)

    specs = _gla_in_specs(T) + [pl.BlockSpec((nC, None, HV, HK), lambda h: (0, h, 0, 0)),
                                pl.BlockSpec((T, HV), lambda h: (0, h))]
    return pl.pallas_call(
        body, grid=(GLA_H,), in_specs=specs,
        out_specs=[pl.BlockSpec((T, HK), lambda h: (0, h)), pl.BlockSpec((T, HK), lambda h: (0, h)),
                   pl.BlockSpec((T, HV), lambda h: (0, h)), pl.BlockSpec((T, HK), lambda h: (0, h)),
                   pl.BlockSpec((1, HK), lambda h: (0, h))],
        out_shape=[_sds((T, GLA_H * HK), BF), _sds((T, GLA_H * HK), BF), _sds((T, GLA_H * HV), BF),
                   _sds((T, GLA_H * HK), F32), _sds((1, GLA_H * HK), F32)],
        scratch_shapes=[pltpu.VMEM((HV, HK), F32)], name="gla_bwd", compiler_params=_cp("parallel"),
    )(zmain, zmain, zmain, ug, gate_b, S, do)


def _gla_post_fn(o, r, g):
    outs = []
    for h in range(GLA_H):
        on = _rms_parts(o[:, h * HV:(h + 1) * HV])[0] * g
        outs.append(on * _silu(r[:, h * HV:(h + 1) * HV]))
    return jnp.concatenate(outs, axis=1)


def _gla_post_bwd_fn(da, o, r, g):
    dos, drs = [], []
    dg = jnp.zeros((1, HV), F32)
    for h in range(GLA_H):
        sl = slice(h * HV, (h + 1) * HV)
        xh, rstd = _rms_parts(o[:, sl])
        drs.append(da[:, sl] * xh * g * _dsilu(r[:, sl]))
        don = da[:, sl] * _silu(r[:, sl])
        dg = dg + jnp.sum(don * xh, axis=0, keepdims=True)
        dxh = don * g
        dos.append(rstd * (dxh - xh * jnp.mean(dxh * xh, axis=-1, keepdims=True)))
    return jnp.concatenate(dos, axis=1), jnp.concatenate(drs, axis=1), dg


def conv_fwd(u, dw, dwb):
    T, C = u.shape
    TB = min(T, 256)

    def body(u_ref, w_ref, b_ref, y_ref, pad_ref):
        pad_ref[0:CONV_PAD, :] = jnp.zeros((CONV_PAD, LANES), F32)
        pad_ref[CONV_PAD:CONV_PAD + T, :] = u_ref[...]
        off = CONV_PAD - (CONV_W - 1)
        for t0 in range(0, T, TB):
            acc = jnp.zeros((TB, LANES), F32) + b_ref[...]
            for j in range(CONV_W):
                acc = acc + w_ref[j:j + 1, :] * pad_ref[t0 + off + j:t0 + off + j + TB, :]
            y_ref[t0:t0 + TB, :] = acc

    col = lambda i: (0, i)
    return pl.pallas_call(
        body, grid=(C // LANES,),
        in_specs=[pl.BlockSpec((T, LANES), col), pl.BlockSpec((CONV_W, LANES), col), pl.BlockSpec((1, LANES), col)],
        out_specs=pl.BlockSpec((T, LANES), col), out_shape=_sds((T, C), F32),
        scratch_shapes=[pltpu.VMEM((T + CONV_PAD, LANES), F32)], name="conv_fwd", compiler_params=_cp("parallel"),
    )(u, dw, dwb)


def conv_bwd(dy, u, dw):
    T, C = u.shape
    TB = min(T, 256)

    def body(dy_ref, u_ref, w_ref, du_ref, dw_ref, db_ref, upad, dypad):
        upad[0:CONV_PAD, :] = jnp.zeros((CONV_PAD, LANES), F32)
        upad[CONV_PAD:CONV_PAD + T, :] = u_ref[...]
        dypad[0:T, :] = dy_ref[...]
        dypad[T:T + CONV_PAD, :] = jnp.zeros((CONV_PAD, LANES), F32)
        off = CONV_PAD - (CONV_W - 1)
        for t0 in range(0, T, TB):
            acc = jnp.zeros((TB, LANES), F32)
            for j in range(CONV_W):
                s = t0 + (CONV_W - 1) - j
                acc = acc + w_ref[j:j + 1, :] * dypad[s:s + TB, :]
            du_ref[t0:t0 + TB, :] = acc
        for j in range(CONV_W):
            acc = jnp.zeros((TB, LANES), F32)
            for t0 in range(0, T, TB):
                acc = acc + dy_ref[t0:t0 + TB, :] * upad[t0 + off + j:t0 + off + j + TB, :]
            dw_ref[j:j + 1, :] = jnp.sum(acc, axis=0, keepdims=True)
        db_ref[...] = jnp.sum(dy_ref[...], axis=0, keepdims=True)

    col = lambda i: (0, i)
    return pl.pallas_call(
        body, grid=(C // LANES,),
        in_specs=[pl.BlockSpec((T, LANES), col), pl.BlockSpec((T, LANES), col), pl.BlockSpec((CONV_W, LANES), col)],
        out_specs=[pl.BlockSpec((T, LANES), col), pl.BlockSpec((CONV_W, LANES), col), pl.BlockSpec((1, LANES), col)],
        out_shape=[_sds((T, C), F32), _sds((CONV_W, C), F32), _sds((1, C), F32)],
        scratch_shapes=[pltpu.VMEM((T + CONV_PAD, LANES), F32), pltpu.VMEM((T + CONV_PAD, LANES), F32)],
        name="conv_bwd", compiler_params=_cp("parallel"),
    )(dy, u, dw)


def _ln_parts(x):
    mu = jnp.mean(x, axis=-1, keepdims=True)
    xc = x - mu
    rstd = lax.rsqrt(jnp.mean(xc * xc, axis=-1, keepdims=True) + EPS)
    return xc * rstd, rstd


def _ln_silu_fn(x, g, b):
    return _silu(_ln_parts(x)[0] * g + b)


def _ln_silu_bwd_fn(dbo, x, g, b):
    xh, rstd = _ln_parts(x)
    dy = dbo * _dsilu(xh * g + b)
    dyg = dy * g
    dx = rstd * (dyg - jnp.mean(dyg, axis=-1, keepdims=True) - xh * jnp.mean(dyg * xh, axis=-1, keepdims=True))
    return dx, jnp.sum(dy * xh, axis=0, keepdims=True), jnp.sum(dy, axis=0, keepdims=True)


def _glu_bwd_fn(du, ca, cb):
    s = _sigmoid(cb)
    return du * s, du * ca * s * (1.0 - s)


def _onehot_rel(t):
    w = lax.broadcasted_iota(jnp.int32, (REL_PAD, KW), 1)
    i = lax.broadcasted_iota(jnp.int32, (REL_PAD, KW), 0)
    idx = jnp.clip(t - w + PADK, -REL_CLIP, REL_CLIP) + REL_CLIP
    return (i == idx).astype(F32)


def relbias_tile(rbp):
    def body(rb_ref, o_ref):
        def step(t, carry):
            w = lax.broadcasted_iota(jnp.int32, (ATT_H, KW), 1)
            wc = lax.shift_right_logical(w, 6)
            ok = jnp.logical_and(wc >= t // CHUNK, wc <= t // CHUNK + LEFT_CHUNKS)
            o_ref[t] = jnp.where(ok, _dot(rb_ref[...], _onehot_rel(t), NN, HI), NEG)
            return carry

        lax.fori_loop(0, QB, step, 0)

    return pl.pallas_call(body, out_shape=_sds((QB, ATT_H, KW), F32), name="relbias_tile",
                          compiler_params=_cp())(rbp)


def relbias_reduce(dbm_t):
    def body(d_ref, o_ref):
        o_ref[...] = jnp.zeros_like(o_ref)

        def step(t, carry):
            o_ref[...] += _dot(d_ref[t], _onehot_rel(t), NT, HI)
            return carry

        lax.fori_loop(0, QB, step, 0)

    return pl.pallas_call(body, out_shape=_sds((ATT_H, REL_PAD), F32), name="relbias_reduce",
                          compiler_params=_cp())(dbm_t)


def _att_scores(q_ref, kp_ref, bm_ref, i):
    q0 = pl.multiple_of(i * QB, QB)
    kw = kp_ref[pl.ds(q0, KW), :]
    s = _bdot(q_ref[...], kw, NT) * (HD ** -0.5) + bm_ref[...]
    w = lax.broadcasted_iota(jnp.int32, (QB, KW), 1)
    return jnp.where(w + q0 >= PADK, s, NEG), kw, q0


def attn_fwd(qkv, kvp, bm):
    T = qkv.shape[0]
    D = ATT_H * HD

    def body(q_ref, kp_ref, vp_ref, bm_ref, o_ref, lse_ref):
        s, _, q0 = _att_scores(q_ref, kp_ref, bm_ref, pl.program_id(1))
        m = jnp.max(s, axis=-1, keepdims=True)
        e = jnp.exp(s - m)
        l = jnp.sum(e, axis=-1, keepdims=True)
        o_ref[...] = _bdot(e * (1.0 / l), vp_ref[pl.ds(q0, KW), :], NN).astype(o_ref.dtype)
        lse_ref[...] = m + jnp.log(l)

    return pl.pallas_call(
        body, grid=(ATT_H, T // QB),
        in_specs=[pl.BlockSpec((QB, HD), lambda h, i: (i, h)), pl.BlockSpec((T + PADK, HD), lambda h, i: (0, h)),
                  pl.BlockSpec((T + PADK, HD), lambda h, i: (0, ATT_H + h)),
                  pl.BlockSpec((None, QB, KW), lambda h, i: (h, 0, 0))],
        out_specs=[pl.BlockSpec((QB, HD), lambda h, i: (i, h)), pl.BlockSpec((None, QB, 1), lambda h, i: (h, i, 0))],
        out_shape=[_sds((T, D), BF), _sds((ATT_H, T, 1), F32)], name="attn_fwd",
        compiler_params=_cp("parallel", "arbitrary"),
    )(qkv, kvp, kvp, bm)


def attn_bwd(qkv, kvp, bm, o, lse, do):
    T = qkv.shape[0]
    D = ATT_H * HD

    def body(q_ref, kp_ref, vp_ref, bm_ref, o_ref, lse_ref, do_ref, dq_ref, dkp_ref, dvp_ref, dbm_ref):
        i = pl.program_id(1)

        @pl.when(i == 0)
        def _():
            dkp_ref[...] = jnp.zeros_like(dkp_ref)
            dvp_ref[...] = jnp.zeros_like(dvp_ref)
            dbm_ref[...] = jnp.zeros_like(dbm_ref)

        s, kw, q0 = _att_scores(q_ref, kp_ref, bm_ref, i)
        p = jnp.exp(s - lse_ref[...])
        dout = do_ref[...]
        dp = _bdot(dout, vp_ref[pl.ds(q0, KW), :], NT)
        delta = jnp.sum(_f(dout) * _f(o_ref[...]), axis=-1, keepdims=True)
        ds = p * (dp - delta)
        dq_ref[...] = (_bdot(ds, kw, NN) * (HD ** -0.5)).astype(dq_ref.dtype)
        dkp_ref[pl.ds(q0, KW), :] += _bdot(ds, q_ref[...], TN) * (HD ** -0.5)
        dvp_ref[pl.ds(q0, KW), :] += _bdot(p, dout, TN)
        dbm_ref[...] += ds

    qspec = pl.BlockSpec((QB, HD), lambda h, i: (i, h))
    kspec = pl.BlockSpec((T + PADK, HD), lambda h, i: (0, h))
    bspec = pl.BlockSpec((None, QB, KW), lambda h, i: (h, 0, 0))
    return pl.pallas_call(
        body, grid=(ATT_H, T // QB),
        in_specs=[qspec, kspec, pl.BlockSpec((T + PADK, HD), lambda h, i: (0, ATT_H + h)), bspec, qspec,
                  pl.BlockSpec((None, QB, 1), lambda h, i: (h, i, 0)), qspec],
        out_specs=[qspec, kspec, kspec, bspec],
        out_shape=[_sds((T, D), BF), _sds((T + PADK, D), F32), _sds((T + PADK, D), F32), _sds((ATT_H, QB, KW), F32)],
        name="attn_bwd", compiler_params=_cp("parallel", "arbitrary"),
    )(qkv, kvp, kvp, bm, o, lse, do)


def _final_fn(x, tgt, g):
    D = x.shape[1]
    xh, rstd = _rms_parts(x)
    diff = xh * g - tgt
    dy = diff * (1.0 / D)
    dxh = dy * g
    dx = rstd * (dxh - xh * jnp.mean(dxh * xh, axis=-1, keepdims=True))
    loss = jnp.sum(jnp.sum(diff * diff, axis=-1, keepdims=True), axis=0, keepdims=True) * (0.5 / D)
    return dx, jnp.sum(dy * xh, axis=0, keepdims=True), jnp.broadcast_to(loss, (1, LANES))


def _adamw_fn(w, g, m, v):
    m = ADAM_B1 * m + (1.0 - ADAM_B1) * g
    v = ADAM_B2 * v + (1.0 - ADAM_B2) * (g * g)
    m_hat = m / (1.0 - ADAM_B1 ** ADAM_STEP)
    v_hat = v / (1.0 - ADAM_B2 ** ADAM_STEP)
    delta = -ADAM_LR * (m_hat / (jnp.sqrt(v_hat) + ADAM_EPS) + ADAM_WD * w)
    return g, delta, m, v


def adamw(w, g, m, v, name):
    shape = w.shape
    C = shape[-1]
    R = w.size // C
    tb = R
    for cand in (256, 128, 64, 32, 16, 8):
        if R > cand and R % cand == 0:
            tb = cand
            break
    outs = rowwise(name, _adamw_fn, [t.reshape(R, C) for t in (w, g, m, v)], [], [(C, F32)] * 4, tb=tb)
    return tuple(t.reshape(shape) for t in outs)


WEIGHTS = ['ffn_norm', 'ffn_w_gate', 'ffn_w_up', 'ffn_w_down', 'mix_norm', 'ab_w_in', 'gla_gate_w', 'gla_gate_b',
           'gla_norm_g', 'conv_dw', 'conv_dw_b', 'conv_ln_g', 'conv_ln_b', 'ab_w_out', 'att_w_qkv', 'att_rel_bias',
           'att_w_o', 'pl_norm', 'pl_w_gate', 'pl_w_proj', 'final_norm']
BIG = ['ffn_w_gate', 'ffn_w_up', 'ffn_w_down', 'ab_w_in', 'ab_w_out', 'att_w_qkv', 'att_w_o', 'pl_w_gate', 'pl_w_proj']


def _canon(shape):
    lead = 1
    for d in shape[:-2]:
        lead *= d
    rows = lead * shape[-2]
    return (2, rows // 2, shape[-1])


def _pack(parts, rows):
    flat = jnp.concatenate([p.reshape(-1) for p in parts])
    return jnp.pad(flat, (0, rows * LANES - flat.shape[0])).reshape(rows, LANES)


def _unpack(flat, shapes):
    out, pos = [], 0
    for s in shapes:
        n = 1
        for d in s:
            n *= d
        out.append(flat[pos:pos + n].reshape(s))
        pos += n
    return out


def _rows_for(shapes):
    n = sum(math.prod(s) for s in shapes)
    return -(-n // (8 * LANES)) * 8


def kernel(x, p, ffn_norm, ffn_w_gate, ffn_w_up, ffn_w_down, mix_norm, ab_w_in, gla_gate_w, gla_gate_b, gla_norm_g, conv_dw, conv_dw_b, conv_ln_g, conv_ln_b, ab_w_out, att_w_qkv, att_rel_bias, att_w_o, pl_norm, pl_w_gate, pl_w_proj, final_norm, loss_target, m_ffn_norm, m_ffn_w_gate, m_ffn_w_up, m_ffn_w_down, m_mix_norm, m_ab_w_in, m_gla_gate_w, m_gla_gate_b, m_gla_norm_g, m_conv_dw, m_conv_dw_b, m_conv_ln_g, m_conv_ln_b, m_ab_w_out, m_att_w_qkv, m_att_rel_bias, m_att_w_o, m_pl_norm, m_pl_w_gate, m_pl_w_proj, m_final_norm, v_ffn_norm, v_ffn_w_gate, v_ffn_w_up, v_ffn_w_down, v_mix_norm, v_ab_w_in, v_gla_gate_w, v_gla_gate_b, v_gla_norm_g, v_conv_dw, v_conv_dw_b, v_conv_ln_g, v_conv_ln_b, v_ab_w_out, v_att_w_qkv, v_att_rel_bias, v_att_w_o, v_pl_norm, v_pl_w_gate, v_pl_w_proj, v_final_norm):
    env = dict(locals())
    W = {n: env[n] for n in WEIGHTS}
    M = {n: env["m_" + n] for n in WEIGHTS}
    V = {n: env["v_" + n] for n in WEIGHTS}
    xc, yc_, cc = lax.axis_index("x"), lax.axis_index("y"), lax.axis_index("c")

    x0 = x[0]
    tgt = loss_target[0]
    T, D = x0.shape
    fs = ffn_w_gate.shape[-1]
    ws = ab_w_in.shape[-1]
    AB_IN = N_CHIPS * ws
    n_main = AB_IN - 16
    gz0 = 2 * GLA_H * HK + 2 * GLA_H * HV

    full = {}
    for n in BIG:
        cs = _canon(W[n].shape)
        shard = cast_bf(W[n].reshape(cs[0] * cs[1], cs[2]), "cast_" + n).reshape(cs)
        full[n] = gather4(shard, "gather_" + n).reshape((N_CHIPS,) + W[n].shape)
    WG, WU, WD = full['ffn_w_gate'], full['ffn_w_up'], full['ffn_w_down']
    w_in = jnp.transpose(full['ab_w_in'][:, 0], (1, 0, 2)).reshape(D, AB_IN)
    w_main = jnp.concatenate([w_in[:, :gz0], w_in[:, gz0 + 16:]], axis=1)
    w_gz = jnp.pad(w_in[:, gz0:gz0 + 16], ((0, 0), (0, LANES - 16)))
    w_out = full['ab_w_out'].reshape(D, D)
    w_qkv = full['att_w_qkv'][:, 0]
    w_o = full['att_w_o'].reshape(D, D)
    w_plg = full['pl_w_gate']
    w_plp = full['pl_w_proj']
    qs = w_qkv.shape[-1]

    small_sharded = [ffn_norm, gla_gate_w, conv_dw]
    rows_s = _rows_for([t.shape for t in small_sharded])
    got = allgather8(_pack(small_sharded, rows_s), "gather_small").reshape(N_CHIPS, 2, rows_s * LANES)[:, 0]
    per_chip = [_unpack(got[k], [t.shape for t in small_sharded]) for k in range(N_CHIPS)]
    ffn_norm_f, gate_w_f, conv_dw_f = [jnp.concatenate([per_chip[k][t] for k in range(N_CHIPS)], axis=-1)
                                       for t in range(3)]
    gate_w_p = jnp.pad(gate_w_f[0], ((0, LANES - 16), (0, 0)))
    conv_w = conv_dw_f[0]
    rb_p = jnp.pad(att_rel_bias[0], ((0, 0), (0, REL_PAD - att_rel_bias.shape[-1])))

    G = {n: jnp.zeros((N_CHIPS,) + W[n].shape, BF) for n in ('ffn_w_gate', 'ffn_w_up', 'ffn_w_down', 'pl_w_gate', 'pl_w_proj')}
    small_g = {}

    saved = {}
    xs = x0
    xs, saved['f00'] = ffn_fwd(xs, ffn_norm_f[0, 0][None], WG, WU, WD, 0, 0)

    def mixer0_fwd(xin):
        h = rms_fwd(xin, mix_norm[0][None], "mix0_rms")
        zmain = mm_nn("ab_in", h, w_main, F32, 1024)
        gzp = mm_nn("ab_gz", h, w_gz, F32, LANES)
        ug = mm_nn("gla_gate", gzp, gate_w_p, F32, GLA_H * HK)
        o, S = gla_fwd(zmain, ug, gla_gate_b)
        a_out = rowwise("gla_post", _gla_post_fn, [o, ("cols", zmain, GLA_H * HV, 2)], [gla_norm_g], [(GLA_H * HV, BF)])
        u = rowwise("conv_glu", lambda a, b: a * _sigmoid(b), [("cols", zmain, 1024, 3), ("cols", zmain, 1024, 4)], [],
                    [(1024, F32)])
        yc = conv_fwd(u, conv_w, conv_dw_b)
        b_out = rowwise("conv_ln", _ln_silu_fn, [yc], [conv_ln_g, conv_ln_b], [(1024, BF)])
        cat = jnp.concatenate([a_out, b_out], axis=1)
        x2 = mm_nn("ab_out", cat, w_out, F32, 1024, res=xin)
        return x2, (xin, h, zmain, gzp, ug, o, S, u, yc, cat)

    xs, saved['m0'] = mixer0_fwd(xs)
    xs, saved['f01'] = ffn_fwd(xs, ffn_norm_f[0, 1][None], WG, WU, WD, 0, 1)

    def pl_fwd(xin, i):
        h = rms_fwd(xin, pl_norm[i][None], "pl_rms")
        tm, tn = min(T, 1024), 1024
        ks = D // N_CHIPS
        u = mm("pl_gate", NN, (T // tm, D // tn, N_CHIPS), h, (tm, ks), _mk, w_plg, (None, None, ks, tn),
               lambda m, n, k: (k, i, 0, n), (tm, tn), _mn, _sds((T, D), F32))
        e = mm("pl_proj", NN, (T // tm, N_CHIPS, 1), p[i, 0], (tm, p.shape[-1]), _mk, w_plp,
               (None, None, p.shape[-1], ks), lambda m, n, k: (n, i, 0, 0), (tm, ks), _mn, _sds((T, D), F32))
        x2 = rowwise("pl_mix", lambda xv, uv, ev: xv + _sigmoid(uv) * ev, [xin, u, e], [], [(D, F32)])
        return x2, (xin, h, u, e)

    xs, saved['p0'] = pl_fwd(xs, 0)
    xs, saved['f10'] = ffn_fwd(xs, ffn_norm_f[1, 0][None], WG, WU, WD, 1, 0)

    bm_t = relbias_tile(rb_p)
    bm = jnp.transpose(bm_t, (1, 0, 2))

    def mixer1_fwd(xin):
        h = rms_fwd(xin, mix_norm[1][None], "mix1_rms")
        tm, tk, tn = min(T, 1024), 512, 512
        per = qs // tn
        qkv = mm("att_qkv", NN, (T // tm, 3 * D // tn, D // tk), h, (tm, tk), _mk, w_qkv, (None, tk, tn),
                 lambda m, n, k: (n // per, k, n % per), (tm, tn), _mn, _sds((T, 3 * D), BF))
        kvp = jnp.pad(qkv[:, D:], ((PADK, 0), (0, 0)))
        o, lse = attn_fwd(qkv, kvp, bm)
        x2 = mm_nn("att_o", o, w_o, F32, 1024, res=xin)
        return x2, (xin, h, qkv, kvp, o, lse)

    xs, saved['m1'] = mixer1_fwd(xs)
    xs, saved['f11'] = ffn_fwd(xs, ffn_norm_f[1, 1][None], WG, WU, WD, 1, 1)
    xs, saved['p1'] = pl_fwd(xs, 1)

    dx, small_g['final_norm'], loss_acc = rowwise("loss_head", _final_fn, [xs, tgt], [final_norm[None]], [(D, F32)],
                                                  [(1, D), (1, LANES)])
    loss = lax.psum(loss_acc[0, 0], ("x", "y", "c"))

    def pl_bwd(dx2, sv, i):
        xin, h, u, e = sv
        tm, tn = min(T, 1024), 1024
        ks = D // N_CHIPS

        def fn(d, uv, ev):
            s = _sigmoid(uv)
            return d * s, d * ev * s * (1.0 - s)

        de, du = rowwise("pl_mix_bwd", fn, [dx2, u, e], [], [(D, BF), (D, BF)])
        G['pl_w_proj'] = mm("pl_dproj", TN, (1, N_CHIPS, T // tm), p[i, 0], (tm, p.shape[-1]), _km, de, (tm, ks), _kn,
                            (None, None, p.shape[-1], ks), lambda m, n, k: (n, i, 0, 0), None, into=G['pl_w_proj'])
        G['pl_w_gate'] = mm("pl_dgate", TN, (N_CHIPS, D // tn, T // tm), h, (tm, ks), _km, du, (tm, tn), _kn,
                            (None, None, ks, tn), lambda m, n, k: (m, i, 0, n), None, into=G['pl_w_gate'])
        dh = mm("pl_dh", NT, (T // tm, N_CHIPS, D // 512), du, (tm, 512), _mk, w_plg, (None, None, ks, 512),
                lambda m, n, k: (n, i, 0, k), (tm, ks), _mn, _sds((T, D), F32))
        return rms_bwd(xin, pl_norm[i][None], dh, dx2, "pl_rms_bwd")

    def ffn_b(dx2, key, i, j):
        dxn, dgain, G['ffn_w_gate'], G['ffn_w_up'], G['ffn_w_down'] = ffn_bwd(
            dx2, saved[key], ffn_norm_f[i, j][None], WG, WU, WD, i, j, G['ffn_w_gate'], G['ffn_w_up'], G['ffn_w_down'])
        return dxn, dgain

    def mixer1_bwd(dx2, sv):
        xin, h, qkv, kvp, o, lse = sv
        do = mm_nt("att_do", dx2, w_o, BF, 1024)
        dwo = mm_tn("att_dwo", o, dx2, BF, 1024, 1024)
        dq, dkp, dvp, dbm = attn_bwd(qkv, kvp, bm, o, lse, do)
        dqkv = jnp.concatenate([dq, dkp[PADK:].astype(BF), dvp[PADK:].astype(BF)], axis=1)
        tm, tn, tkk = min(T, 1024), 512, min(T, 1024)
        per = qs // tn
        dwqkv = mm("att_dwqkv", TN, (D // 1024, 3 * D // tn, T // tkk), h, (tkk, 1024), _km, dqkv, (tkk, tn), _kn,
                   (None, 1024, tn), lambda m, n, k: (n // per, m, n % per), _sds((N_CHIPS, D, qs), BF))
        dh = mm("att_dh", NT, (T // tm, D // 1024, 3 * D // tn), dqkv, (tm, tn), _mk, w_qkv, (None, 1024, tn),
                lambda m, n, k: (k // per, n, k % per), (tm, 1024), _mn, _sds((T, D), F32))
        dxn, dgain = rms_bwd(xin, mix_norm[1][None], dh, dx2, "mix1_rms_bwd")
        drb = relbias_reduce(jnp.transpose(dbm, (1, 0, 2)))
        return dxn, dgain, dwo, dwqkv, drb

    def mixer0_bwd(dx2, sv):
        xin, h, zmain, gzp, ug, o, S, u, yc, cat = sv
        dcat = mm_nt("ab_dcat", dx2, w_out, F32, 1024)
        dwout = mm_tn("ab_dwout", cat, dx2, BF, 1024, 1024)
        do, dr, dgn = rowwise("gla_post_bwd", _gla_post_bwd_fn,
                              [("cols", dcat, GLA_H * HV, 0), o, ("cols", zmain, GLA_H * HV, 2)], [gla_norm_g],
                              [(GLA_H * HV, F32), (GLA_H * HV, BF)], [(1, HV)])
        dyc, dlg, dlb = rowwise("conv_ln_bwd", _ln_silu_bwd_fn, [("cols", dcat, 1024, 1), yc], [conv_ln_g, conv_ln_b],
                                [(1024, F32)], [(1, 1024), (1, 1024)])
        du, ddw, ddwb = conv_bwd(dyc, u, conv_w)
        dca, dcb = rowwise("conv_glu_bwd", _glu_bwd_fn, [du, ("cols", zmain, 1024, 3), ("cols", zmain, 1024, 4)], [],
                           [(1024, BF), (1024, BF)])
        dq, dk, dv, dug, dgb = gla_bwd(zmain, ug, gla_gate_b, S, do)
        dgw = mm_tn("gla_dgate_w", gzp, dug, F32, LANES, GLA_H * HK)
        dgzp = mm_nt("gla_dgz", dug, gate_w_p, F32, LANES)
        dzm = jnp.concatenate([dq, dk, dv, dr, dca, dcb], axis=1)
        dwmain = mm_tn("ab_dwmain", h, dzm, BF, 1024, 1024)
        dwgz = mm_tn("ab_dwgz", h, dgzp, BF, 1024, LANES)
        dh = mm_nt("ab_dh_gz", dgzp, w_gz, F32, 1024, res=mm_nt("ab_dh", dzm, w_main, F32, 1024))
        dxn, dgain = rms_bwd(xin, mix_norm[0][None], dh, dx2, "mix0_rms_bwd")
        dwin = jnp.concatenate([dwmain[:, :gz0], dwgz[:, :16], dwmain[:, gz0:]], axis=1)
        g_win = jnp.transpose(dwin.reshape(D, N_CHIPS, ws), (1, 0, 2))[:, None]
        return dxn, dgain, g_win, dwout, (dgn, dlg, dlb, ddw, ddwb, dgw[:16], dgb)

    dpl, dffn, dmix = [None, None], [[None, None], [None, None]], [None, None]
    dx, dpl[1] = pl_bwd(dx, saved['p1'], 1)
    dx, dffn[1][1] = ffn_b(dx, 'f11', 1, 1)
    dx, dmix[1], dwo, dwqkv, drb = mixer1_bwd(dx, saved['m1'])
    dx, dffn[1][0] = ffn_b(dx, 'f10', 1, 0)
    dx, dpl[0] = pl_bwd(dx, saved['p0'], 0)
    dx, dffn[0][1] = ffn_b(dx, 'f01', 0, 1)
    dx, dmix[0], g_win, dwout, (dgn, dlg, dlb, ddw, ddwb, dgw, dgb) = mixer0_bwd(dx, saved['m0'])
    dx, dffn[0][0] = ffn_b(dx, 'f00', 0, 0)
    grad_x = dx[None]

    G['ab_w_in'] = g_win
    G['ab_w_out'] = dwout.reshape((N_CHIPS,) + ab_w_out.shape)
    G['att_w_qkv'] = dwqkv[:, None]
    G['att_w_o'] = dwo.reshape((N_CHIPS,) + att_w_o.shape)

    grads = {}
    for n in BIG:
        cs = _canon(W[n].shape)
        grads[n] = reduce_scatter(G[n].reshape((N_CHIPS,) + cs), n).reshape(W[n].shape)

    small_g['ffn_norm'] = jnp.stack([jnp.stack([dffn[i][j][0] for j in range(2)]) for i in range(2)])
    small_g['mix_norm'] = jnp.concatenate(dmix, axis=0)
    small_g['gla_gate_w'] = dgw[None]
    small_g['gla_gate_b'] = dgb
    small_g['gla_norm_g'] = dgn
    small_g['conv_dw'] = ddw[None]
    small_g['conv_dw_b'] = ddwb
    small_g['conv_ln_g'] = dlg
    small_g['conv_ln_b'] = dlb
    small_g['att_rel_bias'] = drb[None, :, :att_rel_bias.shape[-1]]
    small_g['pl_norm'] = jnp.concatenate(dpl, axis=0)
    small_g['final_norm'] = small_g['final_norm'][0]
    small_names = [n for n in WEIGHTS if n not in BIG]
    small_shapes = [small_g[n].shape for n in small_names]
    rows_g = _rows_for(small_shapes)
    allp = allgather8(_pack([small_g[n] for n in small_names], rows_g), "gather_small_grads")
    summed = rowwise("sum_small_grads", lambda *v: (((v[0] + v[1]) + (v[2] + v[3])) + ((v[4] + v[5]) + (v[6] + v[7]))),
                     [("lead", allp.reshape(8, rows_g, LANES), d) for d in range(8)], [], [(LANES, F32)], tb=rows_g)
    for n, g in zip(small_names, _unpack(summed.reshape(-1), small_shapes)):
        grads[n] = g
    chip = 2 * xc + yc_
    for n, axis in (('ffn_norm', 2), ('gla_gate_w', 2), ('conv_dw', 2)):
        width = W[n].shape[axis]
        grads[n] = lax.dynamic_slice_in_dim(grads[n], chip * width, width, axis)

    outs = {n: adamw(W[n], grads[n], M[n], V[n], "adamw_" + n) for n in WEIGHTS}
    return (loss, grad_x, *[outs[n][0] for n in WEIGHTS], *[outs[n][1] for n in WEIGHTS],
            *[outs[n][2] for n in WEIGHTS], *[outs[n][3] for n in WEIGHTS])
```

```python
import math

import jax
import jax.numpy as jnp
from jax import lax
from jax.experimental import pallas as pl
from jax.experimental.pallas import tpu as pltpu

F32 = jnp.float32
BF = jnp.bfloat16
MESH = pl.DeviceIdType.MESH
HI = lax.Precision.HIGHEST
V7X_VMEM_LIMIT = 56 * 1024 * 1024
LANES = 128
EPS = 1e-6
NEG = -1e30

CHUNK = 64
LEFT_CHUNKS = 8
QB = 256
KW = QB + LEFT_CHUNKS * CHUNK
PADK = LEFT_CHUNKS * CHUNK
REL_CLIP = 128
REL_PAD = 384
ATT_H = 16
HD = 128
GLA_H = 4
HK = 128
HV = 256
GLA_TAU = 16.0
CONV_W = 31
CONV_PAD = 32
N_CHIPS = 4

ADAM_LR = 0.001
ADAM_B1 = 0.9
ADAM_B2 = 0.999
ADAM_EPS = 1e-08
ADAM_WD = 0.01
ADAM_STEP = 10

NN = ((1,), (0,))
NT = ((1,), (1,))
TN = ((0,), (0,))


def _dot(a, b, dims, prec=None):
    return lax.dot_general(a, b, (dims, ((), ())), preferred_element_type=F32, precision=prec)


def _bdot(a, b, dims):
    return _dot(a.astype(BF), b.astype(BF), dims)


def _cp(*sem):
    return pltpu.CompilerParams(dimension_semantics=sem if sem else None, vmem_limit_bytes=V7X_VMEM_LIMIT)


def _sds(shape, dtype):
    return jax.ShapeDtypeStruct(shape, dtype)


def _sigmoid(x):
    return 1.0 / (1.0 + jnp.exp(-x))


def _silu(x):
    return x * _sigmoid(x)


def _dsilu(x):
    s = _sigmoid(x)
    return s * (1.0 + x * (1.0 - s))


def _f(x):
    return x.astype(F32)


def _row_item(item, tb):
    if not isinstance(item, tuple):
        return item, (tb, item.shape[1]), lambda i, s: (i, 0)
    kind, arr = item[0], item[1]
    if kind == "cols":
        return arr, (tb, item[2]), lambda i, s, blk=item[3]: (i, blk)
    if kind == "leads":
        lead = tuple(item[2])
        return arr, (None,) * len(lead) + (tb, arr.shape[-1]), lambda i, s: lead + (i, 0)
    if kind == "dyn":
        return arr, (None, tb, arr.shape[-1]), lambda i, s, sel=item[2]: (s[sel], i, 0)
    if kind == "dyn4":
        nb = arr.shape[2] // tb
        return arr, (None, None, tb, arr.shape[-1]), lambda i, s, sel=item[2]: (i // nb, s[sel], i % nb, 0)
    raise ValueError(kind)


def _row_block(n, cap):
    for cand in range(min(cap, n) // 16 * 16, 0, -16):
        if n % cand == 0:
            return cand
    return n


def rowwise(name, fn, rows, bcast, outs, accs=(), tb=256, n_rows=None, sidx=None, row_period=None):
    if n_rows is None:
        first = rows[0][1] if isinstance(rows[0], tuple) else rows[0]
        n_rows = first.shape[-2]
    tb = _row_block(n_rows if row_period is None else row_period, tb)
    items = [_row_item(it, tb) for it in rows]
    n_r, n_b, n_o = len(rows), len(bcast), len(outs)
    pre = 0 if sidx is None else 1

    def wrap(f):
        return (lambda i: f(i, None)) if sidx is None else (lambda i, s: f(i, s))

    def body(*refs):
        refs = refs[pre:]
        r, b = refs[:n_r], refs[n_r:n_r + n_b]
        o, a = refs[n_r + n_b:n_r + n_b + n_o], refs[n_r + n_b + n_o:]
        res = fn(*[v[...] for v in r], *[v[...] for v in b])
        if not isinstance(res, tuple):
            res = (res,)
        for ref, val in zip(o, res[:n_o]):
            ref[...] = val.astype(ref.dtype)
        if a:
            @pl.when(pl.program_id(0) == 0)
            def _():
                for ref in a:
                    ref[...] = jnp.zeros_like(ref)
            for ref, val in zip(a, res[n_o:]):
                ref[...] += val

    in_specs = [pl.BlockSpec(bs, wrap(f)) for _, bs, f in items]
    in_specs += [pl.BlockSpec(v.shape, wrap(lambda i, s, nd=v.ndim: (0,) * nd)) for v in bcast]
    out_specs, out_shape = [], []
    for o in outs:
        if o[0] == "dyn":
            _, L, c, dt, sel = o
            out_specs.append(pl.BlockSpec((None, tb, c), wrap(lambda i, s, sel=sel: (s[sel], i, 0))))
            out_shape.append(_sds((L, n_rows, c), dt))
        else:
            c, dt = o
            out_specs.append(pl.BlockSpec((tb, c), wrap(lambda i, s: (i, 0))))
            out_shape.append(_sds((n_rows, c), dt))
    out_specs += [pl.BlockSpec(sh, wrap(lambda i, s: (0, 0))) for sh in accs]
    out_shape += [_sds(sh, F32) for sh in accs]
    operands = [a for a, _, _ in items] + list(bcast)
    grid = (n_rows // tb,)
    if sidx is None:
        res = pl.pallas_call(body, grid=grid, in_specs=in_specs, out_specs=out_specs, out_shape=out_shape, name=name,
                             compiler_params=_cp("arbitrary"))(*operands)
    else:
        spec = pltpu.PrefetchScalarGridSpec(num_scalar_prefetch=1, grid=grid, in_specs=in_specs, out_specs=out_specs)
        res = pl.pallas_call(body, grid_spec=spec, out_shape=out_shape, name=name,
                             compiler_params=_cp("arbitrary"))(sidx, *operands)
    return res[0] if len(res) == 1 else tuple(res)


def mm(name, dims, grid, a, a_bs, a_im, b, b_bs, b_im, o_bs, o_im, out, scale=1.0, res=None):
    nk = grid[2]
    acc_shape = tuple(d for d in o_bs if d is not None)

    def body(*refs):
        a_ref, b_ref = refs[0], refs[1]
        pos = 2
        res_ref = None
        if res is not None:
            res_ref = refs[pos]
            pos += 1
        o_ref = refs[pos]
        part = _bdot(a_ref[...], b_ref[...], dims)

        def finish(acc):
            v = acc * scale if scale != 1.0 else acc
            if res_ref is not None:
                v = v + _f(res_ref[...])
            o_ref[...] = v.astype(o_ref.dtype)

        if nk == 1:
            finish(part)
        else:
            acc_ref = refs[pos + 1]
            k = pl.program_id(2)

            @pl.when(k == 0)
            def _():
                acc_ref[...] = part

            @pl.when(k > 0)
            def _():
                acc_ref[...] += part

            @pl.when(k == nk - 1)
            def _():
                finish(acc_ref[...])

    operands = [a, b]
    in_specs = [pl.BlockSpec(a_bs, a_im), pl.BlockSpec(b_bs, b_im)]
    if res is not None:
        operands.append(res)
        in_specs.append(pl.BlockSpec(o_bs, o_im))
    scratch = [pltpu.VMEM(acc_shape, F32)] if nk > 1 else []
    return pl.pallas_call(body, grid=grid, in_specs=in_specs, out_specs=pl.BlockSpec(o_bs, o_im), out_shape=out,
                          scratch_shapes=scratch, name=name,
                          compiler_params=_cp("parallel", "parallel", "arbitrary"))(*operands)


def _mk(m, n, k):
    return (m, k)


def _mn(m, n, k):
    return (m, n)


def _km(m, n, k):
    return (k, m)


def _kn(m, n, k):
    return (k, n)


def _nk(m, n, k):
    return (n, k)


def mm_nn(name, a, b, out_dtype, tn, tk=512, scale=1.0, res=None):
    T, K = a.shape
    N = b.shape[1]
    tm, tk, tn = min(T, 1024), min(tk, K), min(tn, N)
    return mm(name, NN, (T // tm, N // tn, K // tk), a, (tm, tk), _mk, b, (tk, tn), _kn, (tm, tn), _mn,
              _sds((T, N), out_dtype), scale=scale, res=res)


def mm_nt(name, a, b, out_dtype, tn, tk=512, scale=1.0, res=None):
    T, K = a.shape
    N = b.shape[0]
    tm, tk, tn = min(T, 1024), min(tk, K), min(tn, N)
    return mm(name, NT, (T // tm, N // tn, K // tk), a, (tm, tk), _mk, b, (tn, tk), _nk, (tm, tn), _mn,
              _sds((T, N), out_dtype), scale=scale, res=res)


def mm_tn(name, a, b, out_dtype, tm, tn, scale=1.0):
    T, M = a.shape
    N = b.shape[1]
    tk, tm, tn = min(T, 1024), min(tm, M), min(tn, N)
    return mm(name, TN, (M // tm, N // tn, T // tk), a, (tk, tm), _km, b, (tk, tn), _kn, (tm, tn), _mn,
              _sds((M, N), out_dtype), scale=scale)


HBM_SPEC = pl.BlockSpec(memory_space=pl.ANY)


def _place():
    x, y, c = lax.axis_index("x"), lax.axis_index("y"), lax.axis_index("c")
    chips = [(1 - x, y), (x, 1 - y), (1 - x, 1 - y)]
    return x, y, c, chips


def allgather8(v, name):
    m_per, n = v.shape

    def body(x_ref, out_ref, send_sems, recv_sems, local_sem):
        x, y, c, chips = _place()
        me, sibling = (x, y, c), (x, y, 1 - c)

        def rows(px, py, pc):
            return out_ref.at[pl.ds((4 * px + 2 * py + pc) * m_per, m_per), :]

        def copy(k, block, to, src=None):
            return pltpu.make_async_remote_copy(
                src_ref=rows(*block) if src is None else src, dst_ref=rows(*block),
                send_sem=send_sems.at[k], recv_sem=recv_sems.at[k], device_id=to, device_id_type=MESH)

        mine = pltpu.make_async_copy(x_ref, rows(*me), local_sem)
        mine.start()
        first = [copy(0, me, sibling, src=x_ref)]
        first += [copy(1 + j, me, (*chip, c), src=x_ref) for j, chip in enumerate(chips)]
        for cp in first:
            cp.start()
        passed = [copy(4 + j, (*chip, c), sibling) for j, chip in enumerate(chips)]
        for j, chip in enumerate(chips):
            copy(1 + j, (*chip, c), me).wait_recv()
            passed[j].start()
        copy(0, sibling, me).wait_recv()
        for j, chip in enumerate(chips):
            copy(4 + j, (*chip, 1 - c), me).wait_recv()
        for cp in first + passed:
            cp.wait_send()
        mine.wait()

    return pl.pallas_call(
        body, out_shape=_sds((8 * m_per, n), v.dtype), in_specs=[pl.BlockSpec(memory_space=pltpu.VMEM)],
        out_specs=pl.BlockSpec(memory_space=pltpu.VMEM), name=name,
        scratch_shapes=[pltpu.SemaphoreType.DMA((7,)), pltpu.SemaphoreType.DMA((7,)), pltpu.SemaphoreType.DMA],
    )(v)


D2D_PIECES = 4


def _pieces(ref, n):
    rows = ref.shape[0] // n
    return [ref.at[pl.ds(q * rows, rows)] for q in range(n)]


def _start_in_pieces(src, dst, ssem, rsem, to, n):
    for s_q, d_q in zip(_pieces(src, n), _pieces(dst, n)):
        pltpu.make_async_remote_copy(src_ref=s_q, dst_ref=d_q, send_sem=ssem, recv_sem=rsem, device_id=to,
                                     device_id_type=MESH).start()


def _whole(src, dst, ssem, rsem, to):
    return pltpu.make_async_remote_copy(src_ref=src, dst_ref=dst, send_sem=ssem, recv_sem=rsem, device_id=to,
                                        device_id_type=MESH)


def gather4(buf, name):
    _, R, B = buf.shape
    hr = R // 2

    def body(_, o_ref, ssem, rsem, fssem, frsem):
        x, y, c, chips = _place()
        me = 2 * x + y
        sib = (x, y, 1 - c)

        def half(chip, h):
            return o_ref.at[2 * chip[0] + chip[1], pl.ds(h * hr, hr)]

        sends = [_whole(half((x, y), c), half((x, y), c), ssem.at[j], rsem.at[j], (*chip, c))
                 for j, chip in enumerate(chips)]
        for cp in sends:
            cp.start()
        for j, chip in enumerate(chips):
            _whole(half(chip, c), half(chip, c), ssem.at[j], rsem.at[j], (x, y, c)).wait_recv()
            _start_in_pieces(half(chip, c), half(chip, c), fssem.at[j], frsem.at[j], sib, D2D_PIECES)
        for j, chip in enumerate(chips):
            _whole(half(chip, 1 - c), half(chip, 1 - c), fssem.at[j], frsem.at[j], sib).wait_recv()
        for j, chip in enumerate(chips):
            sends[j].wait_send()
            _whole(half(chip, c), half(chip, c), fssem.at[j], frsem.at[j], sib).wait_send()

    return pl.pallas_call(
        body, out_shape=_sds(buf.shape, buf.dtype), in_specs=[HBM_SPEC], out_specs=HBM_SPEC, name=name,
        input_output_aliases={0: 0},
        scratch_shapes=[pltpu.SemaphoreType.DMA((3,)), pltpu.SemaphoreType.DMA((3,)), pltpu.SemaphoreType.DMA((3,)),
                        pltpu.SemaphoreType.DMA((3,))],
    )(buf)


def rs_sibling(g, name):
    _, _, hr, B = g.shape

    def body(g_ref, got_ref, ssem, rsem):
        x, y, c, _ = _place()
        sib = (x, y, 1 - c)
        for k in range(N_CHIPS):
            _start_in_pieces(g_ref.at[k, 1 - c], got_ref.at[k], ssem.at[k], rsem.at[k], sib, 2)
        for k in range(N_CHIPS):
            _whole(g_ref.at[k, 1 - c], got_ref.at[k], ssem.at[k], rsem.at[k], sib).wait()

    return pl.pallas_call(
        body, out_shape=_sds((N_CHIPS, hr, B), g.dtype), in_specs=[HBM_SPEC], out_specs=HBM_SPEC, name=name,
        scratch_shapes=[pltpu.SemaphoreType.DMA((N_CHIPS,)), pltpu.SemaphoreType.DMA((N_CHIPS,))],
    )(g)


def rs_chips(h, name):
    _, hr, B = h.shape

    def body(h_ref, got_ref, ssem, rsem):
        x, y, c, chips = _place()
        remote = [_whole(h_ref.at[2 * chip[0] + chip[1]], got_ref.at[j], ssem.at[j], rsem.at[j], (*chip, c))
                  for j, chip in enumerate(chips)]
        for cp in remote:
            cp.start()
        for cp in remote:
            cp.wait()

    return pl.pallas_call(
        body, out_shape=_sds((3, hr, B), h.dtype), in_specs=[HBM_SPEC], out_specs=HBM_SPEC, name=name,
        scratch_shapes=[pltpu.SemaphoreType.DMA((3,)), pltpu.SemaphoreType.DMA((3,))],
    )(h)


def rs_share(r, name):
    _, hr, B = r.shape

    def body(_, o_ref, ssem, rsem):
        x, y, c, _ = _place()
        sib = (x, y, 1 - c)
        _start_in_pieces(o_ref.at[c], o_ref.at[c], ssem, rsem, sib, D2D_PIECES)
        _whole(o_ref.at[c], o_ref.at[1 - c], ssem, rsem, sib).wait()

    return pl.pallas_call(
        body, out_shape=_sds(r.shape, r.dtype), in_specs=[HBM_SPEC], out_specs=HBM_SPEC, name=name,
        input_output_aliases={0: 0}, scratch_shapes=[pltpu.SemaphoreType.DMA, pltpu.SemaphoreType.DMA],
    )(r)


def reduce_scatter(g, sidx, tag):
    _, R, B = g.shape
    hr = R // 2
    g4 = g.reshape(N_CHIPS, 2, hr, B)
    got = rs_sibling(g4, "rs_sibling_" + tag)
    h = rowwise("rs_add2_" + tag, lambda a, b: _f(a) + _f(b), [("dyn4", g4, 1), got.reshape(N_CHIPS * hr, B)], [],
                [(B, BF)], tb=512, n_rows=N_CHIPS * hr, sidx=sidx, row_period=hr)
    h = h.reshape(N_CHIPS, hr, B)
    got2 = rs_chips(h, "rs_chips_" + tag)
    r = rowwise("rs_add4_" + tag, lambda a, b, c, d: ((_f(a) + _f(b)) + _f(c)) + _f(d),
                [("dyn", h, 0), ("leads", got2, (0,)), ("leads", got2, (1,)), ("leads", got2, (2,))], [],
                [("dyn", 2, B, F32, 1)], tb=512, n_rows=hr, sidx=sidx)
    return rs_share(r, "rs_share_" + tag).reshape(R, B)


def _rms_parts(x):
    rstd = lax.rsqrt(jnp.mean(x * x, axis=-1, keepdims=True) + EPS)
    return x * rstd, rstd


def rms_fwd(x, g, name):
    return rowwise(name, lambda xv, gv: _rms_parts(xv)[0] * gv, [x], [g], [(x.shape[1], BF)])


def _rms_bwd_fn(x, dh, dres, g):
    xh, rstd = _rms_parts(x)
    dxh = _f(dh) * g
    dx = rstd * (dxh - xh * jnp.mean(dxh * xh, axis=-1, keepdims=True)) + dres
    return dx, jnp.sum(_f(dh) * xh, axis=0, keepdims=True)


def rms_bwd(x, g, dh, dres, name):
    D = x.shape[1]
    return rowwise(name, _rms_bwd_fn, [x, dh, dres], [g], [(D, F32)], [(1, D)])


def cast_unit(w, lead, sidx, name):
    R, B = w.shape[-2:]
    return rowwise(name, lambda v: v, [("leads", w, lead)], [], [("dyn", N_CHIPS, B, BF, 0)], n_rows=R, sidx=sidx)


def ffn_fwd(xin, gain, wg, wu, wd):
    T, D = xin.shape
    fs = wg.shape[-1]
    F = N_CHIPS * fs
    tm, tk = min(T, 1024), 512
    h = rms_fwd(xin, gain, "ffn_rms")

    def up(wt, nm):
        return mm(nm, NN, (T // tm, N_CHIPS, D // tk), h, (tm, tk), _mk, wt, (None, tk, fs),
                  lambda m, n, k: (n, k, 0), (tm, fs), _mn, _sds((T, F), BF))

    zg, zu = up(wg, "ffn_gate"), up(wu, "ffn_up")
    a = rowwise("ffn_swiglu", lambda g, u: _silu(_f(g)) * _f(u), [zg, zu], [], [(F, BF)])
    x2 = mm_nn("ffn_down", a, wd.reshape(F, D), F32, 1024, tk=fs, scale=0.5, res=xin)
    return x2, (xin, h, zg, zu, a)


def _swiglu_bwd_fn(da, zg, zu):
    da, zg, zu = _f(da), _f(zg), _f(zu)
    return da * zu * _dsilu(zg), da * _silu(zg)


def ffn_bwd(dx2, saved, gain, wg, wu, wd):
    xin, h, zg, zu, a = saved
    T, D = xin.shape
    fs = wg.shape[-1]
    F = N_CHIPS * fs
    tm, tn, tkk = min(T, 1024), 1024, min(T, 1024)
    da = mm_nt("ffn_da", dx2, wd.reshape(F, D), BF, fs, scale=0.5)
    g_wd = mm_tn("ffn_dwd", a, dx2, BF, fs, 1024, scale=0.5).reshape(N_CHIPS, fs, D)
    dzg, dzu = rowwise("ffn_dswiglu", _swiglu_bwd_fn, [da, zg, zu], [], [(F, BF), (F, BF)])

    def dw(dz, nm):
        return mm(nm, TN, (D // tn, N_CHIPS, T // tkk), h, (tkk, tn), _km, dz, (tkk, fs), _kn,
                  (None, tn, fs), lambda m, n, k: (n, m, 0), _sds((N_CHIPS, D, fs), BF))

    g_wg, g_wu = dw(dzg, "ffn_dwg"), dw(dzu, "ffn_dwu")

    def dh_part(dz, wt, nm, res):
        return mm(nm, NT, (T // tm, D // tn, N_CHIPS), dz, (tm, fs), _mk, wt, (None, tn, fs),
                  lambda m, n, k: (k, n, 0), (tm, tn), _mn, _sds((T, D), F32), res=res)

    dh = dh_part(dzu, wu, "ffn_dh_u", dh_part(dzg, wg, "ffn_dh_g", None))
    dx, dgain = rms_bwd(xin, gain, dh, dx2, "ffn_rms_bwd")
    return dx, dgain, g_wg, g_wu, g_wd


def _gla_chunk(q_ref, k_ref, v_ref, u_ref, b_ref, rows):
    r = lax.broadcasted_iota(jnp.int32, (CHUNK, CHUNK), 0)
    c = lax.broadcasted_iota(jnp.int32, (CHUNK, CHUNK), 1)
    causal = c <= r
    u = u_ref[rows, :] + b_ref[...]
    g = (jnp.minimum(u, 0.0) - jnp.log(1.0 + jnp.exp(-jnp.abs(u)))) * (1.0 / GLA_TAU)
    b = _dot(causal.astype(F32), g, NN, HI)
    last = lax.broadcasted_iota(jnp.int32, (CHUNK, HK), 0) == CHUNK - 1
    blast = jnp.sum(jnp.where(last, b, 0.0), axis=0, keepdims=True)
    eb = jnp.exp(b)
    qb = q_ref[rows, :] * (HK ** -0.5) * eb
    k = k_ref[rows, :]
    kb = k * jnp.exp(-b)
    kl = k * jnp.exp(blast - b)
    A = jnp.where(causal, _bdot(qb, kb, NT), 0.0)
    return causal, u, b, blast, eb, qb, kb, kl, A


def _gla_in_specs(T):
    return [pl.BlockSpec((T, HK), lambda h: (0, h)), pl.BlockSpec((T, HK), lambda h: (0, GLA_H + h)),
            pl.BlockSpec((T, HV), lambda h: (0, GLA_H + h)), pl.BlockSpec((T, HK), lambda h: (0, h)),
            pl.BlockSpec((1, HK), lambda h: (0, h))]


def gla_fwd(zmain, ug, gate_b):
    T = zmain.shape[0]
    nC = T // CHUNK

    def body(q_ref, k_ref, v_ref, u_ref, b_ref, o_ref, s_ref, st_ref):
        st_ref[...] = jnp.zeros_like(st_ref)

        def step(n, carry):
            rows = pl.ds(pl.multiple_of(n * CHUNK, CHUNK), CHUNK)
            _, _, _, blast, _, qb, _, kl, A = _gla_chunk(q_ref, k_ref, v_ref, u_ref, b_ref, rows)
            v = v_ref[rows, :]
            ST = st_ref[...]
            s_ref[n] = ST
            o_ref[rows, :] = _bdot(qb, ST, NT) + _bdot(A, v, NN)
            st_ref[...] = ST * jnp.exp(blast) + _bdot(v, kl, TN)
            return carry

        lax.fori_loop(0, nC, step, 0)

    return pl.pallas_call(
        body, grid=(GLA_H,), in_specs=_gla_in_specs(T),
        out_specs=[pl.BlockSpec((T, HV), lambda h: (0, h)), pl.BlockSpec((nC, None, HV, HK), lambda h: (0, h, 0, 0))],
        out_shape=[_sds((T, GLA_H * HV), F32), _sds((nC, GLA_H, HV, HK), F32)],
        scratch_shapes=[pltpu.VMEM((HV, HK), F32)], name="gla_fwd", compiler_params=_cp("parallel"),
    )(zmain, zmain, zmain, ug, gate_b)


def gla_bwd(zmain, ug, gate_b, S, do):
    T = zmain.shape[0]
    nC = T // CHUNK

    def body(q_ref, k_ref, v_ref, u_ref, b_ref, s_ref, do_ref, dq_ref, dk_ref, dv_ref, du_ref, dgb_ref, dst_ref):
        dst_ref[...] = jnp.zeros_like(dst_ref)
        dgb_ref[...] = jnp.zeros_like(dgb_ref)

        def step(it, carry):
            n = nC - 1 - it
            rows = pl.ds(pl.multiple_of(n * CHUNK, CHUNK), CHUNK)
            causal, u, b, blast, eb, qb, kb, kl, A = _gla_chunk(q_ref, k_ref, v_ref, u_ref, b_ref, rows)
            v = v_ref[rows, :]
            dout = do_ref[rows, :]
            ST = s_ref[n]
            dST = dst_ref[...]
            elast = jnp.exp(blast)
            dA = jnp.where(causal, _bdot(dout, v, NT), 0.0)
            dv_ref[rows, :] = (_bdot(A, dout, TN) + _bdot(kl, dST, NT)).astype(dv_ref.dtype)
            dqb = _bdot(dout, ST, NN) + _bdot(dA, kb, NN)
            dkb = _bdot(dA, qb, TN)
            dkl = _bdot(v, dST, NN)
            ddec = jnp.sum(ST * dST, axis=0, keepdims=True)
            dst_ref[...] = dST * elast + _bdot(dout, qb, TN)
            dq_ref[rows, :] = (dqb * eb * (HK ** -0.5)).astype(dq_ref.dtype)
            dk_ref[rows, :] = (dkb * jnp.exp(-b) + dkl * jnp.exp(blast - b)).astype(dk_ref.dtype)
            db = dqb * qb - dkb * kb - dkl * kl
            dbl = jnp.sum(dkl * kl, axis=0, keepdims=True) + elast * ddec
            dg = _dot(jnp.logical_not(causal).astype(F32) + jnp.where(
                lax.broadcasted_iota(jnp.int32, (CHUNK, CHUNK), 0) == lax.broadcasted_iota(jnp.int32, (CHUNK, CHUNK), 1),
                1.0, 0.0), db, NN, HI) + dbl
            du = dg * (1.0 / GLA_TAU) / (1.0 + jnp.exp(u))
            du_ref[rows, :] = du
            dgb_ref[...] += jnp.sum(du, axis=0, keepdims=True)
            return carry

        lax.fori_loop(0, nC, step, 0)

    specs = _gla_in_specs(T) + [pl.BlockSpec((nC, None, HV, HK), lambda h: (0, h, 0, 0)),
                                pl.BlockSpec((T, HV), lambda h: (0, h))]
    return pl.pallas_call(
        body, grid=(GLA_H,), in_specs=specs,
        out_specs=[pl.BlockSpec((T, HK), lambda h: (0, h)), pl.BlockSpec((T, HK), lambda h: (0, h)),
                   pl.BlockSpec((T, HV), lambda h: (0, h)), pl.BlockSpec((T, HK), lambda h: (0, h)),
                   pl.BlockSpec((1, HK), lambda h: (0, h))],
        out_shape=[_sds((T, GLA_H * HK), BF), _sds((T, GLA_H * HK), BF), _sds((T, GLA_H * HV), BF),
                   _sds((T, GLA_H * HK), F32), _sds((1, GLA_H * HK), F32)],
        scratch_shapes=[pltpu.VMEM((HV, HK), F32)], name="gla_bwd", compiler_params=_cp("parallel"),
    )(zmain, zmain, zmain, ug, gate_b, S, do)


def _gla_post_fn(o, r, g):
    outs = []
    for h in range(GLA_H):
        on = _rms_parts(o[:, h * HV:(h + 1) * HV])[0] * g
        outs.append(on * _silu(r[:, h * HV:(h + 1) * HV]))
    return jnp.concatenate(outs, axis=1)


def _gla_post_bwd_fn(da, o, r, g):
    dos, drs = [], []
    dg = jnp.zeros((1, HV), F32)
    for h in range(GLA_H):
        sl = slice(h * HV, (h + 1) * HV)
        xh, rstd = _rms_parts(o[:, sl])
        drs.append(da[:, sl] * xh * g * _dsilu(r[:, sl]))
        don = da[:, sl] * _silu(r[:, sl])
        dg = dg + jnp.sum(don * xh, axis=0, keepdims=True)
        dxh = don * g
        dos.append(rstd * (dxh - xh * jnp.mean(dxh * xh, axis=-1, keepdims=True)))
    return jnp.concatenate(dos, axis=1), jnp.concatenate(drs, axis=1), dg


def conv_fwd(u, dw, dwb):
    T, C = u.shape
    TB = min(T, 256)

    def body(u_ref, w_ref, b_ref, y_ref, pad_ref):
        pad_ref[0:CONV_PAD, :] = jnp.zeros((CONV_PAD, LANES), F32)
        pad_ref[CONV_PAD:CONV_PAD + T, :] = u_ref[...]
        off = CONV_PAD - (CONV_W - 1)
        for t0 in range(0, T, TB):
            acc = jnp.zeros((TB, LANES), F32) + b_ref[...]
            for j in range(CONV_W):
                acc = acc + w_ref[j:j + 1, :] * pad_ref[t0 + off + j:t0 + off + j + TB, :]
            y_ref[t0:t0 + TB, :] = acc

    col = lambda i: (0, i)
    return pl.pallas_call(
        body, grid=(C // LANES,),
        in_specs=[pl.BlockSpec((T, LANES), col), pl.BlockSpec((CONV_W, LANES), col), pl.BlockSpec((1, LANES), col)],
        out_specs=pl.BlockSpec((T, LANES), col), out_shape=_sds((T, C), F32),
        scratch_shapes=[pltpu.VMEM((T + CONV_PAD, LANES), F32)], name="conv_fwd", compiler_params=_cp("parallel"),
    )(u, dw, dwb)


def conv_bwd(dy, u, dw):
    T, C = u.shape
    TB = min(T, 256)

    def body(dy_ref, u_ref, w_ref, du_ref, dw_ref, db_ref, upad, dypad):
        upad[0:CONV_PAD, :] = jnp.zeros((CONV_PAD, LANES), F32)
        upad[CONV_PAD:CONV_PAD + T, :] = u_ref[...]
        dypad[0:T, :] = dy_ref[...]
        dypad[T:T + CONV_PAD, :] = jnp.zeros((CONV_PAD, LANES), F32)
        off = CONV_PAD - (CONV_W - 1)
        for t0 in range(0, T, TB):
            acc = jnp.zeros((TB, LANES), F32)
            for j in range(CONV_W):
                s = t0 + (CONV_W - 1) - j
                acc = acc + w_ref[j:j + 1, :] * dypad[s:s + TB, :]
            du_ref[t0:t0 + TB, :] = acc
        for j in range(CONV_W):
            acc = jnp.zeros((TB, LANES), F32)
            for t0 in range(0, T, TB):
                acc = acc + dy_ref[t0:t0 + TB, :] * upad[t0 + off + j:t0 + off + j + TB, :]
            dw_ref[j:j + 1, :] = jnp.sum(acc, axis=0, keepdims=True)
        db_ref[...] = jnp.sum(dy_ref[...], axis=0, keepdims=True)

    col = lambda i: (0, i)
    return pl.pallas_call(
        body, grid=(C // LANES,),
        in_specs=[pl.BlockSpec((T, LANES), col), pl.BlockSpec((T, LANES), col), pl.BlockSpec((CONV_W, LANES), col)],
        out_specs=[pl.BlockSpec((T, LANES), col), pl.BlockSpec((CONV_W, LANES), col), pl.BlockSpec((1, LANES), col)],
        out_shape=[_sds((T, C), F32), _sds((CONV_W, C), F32), _sds((1, C), F32)],
        scratch_shapes=[pltpu.VMEM((T + CONV_PAD, LANES), F32), pltpu.VMEM((T + CONV_PAD, LANES), F32)],
        name="conv_bwd", compiler_params=_cp("parallel"),
    )(dy, u, dw)


def _ln_parts(x):
    mu = jnp.mean(x, axis=-1, keepdims=True)
    xc = x - mu
    rstd = lax.rsqrt(jnp.mean(xc * xc, axis=-1, keepdims=True) + EPS)
    return xc * rstd, rstd


def _ln_silu_fn(x, g, b):
    return _silu(_ln_parts(x)[0] * g + b)


def _ln_silu_bwd_fn(dbo, x, g, b):
    xh, rstd = _ln_parts(x)
    dy = dbo * _dsilu(xh * g + b)
    dyg = dy * g
    dx = rstd * (dyg - jnp.mean(dyg, axis=-1, keepdims=True) - xh * jnp.mean(dyg * xh, axis=-1, keepdims=True))
    return dx, jnp.sum(dy * xh, axis=0, keepdims=True), jnp.sum(dy, axis=0, keepdims=True)


def _glu_bwd_fn(du, ca, cb):
    s = _sigmoid(cb)
    return du * s, du * ca * s * (1.0 - s)


DIAGS = QB + KW


def _onehot_diag():
    j = lax.broadcasted_iota(jnp.int32, (REL_PAD, DIAGS), 1)
    i = lax.broadcasted_iota(jnp.int32, (REL_PAD, DIAGS), 0)
    return (i == jnp.clip(KW - j, -REL_CLIP, REL_CLIP) + REL_CLIP).astype(F32)


def relbias_tile(rbp):
    def body(rb_ref, o_ref, e_ref):
        e_ref[...] = _dot(rb_ref[...], _onehot_diag(), NN, HI)
        tc = lax.shift_right_logical(lax.broadcasted_iota(jnp.int32, (QB, KW), 0), 6)
        wc = lax.shift_right_logical(lax.broadcasted_iota(jnp.int32, (QB, KW), 1), 6)
        ok = jnp.logical_and(wc >= tc, wc <= tc + LEFT_CHUNKS)
        for h in range(ATT_H):
            spread = pltpu.roll(jnp.broadcast_to(e_ref[h:h + 1, :], (QB, DIAGS)), KW, 1, stride=1, stride_axis=0)
            o_ref[h] = jnp.where(ok, spread[:, :KW], NEG)

    return pl.pallas_call(body, out_shape=_sds((ATT_H, QB, KW), F32), name="relbias_tile",
                          scratch_shapes=[pltpu.VMEM((ATT_H, DIAGS), F32)], compiler_params=_cp())(rbp)


def relbias_reduce(dbm):
    def body(d_ref, o_ref, e_ref):
        u = lax.broadcasted_iota(jnp.int32, (QB, QB), 0)
        t = lax.broadcasted_iota(jnp.int32, (QB, QB), 1)
        flip = (u + t == QB - 1).astype(F32)
        for h in range(ATT_H):
            padded = jnp.concatenate([d_ref[h], jnp.zeros((QB, QB), F32)], axis=1)
            lined = pltpu.roll(_dot(flip, padded, NN, HI), 1, 1, stride=1, stride_axis=0)
            e_ref[h:h + 1, :] = jnp.sum(lined, axis=0, keepdims=True)
        o_ref[...] = _dot(e_ref[...], _onehot_diag(), NT, HI)

    return pl.pallas_call(body, out_shape=_sds((ATT_H, REL_PAD), F32), name="relbias_reduce",
                          scratch_shapes=[pltpu.VMEM((ATT_H, DIAGS), F32)], compiler_params=_cp())(dbm)


def _att_scores(q_ref, kp_ref, bm_ref, i):
    q0 = pl.multiple_of(i * QB, QB)
    kw = kp_ref[pl.ds(q0, KW), :]
    s = _bdot(q_ref[...], kw, NT) * (HD ** -0.5) + bm_ref[...]
    w = lax.broadcasted_iota(jnp.int32, (QB, KW), 1)
    return jnp.where(w + q0 >= PADK, s, NEG), kw, q0


def attn_fwd(qkv, kvp, bm):
    T = qkv.shape[0]
    D = ATT_H * HD

    def body(q_ref, kp_ref, vp_ref, bm_ref, o_ref, lse_ref):
        s, _, q0 = _att_scores(q_ref, kp_ref, bm_ref, pl.program_id(1))
        m = jnp.max(s, axis=-1, keepdims=True)
        e = jnp.exp(s - m)
        l = jnp.sum(e, axis=-1, keepdims=True)
        o_ref[...] = _bdot(e * (1.0 / l), vp_ref[pl.ds(q0, KW), :], NN).astype(o_ref.dtype)
        lse_ref[...] = m + jnp.log(l)

    return pl.pallas_call(
        body, grid=(ATT_H, T // QB),
        in_specs=[pl.BlockSpec((QB, HD), lambda h, i: (i, h)), pl.BlockSpec((T + PADK, HD), lambda h, i: (0, h)),
                  pl.BlockSpec((T + PADK, HD), lambda h, i: (0, ATT_H + h)),
                  pl.BlockSpec((None, QB, KW), lambda h, i: (h, 0, 0))],
        out_specs=[pl.BlockSpec((QB, HD), lambda h, i: (i, h)), pl.BlockSpec((None, QB, 1), lambda h, i: (h, i, 0))],
        out_shape=[_sds((T, D), BF), _sds((ATT_H, T, 1), F32)], name="attn_fwd",
        compiler_params=_cp("parallel", "arbitrary"),
    )(qkv, kvp, kvp, bm)


def attn_bwd(qkv, kvp, bm, o, lse, do):
    T = qkv.shape[0]
    D = ATT_H * HD

    def body(q_ref, kp_ref, vp_ref, bm_ref, o_ref, lse_ref, do_ref, dq_ref, dkp_ref, dvp_ref, dbm_ref):
        i = pl.program_id(1)

        @pl.when(i == 0)
        def _():
            dkp_ref[...] = jnp.zeros_like(dkp_ref)
            dvp_ref[...] = jnp.zeros_like(dvp_ref)
            dbm_ref[...] = jnp.zeros_like(dbm_ref)

        s, kw, q0 = _att_scores(q_ref, kp_ref, bm_ref, i)
        p = jnp.exp(s - lse_ref[...])
        dout = do_ref[...]
        dp = _bdot(dout, vp_ref[pl.ds(q0, KW), :], NT)
        delta = jnp.sum(_f(dout) * _f(o_ref[...]), axis=-1, keepdims=True)
        ds = p * (dp - delta)
        dq_ref[...] = (_bdot(ds, kw, NN) * (HD ** -0.5)).astype(dq_ref.dtype)
        dkp_ref[pl.ds(q0, KW), :] += _bdot(ds, q_ref[...], TN) * (HD ** -0.5)
        dvp_ref[pl.ds(q0, KW), :] += _bdot(p, dout, TN)
        dbm_ref[...] += ds

    qspec = pl.BlockSpec((QB, HD), lambda h, i: (i, h))
    kspec = pl.BlockSpec((T + PADK, HD), lambda h, i: (0, h))
    bspec = pl.BlockSpec((None, QB, KW), lambda h, i: (h, 0, 0))
    return pl.pallas_call(
        body, grid=(ATT_H, T // QB),
        in_specs=[qspec, kspec, pl.BlockSpec((T + PADK, HD), lambda h, i: (0, ATT_H + h)), bspec, qspec,
                  pl.BlockSpec((None, QB, 1), lambda h, i: (h, i, 0)), qspec],
        out_specs=[qspec, kspec, kspec, bspec],
        out_shape=[_sds((T, D), BF), _sds((T + PADK, D), F32), _sds((T + PADK, D), F32), _sds((ATT_H, QB, KW), F32)],
        name="attn_bwd", compiler_params=_cp("parallel", "arbitrary"),
    )(qkv, kvp, kvp, bm, o, lse, do)


def _final_fn(x, tgt, g):
    D = x.shape[1]
    xh, rstd = _rms_parts(x)
    diff = xh * g - tgt
    dy = diff * (1.0 / D)
    dxh = dy * g
    dx = rstd * (dxh - xh * jnp.mean(dxh * xh, axis=-1, keepdims=True))
    loss = jnp.sum(jnp.sum(diff * diff, axis=-1, keepdims=True), axis=0, keepdims=True) * (0.5 / D)
    return dx, jnp.sum(dy * xh, axis=0, keepdims=True), jnp.broadcast_to(loss, (1, LANES))


def _adamw_fn(w, g, m, v):
    m = ADAM_B1 * m + (1.0 - ADAM_B1) * g
    v = ADAM_B2 * v + (1.0 - ADAM_B2) * (g * g)
    m_hat = m / (1.0 - ADAM_B1 ** ADAM_STEP)
    v_hat = v / (1.0 - ADAM_B2 ** ADAM_STEP)
    delta = -ADAM_LR * (m_hat / (jnp.sqrt(v_hat) + ADAM_EPS) + ADAM_WD * w)
    return g, delta, m, v


def adamw(w, g, m, v, name):
    shape = w.shape
    C = shape[-1]
    R = w.size // C
    outs = rowwise(name, _adamw_fn, [t.reshape(R, C) for t in (w, g, m, v)], [], [(C, F32)] * 4)
    return tuple(t.reshape(shape) for t in outs)


def adamw_units(w, m, v, g_units, name):
    lead_shape = w.shape[:-2]
    R, B = w.shape[-2:]
    leads = [tuple(int(i) for i in idx) for idx in sorted(g_units)]
    tb = _row_block(R, 256)
    nb = R // tb
    n_u = len(leads)

    def lead_of(u):
        idx, rem = [], u
        for d in reversed(lead_shape):
            idx.append(rem % d)
            rem = rem // d
        return tuple(reversed(idx))

    def body(*refs):
        w_ref, m_ref, v_ref = refs[:3]
        g_refs = refs[3:3 + n_u]
        o_refs = refs[3 + n_u:]
        for own in range(n_u):
            @pl.when(pl.program_id(0) == own)
            def _(own=own):
                res = _adamw_fn(w_ref[...], g_refs[own][...], m_ref[...], v_ref[...])
                for ref, val in zip(o_refs, res):
                    ref[...] = val

    native = pl.BlockSpec((None,) * len(lead_shape) + (tb, B), lambda u, r: lead_of(u) + (r, 0))
    g_specs = [pl.BlockSpec((tb, B), lambda u, r, own=own: (jnp.where(u == own, r, 0), 0)) for own in range(n_u)]
    assert leads == [lead_of(u) for u in range(n_u)]
    return pl.pallas_call(
        body, grid=(n_u, nb), in_specs=[native] * 3 + g_specs, out_specs=[native] * 4,
        out_shape=[_sds(w.shape, F32)] * 4, name=name, compiler_params=_cp("arbitrary", "arbitrary"),
    )(w, m, v, *[g_units[idx] for idx in leads])


WEIGHTS = ['ffn_norm', 'ffn_w_gate', 'ffn_w_up', 'ffn_w_down', 'mix_norm', 'ab_w_in', 'gla_gate_w', 'gla_gate_b',
           'gla_norm_g', 'conv_dw', 'conv_dw_b', 'conv_ln_g', 'conv_ln_b', 'ab_w_out', 'att_w_qkv', 'att_rel_bias',
           'att_w_o', 'pl_norm', 'pl_w_gate', 'pl_w_proj', 'final_norm']
BIG = ['ffn_w_gate', 'ffn_w_up', 'ffn_w_down', 'ab_w_in', 'ab_w_out', 'att_w_qkv', 'att_w_o', 'pl_w_gate', 'pl_w_proj']


def _leads(shape):
    out = [()]
    for d in shape[:-2]:
        out = [idx + (i,) for idx in out for i in range(d)]
    return out


def _pack(parts, rows):
    flat = jnp.concatenate([p.reshape(-1) for p in parts])
    return jnp.pad(flat, (0, rows * LANES - flat.shape[0])).reshape(rows, LANES)


def _unpack(flat, shapes):
    out, pos = [], 0
    for s in shapes:
        n = 1
        for d in s:
            n *= d
        out.append(flat[pos:pos + n].reshape(s))
        pos += n
    return out


def _rows_for(shapes):
    n = sum(math.prod(s) for s in shapes)
    return -(-n // (8 * LANES)) * 8


def kernel(x, p, ffn_norm, ffn_w_gate, ffn_w_up, ffn_w_down, mix_norm, ab_w_in, gla_gate_w, gla_gate_b, gla_norm_g, conv_dw, conv_dw_b, conv_ln_g, conv_ln_b, ab_w_out, att_w_qkv, att_rel_bias, att_w_o, pl_norm, pl_w_gate, pl_w_proj, final_norm, loss_target, m_ffn_norm, m_ffn_w_gate, m_ffn_w_up, m_ffn_w_down, m_mix_norm, m_ab_w_in, m_gla_gate_w, m_gla_gate_b, m_gla_norm_g, m_conv_dw, m_conv_dw_b, m_conv_ln_g, m_conv_ln_b, m_ab_w_out, m_att_w_qkv, m_att_rel_bias, m_att_w_o, m_pl_norm, m_pl_w_gate, m_pl_w_proj, m_final_norm, v_ffn_norm, v_ffn_w_gate, v_ffn_w_up, v_ffn_w_down, v_mix_norm, v_ab_w_in, v_gla_gate_w, v_gla_gate_b, v_gla_norm_g, v_conv_dw, v_conv_dw_b, v_conv_ln_g, v_conv_ln_b, v_ab_w_out, v_att_w_qkv, v_att_rel_bias, v_att_w_o, v_pl_norm, v_pl_w_gate, v_pl_w_proj, v_final_norm):
    env = dict(locals())
    W = {n: env[n] for n in WEIGHTS}
    M = {n: env["m_" + n] for n in WEIGHTS}
    V = {n: env["v_" + n] for n in WEIGHTS}
    xc, yc_, cc = lax.axis_index("x"), lax.axis_index("y"), lax.axis_index("c")

    x0 = x[0]
    tgt = loss_target[0]
    T, D = x0.shape
    fs = ffn_w_gate.shape[-1]
    ws = ab_w_in.shape[-1]
    AB_IN = N_CHIPS * ws
    gz0 = 2 * GLA_H * HK + 2 * GLA_H * HV
    sidx = jnp.stack([2 * xc + yc_, cc]).astype(jnp.int32)

    full = {}
    for n in BIG:
        for lead in _leads(W[n].shape):
            tag = n + "".join(str(i) for i in lead)
            full[n, lead] = gather4(cast_unit(W[n], lead, sidx, "cast_" + tag), "gather_" + tag)
    w_in = jnp.transpose(full['ab_w_in', (0,)], (1, 0, 2)).reshape(D, AB_IN)
    w_main = jnp.concatenate([w_in[:, :gz0], w_in[:, gz0 + 16:]], axis=1)
    w_gz = jnp.pad(w_in[:, gz0:gz0 + 16], ((0, 0), (0, LANES - 16)))
    w_out = full['ab_w_out', (0,)].reshape(D, D)
    w_qkv = full['att_w_qkv', (0,)]
    w_o = full['att_w_o', (0,)].reshape(D, D)
    qs = w_qkv.shape[-1]

    small_sharded = [ffn_norm, gla_gate_w, conv_dw]
    rows_s = _rows_for([t.shape for t in small_sharded])
    got = allgather8(_pack(small_sharded, rows_s), "gather_small").reshape(N_CHIPS, 2, rows_s * LANES)[:, 0]
    per_chip = [_unpack(got[k], [t.shape for t in small_sharded]) for k in range(N_CHIPS)]
    ffn_norm_f, gate_w_f, conv_dw_f = [jnp.concatenate([per_chip[k][t] for k in range(N_CHIPS)], axis=-1)
                                       for t in range(3)]
    gate_w_p = jnp.pad(gate_w_f[0], ((0, LANES - 16), (0, 0)))
    conv_w = conv_dw_f[0]
    rb_p = jnp.pad(att_rel_bias[0], ((0, 0), (0, REL_PAD - att_rel_bias.shape[-1])))

    G = {}
    small_g = {}

    def ffn_w(i, j):
        return full['ffn_w_gate', (i, j)], full['ffn_w_up', (i, j)], full['ffn_w_down', (i, j)]

    saved = {}
    xs = x0
    xs, saved['f00'] = ffn_fwd(xs, ffn_norm_f[0, 0][None], *ffn_w(0, 0))

    def mixer0_fwd(xin):
        h = rms_fwd(xin, mix_norm[0][None], "mix0_rms")
        zmain = mm_nn("ab_in", h, w_main, F32, 1024)
        gzp = mm_nn("ab_gz", h, w_gz, F32, LANES)
        ug = mm_nn("gla_gate", gzp, gate_w_p, F32, GLA_H * HK)
        o, S = gla_fwd(zmain, ug, gla_gate_b)
        a_out = rowwise("gla_post", _gla_post_fn, [o, ("cols", zmain, GLA_H * HV, 2)], [gla_norm_g], [(GLA_H * HV, BF)])
        u = rowwise("conv_glu", lambda a, b: a * _sigmoid(b), [("cols", zmain, 1024, 3), ("cols", zmain, 1024, 4)], [],
                    [(1024, F32)])
        yc = conv_fwd(u, conv_w, conv_dw_b)
        b_out = rowwise("conv_ln", _ln_silu_fn, [yc], [conv_ln_g, conv_ln_b], [(1024, BF)])
        cat = jnp.concatenate([a_out, b_out], axis=1)
        x2 = mm_nn("ab_out", cat, w_out, F32, 1024, res=xin)
        return x2, (xin, h, zmain, gzp, ug, o, S, u, yc, cat)

    xs, saved['m0'] = mixer0_fwd(xs)
    xs, saved['f01'] = ffn_fwd(xs, ffn_norm_f[0, 1][None], *ffn_w(0, 1))
    ks = D // N_CHIPS
    dp = p.shape[-1]

    def pl_fwd(xin, i):
        h = rms_fwd(xin, pl_norm[i][None], "pl_rms")
        tm = min(T, 1024)
        u = mm_nn("pl_gate", h, full['pl_w_gate', (i,)].reshape(D, D), F32, 1024)
        e = mm("pl_proj", NN, (T // tm, N_CHIPS, 1), p[i, 0], (tm, dp), _mk, full['pl_w_proj', (i,)],
               (None, dp, ks), lambda m, n, k: (n, 0, 0), (tm, ks), _mn, _sds((T, D), F32))
        x2 = rowwise("pl_mix", lambda xv, uv, ev: xv + _sigmoid(uv) * ev, [xin, u, e], [], [(D, F32)])
        return x2, (xin, h, u, e)

    xs, saved['p0'] = pl_fwd(xs, 0)
    xs, saved['f10'] = ffn_fwd(xs, ffn_norm_f[1, 0][None], *ffn_w(1, 0))

    bm = relbias_tile(rb_p)

    def mixer1_fwd(xin):
        h = rms_fwd(xin, mix_norm[1][None], "mix1_rms")
        tm, tk, tn = min(T, 1024), 512, 512
        per = qs // tn
        qkv = mm("att_qkv", NN, (T // tm, 3 * D // tn, D // tk), h, (tm, tk), _mk, w_qkv, (None, tk, tn),
                 lambda m, n, k: (n // per, k, n % per), (tm, tn), _mn, _sds((T, 3 * D), BF))
        kvp = jnp.pad(qkv[:, D:], ((PADK, 0), (0, 0)))
        o, lse = attn_fwd(qkv, kvp, bm)
        x2 = mm_nn("att_o", o, w_o, F32, 1024, res=xin)
        return x2, (xin, h, qkv, kvp, o, lse)

    xs, saved['m1'] = mixer1_fwd(xs)
    xs, saved['f11'] = ffn_fwd(xs, ffn_norm_f[1, 1][None], *ffn_w(1, 1))
    xs, saved['p1'] = pl_fwd(xs, 1)

    dx, small_g['final_norm'], loss_acc = rowwise("loss_head", _final_fn, [xs, tgt], [final_norm[None]], [(D, F32)],
                                                  [(1, D), (1, LANES)])
    loss = lax.psum(loss_acc[0, 0], ("x", "y", "c"))

    def pl_bwd(dx2, sv, i):
        xin, h, u, e = sv
        tm = min(T, 1024)

        def fn(d, uv, ev):
            s = _sigmoid(uv)
            return d * s, d * ev * s * (1.0 - s)

        de, du = rowwise("pl_mix_bwd", fn, [dx2, u, e], [], [(D, BF), (D, BF)])
        G['pl_w_proj', (i,)] = mm("pl_dproj", TN, (1, N_CHIPS, T // tm), p[i, 0], (tm, dp), _km, de, (tm, ks), _kn,
                                  (None, dp, ks), lambda m, n, k: (n, 0, 0), _sds((N_CHIPS, dp, ks), BF))
        G['pl_w_gate', (i,)] = mm_tn("pl_dgate", h, du, BF, ks, 1024).reshape(N_CHIPS, ks, D)
        dh = mm_nt("pl_dh", du, full['pl_w_gate', (i,)].reshape(D, D), F32, 1024)
        return rms_bwd(xin, pl_norm[i][None], dh, dx2, "pl_rms_bwd")

    def ffn_b(dx2, key, i, j):
        dxn, dgain, G['ffn_w_gate', (i, j)], G['ffn_w_up', (i, j)], G['ffn_w_down', (i, j)] = ffn_bwd(
            dx2, saved[key], ffn_norm_f[i, j][None], *ffn_w(i, j))
        return dxn, dgain

    def mixer1_bwd(dx2, sv):
        xin, h, qkv, kvp, o, lse = sv
        do = mm_nt("att_do", dx2, w_o, BF, 1024)
        dwo = mm_tn("att_dwo", o, dx2, BF, 1024, 1024)
        dq, dkp, dvp, dbm = attn_bwd(qkv, kvp, bm, o, lse, do)
        dqkv = jnp.concatenate([dq, dkp[PADK:].astype(BF), dvp[PADK:].astype(BF)], axis=1)
        tm, tn, tkk = min(T, 1024), 512, min(T, 1024)
        per = qs // tn
        dwqkv = mm("att_dwqkv", TN, (D // 1024, 3 * D // tn, T // tkk), h, (tkk, 1024), _km, dqkv, (tkk, tn), _kn,
                   (None, 1024, tn), lambda m, n, k: (n // per, m, n % per), _sds((N_CHIPS, D, qs), BF))
        dh = mm("att_dh", NT, (T // tm, D // 1024, 3 * D // tn), dqkv, (tm, tn), _mk, w_qkv, (None, 1024, tn),
                lambda m, n, k: (k // per, n, k % per), (tm, 1024), _mn, _sds((T, D), F32))
        dxn, dgain = rms_bwd(xin, mix_norm[1][None], dh, dx2, "mix1_rms_bwd")
        return dxn, dgain, dwo, dwqkv, relbias_reduce(dbm)

    def mixer0_bwd(dx2, sv):
        xin, h, zmain, gzp, ug, o, S, u, yc, cat = sv
        dcat = mm_nt("ab_dcat", dx2, w_out, F32, 1024)
        dwout = mm_tn("ab_dwout", cat, dx2, BF, 1024, 1024)
        do, dr, dgn = rowwise("gla_post_bwd", _gla_post_bwd_fn,
                              [("cols", dcat, GLA_H * HV, 0), o, ("cols", zmain, GLA_H * HV, 2)], [gla_norm_g],
                              [(GLA_H * HV, F32), (GLA_H * HV, BF)], [(1, HV)])
        dyc, dlg, dlb = rowwise("conv_ln_bwd", _ln_silu_bwd_fn, [("cols", dcat, 1024, 1), yc], [conv_ln_g, conv_ln_b],
                                [(1024, F32)], [(1, 1024), (1, 1024)])
        du, ddw, ddwb = conv_bwd(dyc, u, conv_w)
        dca, dcb = rowwise("conv_glu_bwd", _glu_bwd_fn, [du, ("cols", zmain, 1024, 3), ("cols", zmain, 1024, 4)], [],
                           [(1024, BF), (1024, BF)])
        dq, dk, dv, dug, dgb = gla_bwd(zmain, ug, gla_gate_b, S, do)
        dgw = mm_tn("gla_dgate_w", gzp, dug, F32, LANES, GLA_H * HK)
        dgzp = mm_nt("gla_dgz", dug, gate_w_p, F32, LANES)
        dzm = jnp.concatenate([dq, dk, dv, dr, dca, dcb], axis=1)
        dwmain = mm_tn("ab_dwmain", h, dzm, BF, 1024, 1024)
        dwgz = mm_tn("ab_dwgz", h, dgzp, BF, 1024, LANES)
        dh = mm_nt("ab_dh_gz", dgzp, w_gz, F32, 1024, res=mm_nt("ab_dh", dzm, w_main, F32, 1024))
        dxn, dgain = rms_bwd(xin, mix_norm[0][None], dh, dx2, "mix0_rms_bwd")
        dwin = jnp.concatenate([dwmain[:, :gz0], dwgz[:, :16], dwmain[:, gz0:]], axis=1)
        g_win = jnp.transpose(dwin.reshape(D, N_CHIPS, ws), (1, 0, 2))
        return dxn, dgain, g_win, dwout, (dgn, dlg, dlb, ddw, ddwb, dgw[:16], dgb)

    dpl, dffn, dmix = [None, None], [[None, None], [None, None]], [None, None]
    dx, dpl[1] = pl_bwd(dx, saved['p1'], 1)
    dx, dffn[1][1] = ffn_b(dx, 'f11', 1, 1)
    dx, dmix[1], dwo, dwqkv, drb = mixer1_bwd(dx, saved['m1'])
    dx, dffn[1][0] = ffn_b(dx, 'f10', 1, 0)
    dx, dpl[0] = pl_bwd(dx, saved['p0'], 0)
    dx, dffn[0][1] = ffn_b(dx, 'f01', 0, 1)
    dx, dmix[0], g_win, dwout, (dgn, dlg, dlb, ddw, ddwb, dgw, dgb) = mixer0_bwd(dx, saved['m0'])
    dx, dffn[0][0] = ffn_b(dx, 'f00', 0, 0)
    grad_x = dx[None]

    G['ab_w_in', (0,)] = g_win
    G['ab_w_out', (0,)] = dwout.reshape((N_CHIPS,) + ab_w_out.shape[1:])
    G['att_w_qkv', (0,)] = dwqkv
    G['att_w_o', (0,)] = dwo.reshape((N_CHIPS,) + att_w_o.shape[1:])

    grads, big_g = {}, {}
    for n in BIG:
        for lead in _leads(W[n].shape):
            big_g[n, lead] = reduce_scatter(G[n, lead], sidx, n + "".join(str(i) for i in lead))

    small_g['ffn_norm'] = jnp.stack([jnp.stack([dffn[i][j][0] for j in range(2)]) for i in range(2)])
    small_g['mix_norm'] = jnp.concatenate(dmix, axis=0)
    small_g['gla_gate_w'] = dgw[None]
    small_g['gla_gate_b'] = dgb
    small_g['gla_norm_g'] = dgn
    small_g['conv_dw'] = ddw[None]
    small_g['conv_dw_b'] = ddwb
    small_g['conv_ln_g'] = dlg
    small_g['conv_ln_b'] = dlb
    small_g['att_rel_bias'] = drb[None, :, :att_rel_bias.shape[-1]]
    small_g['pl_norm'] = jnp.concatenate(dpl, axis=0)
    small_g['final_norm'] = small_g['final_norm'][0]
    small_names = [n for n in WEIGHTS if n not in BIG]
    small_shapes = [small_g[n].shape for n in small_names]
    rows_g = _rows_for(small_shapes)
    allp = allgather8(_pack([small_g[n] for n in small_names], rows_g), "gather_small_grads")
    summed = rowwise("sum_small_grads", lambda *v: (((v[0] + v[1]) + (v[2] + v[3])) + ((v[4] + v[5]) + (v[6] + v[7]))),
                     [("leads", allp.reshape(8, rows_g, LANES), (d,)) for d in range(8)], [], [(LANES, F32)], tb=rows_g)
    for n, g in zip(small_names, _unpack(summed.reshape(-1), small_shapes)):
        grads[n] = g
    chip = 2 * xc + yc_
    for n, axis in (('ffn_norm', 2), ('gla_gate_w', 2), ('conv_dw', 2)):
        width = W[n].shape[axis]
        grads[n] = lax.dynamic_slice_in_dim(grads[n], chip * width, width, axis)

    outs = {n: adamw(W[n], grads[n], M[n], V[n], "adamw_" + n) for n in small_names}
    for n in BIG:
        outs[n] = adamw_units(W[n], M[n], V[n], {lead: big_g[n, lead] for lead in _leads(W[n].shape)}, "adamw_" + n)
    return (loss, grad_x, *[outs[n][0] for n in WEIGHTS], *[outs[n][1] for n in WEIGHTS],
            *[outs[n][2] for n in WEIGHTS], *[outs[n][3] for n in WEIGHTS])
```

```python
import math

import jax
import jax.numpy as jnp
from jax import lax
from jax.experimental import pallas as pl
from jax.experimental.pallas import tpu as pltpu

F32 = jnp.float32
BF = jnp.bfloat16
MESH = pl.DeviceIdType.MESH
HI = lax.Precision.HIGHEST
V7X_VMEM_LIMIT = 56 * 1024 * 1024
LANES = 128
EPS = 1e-6
NEG = -1e30

CHUNK = 64
LEFT_CHUNKS = 8
QB = 256
KW = QB + LEFT_CHUNKS * CHUNK
PADK = LEFT_CHUNKS * CHUNK
REL_CLIP = 128
REL_PAD = 384
ATT_H = 16
HD = 128
GLA_H = 4
HK = 128
HV = 256
GLA_TAU = 16.0
CONV_W = 31
CONV_PAD = 32
N_CHIPS = 4

ADAM_LR = 0.001
ADAM_B1 = 0.9
ADAM_B2 = 0.999
ADAM_EPS = 1e-08
ADAM_WD = 0.01
ADAM_STEP = 10

NN = ((1,), (0,))
NT = ((1,), (1,))
TN = ((0,), (0,))


def _dot(a, b, dims, prec=None):
    return lax.dot_general(a, b, (dims, ((), ())), preferred_element_type=F32, precision=prec)


def _bdot(a, b, dims):
    return _dot(a.astype(BF), b.astype(BF), dims)


def _cp(*sem):
    return pltpu.CompilerParams(dimension_semantics=sem if sem else None, vmem_limit_bytes=V7X_VMEM_LIMIT)


def _sds(shape, dtype):
    return jax.ShapeDtypeStruct(shape, dtype)


def _sigmoid(x):
    return 1.0 / (1.0 + jnp.exp(-x))


def _silu(x):
    return x * _sigmoid(x)


def _dsilu(x):
    s = _sigmoid(x)
    return s * (1.0 + x * (1.0 - s))


def _f(x):
    return x.astype(F32)


def _row_item(item, tb):
    if not isinstance(item, tuple):
        return item, (tb, item.shape[1]), lambda i, s: (i, 0)
    kind, arr = item[0], item[1]
    if kind == "cols":
        return arr, (tb, item[2]), lambda i, s, blk=item[3]: (i, blk)
    if kind == "leads":
        lead = tuple(item[2])
        return arr, (None,) * len(lead) + (tb, arr.shape[-1]), lambda i, s: lead + (i, 0)
    if kind == "dyn":
        return arr, (None, tb, arr.shape[-1]), lambda i, s, sel=item[2]: (s[sel], i, 0)
    if kind == "dyn4":
        nb = arr.shape[2] // tb
        return arr, (None, None, tb, arr.shape[-1]), lambda i, s, sel=item[2]: (i // nb, s[sel], i % nb, 0)
    raise ValueError(kind)


def _row_block(n, cap):
    for cand in range(min(cap, n) // 16 * 16, 0, -16):
        if n % cand == 0:
            return cand
    return n


def rowwise(name, fn, rows, bcast, outs, accs=(), tb=256, n_rows=None, sidx=None, row_period=None, dep=None):
    if n_rows is None:
        first = rows[0][1] if isinstance(rows[0], tuple) else rows[0]
        n_rows = first.shape[-2]
    tb = _row_block(n_rows if row_period is None else row_period, tb)
    items = [_row_item(it, tb) for it in rows]
    n_r, n_b, n_o = len(rows), len(bcast), len(outs)
    pre = 0 if sidx is None else 1

    def wrap(f):
        return (lambda i: f(i, None)) if sidx is None else (lambda i, s: f(i, s))

    def body(*refs):
        refs = refs[pre:]
        r, b = refs[:n_r], refs[n_r:n_r + n_b]
        refs = refs[n_r + n_b + (dep is not None):]
        o, a = refs[:n_o], refs[n_o:]
        res = fn(*[v[...] for v in r], *[v[...] for v in b])
        if not isinstance(res, tuple):
            res = (res,)
        for ref, val in zip(o, res[:n_o]):
            ref[...] = val.astype(ref.dtype)
        if a:
            @pl.when(pl.program_id(0) == 0)
            def _():
                for ref in a:
                    ref[...] = jnp.zeros_like(ref)
            for ref, val in zip(a, res[n_o:]):
                ref[...] += val

    in_specs = [pl.BlockSpec(bs, wrap(f)) for _, bs, f in items]
    in_specs += [pl.BlockSpec(v.shape, wrap(lambda i, s, nd=v.ndim: (0,) * nd)) for v in bcast]
    out_specs, out_shape = [], []
    for o in outs:
        if o[0] == "dyn":
            _, L, c, dt, sel = o
            out_specs.append(pl.BlockSpec((None, tb, c), wrap(lambda i, s, sel=sel: (s[sel], i, 0))))
            out_shape.append(_sds((L, n_rows, c), dt))
        else:
            c, dt = o
            out_specs.append(pl.BlockSpec((tb, c), wrap(lambda i, s: (i, 0))))
            out_shape.append(_sds((n_rows, c), dt))
    out_specs += [pl.BlockSpec(sh, wrap(lambda i, s: (0, 0))) for sh in accs]
    out_shape += [_sds(sh, F32) for sh in accs]
    operands = [a for a, _, _ in items] + list(bcast)
    if dep is not None:
        operands.append(dep)
        in_specs.append(pl.BlockSpec(TOKEN, wrap(lambda i, s: (0, 0))))
    grid = (n_rows // tb,)
    if sidx is None:
        res = pl.pallas_call(body, grid=grid, in_specs=in_specs, out_specs=out_specs, out_shape=out_shape, name=name,
                             compiler_params=_cp("arbitrary"))(*operands)
    else:
        spec = pltpu.PrefetchScalarGridSpec(num_scalar_prefetch=1, grid=grid, in_specs=in_specs, out_specs=out_specs)
        res = pl.pallas_call(body, grid_spec=spec, out_shape=out_shape, name=name,
                             compiler_params=_cp("arbitrary"))(sidx, *operands)
    return res[0] if len(res) == 1 else tuple(res)


TOKEN = (8, LANES)


def mm(name, dims, grid, a, a_bs, a_im, b, b_bs, b_im, o_bs, o_im, out, scale=1.0, res=None, dep=None):
    nk = grid[2]
    acc_shape = tuple(d for d in o_bs if d is not None)

    def body(*refs):
        a_ref, b_ref = refs[0], refs[1]
        pos = 2
        res_ref = None
        if res is not None:
            res_ref = refs[pos]
            pos += 1
        if dep is not None:
            pos += 1
        o_ref = refs[pos]
        part = _bdot(a_ref[...], b_ref[...], dims)

        def finish(acc):
            v = acc * scale if scale != 1.0 else acc
            if res_ref is not None:
                v = v + _f(res_ref[...])
            o_ref[...] = v.astype(o_ref.dtype)

        if nk == 1:
            finish(part)
        else:
            acc_ref = refs[pos + 1]
            k = pl.program_id(2)

            @pl.when(k == 0)
            def _():
                acc_ref[...] = part

            @pl.when(k > 0)
            def _():
                acc_ref[...] += part

            @pl.when(k == nk - 1)
            def _():
                finish(acc_ref[...])

    operands = [a, b]
    in_specs = [pl.BlockSpec(a_bs, a_im), pl.BlockSpec(b_bs, b_im)]
    if res is not None:
        operands.append(res)
        in_specs.append(pl.BlockSpec(o_bs, o_im))
    if dep is not None:
        operands.append(dep)
        in_specs.append(pl.BlockSpec(TOKEN, lambda m, n, k: (0, 0)))
    scratch = [pltpu.VMEM(acc_shape, F32)] if nk > 1 else []
    return pl.pallas_call(body, grid=grid, in_specs=in_specs, out_specs=pl.BlockSpec(o_bs, o_im), out_shape=out,
                          scratch_shapes=scratch, name=name,
                          compiler_params=_cp("parallel", "parallel", "arbitrary"))(*operands)


def _mk(m, n, k):
    return (m, k)


def _mn(m, n, k):
    return (m, n)


def _km(m, n, k):
    return (k, m)


def _kn(m, n, k):
    return (k, n)


def _nk(m, n, k):
    return (n, k)


def mm_nn(name, a, b, out_dtype, tn, tk=512, scale=1.0, res=None):
    T, K = a.shape
    N = b.shape[1]
    tm, tk, tn = min(T, 1024), min(tk, K), min(tn, N)
    return mm(name, NN, (T // tm, N // tn, K // tk), a, (tm, tk), _mk, b, (tk, tn), _kn, (tm, tn), _mn,
              _sds((T, N), out_dtype), scale=scale, res=res)


def mm_nt(name, a, b, out_dtype, tn, tk=512, scale=1.0, res=None, dep=None):
    T, K = a.shape
    N = b.shape[0]
    tm, tk, tn = min(T, 1024), min(tk, K), min(tn, N)
    return mm(name, NT, (T // tm, N // tn, K // tk), a, (tm, tk), _mk, b, (tn, tk), _nk, (tm, tn), _mn,
              _sds((T, N), out_dtype), scale=scale, res=res, dep=dep)


def mm_tn(name, a, b, out_dtype, tm, tn, scale=1.0, dep=None):
    T, M = a.shape
    N = b.shape[1]
    tk, tm, tn = min(T, 1024), min(tm, M), min(tn, N)
    return mm(name, TN, (M // tm, N // tn, T // tk), a, (tk, tm), _km, b, (tk, tn), _kn, (tm, tn), _mn,
              _sds((M, N), out_dtype), scale=scale, dep=dep)


HBM_SPEC = pl.BlockSpec(memory_space=pl.ANY)


def _place():
    x, y, c = lax.axis_index("x"), lax.axis_index("y"), lax.axis_index("c")
    chips = [(1 - x, y), (x, 1 - y), (1 - x, 1 - y)]
    return x, y, c, chips


def allgather8(v, name):
    m_per, n = v.shape

    def body(x_ref, out_ref, send_sems, recv_sems, local_sem):
        x, y, c, chips = _place()
        me, sibling = (x, y, c), (x, y, 1 - c)

        def rows(px, py, pc):
            return out_ref.at[pl.ds((4 * px + 2 * py + pc) * m_per, m_per), :]

        def copy(k, block, to, src=None):
            return pltpu.make_async_remote_copy(
                src_ref=rows(*block) if src is None else src, dst_ref=rows(*block),
                send_sem=send_sems.at[k], recv_sem=recv_sems.at[k], device_id=to, device_id_type=MESH)

        mine = pltpu.make_async_copy(x_ref, rows(*me), local_sem)
        mine.start()
        first = [copy(0, me, sibling, src=x_ref)]
        first += [copy(1 + j, me, (*chip, c), src=x_ref) for j, chip in enumerate(chips)]
        for cp in first:
            cp.start()
        passed = [copy(4 + j, (*chip, c), sibling) for j, chip in enumerate(chips)]
        for j, chip in enumerate(chips):
            copy(1 + j, (*chip, c), me).wait_recv()
            passed[j].start()
        copy(0, sibling, me).wait_recv()
        for j, chip in enumerate(chips):
            copy(4 + j, (*chip, 1 - c), me).wait_recv()
        for cp in first + passed:
            cp.wait_send()
        mine.wait()

    return pl.pallas_call(
        body, out_shape=_sds((8 * m_per, n), v.dtype), in_specs=[pl.BlockSpec(memory_space=pltpu.VMEM)],
        out_specs=pl.BlockSpec(memory_space=pltpu.VMEM), name=name,
        scratch_shapes=[pltpu.SemaphoreType.DMA((7,)), pltpu.SemaphoreType.DMA((7,)), pltpu.SemaphoreType.DMA],
    )(v)


D2D_PIECES = 4


def _pieces(ref, n):
    rows = ref.shape[0] // n
    return [ref.at[pl.ds(q * rows, rows)] for q in range(n)]


def _start_in_pieces(src, dst, ssem, rsem, to, n):
    for s_q, d_q in zip(_pieces(src, n), _pieces(dst, n)):
        pltpu.make_async_remote_copy(src_ref=s_q, dst_ref=d_q, send_sem=ssem, recv_sem=rsem, device_id=to,
                                     device_id_type=MESH).start()


def _whole(src, dst, ssem, rsem, to):
    return pltpu.make_async_remote_copy(src_ref=src, dst_ref=dst, send_sem=ssem, recv_sem=rsem, device_id=to,
                                        device_id_type=MESH)


HBM_ONLY = pl.BlockSpec(memory_space=pltpu.HBM)
SEM_SPEC = pl.BlockSpec(memory_space=pltpu.SEMAPHORE)
DATAFLOW = pltpu.SideEffectType.DATAFLOW_SIDE_EFFECTING


def _gather_copies(bufs):
    x, y, c, chips = _place()
    out = []
    for o_ref in bufs:
        hr = o_ref.shape[1] // 2

        def half(chip, o_ref=o_ref, hr=hr):
            return o_ref.at[2 * chip[0] + chip[1], pl.ds(c * hr, hr)]

        out += [(half((x, y)), half((x, y)), (*chip, c), half(chip)) for chip in chips]
    return out


def _scatter_copies(arrays):
    x, y, c, chips = _place()
    n = len(arrays) // 2
    out = []
    for h_ref, got_ref in zip(arrays[:n], arrays[n:]):
        out += [(h_ref.at[2 * chip[0] + chip[1]], got_ref.at[j], (*chip, c), got_ref.at[j])
                for j, chip in enumerate(chips)]
    return out


def split_start(name, arrays, copies_of, n_copies, after):
    n = len(arrays)

    def body(*refs):
        ssem, rsem = refs[n + 1], refs[n + 2]
        for i, (src, dst, to, _) in enumerate(copies_of(refs[n + 3:2 * n + 3])):
            _whole(src, dst, ssem.at[i], rsem.at[i], to).start()
        refs[2 * n + 3][...] = jnp.zeros(TOKEN, F32)

    res = pl.pallas_call(
        body, name=name,
        out_shape=(pltpu.SemaphoreType.DMA((n_copies,)), pltpu.SemaphoreType.DMA((n_copies,)),
                   *[pltpu.HBM(a.shape, a.dtype) for a in arrays], _sds(TOKEN, F32)),
        in_specs=[HBM_ONLY] * n + [pl.BlockSpec(memory_space=pl.ANY)],
        out_specs=(SEM_SPEC, SEM_SPEC, *[HBM_ONLY] * n, pl.BlockSpec(memory_space=pltpu.VMEM)),
        input_output_aliases={i: 2 + i for i in range(n)},
        compiler_params=pltpu.CompilerParams(has_side_effects=DATAFLOW),
    )(*[pltpu.with_memory_space_constraint(a, pltpu.HBM) for a in arrays], after)
    return res[0], res[1], list(res[2:2 + n]), res[2 + n]


def split_wait(name, ssem, rsem, arrays, copies_of, after):
    n = len(arrays)

    def body(*refs):
        s_ref, r_ref = refs[n], refs[n + 1]
        for i, (src, _, to, mine) in enumerate(copies_of(refs[:n])):
            cp = _whole(src, mine, s_ref.at[i], r_ref.at[i], to)
            cp.wait_send()
            cp.wait_recv()

    res = pl.pallas_call(
        body, name=name, out_shape=tuple(pltpu.HBM(a.shape, a.dtype) for a in arrays),
        in_specs=[HBM_ONLY] * n + [SEM_SPEC, SEM_SPEC, pl.BlockSpec(memory_space=pl.ANY)],
        out_specs=tuple([HBM_ONLY] * n), input_output_aliases={i: i for i in range(n)},
        compiler_params=pltpu.CompilerParams(has_side_effects=DATAFLOW),
    )(*arrays, ssem, rsem, after)
    return list(res)


def gather_pass_on(bufs, name):
    n = len(bufs)

    def body(*refs):
        o_refs, ssem, rsem = refs[n:2 * n], refs[2 * n], refs[2 * n + 1]
        x, y, c, chips = _place()
        sib = (x, y, 1 - c)

        def half(o_ref, chip, h):
            hr = o_ref.shape[1] // 2
            return o_ref.at[2 * chip[0] + chip[1], pl.ds(h * hr, hr)]

        for u, o_ref in enumerate(o_refs):
            for j, chip in enumerate(chips):
                _start_in_pieces(half(o_ref, chip, c), half(o_ref, chip, c), ssem.at[3 * u + j], rsem.at[3 * u + j], sib,
                                 D2D_PIECES)
        for u, o_ref in enumerate(o_refs):
            for j, chip in enumerate(chips):
                _whole(half(o_ref, chip, c), half(o_ref, chip, 1 - c), ssem.at[3 * u + j], rsem.at[3 * u + j], sib).wait()

    res = pl.pallas_call(
        body, out_shape=tuple(_sds(b.shape, b.dtype) for b in bufs), in_specs=[HBM_SPEC] * n,
        out_specs=tuple([HBM_SPEC] * n), name=name, input_output_aliases={i: i for i in range(n)},
        scratch_shapes=[pltpu.SemaphoreType.DMA((3 * n,)), pltpu.SemaphoreType.DMA((3 * n,))],
    )(*bufs)
    return list(res)


def rs_sibling(g, name):
    _, _, hr, B = g.shape

    def body(g_ref, got_ref, ssem, rsem):
        x, y, c, _ = _place()
        sib = (x, y, 1 - c)
        for k in range(N_CHIPS):
            _start_in_pieces(g_ref.at[k, 1 - c], got_ref.at[k], ssem.at[k], rsem.at[k], sib, 2)
        for k in range(N_CHIPS):
            _whole(g_ref.at[k, 1 - c], got_ref.at[k], ssem.at[k], rsem.at[k], sib).wait()

    return pl.pallas_call(
        body, out_shape=_sds((N_CHIPS, hr, B), g.dtype), in_specs=[HBM_SPEC], out_specs=HBM_SPEC, name=name,
        scratch_shapes=[pltpu.SemaphoreType.DMA((N_CHIPS,)), pltpu.SemaphoreType.DMA((N_CHIPS,))],
    )(g)


def rs_share(r, name):
    _, hr, B = r.shape

    def body(_, o_ref, ssem, rsem):
        x, y, c, _ = _place()
        sib = (x, y, 1 - c)
        _start_in_pieces(o_ref.at[c], o_ref.at[c], ssem, rsem, sib, D2D_PIECES)
        _whole(o_ref.at[c], o_ref.at[1 - c], ssem, rsem, sib).wait()

    return pl.pallas_call(
        body, out_shape=_sds(r.shape, r.dtype), in_specs=[HBM_SPEC], out_specs=HBM_SPEC, name=name,
        input_output_aliases={0: 0}, scratch_shapes=[pltpu.SemaphoreType.DMA, pltpu.SemaphoreType.DMA],
    )(r)


def reduce_chip(g, sidx, tag):
    _, R, B = g.shape
    hr = R // 2
    g4 = g.reshape(N_CHIPS, 2, hr, B)
    got = rs_sibling(g4, "rs_sibling_" + tag)
    h = rowwise("rs_add2_" + tag, lambda a, b: _f(a) + _f(b), [("dyn4", g4, 1), got.reshape(N_CHIPS * hr, B)], [],
                [(B, BF)], tb=512, n_rows=N_CHIPS * hr, sidx=sidx, row_period=hr)
    return h.reshape(N_CHIPS, hr, B)


def reduce_finish(h, got2, sidx, tag):
    _, hr, B = h.shape
    r = rowwise("rs_add4_" + tag, lambda a, b, c, d: ((_f(a) + _f(b)) + _f(c)) + _f(d),
                [("dyn", h, 0), ("leads", got2, (0,)), ("leads", got2, (1,)), ("leads", got2, (2,))], [],
                [("dyn", 2, B, F32, 1)], tb=512, n_rows=hr, sidx=sidx)
    return rs_share(r, "rs_share_" + tag).reshape(2 * hr, B)


def _rms_parts(x):
    rstd = lax.rsqrt(jnp.mean(x * x, axis=-1, keepdims=True) + EPS)
    return x * rstd, rstd


def rms_fwd(x, g, name, dep=None):
    return rowwise(name, lambda xv, gv: _rms_parts(xv)[0] * gv, [x], [g], [(x.shape[1], BF)], dep=dep)


def _rms_bwd_fn(x, dh, dres, g):
    xh, rstd = _rms_parts(x)
    dxh = _f(dh) * g
    dx = rstd * (dxh - xh * jnp.mean(dxh * xh, axis=-1, keepdims=True)) + dres
    return dx, jnp.sum(_f(dh) * xh, axis=0, keepdims=True)


def rms_bwd(x, g, dh, dres, name):
    D = x.shape[1]
    return rowwise(name, _rms_bwd_fn, [x, dh, dres], [g], [(D, F32)], [(1, D)])


def cast_unit(w, lead, sidx, name):
    R, B = w.shape[-2:]
    return rowwise(name, lambda v: v, [("leads", w, lead)], [], [("dyn", N_CHIPS, B, BF, 0)], n_rows=R, sidx=sidx)


def ffn_fwd(xin, gain, wg, wu, wd, dep=None):
    T, D = xin.shape
    fs = wg.shape[-1]
    F = N_CHIPS * fs
    tm, tk = min(T, 1024), 512
    h = rms_fwd(xin, gain, "ffn_rms", dep)

    def up(wt, nm):
        return mm(nm, NN, (T // tm, N_CHIPS, D // tk), h, (tm, tk), _mk, wt, (None, tk, fs),
                  lambda m, n, k: (n, k, 0), (tm, fs), _mn, _sds((T, F), BF))

    zg, zu = up(wg, "ffn_gate"), up(wu, "ffn_up")
    a = rowwise("ffn_swiglu", lambda g, u: _silu(_f(g)) * _f(u), [zg, zu], [], [(F, BF)])
    x2 = mm_nn("ffn_down", a, wd.reshape(F, D), F32, 1024, tk=fs, scale=0.5, res=xin)
    return x2, (xin, h, zg, zu, a)


def _swiglu_bwd_fn(da, zg, zu):
    da, zg, zu = _f(da), _f(zg), _f(zu)
    return da * zu * _dsilu(zg), da * _silu(zg)


def ffn_bwd(dx2, saved, gain, wg, wu, wd, dep=None):
    xin, h, zg, zu, a = saved
    T, D = xin.shape
    fs = wg.shape[-1]
    F = N_CHIPS * fs
    tm, tn, tkk = min(T, 1024), 1024, min(T, 1024)
    da = mm_nt("ffn_da", dx2, wd.reshape(F, D), BF, fs, scale=0.5, dep=dep)
    g_wd = mm_tn("ffn_dwd", a, dx2, BF, fs, 1024, scale=0.5, dep=dep).reshape(N_CHIPS, fs, D)
    dzg, dzu = rowwise("ffn_dswiglu", _swiglu_bwd_fn, [da, zg, zu], [], [(F, BF), (F, BF)])

    def dw(dz, nm):
        return mm(nm, TN, (D // tn, N_CHIPS, T // tkk), h, (tkk, tn), _km, dz, (tkk, fs), _kn,
                  (None, tn, fs), lambda m, n, k: (n, m, 0), _sds((N_CHIPS, D, fs), BF))

    g_wg, g_wu = dw(dzg, "ffn_dwg"), dw(dzu, "ffn_dwu")

    def dh_part(dz, wt, nm, res):
        return mm(nm, NT, (T // tm, D // tn, N_CHIPS), dz, (tm, fs), _mk, wt, (None, tn, fs),
                  lambda m, n, k: (k, n, 0), (tm, tn), _mn, _sds((T, D), F32), res=res)

    dh = dh_part(dzu, wu, "ffn_dh_u", dh_part(dzg, wg, "ffn_dh_g", None))
    dx, dgain = rms_bwd(xin, gain, dh, dx2, "ffn_rms_bwd")
    return dx, dgain, g_wg, g_wu, g_wd


def _gla_chunk(q_ref, k_ref, v_ref, u_ref, b_ref, rows):
    r = lax.broadcasted_iota(jnp.int32, (CHUNK, CHUNK), 0)
    c = lax.broadcasted_iota(jnp.int32, (CHUNK, CHUNK), 1)
    causal = c <= r
    u = u_ref[rows, :] + b_ref[...]
    g = (jnp.minimum(u, 0.0) - jnp.log(1.0 + jnp.exp(-jnp.abs(u)))) * (1.0 / GLA_TAU)
    b = _dot(causal.astype(F32), g, NN, HI)
    last = lax.broadcasted_iota(jnp.int32, (CHUNK, HK), 0) == CHUNK - 1
    blast = jnp.sum(jnp.where(last, b, 0.0), axis=0, keepdims=True)
    eb = jnp.exp(b)
    qb = q_ref[rows, :] * (HK ** -0.5) * eb
    k = k_ref[rows, :]
    kb = k * jnp.exp(-b)
    kl = k * jnp.exp(blast - b)
    A = jnp.where(causal, _bdot(qb, kb, NT), 0.0)
    return causal, u, b, blast, eb, qb, kb, kl, A


def _gla_in_specs(T):
    return [pl.BlockSpec((T, HK), lambda h: (0, h)), pl.BlockSpec((T, HK), lambda h: (0, GLA_H + h)),
            pl.BlockSpec((T, HV), lambda h: (0, GLA_H + h)), pl.BlockSpec((T, HK), lambda h: (0, h)),
            pl.BlockSpec((1, HK), lambda h: (0, h))]


def gla_fwd(zmain, ug, gate_b):
    T = zmain.shape[0]
    nC = T // CHUNK

    def body(q_ref, k_ref, v_ref, u_ref, b_ref, o_ref, s_ref, st_ref):
        st_ref[...] = jnp.zeros_like(st_ref)

        def step(n, carry):
            rows = pl.ds(pl.multiple_of(n * CHUNK, CHUNK), CHUNK)
            _, _, _, blast, _, qb, _, kl, A = _gla_chunk(q_ref, k_ref, v_ref, u_ref, b_ref, rows)
            v = v_ref[rows, :]
            ST = st_ref[...]
            s_ref[n] = ST
            o_ref[rows, :] = _bdot(qb, ST, NT) + _bdot(A, v, NN)
            st_ref[...] = ST * jnp.exp(blast) + _bdot(v, kl, TN)
            return carry

        lax.fori_loop(0, nC, step, 0)

    return pl.pallas_call(
        body, grid=(GLA_H,), in_specs=_gla_in_specs(T),
        out_specs=[pl.BlockSpec((T, HV), lambda h: (0, h)), pl.BlockSpec((nC, None, HV, HK), lambda h: (0, h, 0, 0))],
        out_shape=[_sds((T, GLA_H * HV), F32), _sds((nC, GLA_H, HV, HK), F32)],
        scratch_shapes=[pltpu.VMEM((HV, HK), F32)], name="gla_fwd", compiler_params=_cp("parallel"),
    )(zmain, zmain, zmain, ug, gate_b)


def gla_bwd(zmain, ug, gate_b, S, do):
    T = zmain.shape[0]
    nC = T // CHUNK

    def body(q_ref, k_ref, v_ref, u_ref, b_ref, s_ref, do_ref, dq_ref, dk_ref, dv_ref, du_ref, dgb_ref, dst_ref):
        dst_ref[...] = jnp.zeros_like(dst_ref)
        dgb_ref[...] = jnp.zeros_like(dgb_ref)

        def step(it, carry):
            n = nC - 1 - it
            rows = pl.ds(pl.multiple_of(n * CHUNK, CHUNK), CHUNK)
            causal, u, b, blast, eb, qb, kb, kl, A = _gla_chunk(q_ref, k_ref, v_ref, u_ref, b_ref, rows)
            v = v_ref[rows, :]
            dout = do_ref[rows, :]
            ST = s_ref[n]
            dST = dst_ref[...]
            elast = jnp.exp(blast)
            dA = jnp.where(causal, _bdot(dout, v, NT), 0.0)
            dv_ref[rows, :] = (_bdot(A, dout, TN) + _bdot(kl, dST, NT)).astype(dv_ref.dtype)
            dqb = _bdot(dout, ST, NN) + _bdot(dA, kb, NN)
            dkb = _bdot(dA, qb, TN)
            dkl = _bdot(v, dST, NN)
            ddec = jnp.sum(ST * dST, axis=0, keepdims=True)
            dst_ref[...] = dST * elast + _bdot(dout, qb, TN)
            dq_ref[rows, :] = (dqb * eb * (HK ** -0.5)).astype(dq_ref.dtype)
            dk_ref[rows, :] = (dkb * jnp.exp(-b) + dkl * jnp.exp(blast - b)).astype(dk_ref.dtype)
            db = dqb * qb - dkb * kb - dkl * kl
            dbl = jnp.sum(dkl * kl, axis=0, keepdims=True) + elast * ddec
            dg = _dot(jnp.logical_not(causal).astype(F32) + jnp.where(
                lax.broadcasted_iota(jnp.int32, (CHUNK, CHUNK), 0) == lax.broadcasted_iota(jnp.int32, (CHUNK, CHUNK), 1),
                1.0, 0.0), db, NN, HI) + dbl
            du = dg * (1.0 / GLA_TAU) / (1.0 + jnp.exp(u))
            du_ref[rows, :] = du
            dgb_ref[...] += jnp.sum(du, axis=0, keepdims=True)
            return carry

        lax.fori_loop(0, nC, step, 0)

    specs = _gla_in_specs(T) + [pl.BlockSpec((nC, None, HV, HK), lambda h: (0, h, 0, 0)),
                                pl.BlockSpec((T, HV), lambda h: (0, h))]
    return pl.pallas_call(
        body, grid=(GLA_H,), in_specs=specs,
        out_specs=[pl.BlockSpec((T, HK), lambda h: (0, h)), pl.BlockSpec((T, HK), lambda h: (0, h)),
                   pl.BlockSpec((T, HV), lambda h: (0, h)), pl.BlockSpec((T, HK), lambda h: (0, h)),
                   pl.BlockSpec((1, HK), lambda h: (0, h))],
        out_shape=[_sds((T, GLA_H * HK), BF), _sds((T, GLA_H * HK), BF), _sds((T, GLA_H * HV), BF),
                   _sds((T, GLA_H * HK), F32), _sds((1, GLA_H * HK), F32)],
        scratch_shapes=[pltpu.VMEM((HV, HK), F32)], name="gla_bwd", compiler_params=_cp("parallel"),
    )(zmain, zmain, zmain, ug, gate_b, S, do)


def _gla_post_fn(o, r, g):
    outs = []
    for h in range(GLA_H):
        on = _rms_parts(o[:, h * HV:(h + 1) * HV])[0] * g
        outs.append(on * _silu(r[:, h * HV:(h + 1) * HV]))
    return jnp.concatenate(outs, axis=1)


def _gla_post_bwd_fn(da, o, r, g):
    dos, drs = [], []
    dg = jnp.zeros((1, HV), F32)
    for h in range(GLA_H):
        sl = slice(h * HV, (h + 1) * HV)
        xh, rstd = _rms_parts(o[:, sl])
        drs.append(da[:, sl] * xh * g * _dsilu(r[:, sl]))
        don = da[:, sl] * _silu(r[:, sl])
        dg = dg + jnp.sum(don * xh, axis=0, keepdims=True)
        dxh = don * g
        dos.append(rstd * (dxh - xh * jnp.mean(dxh * xh, axis=-1, keepdims=True)))
    return jnp.concatenate(dos, axis=1), jnp.concatenate(drs, axis=1), dg


def conv_fwd(u, dw, dwb):
    T, C = u.shape
    TB = min(T, 256)

    def body(u_ref, w_ref, b_ref, y_ref, pad_ref):
        pad_ref[0:CONV_PAD, :] = jnp.zeros((CONV_PAD, LANES), F32)
        pad_ref[CONV_PAD:CONV_PAD + T, :] = u_ref[...]
        off = CONV_PAD - (CONV_W - 1)
        for t0 in range(0, T, TB):
            acc = jnp.zeros((TB, LANES), F32) + b_ref[...]
            for j in range(CONV_W):
                acc = acc + w_ref[j:j + 1, :] * pad_ref[t0 + off + j:t0 + off + j + TB, :]
            y_ref[t0:t0 + TB, :] = acc

    col = lambda i: (0, i)
    return pl.pallas_call(
        body, grid=(C // LANES,),
        in_specs=[pl.BlockSpec((T, LANES), col), pl.BlockSpec((CONV_W, LANES), col), pl.BlockSpec((1, LANES), col)],
        out_specs=pl.BlockSpec((T, LANES), col), out_shape=_sds((T, C), F32),
        scratch_shapes=[pltpu.VMEM((T + CONV_PAD, LANES), F32)], name="conv_fwd", compiler_params=_cp("parallel"),
    )(u, dw, dwb)


def conv_bwd(dy, u, dw):
    T, C = u.shape
    TB = min(T, 256)

    def body(dy_ref, u_ref, w_ref, du_ref, dw_ref, db_ref, upad, dypad):
        upad[0:CONV_PAD, :] = jnp.zeros((CONV_PAD, LANES), F32)
        upad[CONV_PAD:CONV_PAD + T, :] = u_ref[...]
        dypad[0:T, :] = dy_ref[...]
        dypad[T:T + CONV_PAD, :] = jnp.zeros((CONV_PAD, LANES), F32)
        off = CONV_PAD - (CONV_W - 1)
        for t0 in range(0, T, TB):
            acc = jnp.zeros((TB, LANES), F32)
            for j in range(CONV_W):
                s = t0 + (CONV_W - 1) - j
                acc = acc + w_ref[j:j + 1, :] * dypad[s:s + TB, :]
            du_ref[t0:t0 + TB, :] = acc
        for j in range(CONV_W):
            acc = jnp.zeros((TB, LANES), F32)
            for t0 in range(0, T, TB):
                acc = acc + dy_ref[t0:t0 + TB, :] * upad[t0 + off + j:t0 + off + j + TB, :]
            dw_ref[j:j + 1, :] = jnp.sum(acc, axis=0, keepdims=True)
        db_ref[...] = jnp.sum(dy_ref[...], axis=0, keepdims=True)

    col = lambda i: (0, i)
    return pl.pallas_call(
        body, grid=(C // LANES,),
        in_specs=[pl.BlockSpec((T, LANES), col), pl.BlockSpec((T, LANES), col), pl.BlockSpec((CONV_W, LANES), col)],
        out_specs=[pl.BlockSpec((T, LANES), col), pl.BlockSpec((CONV_W, LANES), col), pl.BlockSpec((1, LANES), col)],
        out_shape=[_sds((T, C), F32), _sds((CONV_W, C), F32), _sds((1, C), F32)],
        scratch_shapes=[pltpu.VMEM((T + CONV_PAD, LANES), F32), pltpu.VMEM((T + CONV_PAD, LANES), F32)],
        name="conv_bwd", compiler_params=_cp("parallel"),
    )(dy, u, dw)


def _ln_parts(x):
    mu = jnp.mean(x, axis=-1, keepdims=True)
    xc = x - mu
    rstd = lax.rsqrt(jnp.mean(xc * xc, axis=-1, keepdims=True) + EPS)
    return xc * rstd, rstd


def _ln_silu_fn(x, g, b):
    return _silu(_ln_parts(x)[0] * g + b)


def _ln_silu_bwd_fn(dbo, x, g, b):
    xh, rstd = _ln_parts(x)
    dy = dbo * _dsilu(xh * g + b)
    dyg = dy * g
    dx = rstd * (dyg - jnp.mean(dyg, axis=-1, keepdims=True) - xh * jnp.mean(dyg * xh, axis=-1, keepdims=True))
    return dx, jnp.sum(dy * xh, axis=0, keepdims=True), jnp.sum(dy, axis=0, keepdims=True)


def _glu_bwd_fn(du, ca, cb):
    s = _sigmoid(cb)
    return du * s, du * ca * s * (1.0 - s)


DIAGS = QB + KW


def _onehot_diag():
    j = lax.broadcasted_iota(jnp.int32, (REL_PAD, DIAGS), 1)
    i = lax.broadcasted_iota(jnp.int32, (REL_PAD, DIAGS), 0)
    return (i == jnp.clip(KW - j, -REL_CLIP, REL_CLIP) + REL_CLIP).astype(F32)


def relbias_tile(rbp):
    def body(rb_ref, o_ref, e_ref):
        e_ref[...] = _dot(rb_ref[...], _onehot_diag(), NN, HI)
        tc = lax.shift_right_logical(lax.broadcasted_iota(jnp.int32, (QB, KW), 0), 6)
        wc = lax.shift_right_logical(lax.broadcasted_iota(jnp.int32, (QB, KW), 1), 6)
        ok = jnp.logical_and(wc >= tc, wc <= tc + LEFT_CHUNKS)
        for h in range(ATT_H):
            spread = pltpu.roll(jnp.broadcast_to(e_ref[h:h + 1, :], (QB, DIAGS)), KW, 1, stride=1, stride_axis=0)
            o_ref[h] = jnp.where(ok, spread[:, :KW], NEG)

    return pl.pallas_call(body, out_shape=_sds((ATT_H, QB, KW), F32), name="relbias_tile",
                          scratch_shapes=[pltpu.VMEM((ATT_H, DIAGS), F32)], compiler_params=_cp())(rbp)


def relbias_reduce(dbm):
    def body(d_ref, o_ref, e_ref):
        u = lax.broadcasted_iota(jnp.int32, (QB, QB), 0)
        t = lax.broadcasted_iota(jnp.int32, (QB, QB), 1)
        flip = (u + t == QB - 1).astype(F32)
        for h in range(ATT_H):
            padded = jnp.concatenate([d_ref[h], jnp.zeros((QB, QB), F32)], axis=1)
            lined = pltpu.roll(_dot(flip, padded, NN, HI), 1, 1, stride=1, stride_axis=0)
            e_ref[h:h + 1, :] = jnp.sum(lined, axis=0, keepdims=True)
        o_ref[...] = _dot(e_ref[...], _onehot_diag(), NT, HI)

    return pl.pallas_call(body, out_shape=_sds((ATT_H, REL_PAD), F32), name="relbias_reduce",
                          scratch_shapes=[pltpu.VMEM((ATT_H, DIAGS), F32)], compiler_params=_cp())(dbm)


def _att_scores(q_ref, kp_ref, bm_ref, i):
    q0 = pl.multiple_of(i * QB, QB)
    kw = kp_ref[pl.ds(q0, KW), :]
    s = _bdot(q_ref[...], kw, NT) * (HD ** -0.5) + bm_ref[...]
    w = lax.broadcasted_iota(jnp.int32, (QB, KW), 1)
    return jnp.where(w + q0 >= PADK, s, NEG), kw, q0


def attn_fwd(qkv, kvp, bm):
    T = qkv.shape[0]
    D = ATT_H * HD

    def body(q_ref, kp_ref, vp_ref, bm_ref, o_ref, lse_ref):
        s, _, q0 = _att_scores(q_ref, kp_ref, bm_ref, pl.program_id(1))
        m = jnp.max(s, axis=-1, keepdims=True)
        e = jnp.exp(s - m)
        l = jnp.sum(e, axis=-1, keepdims=True)
        o_ref[...] = _bdot(e * (1.0 / l), vp_ref[pl.ds(q0, KW), :], NN).astype(o_ref.dtype)
        lse_ref[...] = m + jnp.log(l)

    return pl.pallas_call(
        body, grid=(ATT_H, T // QB),
        in_specs=[pl.BlockSpec((QB, HD), lambda h, i: (i, h)), pl.BlockSpec((T + PADK, HD), lambda h, i: (0, h)),
                  pl.BlockSpec((T + PADK, HD), lambda h, i: (0, ATT_H + h)),
                  pl.BlockSpec((None, QB, KW), lambda h, i: (h, 0, 0))],
        out_specs=[pl.BlockSpec((QB, HD), lambda h, i: (i, h)), pl.BlockSpec((None, QB, 1), lambda h, i: (h, i, 0))],
        out_shape=[_sds((T, D), BF), _sds((ATT_H, T, 1), F32)], name="attn_fwd",
        compiler_params=_cp("parallel", "arbitrary"),
    )(qkv, kvp, kvp, bm)


def attn_bwd(qkv, kvp, bm, o, lse, do):
    T = qkv.shape[0]
    D = ATT_H * HD

    def body(q_ref, kp_ref, vp_ref, bm_ref, o_ref, lse_ref, do_ref, dq_ref, dkp_ref, dvp_ref, dbm_ref):
        i = pl.program_id(1)

        @pl.when(i == 0)
        def _():
            dkp_ref[...] = jnp.zeros_like(dkp_ref)
            dvp_ref[...] = jnp.zeros_like(dvp_ref)
            dbm_ref[...] = jnp.zeros_like(dbm_ref)

        s, kw, q0 = _att_scores(q_ref, kp_ref, bm_ref, i)
        p = jnp.exp(s - lse_ref[...])
        dout = do_ref[...]
        dp = _bdot(dout, vp_ref[pl.ds(q0, KW), :], NT)
        delta = jnp.sum(_f(dout) * _f(o_ref[...]), axis=-1, keepdims=True)
        ds = p * (dp - delta)
        dq_ref[...] = (_bdot(ds, kw, NN) * (HD ** -0.5)).astype(dq_ref.dtype)
        dkp_ref[pl.ds(q0, KW), :] += _bdot(ds, q_ref[...], TN) * (HD ** -0.5)
        dvp_ref[pl.ds(q0, KW), :] += _bdot(p, dout, TN)
        dbm_ref[...] += ds

    qspec = pl.BlockSpec((QB, HD), lambda h, i: (i, h))
    kspec = pl.BlockSpec((T + PADK, HD), lambda h, i: (0, h))
    bspec = pl.BlockSpec((None, QB, KW), lambda h, i: (h, 0, 0))
    return pl.pallas_call(
        body, grid=(ATT_H, T // QB),
        in_specs=[qspec, kspec, pl.BlockSpec((T + PADK, HD), lambda h, i: (0, ATT_H + h)), bspec, qspec,
                  pl.BlockSpec((None, QB, 1), lambda h, i: (h, i, 0)), qspec],
        out_specs=[qspec, kspec, kspec, bspec],
        out_shape=[_sds((T, D), BF), _sds((T + PADK, D), F32), _sds((T + PADK, D), F32), _sds((ATT_H, QB, KW), F32)],
        name="attn_bwd", compiler_params=_cp("parallel", "arbitrary"),
    )(qkv, kvp, kvp, bm, o, lse, do)


def _final_fn(x, tgt, g):
    D = x.shape[1]
    xh, rstd = _rms_parts(x)
    diff = xh * g - tgt
    dy = diff * (1.0 / D)
    dxh = dy * g
    dx = rstd * (dxh - xh * jnp.mean(dxh * xh, axis=-1, keepdims=True))
    loss = jnp.sum(jnp.sum(diff * diff, axis=-1, keepdims=True), axis=0, keepdims=True) * (0.5 / D)
    return dx, jnp.sum(dy * xh, axis=0, keepdims=True), jnp.broadcast_to(loss, (1, LANES))


def _adamw_fn(w, g, m, v):
    m = ADAM_B1 * m + (1.0 - ADAM_B1) * g
    v = ADAM_B2 * v + (1.0 - ADAM_B2) * (g * g)
    m_hat = m / (1.0 - ADAM_B1 ** ADAM_STEP)
    v_hat = v / (1.0 - ADAM_B2 ** ADAM_STEP)
    delta = -ADAM_LR * (m_hat / (jnp.sqrt(v_hat) + ADAM_EPS) + ADAM_WD * w)
    return g, delta, m, v


def adamw(w, g, m, v, name):
    shape = w.shape
    C = shape[-1]
    R = w.size // C
    outs = rowwise(name, _adamw_fn, [t.reshape(R, C) for t in (w, g, m, v)], [], [(C, F32)] * 4)
    return tuple(t.reshape(shape) for t in outs)


def adamw_units(w, m, v, g_units, name):
    lead_shape = w.shape[:-2]
    R, B = w.shape[-2:]
    leads = [tuple(int(i) for i in idx) for idx in sorted(g_units)]
    tb = _row_block(R, 256)
    nb = R // tb
    n_u = len(leads)

    def lead_of(u):
        idx, rem = [], u
        for d in reversed(lead_shape):
            idx.append(rem % d)
            rem = rem // d
        return tuple(reversed(idx))

    def body(*refs):
        w_ref, m_ref, v_ref = refs[:3]
        g_refs = refs[3:3 + n_u]
        o_refs = refs[3 + n_u:]
        for own in range(n_u):
            @pl.when(pl.program_id(0) == own)
            def _(own=own):
                res = _adamw_fn(w_ref[...], g_refs[own][...], m_ref[...], v_ref[...])
                for ref, val in zip(o_refs, res):
                    ref[...] = val

    native = pl.BlockSpec((None,) * len(lead_shape) + (tb, B), lambda u, r: lead_of(u) + (r, 0))
    g_specs = [pl.BlockSpec((tb, B), lambda u, r, own=own: (jnp.where(u == own, r, 0), 0)) for own in range(n_u)]
    assert leads == [lead_of(u) for u in range(n_u)]
    return pl.pallas_call(
        body, grid=(n_u, nb), in_specs=[native] * 3 + g_specs, out_specs=[native] * 4,
        out_shape=[_sds(w.shape, F32)] * 4, name=name, compiler_params=_cp("arbitrary", "arbitrary"),
    )(w, m, v, *[g_units[idx] for idx in leads])


WEIGHTS = ['ffn_norm', 'ffn_w_gate', 'ffn_w_up', 'ffn_w_down', 'mix_norm', 'ab_w_in', 'gla_gate_w', 'gla_gate_b',
           'gla_norm_g', 'conv_dw', 'conv_dw_b', 'conv_ln_g', 'conv_ln_b', 'ab_w_out', 'att_w_qkv', 'att_rel_bias',
           'att_w_o', 'pl_norm', 'pl_w_gate', 'pl_w_proj', 'final_norm']
BIG = ['ffn_w_gate', 'ffn_w_up', 'ffn_w_down', 'ab_w_in', 'ab_w_out', 'att_w_qkv', 'att_w_o', 'pl_w_gate', 'pl_w_proj']


def _leads(shape):
    out = [()]
    for d in shape[:-2]:
        out = [idx + (i,) for idx in out for i in range(d)]
    return out


def _pack(parts, rows):
    flat = jnp.concatenate([p.reshape(-1) for p in parts])
    return jnp.pad(flat, (0, rows * LANES - flat.shape[0])).reshape(rows, LANES)


def _unpack(flat, shapes):
    out, pos = [], 0
    for s in shapes:
        n = 1
        for d in s:
            n *= d
        out.append(flat[pos:pos + n].reshape(s))
        pos += n
    return out


def _rows_for(shapes):
    n = sum(math.prod(s) for s in shapes)
    return -(-n // (8 * LANES)) * 8


def kernel(x, p, ffn_norm, ffn_w_gate, ffn_w_up, ffn_w_down, mix_norm, ab_w_in, gla_gate_w, gla_gate_b, gla_norm_g, conv_dw, conv_dw_b, conv_ln_g, conv_ln_b, ab_w_out, att_w_qkv, att_rel_bias, att_w_o, pl_norm, pl_w_gate, pl_w_proj, final_norm, loss_target, m_ffn_norm, m_ffn_w_gate, m_ffn_w_up, m_ffn_w_down, m_mix_norm, m_ab_w_in, m_gla_gate_w, m_gla_gate_b, m_gla_norm_g, m_conv_dw, m_conv_dw_b, m_conv_ln_g, m_conv_ln_b, m_ab_w_out, m_att_w_qkv, m_att_rel_bias, m_att_w_o, m_pl_norm, m_pl_w_gate, m_pl_w_proj, m_final_norm, v_ffn_norm, v_ffn_w_gate, v_ffn_w_up, v_ffn_w_down, v_mix_norm, v_ab_w_in, v_gla_gate_w, v_gla_gate_b, v_gla_norm_g, v_conv_dw, v_conv_dw_b, v_conv_ln_g, v_conv_ln_b, v_ab_w_out, v_att_w_qkv, v_att_rel_bias, v_att_w_o, v_pl_norm, v_pl_w_gate, v_pl_w_proj, v_final_norm):
    env = dict(locals())
    W = {n: env[n] for n in WEIGHTS}
    M = {n: env["m_" + n] for n in WEIGHTS}
    V = {n: env["v_" + n] for n in WEIGHTS}
    xc, yc_, cc = lax.axis_index("x"), lax.axis_index("y"), lax.axis_index("c")

    x0 = x[0]
    tgt = loss_target[0]
    T, D = x0.shape
    fs = ffn_w_gate.shape[-1]
    ws = ab_w_in.shape[-1]
    AB_IN = N_CHIPS * ws
    gz0 = 2 * GLA_H * HK + 2 * GLA_H * HV
    sidx = jnp.stack([2 * xc + yc_, cc]).astype(jnp.int32)

    def ffn_keys(i, j):
        return [('ffn_w_gate', (i, j)), ('ffn_w_up', (i, j)), ('ffn_w_down', (i, j))]

    order = [ffn_keys(0, 0), [('ab_w_in', (0,)), ('ab_w_out', (0,))], ffn_keys(0, 1),
             [('pl_w_gate', (0,)), ('pl_w_proj', (0,))], ffn_keys(1, 0), [('att_w_qkv', (0,)), ('att_w_o', (0,))],
             ffn_keys(1, 1), [('pl_w_gate', (1,)), ('pl_w_proj', (1,))]]
    full, in_flight = {}, []
    tok = jnp.zeros(TOKEN, F32)
    for s, keys in enumerate(order):
        parts = [cast_unit(W[n], lead, sidx, "cast_" + n + "".join(str(i) for i in lead)) for n, lead in keys]
        ssem, rsem, thru, tok = split_start("gather_start_%d" % s, parts, _gather_copies, 3 * len(keys), tok)
        in_flight.append((ssem, rsem, thru))

    def arrive(s, after):
        ssem, rsem, thru = in_flight[s]
        landed = split_wait("gather_wait_%d" % s, ssem, rsem, thru, _gather_copies, after)
        for key, buf in zip(order[s], gather_pass_on(landed, "gather_pass_on_%d" % s)):
            full[key] = buf

    small_sharded = [ffn_norm, gla_gate_w, conv_dw]
    rows_s = _rows_for([t.shape for t in small_sharded])
    got = allgather8(_pack(small_sharded, rows_s), "gather_small").reshape(N_CHIPS, 2, rows_s * LANES)[:, 0]
    per_chip = [_unpack(got[k], [t.shape for t in small_sharded]) for k in range(N_CHIPS)]
    ffn_norm_f, gate_w_f, conv_dw_f = [jnp.concatenate([per_chip[k][t] for k in range(N_CHIPS)], axis=-1)
                                       for t in range(3)]
    gate_w_p = jnp.pad(gate_w_f[0], ((0, LANES - 16), (0, 0)))
    conv_w = conv_dw_f[0]
    rb_p = jnp.pad(att_rel_bias[0], ((0, 0), (0, REL_PAD - att_rel_bias.shape[-1])))

    G = {}
    small_g = {}

    def ffn_w(i, j):
        return full['ffn_w_gate', (i, j)], full['ffn_w_up', (i, j)], full['ffn_w_down', (i, j)]

    saved = {}
    xs = x0
    arrive(0, tok)
    xs, saved['f00'] = ffn_fwd(xs, ffn_norm_f[0, 0][None], *ffn_w(0, 0), dep=tok)
    arrive(1, xs)
    w_in = jnp.transpose(full['ab_w_in', (0,)], (1, 0, 2)).reshape(D, AB_IN)
    w_main = jnp.concatenate([w_in[:, :gz0], w_in[:, gz0 + 16:]], axis=1)
    w_gz = jnp.pad(w_in[:, gz0:gz0 + 16], ((0, 0), (0, LANES - 16)))
    w_out = full['ab_w_out', (0,)].reshape(D, D)

    def mixer0_fwd(xin):
        h = rms_fwd(xin, mix_norm[0][None], "mix0_rms")
        zmain = mm_nn("ab_in", h, w_main, F32, 1024)
        gzp = mm_nn("ab_gz", h, w_gz, F32, LANES)
        ug = mm_nn("gla_gate", gzp, gate_w_p, F32, GLA_H * HK)
        o, S = gla_fwd(zmain, ug, gla_gate_b)
        a_out = rowwise("gla_post", _gla_post_fn, [o, ("cols", zmain, GLA_H * HV, 2)], [gla_norm_g], [(GLA_H * HV, BF)])
        u = rowwise("conv_glu", lambda a, b: a * _sigmoid(b), [("cols", zmain, 1024, 3), ("cols", zmain, 1024, 4)], [],
                    [(1024, F32)])
        yc = conv_fwd(u, conv_w, conv_dw_b)
        b_out = rowwise("conv_ln", _ln_silu_fn, [yc], [conv_ln_g, conv_ln_b], [(1024, BF)])
        cat = jnp.concatenate([a_out, b_out], axis=1)
        x2 = mm_nn("ab_out", cat, w_out, F32, 1024, res=xin)
        return x2, (xin, h, zmain, gzp, ug, o, S, u, yc, cat)

    xs, saved['m0'] = mixer0_fwd(xs)
    arrive(2, xs)
    xs, saved['f01'] = ffn_fwd(xs, ffn_norm_f[0, 1][None], *ffn_w(0, 1))
    ks = D // N_CHIPS
    dp = p.shape[-1]

    def pl_fwd(xin, i):
        h = rms_fwd(xin, pl_norm[i][None], "pl_rms")
        tm = min(T, 1024)
        u = mm_nn("pl_gate", h, full['pl_w_gate', (i,)].reshape(D, D), F32, 1024)
        e = mm("pl_proj", NN, (T // tm, N_CHIPS, 1), p[i, 0], (tm, dp), _mk, full['pl_w_proj', (i,)],
               (None, dp, ks), lambda m, n, k: (n, 0, 0), (tm, ks), _mn, _sds((T, D), F32))
        x2 = rowwise("pl_mix", lambda xv, uv, ev: xv + _sigmoid(uv) * ev, [xin, u, e], [], [(D, F32)])
        return x2, (xin, h, u, e)

    arrive(3, xs)
    xs, saved['p0'] = pl_fwd(xs, 0)
    arrive(4, xs)
    xs, saved['f10'] = ffn_fwd(xs, ffn_norm_f[1, 0][None], *ffn_w(1, 0))
    arrive(5, xs)
    w_qkv = full['att_w_qkv', (0,)]
    w_o = full['att_w_o', (0,)].reshape(D, D)
    qs = w_qkv.shape[-1]

    bm = relbias_tile(rb_p)

    def mixer1_fwd(xin):
        h = rms_fwd(xin, mix_norm[1][None], "mix1_rms")
        tm, tk, tn = min(T, 1024), 512, 512
        per = qs // tn
        qkv = mm("att_qkv", NN, (T // tm, 3 * D // tn, D // tk), h, (tm, tk), _mk, w_qkv, (None, tk, tn),
                 lambda m, n, k: (n // per, k, n % per), (tm, tn), _mn, _sds((T, 3 * D), BF))
        kvp = jnp.pad(qkv[:, D:], ((PADK, 0), (0, 0)))
        o, lse = attn_fwd(qkv, kvp, bm)
        x2 = mm_nn("att_o", o, w_o, F32, 1024, res=xin)
        return x2, (xin, h, qkv, kvp, o, lse)

    xs, saved['m1'] = mixer1_fwd(xs)
    arrive(6, xs)
    xs, saved['f11'] = ffn_fwd(xs, ffn_norm_f[1, 1][None], *ffn_w(1, 1))
    arrive(7, xs)
    xs, saved['p1'] = pl_fwd(xs, 1)

    dx, small_g['final_norm'], loss_acc = rowwise("loss_head", _final_fn, [xs, tgt], [final_norm[None]], [(D, F32)],
                                                  [(1, D), (1, LANES)])
    loss = lax.psum(loss_acc[0, 0], ("x", "y", "c"))

    def pl_bwd(dx2, sv, i, dep):
        xin, h, u, e = sv
        tm = min(T, 1024)

        def fn(d, uv, ev):
            s = _sigmoid(uv)
            return d * s, d * ev * s * (1.0 - s)

        de, du = rowwise("pl_mix_bwd", fn, [dx2, u, e], [], [(D, BF), (D, BF)], dep=dep)
        G['pl_w_proj', (i,)] = mm("pl_dproj", TN, (1, N_CHIPS, T // tm), p[i, 0], (tm, dp), _km, de, (tm, ks), _kn,
                                  (None, dp, ks), lambda m, n, k: (n, 0, 0), _sds((N_CHIPS, dp, ks), BF))
        G['pl_w_gate', (i,)] = mm_tn("pl_dgate", h, du, BF, ks, 1024).reshape(N_CHIPS, ks, D)
        dh = mm_nt("pl_dh", du, full['pl_w_gate', (i,)].reshape(D, D), F32, 1024)
        return rms_bwd(xin, pl_norm[i][None], dh, dx2, "pl_rms_bwd")

    def ffn_b(dx2, key, i, j, dep):
        dxn, dgain, G['ffn_w_gate', (i, j)], G['ffn_w_up', (i, j)], G['ffn_w_down', (i, j)] = ffn_bwd(
            dx2, saved[key], ffn_norm_f[i, j][None], *ffn_w(i, j), dep=dep)
        return dxn, dgain

    def mixer1_bwd(dx2, sv, dep):
        xin, h, qkv, kvp, o, lse = sv
        do = mm_nt("att_do", dx2, w_o, BF, 1024, dep=dep)
        dwo = mm_tn("att_dwo", o, dx2, BF, 1024, 1024, dep=dep)
        dq, dkp, dvp, dbm = attn_bwd(qkv, kvp, bm, o, lse, do)
        dqkv = jnp.concatenate([dq, dkp[PADK:].astype(BF), dvp[PADK:].astype(BF)], axis=1)
        tm, tn, tkk = min(T, 1024), 512, min(T, 1024)
        per = qs // tn
        dwqkv = mm("att_dwqkv", TN, (D // 1024, 3 * D // tn, T // tkk), h, (tkk, 1024), _km, dqkv, (tkk, tn), _kn,
                   (None, 1024, tn), lambda m, n, k: (n // per, m, n % per), _sds((N_CHIPS, D, qs), BF))
        dh = mm("att_dh", NT, (T // tm, D // 1024, 3 * D // tn), dqkv, (tm, tn), _mk, w_qkv, (None, 1024, tn),
                lambda m, n, k: (k // per, n, k % per), (tm, 1024), _mn, _sds((T, D), F32))
        dxn, dgain = rms_bwd(xin, mix_norm[1][None], dh, dx2, "mix1_rms_bwd")
        return dxn, dgain, dwo, dwqkv, relbias_reduce(dbm)

    def mixer0_bwd(dx2, sv, dep):
        xin, h, zmain, gzp, ug, o, S, u, yc, cat = sv
        dcat = mm_nt("ab_dcat", dx2, w_out, F32, 1024, dep=dep)
        dwout = mm_tn("ab_dwout", cat, dx2, BF, 1024, 1024, dep=dep)
        do, dr, dgn = rowwise("gla_post_bwd", _gla_post_bwd_fn,
                              [("cols", dcat, GLA_H * HV, 0), o, ("cols", zmain, GLA_H * HV, 2)], [gla_norm_g],
                              [(GLA_H * HV, F32), (GLA_H * HV, BF)], [(1, HV)])
        dyc, dlg, dlb = rowwise("conv_ln_bwd", _ln_silu_bwd_fn, [("cols", dcat, 1024, 1), yc], [conv_ln_g, conv_ln_b],
                                [(1024, F32)], [(1, 1024), (1, 1024)])
        du, ddw, ddwb = conv_bwd(dyc, u, conv_w)
        dca, dcb = rowwise("conv_glu_bwd", _glu_bwd_fn, [du, ("cols", zmain, 1024, 3), ("cols", zmain, 1024, 4)], [],
                           [(1024, BF), (1024, BF)])
        dq, dk, dv, dug, dgb = gla_bwd(zmain, ug, gla_gate_b, S, do)
        dgw = mm_tn("gla_dgate_w", gzp, dug, F32, LANES, GLA_H * HK)
        dgzp = mm_nt("gla_dgz", dug, gate_w_p, F32, LANES)
        dzm = jnp.concatenate([dq, dk, dv, dr, dca, dcb], axis=1)
        dwmain = mm_tn("ab_dwmain", h, dzm, BF, 1024, 1024)
        dwgz = mm_tn("ab_dwgz", h, dgzp, BF, 1024, LANES)
        dh = mm_nt("ab_dh_gz", dgzp, w_gz, F32, 1024, res=mm_nt("ab_dh", dzm, w_main, F32, 1024))
        dxn, dgain = rms_bwd(xin, mix_norm[0][None], dh, dx2, "mix0_rms_bwd")
        dwin = jnp.concatenate([dwmain[:, :gz0], dwgz[:, :16], dwmain[:, gz0:]], axis=1)
        g_win = jnp.transpose(dwin.reshape(D, N_CHIPS, ws), (1, 0, 2))
        return dxn, dgain, g_win, dwout, (dgn, dlg, dlb, ddw, ddwb, dgw[:16], dgb)

    scattering = []

    def scatter(s, tok):
        tags = [n + "".join(str(i) for i in lead) for n, lead in order[s]]
        hs = [reduce_chip(G[key], sidx, tag) for key, tag in zip(order[s], tags)]
        land = [lax.empty((3,) + h.shape[1:], h.dtype) for h in hs]
        ssem, rsem, thru, tok = split_start("scatter_start_%d" % s, hs + land, _scatter_copies, 3 * len(hs), tok)
        scattering.append((s, tags, ssem, rsem, thru))
        return tok

    dpl, dffn, dmix = [None, None], [[None, None], [None, None]], [None, None]
    tok = jnp.zeros(TOKEN, F32)
    dx, dpl[1] = pl_bwd(dx, saved['p1'], 1, tok)
    tok = scatter(7, tok)
    dx, dffn[1][1] = ffn_b(dx, 'f11', 1, 1, tok)
    tok = scatter(6, tok)
    dx, dmix[1], dwo, dwqkv, drb = mixer1_bwd(dx, saved['m1'], tok)
    G['att_w_qkv', (0,)] = dwqkv
    G['att_w_o', (0,)] = dwo.reshape((N_CHIPS,) + att_w_o.shape[1:])
    tok = scatter(5, tok)
    dx, dffn[1][0] = ffn_b(dx, 'f10', 1, 0, tok)
    tok = scatter(4, tok)
    dx, dpl[0] = pl_bwd(dx, saved['p0'], 0, tok)
    tok = scatter(3, tok)
    dx, dffn[0][1] = ffn_b(dx, 'f01', 0, 1, tok)
    tok = scatter(2, tok)
    dx, dmix[0], g_win, dwout, (dgn, dlg, dlb, ddw, ddwb, dgw, dgb) = mixer0_bwd(dx, saved['m0'], tok)
    G['ab_w_in', (0,)] = g_win
    G['ab_w_out', (0,)] = dwout.reshape((N_CHIPS,) + ab_w_out.shape[1:])
    tok = scatter(1, tok)
    dx, dffn[0][0] = ffn_b(dx, 'f00', 0, 0, tok)
    tok = scatter(0, tok)
    grad_x = dx[None]

    grads, big_g = {}, {}
    after = dx
    for s, tags, ssem, rsem, thru in scattering:
        n_u = len(tags)
        landed = split_wait("scatter_wait_%d" % s, ssem, rsem, thru, _scatter_copies, after)
        for key, tag, h, got2 in zip(order[s], tags, landed[:n_u], landed[n_u:]):
            big_g[key] = reduce_finish(h, got2, sidx, tag)
        after = big_g[order[s][-1]]

    small_g['ffn_norm'] = jnp.stack([jnp.stack([dffn[i][j][0] for j in range(2)]) for i in range(2)])
    small_g['mix_norm'] = jnp.concatenate(dmix, axis=0)
    small_g['gla_gate_w'] = dgw[None]
    small_g['gla_gate_b'] = dgb
    small_g['gla_norm_g'] = dgn
    small_g['conv_dw'] = ddw[None]
    small_g['conv_dw_b'] = ddwb
    small_g['conv_ln_g'] = dlg
    small_g['conv_ln_b'] = dlb
    small_g['att_rel_bias'] = drb[None, :, :att_rel_bias.shape[-1]]
    small_g['pl_norm'] = jnp.concatenate(dpl, axis=0)
    small_g['final_norm'] = small_g['final_norm'][0]
    small_names = [n for n in WEIGHTS if n not in BIG]
    small_shapes = [small_g[n].shape for n in small_names]
    rows_g = _rows_for(small_shapes)
    allp = allgather8(_pack([small_g[n] for n in small_names], rows_g), "gather_small_grads")
    summed = rowwise("sum_small_grads", lambda *v: (((v[0] + v[1]) + (v[2] + v[3])) + ((v[4] + v[5]) + (v[6] + v[7]))),
                     [("leads", allp.reshape(8, rows_g, LANES), (d,)) for d in range(8)], [], [(LANES, F32)], tb=rows_g)
    for n, g in zip(small_names, _unpack(summed.reshape(-1), small_shapes)):
        grads[n] = g
    chip = 2 * xc + yc_
    for n, axis in (('ffn_norm', 2), ('gla_gate_w', 2), ('conv_dw', 2)):
        width = W[n].shape[axis]
        grads[n] = lax.dynamic_slice_in_dim(grads[n], chip * width, width, axis)

    outs = {n: adamw(W[n], grads[n], M[n], V[n], "adamw_" + n) for n in small_names}
    for n in BIG:
        outs[n] = adamw_units(W[n], M[n], V[n], {lead: big_g[n, lead] for lead in _leads(W[n].shape)}, "adamw_" + n)
    return (loss, grad_x, *[outs[n][0] for n in WEIGHTS], *[outs[n][1] for n in WEIGHTS],
            *[outs[n][2] for n in WEIGHTS], *[outs[n][3] for n in WEIGHTS])
```

```python
import math

import jax
import jax.numpy as jnp
from jax import lax
from jax.experimental import pallas as pl
from jax.experimental.pallas import tpu as pltpu

F32 = jnp.float32
BF = jnp.bfloat16
MESH = pl.DeviceIdType.MESH
HI = lax.Precision.HIGHEST
V7X_VMEM_LIMIT = 56 * 1024 * 1024
LANES = 128
EPS = 1e-6
NEG = -1e30

CHUNK = 64
LEFT_CHUNKS = 8
QB = 256
KW = QB + LEFT_CHUNKS * CHUNK
PADK = LEFT_CHUNKS * CHUNK
REL_CLIP = 128
REL_PAD = 384
ATT_H = 16
HD = 128
GLA_H = 4
HK = 128
HV = 256
GLA_TAU = 16.0
CONV_W = 31
CONV_PAD = 32
N_CHIPS = 4

ADAM_LR = 0.001
ADAM_B1 = 0.9
ADAM_B2 = 0.999
ADAM_EPS = 1e-08
ADAM_WD = 0.01
ADAM_STEP = 10

NN = ((1,), (0,))
NT = ((1,), (1,))
TN = ((0,), (0,))


def _dot(a, b, dims, prec=None):
    return lax.dot_general(a, b, (dims, ((), ())), preferred_element_type=F32, precision=prec)


def _bdot(a, b, dims):
    return _dot(a.astype(BF), b.astype(BF), dims)


def _cp(*sem):
    return pltpu.CompilerParams(dimension_semantics=sem if sem else None, vmem_limit_bytes=V7X_VMEM_LIMIT)


def _sds(shape, dtype):
    return jax.ShapeDtypeStruct(shape, dtype)


def _sigmoid(x):
    return 1.0 / (1.0 + jnp.exp(-x))


def _silu(x):
    return x * _sigmoid(x)


def _dsilu(x):
    s = _sigmoid(x)
    return s * (1.0 + x * (1.0 - s))


def _f(x):
    return x.astype(F32)


def _row_item(item, tb):
    if not isinstance(item, tuple):
        return item, (tb, item.shape[1]), lambda i, s: (i, 0)
    kind, arr = item[0], item[1]
    if kind == "cols":
        return arr, (tb, item[2]), lambda i, s, blk=item[3]: (i, blk)
    if kind == "leads":
        lead = tuple(item[2])
        return arr, (None,) * len(lead) + (tb, arr.shape[-1]), lambda i, s: lead + (i, 0)
    if kind == "dyn":
        return arr, (None, tb, arr.shape[-1]), lambda i, s, sel=item[2]: (s[sel], i, 0)
    if kind == "dyn4":
        nb = arr.shape[2] // tb
        return arr, (None, None, tb, arr.shape[-1]), lambda i, s, sel=item[2]: (i // nb, s[sel], i % nb, 0)
    raise ValueError(kind)


def _row_block(n, cap):
    for cand in range(min(cap, n) // 16 * 16, 0, -16):
        if n % cand == 0:
            return cand
    return n


def rowwise(name, fn, rows, bcast, outs, accs=(), tb=256, n_rows=None, sidx=None, row_period=None, dep=None):
    if n_rows is None:
        first = rows[0][1] if isinstance(rows[0], tuple) else rows[0]
        n_rows = first.shape[-2]
    tb = _row_block(n_rows if row_period is None else row_period, tb)
    items = [_row_item(it, tb) for it in rows]
    n_r, n_b, n_o = len(rows), len(bcast), len(outs)
    pre = 0 if sidx is None else 1

    def wrap(f):
        return (lambda i: f(i, None)) if sidx is None else (lambda i, s: f(i, s))

    def body(*refs):
        refs = refs[pre:]
        r, b = refs[:n_r], refs[n_r:n_r + n_b]
        refs = refs[n_r + n_b + (dep is not None):]
        o, a = refs[:n_o], refs[n_o:]
        res = fn(*[v[...] for v in r], *[v[...] for v in b])
        if not isinstance(res, tuple):
            res = (res,)
        for ref, val in zip(o, res[:n_o]):
            ref[...] = val.astype(ref.dtype)
        if a:
            @pl.when(pl.program_id(0) == 0)
            def _():
                for ref in a:
                    ref[...] = jnp.zeros_like(ref)
            for ref, val in zip(a, res[n_o:]):
                ref[...] += val

    in_specs = [pl.BlockSpec(bs, wrap(f)) for _, bs, f in items]
    in_specs += [pl.BlockSpec(v.shape, wrap(lambda i, s, nd=v.ndim: (0,) * nd)) for v in bcast]
    out_specs, out_shape = [], []
    for o in outs:
        if o[0] == "dyn":
            _, L, c, dt, sel = o
            out_specs.append(pl.BlockSpec((None, tb, c), wrap(lambda i, s, sel=sel: (s[sel], i, 0))))
            out_shape.append(_sds((L, n_rows, c), dt))
        else:
            c, dt = o
            out_specs.append(pl.BlockSpec((tb, c), wrap(lambda i, s: (i, 0))))
            out_shape.append(_sds((n_rows, c), dt))
    out_specs += [pl.BlockSpec(sh, wrap(lambda i, s: (0, 0))) for sh in accs]
    out_shape += [_sds(sh, F32) for sh in accs]
    operands = [a for a, _, _ in items] + list(bcast)
    if dep is not None:
        operands.append(dep)
        in_specs.append(pl.BlockSpec(TOKEN, wrap(lambda i, s: (0, 0))))
    grid = (n_rows // tb,)
    if sidx is None:
        res = pl.pallas_call(body, grid=grid, in_specs=in_specs, out_specs=out_specs, out_shape=out_shape, name=name,
                             compiler_params=_cp("arbitrary"))(*operands)
    else:
        spec = pltpu.PrefetchScalarGridSpec(num_scalar_prefetch=1, grid=grid, in_specs=in_specs, out_specs=out_specs)
        res = pl.pallas_call(body, grid_spec=spec, out_shape=out_shape, name=name,
                             compiler_params=_cp("arbitrary"))(sidx, *operands)
    return res[0] if len(res) == 1 else tuple(res)


TOKEN = (8, LANES)


def mm(name, dims, grid, a, a_bs, a_im, b, b_bs, b_im, o_bs, o_im, out, scale=1.0, res=None, dep=None):
    nk = grid[2]
    acc_shape = tuple(d for d in o_bs if d is not None)

    def body(*refs):
        a_ref, b_ref = refs[0], refs[1]
        pos = 2
        res_ref = None
        if res is not None:
            res_ref = refs[pos]
            pos += 1
        if dep is not None:
            pos += 1
        o_ref = refs[pos]
        part = _bdot(a_ref[...], b_ref[...], dims)

        def finish(acc):
            v = acc * scale if scale != 1.0 else acc
            if res_ref is not None:
                v = v + _f(res_ref[...])
            o_ref[...] = v.astype(o_ref.dtype)

        if nk == 1:
            finish(part)
        else:
            acc_ref = refs[pos + 1]
            k = pl.program_id(2)

            @pl.when(k == 0)
            def _():
                acc_ref[...] = part

            @pl.when(k > 0)
            def _():
                acc_ref[...] += part

            @pl.when(k == nk - 1)
            def _():
                finish(acc_ref[...])

    operands = [a, b]
    in_specs = [pl.BlockSpec(a_bs, a_im), pl.BlockSpec(b_bs, b_im)]
    if res is not None:
        operands.append(res)
        in_specs.append(pl.BlockSpec(o_bs, o_im))
    if dep is not None:
        operands.append(dep)
        in_specs.append(pl.BlockSpec(TOKEN, lambda m, n, k: (0, 0)))
    scratch = [pltpu.VMEM(acc_shape, F32)] if nk > 1 else []
    return pl.pallas_call(body, grid=grid, in_specs=in_specs, out_specs=pl.BlockSpec(o_bs, o_im), out_shape=out,
                          scratch_shapes=scratch, name=name,
                          compiler_params=_cp("parallel", "parallel", "arbitrary"))(*operands)


def _mk(m, n, k):
    return (m, k)


def _mn(m, n, k):
    return (m, n)


def _km(m, n, k):
    return (k, m)


def _kn(m, n, k):
    return (k, n)


def _nk(m, n, k):
    return (n, k)


def mm_nn(name, a, b, out_dtype, tn, tk=512, scale=1.0, res=None):
    T, K = a.shape
    N = b.shape[1]
    tm, tk, tn = min(T, 1024), min(tk, K), min(tn, N)
    return mm(name, NN, (T // tm, N // tn, K // tk), a, (tm, tk), _mk, b, (tk, tn), _kn, (tm, tn), _mn,
              _sds((T, N), out_dtype), scale=scale, res=res)


def mm_nt(name, a, b, out_dtype, tn, tk=512, scale=1.0, res=None, dep=None):
    T, K = a.shape
    N = b.shape[0]
    tm, tk, tn = min(T, 1024), min(tk, K), min(tn, N)
    return mm(name, NT, (T // tm, N // tn, K // tk), a, (tm, tk), _mk, b, (tn, tk), _nk, (tm, tn), _mn,
              _sds((T, N), out_dtype), scale=scale, res=res, dep=dep)


def mm_tn(name, a, b, out_dtype, tm, tn, scale=1.0, dep=None):
    T, M = a.shape
    N = b.shape[1]
    tk, tm, tn = min(T, 1024), min(tm, M), min(tn, N)
    return mm(name, TN, (M // tm, N // tn, T // tk), a, (tk, tm), _km, b, (tk, tn), _kn, (tm, tn), _mn,
              _sds((M, N), out_dtype), scale=scale, dep=dep)


HBM_SPEC = pl.BlockSpec(memory_space=pl.ANY)


def _place():
    x, y, c = lax.axis_index("x"), lax.axis_index("y"), lax.axis_index("c")
    chips = [(1 - x, y), (x, 1 - y), (1 - x, 1 - y)]
    return x, y, c, chips


def allgather8(v, name):
    m_per, n = v.shape

    def body(x_ref, out_ref, send_sems, recv_sems, local_sem):
        x, y, c, chips = _place()
        me, sibling = (x, y, c), (x, y, 1 - c)

        def rows(px, py, pc):
            return out_ref.at[pl.ds((4 * px + 2 * py + pc) * m_per, m_per), :]

        def copy(k, block, to, src=None):
            return pltpu.make_async_remote_copy(
                src_ref=rows(*block) if src is None else src, dst_ref=rows(*block),
                send_sem=send_sems.at[k], recv_sem=recv_sems.at[k], device_id=to, device_id_type=MESH)

        mine = pltpu.make_async_copy(x_ref, rows(*me), local_sem)
        mine.start()
        first = [copy(0, me, sibling, src=x_ref)]
        first += [copy(1 + j, me, (*chip, c), src=x_ref) for j, chip in enumerate(chips)]
        for cp in first:
            cp.start()
        passed = [copy(4 + j, (*chip, c), sibling) for j, chip in enumerate(chips)]
        for j, chip in enumerate(chips):
            copy(1 + j, (*chip, c), me).wait_recv()
            passed[j].start()
        copy(0, sibling, me).wait_recv()
        for j, chip in enumerate(chips):
            copy(4 + j, (*chip, 1 - c), me).wait_recv()
        for cp in first + passed:
            cp.wait_send()
        mine.wait()

    return pl.pallas_call(
        body, out_shape=_sds((8 * m_per, n), v.dtype), in_specs=[pl.BlockSpec(memory_space=pltpu.VMEM)],
        out_specs=pl.BlockSpec(memory_space=pltpu.VMEM), name=name,
        scratch_shapes=[pltpu.SemaphoreType.DMA((7,)), pltpu.SemaphoreType.DMA((7,)), pltpu.SemaphoreType.DMA],
    )(v)


D2D_PIECES = 4


def _pieces(ref, n):
    rows = ref.shape[0] // n
    return [ref.at[pl.ds(q * rows, rows)] for q in range(n)]


def _start_in_pieces(src, dst, ssem, rsem, to, n):
    for s_q, d_q in zip(_pieces(src, n), _pieces(dst, n)):
        pltpu.make_async_remote_copy(src_ref=s_q, dst_ref=d_q, send_sem=ssem, recv_sem=rsem, device_id=to,
                                     device_id_type=MESH).start()


def _whole(src, dst, ssem, rsem, to):
    return pltpu.make_async_remote_copy(src_ref=src, dst_ref=dst, send_sem=ssem, recv_sem=rsem, device_id=to,
                                        device_id_type=MESH)


HBM_ONLY = pl.BlockSpec(memory_space=pltpu.HBM)
SEM_SPEC = pl.BlockSpec(memory_space=pltpu.SEMAPHORE)
DATAFLOW = pltpu.SideEffectType.DATAFLOW_SIDE_EFFECTING


def _gather_copies(bufs):
    x, y, c, chips = _place()
    out = []
    for o_ref in bufs:
        hr = o_ref.shape[1] // 2

        def half(chip, o_ref=o_ref, hr=hr):
            return o_ref.at[2 * chip[0] + chip[1], pl.ds(c * hr, hr)]

        out += [(half((x, y)), half((x, y)), (*chip, c), half(chip)) for chip in chips]
    return out


def _scatter_copies(arrays):
    x, y, c, chips = _place()
    n = len(arrays) // 2
    out = []
    for h_ref, got_ref in zip(arrays[:n], arrays[n:]):
        out += [(h_ref.at[2 * chip[0] + chip[1]], got_ref.at[j], (*chip, c), got_ref.at[j])
                for j, chip in enumerate(chips)]
    return out


def split_start(name, arrays, copies_of, n_copies, after):
    n = len(arrays)

    def body(*refs):
        ssem, rsem = refs[n + 1], refs[n + 2]
        for i, (src, dst, to, _) in enumerate(copies_of(refs[n + 3:2 * n + 3])):
            _whole(src, dst, ssem.at[i], rsem.at[i], to).start()
        refs[2 * n + 3][...] = jnp.zeros(TOKEN, F32)

    res = pl.pallas_call(
        body, name=name,
        out_shape=(pltpu.SemaphoreType.DMA((n_copies,)), pltpu.SemaphoreType.DMA((n_copies,)),
                   *[pltpu.HBM(a.shape, a.dtype) for a in arrays], _sds(TOKEN, F32)),
        in_specs=[HBM_ONLY] * n + [pl.BlockSpec(memory_space=pl.ANY)],
        out_specs=(SEM_SPEC, SEM_SPEC, *[HBM_ONLY] * n, pl.BlockSpec(memory_space=pltpu.VMEM)),
        input_output_aliases={i: 2 + i for i in range(n)},
        compiler_params=pltpu.CompilerParams(has_side_effects=DATAFLOW),
    )(*[pltpu.with_memory_space_constraint(a, pltpu.HBM) for a in arrays], after)
    return res[0], res[1], list(res[2:2 + n]), res[2 + n]


def split_wait(name, ssem, rsem, arrays, copies_of, after):
    n = len(arrays)

    def body(*refs):
        s_ref, r_ref = refs[n], refs[n + 1]
        for i, (src, _, to, mine) in enumerate(copies_of(refs[:n])):
            cp = _whole(src, mine, s_ref.at[i], r_ref.at[i], to)
            cp.wait_send()
            cp.wait_recv()

    res = pl.pallas_call(
        body, name=name, out_shape=tuple(pltpu.HBM(a.shape, a.dtype) for a in arrays),
        in_specs=[HBM_ONLY] * n + [SEM_SPEC, SEM_SPEC, pl.BlockSpec(memory_space=pl.ANY)],
        out_specs=tuple([HBM_ONLY] * n), input_output_aliases={i: i for i in range(n)},
        compiler_params=pltpu.CompilerParams(has_side_effects=DATAFLOW),
    )(*arrays, ssem, rsem, after)
    return list(res)


def gather_pass_on(bufs, name):
    n = len(bufs)

    def body(*refs):
        o_refs, ssem, rsem = refs[n:2 * n], refs[2 * n], refs[2 * n + 1]
        x, y, c, chips = _place()
        sib = (x, y, 1 - c)

        def half(o_ref, chip, h):
            hr = o_ref.shape[1] // 2
            return o_ref.at[2 * chip[0] + chip[1], pl.ds(h * hr, hr)]

        for u, o_ref in enumerate(o_refs):
            for j, chip in enumerate(chips):
                _start_in_pieces(half(o_ref, chip, c), half(o_ref, chip, c), ssem.at[3 * u + j], rsem.at[3 * u + j], sib,
                                 D2D_PIECES)
        for u, o_ref in enumerate(o_refs):
            for j, chip in enumerate(chips):
                _whole(half(o_ref, chip, c), half(o_ref, chip, 1 - c), ssem.at[3 * u + j], rsem.at[3 * u + j], sib).wait()

    res = pl.pallas_call(
        body, out_shape=tuple(_sds(b.shape, b.dtype) for b in bufs), in_specs=[HBM_SPEC] * n,
        out_specs=tuple([HBM_SPEC] * n), name=name, input_output_aliases={i: i for i in range(n)},
        scratch_shapes=[pltpu.SemaphoreType.DMA((3 * n,)), pltpu.SemaphoreType.DMA((3 * n,))],
    )(*bufs)
    return list(res)


def _sibling_copies(arrays):
    x, y, c, _ = _place()
    n = len(arrays) // 2
    out = []
    for g_ref, got_ref in zip(arrays[:n], arrays[n:]):
        out += [(g_ref.at[k, 1 - c], got_ref.at[k], (x, y, 1 - c), got_ref.at[k]) for k in range(N_CHIPS)]
    return out


def _share_copies(arrays):
    x, y, c, _ = _place()
    return [(r_ref.at[c], r_ref.at[c], (x, y, 1 - c), r_ref.at[1 - c]) for r_ref in arrays]


def add_sibling(g4, got, sidx, tag):
    _, _, hr, B = g4.shape
    h = rowwise("rs_add2_" + tag, lambda a, b: _f(a) + _f(b), [("dyn4", g4, 1), got.reshape(N_CHIPS * hr, B)], [],
                [(B, BF)], tb=512, n_rows=N_CHIPS * hr, sidx=sidx, row_period=hr)
    return h.reshape(N_CHIPS, hr, B)


def add_chips(h, got2, sidx, tag):
    _, hr, B = h.shape
    return rowwise("rs_add4_" + tag, lambda a, b, c, d: ((_f(a) + _f(b)) + _f(c)) + _f(d),
                   [("dyn", h, 0), ("leads", got2, (0,)), ("leads", got2, (1,)), ("leads", got2, (2,))], [],
                   [("dyn", 2, B, F32, 1)], tb=512, n_rows=hr, sidx=sidx)


def _rms_parts(x):
    rstd = lax.rsqrt(jnp.mean(x * x, axis=-1, keepdims=True) + EPS)
    return x * rstd, rstd


def rms_fwd(x, g, name, dep=None):
    return rowwise(name, lambda xv, gv: _rms_parts(xv)[0] * gv, [x], [g], [(x.shape[1], BF)], dep=dep)


def _rms_bwd_fn(x, dh, dres, g):
    xh, rstd = _rms_parts(x)
    dxh = _f(dh) * g
    dx = rstd * (dxh - xh * jnp.mean(dxh * xh, axis=-1, keepdims=True)) + dres
    return dx, jnp.sum(_f(dh) * xh, axis=0, keepdims=True)


def rms_bwd(x, g, dh, dres, name):
    D = x.shape[1]
    return rowwise(name, _rms_bwd_fn, [x, dh, dres], [g], [(D, F32)], [(1, D)])


def cast_unit(w, lead, sidx, name):
    R, B = w.shape[-2:]
    return rowwise(name, lambda v: v, [("leads", w, lead)], [], [("dyn", N_CHIPS, B, BF, 0)], n_rows=R, sidx=sidx)


def ffn_fwd(xin, gain, wg, wu, wd, dep=None):
    T, D = xin.shape
    fs = wg.shape[-1]
    F = N_CHIPS * fs
    tm, tk = min(T, 1024), 512
    h = rms_fwd(xin, gain, "ffn_rms", dep)

    def up(wt, nm):
        return mm(nm, NN, (T // tm, N_CHIPS, D // tk), h, (tm, tk), _mk, wt, (None, tk, fs),
                  lambda m, n, k: (n, k, 0), (tm, fs), _mn, _sds((T, F), BF))

    zg, zu = up(wg, "ffn_gate"), up(wu, "ffn_up")
    a = rowwise("ffn_swiglu", lambda g, u: _silu(_f(g)) * _f(u), [zg, zu], [], [(F, BF)])
    x2 = mm_nn("ffn_down", a, wd.reshape(F, D), F32, 1024, tk=fs, scale=0.5, res=xin)
    return x2, (xin, h, zg, zu, a)


def _swiglu_bwd_fn(da, zg, zu):
    da, zg, zu = _f(da), _f(zg), _f(zu)
    return da * zu * _dsilu(zg), da * _silu(zg)


def ffn_bwd(dx2, saved, gain, wg, wu, wd, dep=None):
    xin, h, zg, zu, a = saved
    T, D = xin.shape
    fs = wg.shape[-1]
    F = N_CHIPS * fs
    tm, tn, tkk = min(T, 1024), 1024, min(T, 1024)
    da = mm_nt("ffn_da", dx2, wd.reshape(F, D), BF, fs, scale=0.5, dep=dep)
    g_wd = mm_tn("ffn_dwd", a, dx2, BF, fs, 1024, scale=0.5, dep=dep).reshape(N_CHIPS, fs, D)
    dzg, dzu = rowwise("ffn_dswiglu", _swiglu_bwd_fn, [da, zg, zu], [], [(F, BF), (F, BF)])

    def dw(dz, nm):
        return mm(nm, TN, (D // tn, N_CHIPS, T // tkk), h, (tkk, tn), _km, dz, (tkk, fs), _kn,
                  (None, tn, fs), lambda m, n, k: (n, m, 0), _sds((N_CHIPS, D, fs), BF))

    g_wg, g_wu = dw(dzg, "ffn_dwg"), dw(dzu, "ffn_dwu")

    def dh_part(dz, wt, nm, res):
        return mm(nm, NT, (T // tm, D // tn, N_CHIPS), dz, (tm, fs), _mk, wt, (None, tn, fs),
                  lambda m, n, k: (k, n, 0), (tm, tn), _mn, _sds((T, D), F32), res=res)

    dh = dh_part(dzu, wu, "ffn_dh_u", dh_part(dzg, wg, "ffn_dh_g", None))
    dx, dgain = rms_bwd(xin, gain, dh, dx2, "ffn_rms_bwd")
    return dx, dgain, g_wg, g_wu, g_wd


def _gla_chunk(q_ref, k_ref, v_ref, u_ref, b_ref, rows):
    r = lax.broadcasted_iota(jnp.int32, (CHUNK, CHUNK), 0)
    c = lax.broadcasted_iota(jnp.int32, (CHUNK, CHUNK), 1)
    causal = c <= r
    u = u_ref[rows, :] + b_ref[...]
    g = (jnp.minimum(u, 0.0) - jnp.log(1.0 + jnp.exp(-jnp.abs(u)))) * (1.0 / GLA_TAU)
    b = _dot(causal.astype(F32), g, NN, HI)
    last = lax.broadcasted_iota(jnp.int32, (CHUNK, HK), 0) == CHUNK - 1
    blast = jnp.sum(jnp.where(last, b, 0.0), axis=0, keepdims=True)
    eb = jnp.exp(b)
    qb = q_ref[rows, :] * (HK ** -0.5) * eb
    k = k_ref[rows, :]
    kb = k * jnp.exp(-b)
    kl = k * jnp.exp(blast - b)
    A = jnp.where(causal, _bdot(qb, kb, NT), 0.0)
    return causal, u, b, blast, eb, qb, kb, kl, A


def _gla_in_specs(T):
    return [pl.BlockSpec((T, HK), lambda h: (0, h)), pl.BlockSpec((T, HK), lambda h: (0, GLA_H + h)),
            pl.BlockSpec((T, HV), lambda h: (0, GLA_H + h)), pl.BlockSpec((T, HK), lambda h: (0, h)),
            pl.BlockSpec((1, HK), lambda h: (0, h))]


def gla_fwd(zmain, ug, gate_b):
    T = zmain.shape[0]
    nC = T // CHUNK

    def body(q_ref, k_ref, v_ref, u_ref, b_ref, o_ref, s_ref, st_ref):
        st_ref[...] = jnp.zeros_like(st_ref)

        def step(n, carry):
            rows = pl.ds(pl.multiple_of(n * CHUNK, CHUNK), CHUNK)
            _, _, _, blast, _, qb, _, kl, A = _gla_chunk(q_ref, k_ref, v_ref, u_ref, b_ref, rows)
            v = v_ref[rows, :]
            ST = st_ref[...]
            s_ref[n] = ST
            o_ref[rows, :] = _bdot(qb, ST, NT) + _bdot(A, v, NN)
            st_ref[...] = ST * jnp.exp(blast) + _bdot(v, kl, TN)
            return carry

        lax.fori_loop(0, nC, step, 0)

    return pl.pallas_call(
        body, grid=(GLA_H,), in_specs=_gla_in_specs(T),
        out_specs=[pl.BlockSpec((T, HV), lambda h: (0, h)), pl.BlockSpec((nC, None, HV, HK), lambda h: (0, h, 0, 0))],
        out_shape=[_sds((T, GLA_H * HV), F32), _sds((nC, GLA_H, HV, HK), F32)],
        scratch_shapes=[pltpu.VMEM((HV, HK), F32)], name="gla_fwd", compiler_params=_cp("parallel"),
    )(zmain, zmain, zmain, ug, gate_b)


def gla_bwd(zmain, ug, gate_b, S, do):
    T = zmain.shape[0]
    nC = T // CHUNK

    def body(q_ref, k_ref, v_ref, u_ref, b_ref, s_ref, do_ref, dq_ref, dk_ref, dv_ref, du_ref, dgb_ref, dst_ref):
        dst_ref[...] = jnp.zeros_like(dst_ref)
        dgb_ref[...] = jnp.zeros_like(dgb_ref)

        def step(it, carry):
            n = nC - 1 - it
            rows = pl.ds(pl.multiple_of(n * CHUNK, CHUNK), CHUNK)
            causal, u, b, blast, eb, qb, kb, kl, A = _gla_chunk(q_ref, k_ref, v_ref, u_ref, b_ref, rows)
            v = v_ref[rows, :]
            dout = do_ref[rows, :]
            ST = s_ref[n]
            dST = dst_ref[...]
            elast = jnp.exp(blast)
            dA = jnp.where(causal, _bdot(dout, v, NT), 0.0)
            dv_ref[rows, :] = (_bdot(A, dout, TN) + _bdot(kl, dST, NT)).astype(dv_ref.dtype)
            dqb = _bdot(dout, ST, NN) + _bdot(dA, kb, NN)
            dkb = _bdot(dA, qb, TN)
            dkl = _bdot(v, dST, NN)
            ddec = jnp.sum(ST * dST, axis=0, keepdims=True)
            dst_ref[...] = dST * elast + _bdot(dout, qb, TN)
            dq_ref[rows, :] = (dqb * eb * (HK ** -0.5)).astype(dq_ref.dtype)
            dk_ref[rows, :] = (dkb * jnp.exp(-b) + dkl * jnp.exp(blast - b)).astype(dk_ref.dtype)
            db = dqb * qb - dkb * kb - dkl * kl
            dbl = jnp.sum(dkl * kl, axis=0, keepdims=True) + elast * ddec
            dg = _dot(jnp.logical_not(causal).astype(F32) + jnp.where(
                lax.broadcasted_iota(jnp.int32, (CHUNK, CHUNK), 0) == lax.broadcasted_iota(jnp.int32, (CHUNK, CHUNK), 1),
                1.0, 0.0), db, NN, HI) + dbl
            du = dg * (1.0 / GLA_TAU) / (1.0 + jnp.exp(u))
            du_ref[rows, :] = du
            dgb_ref[...] += jnp.sum(du, axis=0, keepdims=True)
            return carry

        lax.fori_loop(0, nC, step, 0)

    specs = _gla_in_specs(T) + [pl.BlockSpec((nC, None, HV, HK), lambda h: (0, h, 0, 0)),
                                pl.BlockSpec((T, HV), lambda h: (0, h))]
    return pl.pallas_call(
        body, grid=(GLA_H,), in_specs=specs,
        out_specs=[pl.BlockSpec((T, HK), lambda h: (0, h)), pl.BlockSpec((T, HK), lambda h: (0, h)),
                   pl.BlockSpec((T, HV), lambda h: (0, h)), pl.BlockSpec((T, HK), lambda h: (0, h)),
                   pl.BlockSpec((1, HK), lambda h: (0, h))],
        out_shape=[_sds((T, GLA_H * HK), BF), _sds((T, GLA_H * HK), BF), _sds((T, GLA_H * HV), BF),
                   _sds((T, GLA_H * HK), F32), _sds((1, GLA_H * HK), F32)],
        scratch_shapes=[pltpu.VMEM((HV, HK), F32)], name="gla_bwd", compiler_params=_cp("parallel"),
    )(zmain, zmain, zmain, ug, gate_b, S, do)


def _gla_post_fn(o, r, g):
    outs = []
    for h in range(GLA_H):
        on = _rms_parts(o[:, h * HV:(h + 1) * HV])[0] * g
        outs.append(on * _silu(r[:, h * HV:(h + 1) * HV]))
    return jnp.concatenate(outs, axis=1)


def _gla_post_bwd_fn(da, o, r, g):
    dos, drs = [], []
    dg = jnp.zeros((1, HV), F32)
    for h in range(GLA_H):
        sl = slice(h * HV, (h + 1) * HV)
        xh, rstd = _rms_parts(o[:, sl])
        drs.append(da[:, sl] * xh * g * _dsilu(r[:, sl]))
        don = da[:, sl] * _silu(r[:, sl])
        dg = dg + jnp.sum(don * xh, axis=0, keepdims=True)
        dxh = don * g
        dos.append(rstd * (dxh - xh * jnp.mean(dxh * xh, axis=-1, keepdims=True)))
    return jnp.concatenate(dos, axis=1), jnp.concatenate(drs, axis=1), dg


def conv_fwd(u, dw, dwb):
    T, C = u.shape
    TB = min(T, 256)

    def body(u_ref, w_ref, b_ref, y_ref, pad_ref):
        pad_ref[0:CONV_PAD, :] = jnp.zeros((CONV_PAD, LANES), F32)
        pad_ref[CONV_PAD:CONV_PAD + T, :] = u_ref[...]
        off = CONV_PAD - (CONV_W - 1)
        for t0 in range(0, T, TB):
            acc = jnp.zeros((TB, LANES), F32) + b_ref[...]
            for j in range(CONV_W):
                acc = acc + w_ref[j:j + 1, :] * pad_ref[t0 + off + j:t0 + off + j + TB, :]
            y_ref[t0:t0 + TB, :] = acc

    col = lambda i: (0, i)
    return pl.pallas_call(
        body, grid=(C // LANES,),
        in_specs=[pl.BlockSpec((T, LANES), col), pl.BlockSpec((CONV_W, LANES), col), pl.BlockSpec((1, LANES), col)],
        out_specs=pl.BlockSpec((T, LANES), col), out_shape=_sds((T, C), F32),
        scratch_shapes=[pltpu.VMEM((T + CONV_PAD, LANES), F32)], name="conv_fwd", compiler_params=_cp("parallel"),
    )(u, dw, dwb)


def conv_bwd(dy, u, dw):
    T, C = u.shape
    TB = min(T, 256)

    def body(dy_ref, u_ref, w_ref, du_ref, dw_ref, db_ref, upad, dypad):
        upad[0:CONV_PAD, :] = jnp.zeros((CONV_PAD, LANES), F32)
        upad[CONV_PAD:CONV_PAD + T, :] = u_ref[...]
        dypad[0:T, :] = dy_ref[...]
        dypad[T:T + CONV_PAD, :] = jnp.zeros((CONV_PAD, LANES), F32)
        off = CONV_PAD - (CONV_W - 1)
        for t0 in range(0, T, TB):
            acc = jnp.zeros((TB, LANES), F32)
            for j in range(CONV_W):
                s = t0 + (CONV_W - 1) - j
                acc = acc + w_ref[j:j + 1, :] * dypad[s:s + TB, :]
            du_ref[t0:t0 + TB, :] = acc
        for j in range(CONV_W):
            acc = jnp.zeros((TB, LANES), F32)
            for t0 in range(0, T, TB):
                acc = acc + dy_ref[t0:t0 + TB, :] * upad[t0 + off + j:t0 + off + j + TB, :]
            dw_ref[j:j + 1, :] = jnp.sum(acc, axis=0, keepdims=True)
        db_ref[...] = jnp.sum(dy_ref[...], axis=0, keepdims=True)

    col = lambda i: (0, i)
    return pl.pallas_call(
        body, grid=(C // LANES,),
        in_specs=[pl.BlockSpec((T, LANES), col), pl.BlockSpec((T, LANES), col), pl.BlockSpec((CONV_W, LANES), col)],
        out_specs=[pl.BlockSpec((T, LANES), col), pl.BlockSpec((CONV_W, LANES), col), pl.BlockSpec((1, LANES), col)],
        out_shape=[_sds((T, C), F32), _sds((CONV_W, C), F32), _sds((1, C), F32)],
        scratch_shapes=[pltpu.VMEM((T + CONV_PAD, LANES), F32), pltpu.VMEM((T + CONV_PAD, LANES), F32)],
        name="conv_bwd", compiler_params=_cp("parallel"),
    )(dy, u, dw)


def _ln_parts(x):
    mu = jnp.mean(x, axis=-1, keepdims=True)
    xc = x - mu
    rstd = lax.rsqrt(jnp.mean(xc * xc, axis=-1, keepdims=True) + EPS)
    return xc * rstd, rstd


def _ln_silu_fn(x, g, b):
    return _silu(_ln_parts(x)[0] * g + b)


def _ln_silu_bwd_fn(dbo, x, g, b):
    xh, rstd = _ln_parts(x)
    dy = dbo * _dsilu(xh * g + b)
    dyg = dy * g
    dx = rstd * (dyg - jnp.mean(dyg, axis=-1, keepdims=True) - xh * jnp.mean(dyg * xh, axis=-1, keepdims=True))
    return dx, jnp.sum(dy * xh, axis=0, keepdims=True), jnp.sum(dy, axis=0, keepdims=True)


def _glu_bwd_fn(du, ca, cb):
    s = _sigmoid(cb)
    return du * s, du * ca * s * (1.0 - s)


DIAGS = QB + KW


def _onehot_diag():
    j = lax.broadcasted_iota(jnp.int32, (REL_PAD, DIAGS), 1)
    i = lax.broadcasted_iota(jnp.int32, (REL_PAD, DIAGS), 0)
    return (i == jnp.clip(KW - j, -REL_CLIP, REL_CLIP) + REL_CLIP).astype(F32)


def relbias_tile(rbp):
    def body(rb_ref, o_ref, e_ref):
        e_ref[...] = _dot(rb_ref[...], _onehot_diag(), NN, HI)
        tc = lax.shift_right_logical(lax.broadcasted_iota(jnp.int32, (QB, KW), 0), 6)
        wc = lax.shift_right_logical(lax.broadcasted_iota(jnp.int32, (QB, KW), 1), 6)
        ok = jnp.logical_and(wc >= tc, wc <= tc + LEFT_CHUNKS)
        for h in range(ATT_H):
            spread = pltpu.roll(jnp.broadcast_to(e_ref[h:h + 1, :], (QB, DIAGS)), KW, 1, stride=1, stride_axis=0)
            o_ref[h] = jnp.where(ok, spread[:, :KW], NEG)

    return pl.pallas_call(body, out_shape=_sds((ATT_H, QB, KW), F32), name="relbias_tile",
                          scratch_shapes=[pltpu.VMEM((ATT_H, DIAGS), F32)], compiler_params=_cp())(rbp)


def relbias_reduce(dbm):
    def body(d_ref, o_ref, e_ref):
        u = lax.broadcasted_iota(jnp.int32, (QB, QB), 0)
        t = lax.broadcasted_iota(jnp.int32, (QB, QB), 1)
        flip = (u + t == QB - 1).astype(F32)
        for h in range(ATT_H):
            padded = jnp.concatenate([d_ref[h], jnp.zeros((QB, QB), F32)], axis=1)
            lined = pltpu.roll(_dot(flip, padded, NN, HI), 1, 1, stride=1, stride_axis=0)
            e_ref[h:h + 1, :] = jnp.sum(lined, axis=0, keepdims=True)
        o_ref[...] = _dot(e_ref[...], _onehot_diag(), NT, HI)

    return pl.pallas_call(body, out_shape=_sds((ATT_H, REL_PAD), F32), name="relbias_reduce",
                          scratch_shapes=[pltpu.VMEM((ATT_H, DIAGS), F32)], compiler_params=_cp())(dbm)


def _att_scores(q_ref, kp_ref, bm_ref, i):
    q0 = pl.multiple_of(i * QB, QB)
    kw = kp_ref[pl.ds(q0, KW), :]
    s = _bdot(q_ref[...], kw, NT) * (HD ** -0.5) + bm_ref[...]
    w = lax.broadcasted_iota(jnp.int32, (QB, KW), 1)
    return jnp.where(w + q0 >= PADK, s, NEG), kw, q0


def attn_fwd(qkv, kvp, bm):
    T = qkv.shape[0]
    D = ATT_H * HD

    def body(q_ref, kp_ref, vp_ref, bm_ref, o_ref, lse_ref):
        s, _, q0 = _att_scores(q_ref, kp_ref, bm_ref, pl.program_id(1))
        m = jnp.max(s, axis=-1, keepdims=True)
        e = jnp.exp(s - m)
        l = jnp.sum(e, axis=-1, keepdims=True)
        o_ref[...] = _bdot(e * (1.0 / l), vp_ref[pl.ds(q0, KW), :], NN).astype(o_ref.dtype)
        lse_ref[...] = m + jnp.log(l)

    return pl.pallas_call(
        body, grid=(ATT_H, T // QB),
        in_specs=[pl.BlockSpec((QB, HD), lambda h, i: (i, h)), pl.BlockSpec((T + PADK, HD), lambda h, i: (0, h)),
                  pl.BlockSpec((T + PADK, HD), lambda h, i: (0, ATT_H + h)),
                  pl.BlockSpec((None, QB, KW), lambda h, i: (h, 0, 0))],
        out_specs=[pl.BlockSpec((QB, HD), lambda h, i: (i, h)), pl.BlockSpec((None, QB, 1), lambda h, i: (h, i, 0))],
        out_shape=[_sds((T, D), BF), _sds((ATT_H, T, 1), F32)], name="attn_fwd",
        compiler_params=_cp("parallel", "arbitrary"),
    )(qkv, kvp, kvp, bm)


def attn_bwd(qkv, kvp, bm, o, lse, do):
    T = qkv.shape[0]
    D = ATT_H * HD

    def body(q_ref, kp_ref, vp_ref, bm_ref, o_ref, lse_ref, do_ref, dq_ref, dkp_ref, dvp_ref, dbm_ref):
        i = pl.program_id(1)

        @pl.when(i == 0)
        def _():
            dkp_ref[...] = jnp.zeros_like(dkp_ref)
            dvp_ref[...] = jnp.zeros_like(dvp_ref)
            dbm_ref[...] = jnp.zeros_like(dbm_ref)

        s, kw, q0 = _att_scores(q_ref, kp_ref, bm_ref, i)
        p = jnp.exp(s - lse_ref[...])
        dout = do_ref[...]
        dp = _bdot(dout, vp_ref[pl.ds(q0, KW), :], NT)
        delta = jnp.sum(_f(dout) * _f(o_ref[...]), axis=-1, keepdims=True)
        ds = p * (dp - delta)
        dq_ref[...] = (_bdot(ds, kw, NN) * (HD ** -0.5)).astype(dq_ref.dtype)
        dkp_ref[pl.ds(q0, KW), :] += _bdot(ds, q_ref[...], TN) * (HD ** -0.5)
        dvp_ref[pl.ds(q0, KW), :] += _bdot(p, dout, TN)
        dbm_ref[...] += ds

    qspec = pl.BlockSpec((QB, HD), lambda h, i: (i, h))
    kspec = pl.BlockSpec((T + PADK, HD), lambda h, i: (0, h))
    bspec = pl.BlockSpec((None, QB, KW), lambda h, i: (h, 0, 0))
    return pl.pallas_call(
        body, grid=(ATT_H, T // QB),
        in_specs=[qspec, kspec, pl.BlockSpec((T + PADK, HD), lambda h, i: (0, ATT_H + h)), bspec, qspec,
                  pl.BlockSpec((None, QB, 1), lambda h, i: (h, i, 0)), qspec],
        out_specs=[qspec, kspec, kspec, bspec],
        out_shape=[_sds((T, D), BF), _sds((T + PADK, D), F32), _sds((T + PADK, D), F32), _sds((ATT_H, QB, KW), F32)],
        name="attn_bwd", compiler_params=_cp("parallel", "arbitrary"),
    )(qkv, kvp, kvp, bm, o, lse, do)


def _final_fn(x, tgt, g):
    D = x.shape[1]
    xh, rstd = _rms_parts(x)
    diff = xh * g - tgt
    dy = diff * (1.0 / D)
    dxh = dy * g
    dx = rstd * (dxh - xh * jnp.mean(dxh * xh, axis=-1, keepdims=True))
    loss = jnp.sum(jnp.sum(diff * diff, axis=-1, keepdims=True), axis=0, keepdims=True) * (0.5 / D)
    return dx, jnp.sum(dy * xh, axis=0, keepdims=True), jnp.broadcast_to(loss, (1, LANES))


def _adamw_fn(w, g, m, v):
    m = ADAM_B1 * m + (1.0 - ADAM_B1) * g
    v = ADAM_B2 * v + (1.0 - ADAM_B2) * (g * g)
    m_hat = m / (1.0 - ADAM_B1 ** ADAM_STEP)
    v_hat = v / (1.0 - ADAM_B2 ** ADAM_STEP)
    delta = -ADAM_LR * (m_hat / (jnp.sqrt(v_hat) + ADAM_EPS) + ADAM_WD * w)
    return g, delta, m, v


def adamw(w, g, m, v, name):
    shape = w.shape
    C = shape[-1]
    R = w.size // C
    outs = rowwise(name, _adamw_fn, [t.reshape(R, C) for t in (w, g, m, v)], [], [(C, F32)] * 4)
    return tuple(t.reshape(shape) for t in outs)


def adamw_unit(w, m, v, g, lead, outs, name):
    R, B = w.shape[-2:]
    tb = _row_block(R, 256)
    if outs is None:
        outs = [lax.empty(w.shape, F32) for _ in range(4)]

    def body(w_ref, m_ref, v_ref, g_ref, *rest):
        for ref, val in zip(rest[4:], _adamw_fn(w_ref[...], g_ref[...], m_ref[...], v_ref[...])):
            ref[...] = val

    native = pl.BlockSpec((None,) * len(lead) + (tb, B), lambda r: tuple(lead) + (r, 0))
    return pl.pallas_call(
        body, grid=(R // tb,), in_specs=[native] * 3 + [pl.BlockSpec((tb, B), lambda r: (r, 0))] + [HBM_SPEC] * 4,
        out_specs=[native] * 4, out_shape=[_sds(w.shape, F32)] * 4, input_output_aliases={4 + i: i for i in range(4)},
        name=name, compiler_params=_cp("arbitrary"),
    )(w, m, v, g, *outs)


WEIGHTS = ['ffn_norm', 'ffn_w_gate', 'ffn_w_up', 'ffn_w_down', 'mix_norm', 'ab_w_in', 'gla_gate_w', 'gla_gate_b',
           'gla_norm_g', 'conv_dw', 'conv_dw_b', 'conv_ln_g', 'conv_ln_b', 'ab_w_out', 'att_w_qkv', 'att_rel_bias',
           'att_w_o', 'pl_norm', 'pl_w_gate', 'pl_w_proj', 'final_norm']
BIG = ['ffn_w_gate', 'ffn_w_up', 'ffn_w_down', 'ab_w_in', 'ab_w_out', 'att_w_qkv', 'att_w_o', 'pl_w_gate', 'pl_w_proj']


def _pack(parts, rows):
    flat = jnp.concatenate([p.reshape(-1) for p in parts])
    return jnp.pad(flat, (0, rows * LANES - flat.shape[0])).reshape(rows, LANES)


def _unpack(flat, shapes):
    out, pos = [], 0
    for s in shapes:
        n = 1
        for d in s:
            n *= d
        out.append(flat[pos:pos + n].reshape(s))
        pos += n
    return out


def _rows_for(shapes):
    n = sum(math.prod(s) for s in shapes)
    return -(-n // (8 * LANES)) * 8


def kernel(x, p, ffn_norm, ffn_w_gate, ffn_w_up, ffn_w_down, mix_norm, ab_w_in, gla_gate_w, gla_gate_b, gla_norm_g, conv_dw, conv_dw_b, conv_ln_g, conv_ln_b, ab_w_out, att_w_qkv, att_rel_bias, att_w_o, pl_norm, pl_w_gate, pl_w_proj, final_norm, loss_target, m_ffn_norm, m_ffn_w_gate, m_ffn_w_up, m_ffn_w_down, m_mix_norm, m_ab_w_in, m_gla_gate_w, m_gla_gate_b, m_gla_norm_g, m_conv_dw, m_conv_dw_b, m_conv_ln_g, m_conv_ln_b, m_ab_w_out, m_att_w_qkv, m_att_rel_bias, m_att_w_o, m_pl_norm, m_pl_w_gate, m_pl_w_proj, m_final_norm, v_ffn_norm, v_ffn_w_gate, v_ffn_w_up, v_ffn_w_down, v_mix_norm, v_ab_w_in, v_gla_gate_w, v_gla_gate_b, v_gla_norm_g, v_conv_dw, v_conv_dw_b, v_conv_ln_g, v_conv_ln_b, v_ab_w_out, v_att_w_qkv, v_att_rel_bias, v_att_w_o, v_pl_norm, v_pl_w_gate, v_pl_w_proj, v_final_norm):
    env = dict(locals())
    W = {n: env[n] for n in WEIGHTS}
    M = {n: env["m_" + n] for n in WEIGHTS}
    V = {n: env["v_" + n] for n in WEIGHTS}
    xc, yc_, cc = lax.axis_index("x"), lax.axis_index("y"), lax.axis_index("c")

    x0 = x[0]
    tgt = loss_target[0]
    T, D = x0.shape
    fs = ffn_w_gate.shape[-1]
    ws = ab_w_in.shape[-1]
    AB_IN = N_CHIPS * ws
    gz0 = 2 * GLA_H * HK + 2 * GLA_H * HV
    sidx = jnp.stack([2 * xc + yc_, cc]).astype(jnp.int32)

    def ffn_keys(i, j):
        return [('ffn_w_gate', (i, j)), ('ffn_w_up', (i, j)), ('ffn_w_down', (i, j))]

    order = [ffn_keys(0, 0), [('ab_w_in', (0,)), ('ab_w_out', (0,))], ffn_keys(0, 1),
             [('pl_w_gate', (0,)), ('pl_w_proj', (0,))], ffn_keys(1, 0), [('att_w_qkv', (0,)), ('att_w_o', (0,))],
             ffn_keys(1, 1), [('pl_w_gate', (1,)), ('pl_w_proj', (1,))]]
    full, in_flight = {}, []
    tok = jnp.zeros(TOKEN, F32)
    for s, keys in enumerate(order):
        parts = [cast_unit(W[n], lead, sidx, "cast_" + n + "".join(str(i) for i in lead)) for n, lead in keys]
        ssem, rsem, thru, tok = split_start("gather_start_%d" % s, parts, _gather_copies, 3 * len(keys), tok)
        in_flight.append((ssem, rsem, thru))

    def arrive(s, after):
        ssem, rsem, thru = in_flight[s]
        landed = split_wait("gather_wait_%d" % s, ssem, rsem, thru, _gather_copies, after)
        for key, buf in zip(order[s], gather_pass_on(landed, "gather_pass_on_%d" % s)):
            full[key] = buf

    small_sharded = [ffn_norm, gla_gate_w, conv_dw]
    rows_s = _rows_for([t.shape for t in small_sharded])
    got = allgather8(_pack(small_sharded, rows_s), "gather_small").reshape(N_CHIPS, 2, rows_s * LANES)[:, 0]
    per_chip = [_unpack(got[k], [t.shape for t in small_sharded]) for k in range(N_CHIPS)]
    ffn_norm_f, gate_w_f, conv_dw_f = [jnp.concatenate([per_chip[k][t] for k in range(N_CHIPS)], axis=-1)
                                       for t in range(3)]
    gate_w_p = jnp.pad(gate_w_f[0], ((0, LANES - 16), (0, 0)))
    conv_w = conv_dw_f[0]
    rb_p = jnp.pad(att_rel_bias[0], ((0, 0), (0, REL_PAD - att_rel_bias.shape[-1])))

    G = {}
    small_g = {}

    def ffn_w(i, j):
        return full['ffn_w_gate', (i, j)], full['ffn_w_up', (i, j)], full['ffn_w_down', (i, j)]

    saved = {}
    xs = x0
    arrive(0, tok)
    xs, saved['f00'] = ffn_fwd(xs, ffn_norm_f[0, 0][None], *ffn_w(0, 0), dep=tok)
    arrive(1, xs)
    w_in = jnp.transpose(full['ab_w_in', (0,)], (1, 0, 2)).reshape(D, AB_IN)
    w_main = jnp.concatenate([w_in[:, :gz0], w_in[:, gz0 + 16:]], axis=1)
    w_gz = jnp.pad(w_in[:, gz0:gz0 + 16], ((0, 0), (0, LANES - 16)))
    w_out = full['ab_w_out', (0,)].reshape(D, D)

    def mixer0_fwd(xin):
        h = rms_fwd(xin, mix_norm[0][None], "mix0_rms")
        zmain = mm_nn("ab_in", h, w_main, F32, 1024)
        gzp = mm_nn("ab_gz", h, w_gz, F32, LANES)
        ug = mm_nn("gla_gate", gzp, gate_w_p, F32, GLA_H * HK)
        o, S = gla_fwd(zmain, ug, gla_gate_b)
        a_out = rowwise("gla_post", _gla_post_fn, [o, ("cols", zmain, GLA_H * HV, 2)], [gla_norm_g], [(GLA_H * HV, BF)])
        u = rowwise("conv_glu", lambda a, b: a * _sigmoid(b), [("cols", zmain, 1024, 3), ("cols", zmain, 1024, 4)], [],
                    [(1024, F32)])
        yc = conv_fwd(u, conv_w, conv_dw_b)
        b_out = rowwise("conv_ln", _ln_silu_fn, [yc], [conv_ln_g, conv_ln_b], [(1024, BF)])
        cat = jnp.concatenate([a_out, b_out], axis=1)
        x2 = mm_nn("ab_out", cat, w_out, F32, 1024, res=xin)
        return x2, (xin, h, zmain, gzp, ug, o, S, u, yc, cat)

    xs, saved['m0'] = mixer0_fwd(xs)
    arrive(2, xs)
    xs, saved['f01'] = ffn_fwd(xs, ffn_norm_f[0, 1][None], *ffn_w(0, 1))
    ks = D // N_CHIPS
    dp = p.shape[-1]

    def pl_fwd(xin, i):
        h = rms_fwd(xin, pl_norm[i][None], "pl_rms")
        tm = min(T, 1024)
        u = mm_nn("pl_gate", h, full['pl_w_gate', (i,)].reshape(D, D), F32, 1024)
        e = mm("pl_proj", NN, (T // tm, N_CHIPS, 1), p[i, 0], (tm, dp), _mk, full['pl_w_proj', (i,)],
               (None, dp, ks), lambda m, n, k: (n, 0, 0), (tm, ks), _mn, _sds((T, D), F32))
        x2 = rowwise("pl_mix", lambda xv, uv, ev: xv + _sigmoid(uv) * ev, [xin, u, e], [], [(D, F32)])
        return x2, (xin, h, u, e)

    arrive(3, xs)
    xs, saved['p0'] = pl_fwd(xs, 0)
    arrive(4, xs)
    xs, saved['f10'] = ffn_fwd(xs, ffn_norm_f[1, 0][None], *ffn_w(1, 0))
    arrive(5, xs)
    w_qkv = full['att_w_qkv', (0,)]
    w_o = full['att_w_o', (0,)].reshape(D, D)
    qs = w_qkv.shape[-1]

    bm = relbias_tile(rb_p)

    def mixer1_fwd(xin):
        h = rms_fwd(xin, mix_norm[1][None], "mix1_rms")
        tm, tk, tn = min(T, 1024), 512, 512
        per = qs // tn
        qkv = mm("att_qkv", NN, (T // tm, 3 * D // tn, D // tk), h, (tm, tk), _mk, w_qkv, (None, tk, tn),
                 lambda m, n, k: (n // per, k, n % per), (tm, tn), _mn, _sds((T, 3 * D), BF))
        kvp = jnp.pad(qkv[:, D:], ((PADK, 0), (0, 0)))
        o, lse = attn_fwd(qkv, kvp, bm)
        x2 = mm_nn("att_o", o, w_o, F32, 1024, res=xin)
        return x2, (xin, h, qkv, kvp, o, lse)

    xs, saved['m1'] = mixer1_fwd(xs)
    arrive(6, xs)
    xs, saved['f11'] = ffn_fwd(xs, ffn_norm_f[1, 1][None], *ffn_w(1, 1))
    arrive(7, xs)
    xs, saved['p1'] = pl_fwd(xs, 1)

    dx, small_g['final_norm'], loss_acc = rowwise("loss_head", _final_fn, [xs, tgt], [final_norm[None]], [(D, F32)],
                                                  [(1, D), (1, LANES)])
    loss = lax.psum(loss_acc[0, 0], ("x", "y", "c"))

    def pl_bwd(dx2, sv, i, dep):
        xin, h, u, e = sv
        tm = min(T, 1024)

        def fn(d, uv, ev):
            s = _sigmoid(uv)
            return d * s, d * ev * s * (1.0 - s)

        de, du = rowwise("pl_mix_bwd", fn, [dx2, u, e], [], [(D, BF), (D, BF)], dep=dep)
        G['pl_w_proj', (i,)] = mm("pl_dproj", TN, (1, N_CHIPS, T // tm), p[i, 0], (tm, dp), _km, de, (tm, ks), _kn,
                                  (None, dp, ks), lambda m, n, k: (n, 0, 0), _sds((N_CHIPS, dp, ks), BF))
        G['pl_w_gate', (i,)] = mm_tn("pl_dgate", h, du, BF, ks, 1024).reshape(N_CHIPS, ks, D)
        dh = mm_nt("pl_dh", du, full['pl_w_gate', (i,)].reshape(D, D), F32, 1024)
        return rms_bwd(xin, pl_norm[i][None], dh, dx2, "pl_rms_bwd")

    def ffn_b(dx2, key, i, j, dep):
        dxn, dgain, G['ffn_w_gate', (i, j)], G['ffn_w_up', (i, j)], G['ffn_w_down', (i, j)] = ffn_bwd(
            dx2, saved[key], ffn_norm_f[i, j][None], *ffn_w(i, j), dep=dep)
        return dxn, dgain

    def mixer1_bwd(dx2, sv, dep):
        xin, h, qkv, kvp, o, lse = sv
        do = mm_nt("att_do", dx2, w_o, BF, 1024, dep=dep)
        dwo = mm_tn("att_dwo", o, dx2, BF, 1024, 1024, dep=dep)
        dq, dkp, dvp, dbm = attn_bwd(qkv, kvp, bm, o, lse, do)
        dqkv = jnp.concatenate([dq, dkp[PADK:].astype(BF), dvp[PADK:].astype(BF)], axis=1)
        tm, tn, tkk = min(T, 1024), 512, min(T, 1024)
        per = qs // tn
        dwqkv = mm("att_dwqkv", TN, (D // 1024, 3 * D // tn, T // tkk), h, (tkk, 1024), _km, dqkv, (tkk, tn), _kn,
                   (None, 1024, tn), lambda m, n, k: (n // per, m, n % per), _sds((N_CHIPS, D, qs), BF))
        dh = mm("att_dh", NT, (T // tm, D // 1024, 3 * D // tn), dqkv, (tm, tn), _mk, w_qkv, (None, 1024, tn),
                lambda m, n, k: (k // per, n, k % per), (tm, 1024), _mn, _sds((T, D), F32))
        dxn, dgain = rms_bwd(xin, mix_norm[1][None], dh, dx2, "mix1_rms_bwd")
        return dxn, dgain, dwo, dwqkv, relbias_reduce(dbm)

    def mixer0_bwd(dx2, sv, dep):
        xin, h, zmain, gzp, ug, o, S, u, yc, cat = sv
        dcat = mm_nt("ab_dcat", dx2, w_out, F32, 1024, dep=dep)
        dwout = mm_tn("ab_dwout", cat, dx2, BF, 1024, 1024, dep=dep)
        do, dr, dgn = rowwise("gla_post_bwd", _gla_post_bwd_fn,
                              [("cols", dcat, GLA_H * HV, 0), o, ("cols", zmain, GLA_H * HV, 2)], [gla_norm_g],
                              [(GLA_H * HV, F32), (GLA_H * HV, BF)], [(1, HV)])
        dyc, dlg, dlb = rowwise("conv_ln_bwd", _ln_silu_bwd_fn, [("cols", dcat, 1024, 1), yc], [conv_ln_g, conv_ln_b],
                                [(1024, F32)], [(1, 1024), (1, 1024)])
        du, ddw, ddwb = conv_bwd(dyc, u, conv_w)
        dca, dcb = rowwise("conv_glu_bwd", _glu_bwd_fn, [du, ("cols", zmain, 1024, 3), ("cols", zmain, 1024, 4)], [],
                           [(1024, BF), (1024, BF)])
        dq, dk, dv, dug, dgb = gla_bwd(zmain, ug, gla_gate_b, S, do)
        dgw = mm_tn("gla_dgate_w", gzp, dug, F32, LANES, GLA_H * HK)
        dgzp = mm_nt("gla_dgz", dug, gate_w_p, F32, LANES)
        dzm = jnp.concatenate([dq, dk, dv, dr, dca, dcb], axis=1)
        dwmain = mm_tn("ab_dwmain", h, dzm, BF, 1024, 1024)
        dwgz = mm_tn("ab_dwgz", h, dgzp, BF, 1024, LANES)
        dh = mm_nt("ab_dh_gz", dgzp, w_gz, F32, 1024, res=mm_nt("ab_dh", dzm, w_main, F32, 1024))
        dxn, dgain = rms_bwd(xin, mix_norm[0][None], dh, dx2, "mix0_rms_bwd")
        dwin = jnp.concatenate([dwmain[:, :gz0], dwgz[:, :16], dwmain[:, gz0:]], axis=1)
        g_win = jnp.transpose(dwin.reshape(D, N_CHIPS, ws), (1, 0, 2))
        return dxn, dgain, g_win, dwout, (dgn, dlg, dlb, ddw, ddwb, dgw[:16], dgb)

    scattering, exchanging = [], []

    def tags_of(s):
        return [n + "".join(str(i) for i in lead) for n, lead in order[s]]

    def to_chips(tok):
        if not exchanging:
            return tok
        s, ssem, rsem, thru = exchanging.pop()
        n_u = len(order[s])
        landed = split_wait("sibling_wait_%d" % s, ssem, rsem, thru, _sibling_copies, tok)
        hs = [add_sibling(g4, got, sidx, tag) for g4, got, tag in zip(landed[:n_u], landed[n_u:], tags_of(s))]
        land = [lax.empty((3,) + h.shape[1:], h.dtype) for h in hs]
        ssem, rsem, thru, tok = split_start("scatter_start_%d" % s, hs + land, _scatter_copies, 3 * n_u, tok)
        scattering.append((s, ssem, rsem, thru))
        return tok

    def scatter(s, tok):
        g4s = [G[key].reshape(N_CHIPS, 2, G[key].shape[1] // 2, G[key].shape[2]) for key in order[s]]
        land = [lax.empty((N_CHIPS,) + g4.shape[2:], g4.dtype) for g4 in g4s]
        ssem, rsem, thru, tok = split_start("sibling_start_%d" % s, g4s + land, _sibling_copies, N_CHIPS * len(g4s), tok)
        tok = to_chips(tok)
        exchanging.append((s, ssem, rsem, thru))
        return tok

    dpl, dffn, dmix = [None, None], [[None, None], [None, None]], [None, None]
    tok = jnp.zeros(TOKEN, F32)
    dx, dpl[1] = pl_bwd(dx, saved['p1'], 1, tok)
    tok = scatter(7, tok)
    dx, dffn[1][1] = ffn_b(dx, 'f11', 1, 1, tok)
    tok = scatter(6, tok)
    dx, dmix[1], dwo, dwqkv, drb = mixer1_bwd(dx, saved['m1'], tok)
    G['att_w_qkv', (0,)] = dwqkv
    G['att_w_o', (0,)] = dwo.reshape((N_CHIPS,) + att_w_o.shape[1:])
    tok = scatter(5, tok)
    dx, dffn[1][0] = ffn_b(dx, 'f10', 1, 0, tok)
    tok = scatter(4, tok)
    dx, dpl[0] = pl_bwd(dx, saved['p0'], 0, tok)
    tok = scatter(3, tok)
    dx, dffn[0][1] = ffn_b(dx, 'f01', 0, 1, tok)
    tok = scatter(2, tok)
    dx, dmix[0], g_win, dwout, (dgn, dlg, dlb, ddw, ddwb, dgw, dgb) = mixer0_bwd(dx, saved['m0'], tok)
    G['ab_w_in', (0,)] = g_win
    G['ab_w_out', (0,)] = dwout.reshape((N_CHIPS,) + ab_w_out.shape[1:])
    tok = scatter(1, tok)
    dx, dffn[0][0] = ffn_b(dx, 'f00', 0, 0, tok)
    tok = scatter(0, tok)
    tok = to_chips(tok)
    grad_x = dx[None]

    grads, outs = {}, {}
    sharing = []

    def update(tok):
        if not sharing:
            return
        s, ssem, rsem, thru = sharing.pop()
        landed = split_wait("share_wait_%d" % s, ssem, rsem, thru, _share_copies, tok)
        for (n, lead), tag, r in zip(order[s], tags_of(s), landed):
            outs[n] = adamw_unit(W[n], M[n], V[n], r.reshape(W[n].shape[-2:]), lead, outs.get(n), "adamw_" + tag)

    after = dx
    for s, ssem, rsem, thru in scattering:
        n_u = len(order[s])
        landed = split_wait("scatter_wait_%d" % s, ssem, rsem, thru, _scatter_copies, after)
        rs = [add_chips(h, got2, sidx, tag) for h, got2, tag in zip(landed[:n_u], landed[n_u:], tags_of(s))]
        ssem, rsem, thru, tok = split_start("share_start_%d" % s, rs, _share_copies, n_u, tok)
        update(tok)
        sharing.append((s, ssem, rsem, thru))
        after = tok
    update(tok)

    small_g['ffn_norm'] = jnp.stack([jnp.stack([dffn[i][j][0] for j in range(2)]) for i in range(2)])
    small_g['mix_norm'] = jnp.concatenate(dmix, axis=0)
    small_g['gla_gate_w'] = dgw[None]
    small_g['gla_gate_b'] = dgb
    small_g['gla_norm_g'] = dgn
    small_g['conv_dw'] = ddw[None]
    small_g['conv_dw_b'] = ddwb
    small_g['conv_ln_g'] = dlg
    small_g['conv_ln_b'] = dlb
    small_g['att_rel_bias'] = drb[None, :, :att_rel_bias.shape[-1]]
    small_g['pl_norm'] = jnp.concatenate(dpl, axis=0)
    small_g['final_norm'] = small_g['final_norm'][0]
    small_names = [n for n in WEIGHTS if n not in BIG]
    small_shapes = [small_g[n].shape for n in small_names]
    rows_g = _rows_for(small_shapes)
    allp = allgather8(_pack([small_g[n] for n in small_names], rows_g), "gather_small_grads")
    summed = rowwise("sum_small_grads", lambda *v: (((v[0] + v[1]) + (v[2] + v[3])) + ((v[4] + v[5]) + (v[6] + v[7]))),
                     [("leads", allp.reshape(8, rows_g, LANES), (d,)) for d in range(8)], [], [(LANES, F32)], tb=rows_g)
    for n, g in zip(small_names, _unpack(summed.reshape(-1), small_shapes)):
        grads[n] = g
    chip = 2 * xc + yc_
    for n, axis in (('ffn_norm', 2), ('gla_gate_w', 2), ('conv_dw', 2)):
        width = W[n].shape[axis]
        grads[n] = lax.dynamic_slice_in_dim(grads[n], chip * width, width, axis)

    for n in small_names:
        outs[n] = adamw(W[n], grads[n], M[n], V[n], "adamw_" + n)
    return (loss, grad_x, *[outs[n][0] for n in WEIGHTS], *[outs[n][1] for n in WEIGHTS],
            *[outs[n][2] for n in WEIGHTS], *[outs[n][3] for n in WEIGHTS])
```

```python
import math

import jax
import jax.numpy as jnp
from jax import lax
from jax.experimental import pallas as pl
from jax.experimental.pallas import tpu as pltpu

F32 = jnp.float32
BF = jnp.bfloat16
MESH = pl.DeviceIdType.MESH
HI = lax.Precision.HIGHEST
V7X_VMEM_LIMIT = 56 * 1024 * 1024
LANES = 128
EPS = 1e-6
NEG = -1e30

CHUNK = 64
LEFT_CHUNKS = 8
QB = 256
KW = QB + LEFT_CHUNKS * CHUNK
PADK = LEFT_CHUNKS * CHUNK
REL_CLIP = 128
REL_PAD = 384
ATT_H = 16
HD = 128
GLA_H = 4
HK = 128
HV = 256
GLA_TAU = 16.0
CONV_W = 31
CONV_PAD = 32
N_CHIPS = 4

ADAM_LR = 0.001
ADAM_B1 = 0.9
ADAM_B2 = 0.999
ADAM_EPS = 1e-08
ADAM_WD = 0.01
ADAM_STEP = 10

NN = ((1,), (0,))
NT = ((1,), (1,))
TN = ((0,), (0,))


def _dot(a, b, dims, prec=None):
    return lax.dot_general(a, b, (dims, ((), ())), preferred_element_type=F32, precision=prec)


def _bdot(a, b, dims):
    return _dot(a.astype(BF), b.astype(BF), dims)


def _cp(*sem):
    return pltpu.CompilerParams(dimension_semantics=sem if sem else None, vmem_limit_bytes=V7X_VMEM_LIMIT)


def _sds(shape, dtype):
    return jax.ShapeDtypeStruct(shape, dtype)


def _sigmoid(x):
    return 1.0 / (1.0 + jnp.exp(-x))


def _silu(x):
    return x * _sigmoid(x)


def _dsilu(x):
    s = _sigmoid(x)
    return s * (1.0 + x * (1.0 - s))


def _f(x):
    return x.astype(F32)


def _row_item(item, tb):
    if not isinstance(item, tuple):
        return item, (tb, item.shape[1]), lambda i, s: (i, 0)
    kind, arr = item[0], item[1]
    if kind == "cols":
        return arr, (tb, item[2]), lambda i, s, blk=item[3]: (i, blk)
    if kind == "leads":
        lead = tuple(item[2])
        return arr, (None,) * len(lead) + (tb, arr.shape[-1]), lambda i, s: lead + (i, 0)
    if kind == "dyn":
        return arr, (None, tb, arr.shape[-1]), lambda i, s, sel=item[2]: (s[sel], i, 0)
    if kind == "dyn4":
        nb = arr.shape[2] // tb
        return arr, (None, None, tb, arr.shape[-1]), lambda i, s, sel=item[2]: (i // nb, s[sel], i % nb, 0)
    raise ValueError(kind)


def _row_block(n, cap):
    for cand in range(min(cap, n) // 16 * 16, 0, -16):
        if n % cand == 0:
            return cand
    return n


def rowwise(name, fn, rows, bcast, outs, accs=(), tb=256, n_rows=None, sidx=None, row_period=None, dep=None):
    if n_rows is None:
        first = rows[0][1] if isinstance(rows[0], tuple) else rows[0]
        n_rows = first.shape[-2]
    tb = _row_block(n_rows if row_period is None else row_period, tb)
    items = [_row_item(it, tb) for it in rows]
    n_r, n_b, n_o = len(rows), len(bcast), len(outs)
    pre = 0 if sidx is None else 1

    def wrap(f):
        return (lambda i: f(i, None)) if sidx is None else (lambda i, s: f(i, s))

    def body(*refs):
        refs = refs[pre:]
        r, b = refs[:n_r], refs[n_r:n_r + n_b]
        refs = refs[n_r + n_b + (dep is not None):]
        o, a = refs[:n_o], refs[n_o:]
        res = fn(*[v[...] for v in r], *[v[...] for v in b])
        if not isinstance(res, tuple):
            res = (res,)
        for ref, val in zip(o, res[:n_o]):
            ref[...] = val.astype(ref.dtype)
        if a:
            @pl.when(pl.program_id(0) == 0)
            def _():
                for ref in a:
                    ref[...] = jnp.zeros_like(ref)
            for ref, val in zip(a, res[n_o:]):
                ref[...] += val

    in_specs = [pl.BlockSpec(bs, wrap(f)) for _, bs, f in items]
    in_specs += [pl.BlockSpec(v.shape, wrap(lambda i, s, nd=v.ndim: (0,) * nd)) for v in bcast]
    out_specs, out_shape = [], []
    for o in outs:
        if o[0] == "dyn":
            _, L, c, dt, sel = o
            out_specs.append(pl.BlockSpec((None, tb, c), wrap(lambda i, s, sel=sel: (s[sel], i, 0))))
            out_shape.append(_sds((L, n_rows, c), dt))
        else:
            c, dt = o
            out_specs.append(pl.BlockSpec((tb, c), wrap(lambda i, s: (i, 0))))
            out_shape.append(_sds((n_rows, c), dt))
    out_specs += [pl.BlockSpec(sh, wrap(lambda i, s: (0, 0))) for sh in accs]
    out_shape += [_sds(sh, F32) for sh in accs]
    operands = [a for a, _, _ in items] + list(bcast)
    if dep is not None:
        operands.append(dep)
        in_specs.append(pl.BlockSpec(TOKEN, wrap(lambda i, s: (0, 0))))
    grid = (n_rows // tb,)
    if sidx is None:
        res = pl.pallas_call(body, grid=grid, in_specs=in_specs, out_specs=out_specs, out_shape=out_shape, name=name,
                             compiler_params=_cp("arbitrary"))(*operands)
    else:
        spec = pltpu.PrefetchScalarGridSpec(num_scalar_prefetch=1, grid=grid, in_specs=in_specs, out_specs=out_specs)
        res = pl.pallas_call(body, grid_spec=spec, out_shape=out_shape, name=name,
                             compiler_params=_cp("arbitrary"))(sidx, *operands)
    return res[0] if len(res) == 1 else tuple(res)


TOKEN = (8, LANES)


def mm(name, dims, grid, a, a_bs, a_im, b, b_bs, b_im, o_bs, o_im, out, scale=1.0, res=None, dep=None):
    nk = grid[2]
    acc_shape = tuple(d for d in o_bs if d is not None)

    def body(*refs):
        a_ref, b_ref = refs[0], refs[1]
        pos = 2
        res_ref = None
        if res is not None:
            res_ref = refs[pos]
            pos += 1
        if dep is not None:
            pos += 1
        o_ref = refs[pos]
        part = _bdot(a_ref[...], b_ref[...], dims)

        def finish(acc):
            v = acc * scale if scale != 1.0 else acc
            if res_ref is not None:
                v = v + _f(res_ref[...])
            o_ref[...] = v.astype(o_ref.dtype)

        if nk == 1:
            finish(part)
        else:
            acc_ref = refs[pos + 1]
            k = pl.program_id(2)

            @pl.when(k == 0)
            def _():
                acc_ref[...] = part

            @pl.when(k > 0)
            def _():
                acc_ref[...] += part

            @pl.when(k == nk - 1)
            def _():
                finish(acc_ref[...])

    operands = [a, b]
    in_specs = [pl.BlockSpec(a_bs, a_im), pl.BlockSpec(b_bs, b_im)]
    if res is not None:
        operands.append(res)
        in_specs.append(pl.BlockSpec(o_bs, o_im))
    if dep is not None:
        operands.append(dep)
        in_specs.append(pl.BlockSpec(TOKEN, lambda m, n, k: (0, 0)))
    scratch = [pltpu.VMEM(acc_shape, F32)] if nk > 1 else []
    return pl.pallas_call(body, grid=grid, in_specs=in_specs, out_specs=pl.BlockSpec(o_bs, o_im), out_shape=out,
                          scratch_shapes=scratch, name=name,
                          compiler_params=_cp("parallel", "parallel", "arbitrary"))(*operands)


def _mk(m, n, k):
    return (m, k)


def _mn(m, n, k):
    return (m, n)


def _km(m, n, k):
    return (k, m)


def _kn(m, n, k):
    return (k, n)


def _nk(m, n, k):
    return (n, k)


def mm_nn(name, a, b, out_dtype, tn, tk=None, scale=1.0, res=None):
    T, K = a.shape
    N = b.shape[1]
    tm, tk, tn = min(T, 1024), K if tk is None else min(tk, K), min(tn, N)
    return mm(name, NN, (T // tm, N // tn, K // tk), a, (tm, tk), _mk, b, (tk, tn), _kn, (tm, tn), _mn,
              _sds((T, N), out_dtype), scale=scale, res=res)


def mm_nt(name, a, b, out_dtype, tn, tk=None, scale=1.0, res=None, dep=None):
    T, K = a.shape
    N = b.shape[0]
    tm, tk, tn = min(T, 1024), K if tk is None else min(tk, K), min(tn, N)
    return mm(name, NT, (T // tm, N // tn, K // tk), a, (tm, tk), _mk, b, (tn, tk), _nk, (tm, tn), _mn,
              _sds((T, N), out_dtype), scale=scale, res=res, dep=dep)


def mm_tn(name, a, b, out_dtype, tm, tn, scale=1.0, dep=None):
    T, M = a.shape
    N = b.shape[1]
    tk, tm, tn = T, min(tm, M), min(tn, N)
    return mm(name, TN, (M // tm, N // tn, T // tk), a, (tk, tm), _km, b, (tk, tn), _kn, (tm, tn), _mn,
              _sds((M, N), out_dtype), scale=scale, dep=dep)


HBM_SPEC = pl.BlockSpec(memory_space=pl.ANY)


def _place():
    x, y, c = lax.axis_index("x"), lax.axis_index("y"), lax.axis_index("c")
    chips = [(1 - x, y), (x, 1 - y), (1 - x, 1 - y)]
    return x, y, c, chips


def allgather8(v, name):
    m_per, n = v.shape

    def body(x_ref, out_ref, send_sems, recv_sems, local_sem):
        x, y, c, chips = _place()
        me, sibling = (x, y, c), (x, y, 1 - c)

        def rows(px, py, pc):
            return out_ref.at[pl.ds((4 * px + 2 * py + pc) * m_per, m_per), :]

        def copy(k, block, to, src=None):
            return pltpu.make_async_remote_copy(
                src_ref=rows(*block) if src is None else src, dst_ref=rows(*block),
                send_sem=send_sems.at[k], recv_sem=recv_sems.at[k], device_id=to, device_id_type=MESH)

        mine = pltpu.make_async_copy(x_ref, rows(*me), local_sem)
        mine.start()
        first = [copy(0, me, sibling, src=x_ref)]
        first += [copy(1 + j, me, (*chip, c), src=x_ref) for j, chip in enumerate(chips)]
        for cp in first:
            cp.start()
        passed = [copy(4 + j, (*chip, c), sibling) for j, chip in enumerate(chips)]
        for j, chip in enumerate(chips):
            copy(1 + j, (*chip, c), me).wait_recv()
            passed[j].start()
        copy(0, sibling, me).wait_recv()
        for j, chip in enumerate(chips):
            copy(4 + j, (*chip, 1 - c), me).wait_recv()
        for cp in first + passed:
            cp.wait_send()
        mine.wait()

    return pl.pallas_call(
        body, out_shape=_sds((8 * m_per, n), v.dtype), in_specs=[pl.BlockSpec(memory_space=pltpu.VMEM)],
        out_specs=pl.BlockSpec(memory_space=pltpu.VMEM), name=name,
        scratch_shapes=[pltpu.SemaphoreType.DMA((7,)), pltpu.SemaphoreType.DMA((7,)), pltpu.SemaphoreType.DMA],
    )(v)


D2D_PIECES = 4


def _pieces(ref, n):
    rows = ref.shape[0] // n
    return [ref.at[pl.ds(q * rows, rows)] for q in range(n)]


def _start_in_pieces(src, dst, ssem, rsem, to, n):
    for s_q, d_q in zip(_pieces(src, n), _pieces(dst, n)):
        pltpu.make_async_remote_copy(src_ref=s_q, dst_ref=d_q, send_sem=ssem, recv_sem=rsem, device_id=to,
                                     device_id_type=MESH).start()


def _whole(src, dst, ssem, rsem, to):
    return pltpu.make_async_remote_copy(src_ref=src, dst_ref=dst, send_sem=ssem, recv_sem=rsem, device_id=to,
                                        device_id_type=MESH)


HBM_ONLY = pl.BlockSpec(memory_space=pltpu.HBM)
SEM_SPEC = pl.BlockSpec(memory_space=pltpu.SEMAPHORE)
DATAFLOW = pltpu.SideEffectType.DATAFLOW_SIDE_EFFECTING


def _gather_copies(bufs):
    x, y, c, chips = _place()
    out = []
    for o_ref in bufs:
        hr = o_ref.shape[1] // 2

        def half(chip, o_ref=o_ref, hr=hr):
            return o_ref.at[2 * chip[0] + chip[1], pl.ds(c * hr, hr)]

        out += [(half((x, y)), half((x, y)), (*chip, c), half(chip)) for chip in chips]
    return out


def _scatter_copies(arrays):
    x, y, c, chips = _place()
    n = len(arrays) // 2
    out = []
    for h_ref, got_ref in zip(arrays[:n], arrays[n:]):
        out += [(h_ref.at[2 * chip[0] + chip[1]], got_ref.at[j], (*chip, c), got_ref.at[j])
                for j, chip in enumerate(chips)]
    return out


def split_start(name, arrays, copies_of, n_copies, after):
    n = len(arrays)

    def body(*refs):
        ssem, rsem = refs[n + 1], refs[n + 2]
        for i, (src, dst, to, _) in enumerate(copies_of(refs[n + 3:2 * n + 3])):
            _whole(src, dst, ssem.at[i], rsem.at[i], to).start()
        refs[2 * n + 3][...] = jnp.zeros(TOKEN, F32)

    res = pl.pallas_call(
        body, name=name,
        out_shape=(pltpu.SemaphoreType.DMA((n_copies,)), pltpu.SemaphoreType.DMA((n_copies,)),
                   *[pltpu.HBM(a.shape, a.dtype) for a in arrays], _sds(TOKEN, F32)),
        in_specs=[HBM_ONLY] * n + [pl.BlockSpec(memory_space=pl.ANY)],
        out_specs=(SEM_SPEC, SEM_SPEC, *[HBM_ONLY] * n, pl.BlockSpec(memory_space=pltpu.VMEM)),
        input_output_aliases={i: 2 + i for i in range(n)},
        compiler_params=pltpu.CompilerParams(has_side_effects=DATAFLOW),
    )(*[pltpu.with_memory_space_constraint(a, pltpu.HBM) for a in arrays], after)
    return res[0], res[1], list(res[2:2 + n]), res[2 + n]


def split_wait(name, ssem, rsem, arrays, copies_of, after):
    n = len(arrays)

    def body(*refs):
        s_ref, r_ref = refs[n], refs[n + 1]
        for i, (src, _, to, mine) in enumerate(copies_of(refs[:n])):
            cp = _whole(src, mine, s_ref.at[i], r_ref.at[i], to)
            cp.wait_send()
            cp.wait_recv()

    res = pl.pallas_call(
        body, name=name, out_shape=tuple(pltpu.HBM(a.shape, a.dtype) for a in arrays),
        in_specs=[HBM_ONLY] * n + [SEM_SPEC, SEM_SPEC, pl.BlockSpec(memory_space=pl.ANY)],
        out_specs=tuple([HBM_ONLY] * n), input_output_aliases={i: i for i in range(n)},
        compiler_params=pltpu.CompilerParams(has_side_effects=DATAFLOW),
    )(*arrays, ssem, rsem, after)
    return list(res)


def gather_pass_on(bufs, name):
    n = len(bufs)

    def body(*refs):
        o_refs, ssem, rsem = refs[n:2 * n], refs[2 * n], refs[2 * n + 1]
        x, y, c, chips = _place()
        sib = (x, y, 1 - c)

        def half(o_ref, chip, h):
            hr = o_ref.shape[1] // 2
            return o_ref.at[2 * chip[0] + chip[1], pl.ds(h * hr, hr)]

        for u, o_ref in enumerate(o_refs):
            for j, chip in enumerate(chips):
                _start_in_pieces(half(o_ref, chip, c), half(o_ref, chip, c), ssem.at[3 * u + j], rsem.at[3 * u + j], sib,
                                 D2D_PIECES)
        for u, o_ref in enumerate(o_refs):
            for j, chip in enumerate(chips):
                _whole(half(o_ref, chip, c), half(o_ref, chip, 1 - c), ssem.at[3 * u + j], rsem.at[3 * u + j], sib).wait()

    res = pl.pallas_call(
        body, out_shape=tuple(_sds(b.shape, b.dtype) for b in bufs), in_specs=[HBM_SPEC] * n,
        out_specs=tuple([HBM_SPEC] * n), name=name, input_output_aliases={i: i for i in range(n)},
        scratch_shapes=[pltpu.SemaphoreType.DMA((3 * n,)), pltpu.SemaphoreType.DMA((3 * n,))],
    )(*bufs)
    return list(res)


def _sibling_copies(arrays):
    x, y, c, _ = _place()
    n = len(arrays) // 2
    out = []
    for g_ref, got_ref in zip(arrays[:n], arrays[n:]):
        out += [(g_ref.at[k, 1 - c], got_ref.at[k], (x, y, 1 - c), got_ref.at[k]) for k in range(N_CHIPS)]
    return out


def _share_copies(arrays):
    x, y, c, _ = _place()
    return [(r_ref.at[c], r_ref.at[c], (x, y, 1 - c), r_ref.at[1 - c]) for r_ref in arrays]


def add_sibling(g4, got, sidx, tag):
    _, _, hr, B = g4.shape
    h = rowwise("rs_add2_" + tag, lambda a, b: _f(a) + _f(b), [("dyn4", g4, 1), got.reshape(N_CHIPS * hr, B)], [],
                [(B, BF)], tb=512, n_rows=N_CHIPS * hr, sidx=sidx, row_period=hr)
    return h.reshape(N_CHIPS, hr, B)


def add_chips(h, got2, sidx, tag):
    _, hr, B = h.shape
    return rowwise("rs_add4_" + tag, lambda a, b, c, d: ((_f(a) + _f(b)) + _f(c)) + _f(d),
                   [("dyn", h, 0), ("leads", got2, (0,)), ("leads", got2, (1,)), ("leads", got2, (2,))], [],
                   [("dyn", 2, B, F32, 1)], tb=512, n_rows=hr, sidx=sidx)


def _rms_parts(x):
    rstd = lax.rsqrt(jnp.mean(x * x, axis=-1, keepdims=True) + EPS)
    return x * rstd, rstd


def rms_fwd(x, g, name, dep=None):
    return rowwise(name, lambda xv, gv: _rms_parts(xv)[0] * gv, [x], [g], [(x.shape[1], BF)], dep=dep)


def _rms_bwd_fn(x, dh, dres, g):
    xh, rstd = _rms_parts(x)
    dxh = _f(dh) * g
    dx = rstd * (dxh - xh * jnp.mean(dxh * xh, axis=-1, keepdims=True)) + dres
    return dx, jnp.sum(_f(dh) * xh, axis=0, keepdims=True)


def rms_bwd(x, g, dh, dres, name):
    D = x.shape[1]
    return rowwise(name, _rms_bwd_fn, [x, dh, dres], [g], [(D, F32)], [(1, D)])


def cast_unit(w, lead, sidx, name):
    R, B = w.shape[-2:]
    return rowwise(name, lambda v: v, [("leads", w, lead)], [], [("dyn", N_CHIPS, B, BF, 0)], n_rows=R, sidx=sidx)


def ffn_fwd(xin, gain, wg, wu, wd, dep=None):
    T, D = xin.shape
    fs = wg.shape[-1]
    F = N_CHIPS * fs
    tm = min(T, 1024)
    h = rms_fwd(xin, gain, "ffn_rms", dep)

    def up(wt, nm):
        return mm(nm, NN, (T // tm, N_CHIPS, 1), h, (tm, D), _mk, wt, (None, D, fs),
                  lambda m, n, k: (n, k, 0), (tm, fs), _mn, _sds((T, F), BF))

    zg, zu = up(wg, "ffn_gate"), up(wu, "ffn_up")
    a = rowwise("ffn_swiglu", lambda g, u: _silu(_f(g)) * _f(u), [zg, zu], [], [(F, BF)])
    x2 = mm_nn("ffn_down", a, wd.reshape(F, D), F32, 512, scale=0.5, res=xin)
    return x2, (xin, h, zg, zu, a)


def _swiglu_bwd_fn(da, zg, zu):
    da, zg, zu = _f(da), _f(zg), _f(zu)
    return da * zu * _dsilu(zg), da * _silu(zg)


def ffn_bwd(dx2, saved, gain, wg, wu, wd, dep, sends):
    xin, h, zg, zu, a = saved
    T, D = xin.shape
    fs = wg.shape[-1]
    F = N_CHIPS * fs
    tm, tn, tkk = min(T, 1024), 1024, T
    da = mm_nt("ffn_da", dx2, wd.reshape(F, D), BF, fs, scale=0.5, dep=dep)
    g_wd = mm_tn("ffn_dwd", a, dx2, BF, fs, 1024, scale=0.5, dep=dep).reshape(N_CHIPS, fs, D)
    dzg, dzu = rowwise("ffn_dswiglu", _swiglu_bwd_fn, [da, zg, zu], [], [(F, BF), (F, BF)])

    def dw(dz, nm):
        return mm(nm, TN, (D // tn, N_CHIPS, T // tkk), h, (tkk, tn), _km, dz, (tkk, fs), _kn,
                  (None, tn, fs), lambda m, n, k: (n, m, 0), _sds((N_CHIPS, D, fs), BF))

    g_wg, g_wu = dw(dzg, "ffn_dwg"), dw(dzu, "ffn_dwu")

    if sends is not None:
        dep = sends(g_wg, g_wu, g_wd)

    def dh_part(dz, wt, nm, res, dep=None):
        return mm(nm, NT, (T // tm, D // tn, N_CHIPS), dz, (tm, fs), _mk, wt, (None, tn, fs),
                  lambda m, n, k: (k, n, 0), (tm, tn), _mn, _sds((T, D), F32), res=res, dep=dep)

    dh = dh_part(dzu, wu, "ffn_dh_u", dh_part(dzg, wg, "ffn_dh_g", None, dep))
    dx, dgain = rms_bwd(xin, gain, dh, dx2, "ffn_rms_bwd")
    return dx, dgain, dep


def _gla_chunk(q_ref, k_ref, v_ref, u_ref, b_ref, rows):
    r = lax.broadcasted_iota(jnp.int32, (CHUNK, CHUNK), 0)
    c = lax.broadcasted_iota(jnp.int32, (CHUNK, CHUNK), 1)
    causal = c <= r
    u = u_ref[rows, :] + b_ref[...]
    g = (jnp.minimum(u, 0.0) - jnp.log(1.0 + jnp.exp(-jnp.abs(u)))) * (1.0 / GLA_TAU)
    b = _dot(causal.astype(F32), g, NN, HI)
    last = lax.broadcasted_iota(jnp.int32, (CHUNK, HK), 0) == CHUNK - 1
    blast = jnp.sum(jnp.where(last, b, 0.0), axis=0, keepdims=True)
    eb = jnp.exp(b)
    qb = q_ref[rows, :] * (HK ** -0.5) * eb
    k = k_ref[rows, :]
    kb = k * jnp.exp(-b)
    kl = k * jnp.exp(blast - b)
    A = jnp.where(causal, _bdot(qb, kb, NT), 0.0)
    return causal, u, b, blast, eb, qb, kb, kl, A


def _gla_in_specs(T):
    return [pl.BlockSpec((T, HK), lambda h: (0, h)), pl.BlockSpec((T, HK), lambda h: (0, GLA_H + h)),
            pl.BlockSpec((T, HV), lambda h: (0, GLA_H + h)), pl.BlockSpec((T, HK), lambda h: (0, h)),
            pl.BlockSpec((1, HK), lambda h: (0, h))]


def gla_fwd(zmain, ug, gate_b):
    T = zmain.shape[0]
    nC = T // CHUNK

    def body(q_ref, k_ref, v_ref, u_ref, b_ref, o_ref, s_ref, st_ref):
        st_ref[...] = jnp.zeros_like(st_ref)

        def step(n, carry):
            rows = pl.ds(pl.multiple_of(n * CHUNK, CHUNK), CHUNK)
            _, _, _, blast, _, qb, _, kl, A = _gla_chunk(q_ref, k_ref, v_ref, u_ref, b_ref, rows)
            v = v_ref[rows, :]
            ST = st_ref[...]
            s_ref[n] = ST
            o_ref[rows, :] = _bdot(qb, ST, NT) + _bdot(A, v, NN)
            st_ref[...] = ST * jnp.exp(blast) + _bdot(v, kl, TN)
            return carry

        lax.fori_loop(0, nC, step, 0)

    return pl.pallas_call(
        body, grid=(GLA_H,), in_specs=_gla_in_specs(T),
        out_specs=[pl.BlockSpec((T, HV), lambda h: (0, h)), pl.BlockSpec((nC, None, HV, HK), lambda h: (0, h, 0, 0))],
        out_shape=[_sds((T, GLA_H * HV), F32), _sds((nC, GLA_H, HV, HK), F32)],
        scratch_shapes=[pltpu.VMEM((HV, HK), F32)], name="gla_fwd", compiler_params=_cp("parallel"),
    )(zmain, zmain, zmain, ug, gate_b)


def gla_bwd(zmain, ug, gate_b, S, do):
    T = zmain.shape[0]
    nC = T // CHUNK

    def body(q_ref, k_ref, v_ref, u_ref, b_ref, s_ref, do_ref, dq_ref, dk_ref, dv_ref, du_ref, dgb_ref, dst_ref):
        dst_ref[...] = jnp.zeros_like(dst_ref)
        dgb_ref[...] = jnp.zeros_like(dgb_ref)

        def step(it, carry):
            n = nC - 1 - it
            rows = pl.ds(pl.multiple_of(n * CHUNK, CHUNK), CHUNK)
            causal, u, b, blast, eb, qb, kb, kl, A = _gla_chunk(q_ref, k_ref, v_ref, u_ref, b_ref, rows)
            v = v_ref[rows, :]
            dout = do_ref[rows, :]
            ST = s_ref[n]
            dST = dst_ref[...]
            elast = jnp.exp(blast)
            dA = jnp.where(causal, _bdot(dout, v, NT), 0.0)
            dv_ref[rows, :] = (_bdot(A, dout, TN) + _bdot(kl, dST, NT)).astype(dv_ref.dtype)
            dqb = _bdot(dout, ST, NN) + _bdot(dA, kb, NN)
            dkb = _bdot(dA, qb, TN)
            dkl = _bdot(v, dST, NN)
            ddec = jnp.sum(ST * dST, axis=0, keepdims=True)
            dst_ref[...] = dST * elast + _bdot(dout, qb, TN)
            dq_ref[rows, :] = (dqb * eb * (HK ** -0.5)).astype(dq_ref.dtype)
            dk_ref[rows, :] = (dkb * jnp.exp(-b) + dkl * jnp.exp(blast - b)).astype(dk_ref.dtype)
            db = dqb * qb - dkb * kb - dkl * kl
            dbl = jnp.sum(dkl * kl, axis=0, keepdims=True) + elast * ddec
            dg = _dot(jnp.logical_not(causal).astype(F32) + jnp.where(
                lax.broadcasted_iota(jnp.int32, (CHUNK, CHUNK), 0) == lax.broadcasted_iota(jnp.int32, (CHUNK, CHUNK), 1),
                1.0, 0.0), db, NN, HI) + dbl
            du = dg * (1.0 / GLA_TAU) / (1.0 + jnp.exp(u))
            du_ref[rows, :] = du
            dgb_ref[...] += jnp.sum(du, axis=0, keepdims=True)
            return carry

        lax.fori_loop(0, nC, step, 0)

    specs = _gla_in_specs(T) + [pl.BlockSpec((nC, None, HV, HK), lambda h: (0, h, 0, 0)),
                                pl.BlockSpec((T, HV), lambda h: (0, h))]
    return pl.pallas_call(
        body, grid=(GLA_H,), in_specs=specs,
        out_specs=[pl.BlockSpec((T, HK), lambda h: (0, h)), pl.BlockSpec((T, HK), lambda h: (0, h)),
                   pl.BlockSpec((T, HV), lambda h: (0, h)), pl.BlockSpec((T, HK), lambda h: (0, h)),
                   pl.BlockSpec((1, HK), lambda h: (0, h))],
        out_shape=[_sds((T, GLA_H * HK), BF), _sds((T, GLA_H * HK), BF), _sds((T, GLA_H * HV), BF),
                   _sds((T, GLA_H * HK), F32), _sds((1, GLA_H * HK), F32)],
        scratch_shapes=[pltpu.VMEM((HV, HK), F32)], name="gla_bwd", compiler_params=_cp("parallel"),
    )(zmain, zmain, zmain, ug, gate_b, S, do)


def _gla_post_fn(o, r, g):
    outs = []
    for h in range(GLA_H):
        on = _rms_parts(o[:, h * HV:(h + 1) * HV])[0] * g
        outs.append(on * _silu(r[:, h * HV:(h + 1) * HV]))
    return jnp.concatenate(outs, axis=1)


def _gla_post_bwd_fn(da, o, r, g):
    dos, drs = [], []
    dg = jnp.zeros((1, HV), F32)
    for h in range(GLA_H):
        sl = slice(h * HV, (h + 1) * HV)
        xh, rstd = _rms_parts(o[:, sl])
        drs.append(da[:, sl] * xh * g * _dsilu(r[:, sl]))
        don = da[:, sl] * _silu(r[:, sl])
        dg = dg + jnp.sum(don * xh, axis=0, keepdims=True)
        dxh = don * g
        dos.append(rstd * (dxh - xh * jnp.mean(dxh * xh, axis=-1, keepdims=True)))
    return jnp.concatenate(dos, axis=1), jnp.concatenate(drs, axis=1), dg


def conv_fwd(u, dw, dwb):
    T, C = u.shape
    TB = min(T, 256)

    def body(u_ref, w_ref, b_ref, y_ref, pad_ref):
        pad_ref[0:CONV_PAD, :] = jnp.zeros((CONV_PAD, LANES), F32)
        pad_ref[CONV_PAD:CONV_PAD + T, :] = u_ref[...]
        off = CONV_PAD - (CONV_W - 1)
        for t0 in range(0, T, TB):
            acc = jnp.zeros((TB, LANES), F32) + b_ref[...]
            for j in range(CONV_W):
                acc = acc + w_ref[j:j + 1, :] * pad_ref[t0 + off + j:t0 + off + j + TB, :]
            y_ref[t0:t0 + TB, :] = acc

    col = lambda i: (0, i)
    return pl.pallas_call(
        body, grid=(C // LANES,),
        in_specs=[pl.BlockSpec((T, LANES), col), pl.BlockSpec((CONV_W, LANES), col), pl.BlockSpec((1, LANES), col)],
        out_specs=pl.BlockSpec((T, LANES), col), out_shape=_sds((T, C), F32),
        scratch_shapes=[pltpu.VMEM((T + CONV_PAD, LANES), F32)], name="conv_fwd", compiler_params=_cp("parallel"),
    )(u, dw, dwb)


def conv_bwd(dy, u, dw):
    T, C = u.shape
    TB = min(T, 256)

    def body(dy_ref, u_ref, w_ref, du_ref, dw_ref, db_ref, upad, dypad):
        upad[0:CONV_PAD, :] = jnp.zeros((CONV_PAD, LANES), F32)
        upad[CONV_PAD:CONV_PAD + T, :] = u_ref[...]
        dypad[0:T, :] = dy_ref[...]
        dypad[T:T + CONV_PAD, :] = jnp.zeros((CONV_PAD, LANES), F32)
        off = CONV_PAD - (CONV_W - 1)
        for t0 in range(0, T, TB):
            acc = jnp.zeros((TB, LANES), F32)
            for j in range(CONV_W):
                s = t0 + (CONV_W - 1) - j
                acc = acc + w_ref[j:j + 1, :] * dypad[s:s + TB, :]
            du_ref[t0:t0 + TB, :] = acc
        for j in range(CONV_W):
            acc = jnp.zeros((TB, LANES), F32)
            for t0 in range(0, T, TB):
                acc = acc + dy_ref[t0:t0 + TB, :] * upad[t0 + off + j:t0 + off + j + TB, :]
            dw_ref[j:j + 1, :] = jnp.sum(acc, axis=0, keepdims=True)
        db_ref[...] = jnp.sum(dy_ref[...], axis=0, keepdims=True)

    col = lambda i: (0, i)
    return pl.pallas_call(
        body, grid=(C // LANES,),
        in_specs=[pl.BlockSpec((T, LANES), col), pl.BlockSpec((T, LANES), col), pl.BlockSpec((CONV_W, LANES), col)],
        out_specs=[pl.BlockSpec((T, LANES), col), pl.BlockSpec((CONV_W, LANES), col), pl.BlockSpec((1, LANES), col)],
        out_shape=[_sds((T, C), F32), _sds((CONV_W, C), F32), _sds((1, C), F32)],
        scratch_shapes=[pltpu.VMEM((T + CONV_PAD, LANES), F32), pltpu.VMEM((T + CONV_PAD, LANES), F32)],
        name="conv_bwd", compiler_params=_cp("parallel"),
    )(dy, u, dw)


def _ln_parts(x):
    mu = jnp.mean(x, axis=-1, keepdims=True)
    xc = x - mu
    rstd = lax.rsqrt(jnp.mean(xc * xc, axis=-1, keepdims=True) + EPS)
    return xc * rstd, rstd


def _ln_silu_fn(x, g, b):
    return _silu(_ln_parts(x)[0] * g + b)


def _ln_silu_bwd_fn(dbo, x, g, b):
    xh, rstd = _ln_parts(x)
    dy = dbo * _dsilu(xh * g + b)
    dyg = dy * g
    dx = rstd * (dyg - jnp.mean(dyg, axis=-1, keepdims=True) - xh * jnp.mean(dyg * xh, axis=-1, keepdims=True))
    return dx, jnp.sum(dy * xh, axis=0, keepdims=True), jnp.sum(dy, axis=0, keepdims=True)


def _glu_bwd_fn(du, ca, cb):
    s = _sigmoid(cb)
    return du * s, du * ca * s * (1.0 - s)


DIAGS = QB + KW


def _onehot_diag():
    j = lax.broadcasted_iota(jnp.int32, (REL_PAD, DIAGS), 1)
    i = lax.broadcasted_iota(jnp.int32, (REL_PAD, DIAGS), 0)
    return (i == jnp.clip(KW - j, -REL_CLIP, REL_CLIP) + REL_CLIP).astype(F32)


def relbias_tile(rbp):
    def body(rb_ref, o_ref, e_ref):
        e_ref[...] = _dot(rb_ref[...], _onehot_diag(), NN, HI)
        tc = lax.shift_right_logical(lax.broadcasted_iota(jnp.int32, (QB, KW), 0), 6)
        wc = lax.shift_right_logical(lax.broadcasted_iota(jnp.int32, (QB, KW), 1), 6)
        ok = jnp.logical_and(wc >= tc, wc <= tc + LEFT_CHUNKS)
        for h in range(ATT_H):
            spread = pltpu.roll(jnp.broadcast_to(e_ref[h:h + 1, :], (QB, DIAGS)), KW, 1, stride=1, stride_axis=0)
            o_ref[h] = jnp.where(ok, spread[:, :KW], NEG)

    return pl.pallas_call(body, out_shape=_sds((ATT_H, QB, KW), F32), name="relbias_tile",
                          scratch_shapes=[pltpu.VMEM((ATT_H, DIAGS), F32)], compiler_params=_cp())(rbp)


def relbias_reduce(dbm):
    def body(d_ref, o_ref, e_ref):
        u = lax.broadcasted_iota(jnp.int32, (QB, QB), 0)
        t = lax.broadcasted_iota(jnp.int32, (QB, QB), 1)
        flip = (u + t == QB - 1).astype(F32)
        for h in range(ATT_H):
            padded = jnp.concatenate([d_ref[h], jnp.zeros((QB, QB), F32)], axis=1)
            lined = pltpu.roll(_dot(flip, padded, NN, HI), 1, 1, stride=1, stride_axis=0)
            e_ref[h:h + 1, :] = jnp.sum(lined, axis=0, keepdims=True)
        o_ref[...] = _dot(e_ref[...], _onehot_diag(), NT, HI)

    return pl.pallas_call(body, out_shape=_sds((ATT_H, REL_PAD), F32), name="relbias_reduce",
                          scratch_shapes=[pltpu.VMEM((ATT_H, DIAGS), F32)], compiler_params=_cp())(dbm)


def _att_scores(q_ref, kp_ref, bm_ref, i):
    q0 = pl.multiple_of(i * QB, QB)
    kw = kp_ref[pl.ds(q0, KW), :]
    s = _bdot(q_ref[...], kw, NT) * (HD ** -0.5) + bm_ref[...]
    w = lax.broadcasted_iota(jnp.int32, (QB, KW), 1)
    return jnp.where(w + q0 >= PADK, s, NEG), kw, q0


def attn_fwd(qkv, kvp, bm):
    T = qkv.shape[0]
    D = ATT_H * HD

    def body(q_ref, kp_ref, vp_ref, bm_ref, o_ref, lse_ref):
        s, _, q0 = _att_scores(q_ref, kp_ref, bm_ref, pl.program_id(1))
        m = jnp.max(s, axis=-1, keepdims=True)
        e = jnp.exp(s - m)
        l = jnp.sum(e, axis=-1, keepdims=True)
        o_ref[...] = _bdot(e * (1.0 / l), vp_ref[pl.ds(q0, KW), :], NN).astype(o_ref.dtype)
        lse_ref[...] = m + jnp.log(l)

    return pl.pallas_call(
        body, grid=(ATT_H, T // QB),
        in_specs=[pl.BlockSpec((QB, HD), lambda h, i: (i, h)), pl.BlockSpec((T + PADK, HD), lambda h, i: (0, h)),
                  pl.BlockSpec((T + PADK, HD), lambda h, i: (0, ATT_H + h)),
                  pl.BlockSpec((None, QB, KW), lambda h, i: (h, 0, 0))],
        out_specs=[pl.BlockSpec((QB, HD), lambda h, i: (i, h)), pl.BlockSpec((None, QB, 1), lambda h, i: (h, i, 0))],
        out_shape=[_sds((T, D), BF), _sds((ATT_H, T, 1), F32)], name="attn_fwd",
        compiler_params=_cp("parallel", "arbitrary"),
    )(qkv, kvp, kvp, bm)


def attn_bwd(qkv, kvp, bm, o, lse, do):
    T = qkv.shape[0]
    D = ATT_H * HD

    def body(q_ref, kp_ref, vp_ref, bm_ref, o_ref, lse_ref, do_ref, dq_ref, dkp_ref, dvp_ref, dbm_ref):
        i = pl.program_id(1)

        @pl.when(i == 0)
        def _():
            dkp_ref[...] = jnp.zeros_like(dkp_ref)
            dvp_ref[...] = jnp.zeros_like(dvp_ref)
            dbm_ref[...] = jnp.zeros_like(dbm_ref)

        s, kw, q0 = _att_scores(q_ref, kp_ref, bm_ref, i)
        p = jnp.exp(s - lse_ref[...])
        dout = do_ref[...]
        dp = _bdot(dout, vp_ref[pl.ds(q0, KW), :], NT)
        delta = jnp.sum(_f(dout) * _f(o_ref[...]), axis=-1, keepdims=True)
        ds = p * (dp - delta)
        dq_ref[...] = (_bdot(ds, kw, NN) * (HD ** -0.5)).astype(dq_ref.dtype)
        dkp_ref[pl.ds(q0, KW), :] += _bdot(ds, q_ref[...], TN) * (HD ** -0.5)
        dvp_ref[pl.ds(q0, KW), :] += _bdot(p, dout, TN)
        dbm_ref[...] += ds

    qspec = pl.BlockSpec((QB, HD), lambda h, i: (i, h))
    kspec = pl.BlockSpec((T + PADK, HD), lambda h, i: (0, h))
    bspec = pl.BlockSpec((None, QB, KW), lambda h, i: (h, 0, 0))
    return pl.pallas_call(
        body, grid=(ATT_H, T // QB),
        in_specs=[qspec, kspec, pl.BlockSpec((T + PADK, HD), lambda h, i: (0, ATT_H + h)), bspec, qspec,
                  pl.BlockSpec((None, QB, 1), lambda h, i: (h, i, 0)), qspec],
        out_specs=[qspec, kspec, kspec, bspec],
        out_shape=[_sds((T, D), BF), _sds((T + PADK, D), F32), _sds((T + PADK, D), F32), _sds((ATT_H, QB, KW), F32)],
        name="attn_bwd", compiler_params=_cp("parallel", "arbitrary"),
    )(qkv, kvp, kvp, bm, o, lse, do)


def _final_fn(x, tgt, g):
    D = x.shape[1]
    xh, rstd = _rms_parts(x)
    diff = xh * g - tgt
    dy = diff * (1.0 / D)
    dxh = dy * g
    dx = rstd * (dxh - xh * jnp.mean(dxh * xh, axis=-1, keepdims=True))
    loss = jnp.sum(jnp.sum(diff * diff, axis=-1, keepdims=True), axis=0, keepdims=True) * (0.5 / D)
    return dx, jnp.sum(dy * xh, axis=0, keepdims=True), jnp.broadcast_to(loss, (1, LANES))


def _adamw_fn(w, g, m, v):
    m = ADAM_B1 * m + (1.0 - ADAM_B1) * g
    v = ADAM_B2 * v + (1.0 - ADAM_B2) * (g * g)
    m_hat = m / (1.0 - ADAM_B1 ** ADAM_STEP)
    v_hat = v / (1.0 - ADAM_B2 ** ADAM_STEP)
    delta = -ADAM_LR * (m_hat / (jnp.sqrt(v_hat) + ADAM_EPS) + ADAM_WD * w)
    return g, delta, m, v


def adamw(w, g, m, v, name):
    shape = w.shape
    C = shape[-1]
    R = w.size // C
    outs = rowwise(name, _adamw_fn, [t.reshape(R, C) for t in (w, g, m, v)], [], [(C, F32)] * 4)
    return tuple(t.reshape(shape) for t in outs)


def adamw_unit(w, m, v, g, lead, outs, name):
    R, B = w.shape[-2:]
    tb = _row_block(R, 256)
    if outs is None:
        outs = [lax.empty(w.shape, F32) for _ in range(4)]

    def body(w_ref, m_ref, v_ref, g_ref, *rest):
        for ref, val in zip(rest[4:], _adamw_fn(w_ref[...], g_ref[...], m_ref[...], v_ref[...])):
            ref[...] = val

    native = pl.BlockSpec((None,) * len(lead) + (tb, B), lambda r: tuple(lead) + (r, 0))
    return pl.pallas_call(
        body, grid=(R // tb,), in_specs=[native] * 3 + [pl.BlockSpec((tb, B), lambda r: (r, 0))] + [HBM_SPEC] * 4,
        out_specs=[native] * 4, out_shape=[_sds(w.shape, F32)] * 4, input_output_aliases={4 + i: i for i in range(4)},
        name=name, compiler_params=_cp("arbitrary"),
    )(w, m, v, g, *outs)


WEIGHTS = ['ffn_norm', 'ffn_w_gate', 'ffn_w_up', 'ffn_w_down', 'mix_norm', 'ab_w_in', 'gla_gate_w', 'gla_gate_b',
           'gla_norm_g', 'conv_dw', 'conv_dw_b', 'conv_ln_g', 'conv_ln_b', 'ab_w_out', 'att_w_qkv', 'att_rel_bias',
           'att_w_o', 'pl_norm', 'pl_w_gate', 'pl_w_proj', 'final_norm']
BIG = ['ffn_w_gate', 'ffn_w_up', 'ffn_w_down', 'ab_w_in', 'ab_w_out', 'att_w_qkv', 'att_w_o', 'pl_w_gate', 'pl_w_proj']


def _pack(parts, rows):
    flat = jnp.concatenate([p.reshape(-1) for p in parts])
    return jnp.pad(flat, (0, rows * LANES - flat.shape[0])).reshape(rows, LANES)


def _unpack(flat, shapes):
    out, pos = [], 0
    for s in shapes:
        n = 1
        for d in s:
            n *= d
        out.append(flat[pos:pos + n].reshape(s))
        pos += n
    return out


def _rows_for(shapes):
    n = sum(math.prod(s) for s in shapes)
    return -(-n // (8 * LANES)) * 8


def kernel(x, p, ffn_norm, ffn_w_gate, ffn_w_up, ffn_w_down, mix_norm, ab_w_in, gla_gate_w, gla_gate_b, gla_norm_g, conv_dw, conv_dw_b, conv_ln_g, conv_ln_b, ab_w_out, att_w_qkv, att_rel_bias, att_w_o, pl_norm, pl_w_gate, pl_w_proj, final_norm, loss_target, m_ffn_norm, m_ffn_w_gate, m_ffn_w_up, m_ffn_w_down, m_mix_norm, m_ab_w_in, m_gla_gate_w, m_gla_gate_b, m_gla_norm_g, m_conv_dw, m_conv_dw_b, m_conv_ln_g, m_conv_ln_b, m_ab_w_out, m_att_w_qkv, m_att_rel_bias, m_att_w_o, m_pl_norm, m_pl_w_gate, m_pl_w_proj, m_final_norm, v_ffn_norm, v_ffn_w_gate, v_ffn_w_up, v_ffn_w_down, v_mix_norm, v_ab_w_in, v_gla_gate_w, v_gla_gate_b, v_gla_norm_g, v_conv_dw, v_conv_dw_b, v_conv_ln_g, v_conv_ln_b, v_ab_w_out, v_att_w_qkv, v_att_rel_bias, v_att_w_o, v_pl_norm, v_pl_w_gate, v_pl_w_proj, v_final_norm):
    env = dict(locals())
    W = {n: env[n] for n in WEIGHTS}
    M = {n: env["m_" + n] for n in WEIGHTS}
    V = {n: env["v_" + n] for n in WEIGHTS}
    xc, yc_, cc = lax.axis_index("x"), lax.axis_index("y"), lax.axis_index("c")

    x0 = x[0]
    tgt = loss_target[0]
    T, D = x0.shape
    fs = ffn_w_gate.shape[-1]
    ws = ab_w_in.shape[-1]
    AB_IN = N_CHIPS * ws
    gz0 = 2 * GLA_H * HK + 2 * GLA_H * HV
    sidx = jnp.stack([2 * xc + yc_, cc]).astype(jnp.int32)

    def ffn_keys(i, j):
        return [('ffn_w_gate', (i, j)), ('ffn_w_up', (i, j)), ('ffn_w_down', (i, j))]

    order = [ffn_keys(0, 0), [('ab_w_in', (0,)), ('ab_w_out', (0,))], ffn_keys(0, 1),
             [('pl_w_gate', (0,)), ('pl_w_proj', (0,))], ffn_keys(1, 0), [('att_w_qkv', (0,)), ('att_w_o', (0,))],
             ffn_keys(1, 1), [('pl_w_gate', (1,)), ('pl_w_proj', (1,))]]
    full, in_flight = {}, []
    tok = jnp.zeros(TOKEN, F32)
    for s, keys in enumerate(order):
        parts = [cast_unit(W[n], lead, sidx, "cast_" + n + "".join(str(i) for i in lead)) for n, lead in keys]
        ssem, rsem, thru, tok = split_start("gather_start_%d" % s, parts, _gather_copies, 3 * len(keys), tok)
        in_flight.append((ssem, rsem, thru))

    def arrive(s, after):
        ssem, rsem, thru = in_flight[s]
        landed = split_wait("gather_wait_%d" % s, ssem, rsem, thru, _gather_copies, after)
        for key, buf in zip(order[s], gather_pass_on(landed, "gather_pass_on_%d" % s)):
            full[key] = buf

    small_sharded = [ffn_norm, gla_gate_w, conv_dw]
    rows_s = _rows_for([t.shape for t in small_sharded])
    got = allgather8(_pack(small_sharded, rows_s), "gather_small").reshape(N_CHIPS, 2, rows_s * LANES)[:, 0]
    per_chip = [_unpack(got[k], [t.shape for t in small_sharded]) for k in range(N_CHIPS)]
    ffn_norm_f, gate_w_f, conv_dw_f = [jnp.concatenate([per_chip[k][t] for k in range(N_CHIPS)], axis=-1)
                                       for t in range(3)]
    gate_w_p = jnp.pad(gate_w_f[0], ((0, LANES - 16), (0, 0)))
    conv_w = conv_dw_f[0]
    rb_p = jnp.pad(att_rel_bias[0], ((0, 0), (0, REL_PAD - att_rel_bias.shape[-1])))

    G = {}
    small_g = {}

    def ffn_w(i, j):
        return full['ffn_w_gate', (i, j)], full['ffn_w_up', (i, j)], full['ffn_w_down', (i, j)]

    saved = {}
    xs = x0
    arrive(0, tok)
    xs, saved['f00'] = ffn_fwd(xs, ffn_norm_f[0, 0][None], *ffn_w(0, 0), dep=tok)
    arrive(1, xs)
    w_in = jnp.transpose(full['ab_w_in', (0,)], (1, 0, 2)).reshape(D, AB_IN)
    w_main = jnp.concatenate([w_in[:, :gz0], w_in[:, gz0 + 16:]], axis=1)
    w_gz = jnp.pad(w_in[:, gz0:gz0 + 16], ((0, 0), (0, LANES - 16)))
    w_out = full['ab_w_out', (0,)].reshape(D, D)

    def mixer0_fwd(xin):
        h = rms_fwd(xin, mix_norm[0][None], "mix0_rms")
        zmain = mm_nn("ab_in", h, w_main, F32, 1024)
        gzp = mm_nn("ab_gz", h, w_gz, F32, LANES)
        ug = mm_nn("gla_gate", gzp, gate_w_p, F32, GLA_H * HK)
        o, S = gla_fwd(zmain, ug, gla_gate_b)
        a_out = rowwise("gla_post", _gla_post_fn, [o, ("cols", zmain, GLA_H * HV, 2)], [gla_norm_g], [(GLA_H * HV, BF)])
        u = rowwise("conv_glu", lambda a, b: a * _sigmoid(b), [("cols", zmain, 1024, 3), ("cols", zmain, 1024, 4)], [],
                    [(1024, F32)])
        yc = conv_fwd(u, conv_w, conv_dw_b)
        b_out = rowwise("conv_ln", _ln_silu_fn, [yc], [conv_ln_g, conv_ln_b], [(1024, BF)])
        cat = jnp.concatenate([a_out, b_out], axis=1)
        x2 = mm_nn("ab_out", cat, w_out, F32, 1024, res=xin)
        return x2, (xin, h, zmain, gzp, ug, o, S, u, yc, cat)

    xs, saved['m0'] = mixer0_fwd(xs)
    arrive(2, xs)
    xs, saved['f01'] = ffn_fwd(xs, ffn_norm_f[0, 1][None], *ffn_w(0, 1))
    ks = D // N_CHIPS
    dp = p.shape[-1]

    def pl_fwd(xin, i):
        h = rms_fwd(xin, pl_norm[i][None], "pl_rms")
        tm = min(T, 1024)
        u = mm_nn("pl_gate", h, full['pl_w_gate', (i,)].reshape(D, D), F32, 1024)
        e = mm("pl_proj", NN, (T // tm, N_CHIPS, 1), p[i, 0], (tm, dp), _mk, full['pl_w_proj', (i,)],
               (None, dp, ks), lambda m, n, k: (n, 0, 0), (tm, ks), _mn, _sds((T, D), F32))
        x2 = rowwise("pl_mix", lambda xv, uv, ev: xv + _sigmoid(uv) * ev, [xin, u, e], [], [(D, F32)])
        return x2, (xin, h, u, e)

    arrive(3, xs)
    xs, saved['p0'] = pl_fwd(xs, 0)
    arrive(4, xs)
    xs, saved['f10'] = ffn_fwd(xs, ffn_norm_f[1, 0][None], *ffn_w(1, 0))
    arrive(5, xs)
    w_qkv = full['att_w_qkv', (0,)]
    w_o = full['att_w_o', (0,)].reshape(D, D)
    qs = w_qkv.shape[-1]

    bm = relbias_tile(rb_p)

    def mixer1_fwd(xin):
        h = rms_fwd(xin, mix_norm[1][None], "mix1_rms")
        tm, tn = min(T, 1024), 512
        per = qs // tn
        qkv = mm("att_qkv", NN, (T // tm, 3 * D // tn, 1), h, (tm, D), _mk, w_qkv, (None, D, tn),
                 lambda m, n, k: (n // per, k, n % per), (tm, tn), _mn, _sds((T, 3 * D), BF))
        kvp = jnp.pad(qkv[:, D:], ((PADK, 0), (0, 0)))
        o, lse = attn_fwd(qkv, kvp, bm)
        x2 = mm_nn("att_o", o, w_o, F32, 1024, res=xin)
        return x2, (xin, h, qkv, kvp, o, lse)

    xs, saved['m1'] = mixer1_fwd(xs)
    arrive(6, xs)
    xs, saved['f11'] = ffn_fwd(xs, ffn_norm_f[1, 1][None], *ffn_w(1, 1))
    arrive(7, xs)
    xs, saved['p1'] = pl_fwd(xs, 1)

    dx, small_g['final_norm'], loss_acc = rowwise("loss_head", _final_fn, [xs, tgt], [final_norm[None]], [(D, F32)],
                                                  [(1, D), (1, LANES)])
    loss = lax.psum(loss_acc[0, 0], ("x", "y", "c"))

    def pl_bwd(dx2, sv, i, dep):
        xin, h, u, e = sv
        tm = min(T, 1024)

        def fn(d, uv, ev):
            s = _sigmoid(uv)
            return d * s, d * ev * s * (1.0 - s)

        de, du = rowwise("pl_mix_bwd", fn, [dx2, u, e], [], [(D, BF), (D, BF)], dep=dep)
        G['pl_w_proj', (i,)] = mm("pl_dproj", TN, (1, N_CHIPS, 1), p[i, 0], (T, dp), _km, de, (T, ks), _kn,
                                  (None, dp, ks), lambda m, n, k: (n, 0, 0), _sds((N_CHIPS, dp, ks), BF))
        G['pl_w_gate', (i,)] = mm_tn("pl_dgate", h, du, BF, ks, 1024).reshape(N_CHIPS, ks, D)
        dh = mm_nt("pl_dh", du, full['pl_w_gate', (i,)].reshape(D, D), F32, 1024)
        return rms_bwd(xin, pl_norm[i][None], dh, dx2, "pl_rms_bwd")

    def ffn_b(dx2, key, i, j, s, dep):
        def sends(g_wg, g_wu, g_wd):
            G['ffn_w_gate', (i, j)], G['ffn_w_up', (i, j)], G['ffn_w_down', (i, j)] = g_wg, g_wu, g_wd
            tok = scatter(s, dep)
            return to_chips(tok) if s == 0 else tok

        return ffn_bwd(dx2, saved[key], ffn_norm_f[i, j][None], *ffn_w(i, j), dep, sends)

    def mixer1_bwd(dx2, sv, dep):
        xin, h, qkv, kvp, o, lse = sv
        do = mm_nt("att_do", dx2, w_o, BF, 1024, dep=dep)
        dwo = mm_tn("att_dwo", o, dx2, BF, 1024, 1024, dep=dep)
        dq, dkp, dvp, dbm = attn_bwd(qkv, kvp, bm, o, lse, do)
        dqkv = jnp.concatenate([dq, dkp[PADK:].astype(BF), dvp[PADK:].astype(BF)], axis=1)
        tm, tn = min(T, 1024), 512
        per = qs // tn
        dwqkv = mm("att_dwqkv", TN, (D // 1024, 3 * D // tn, 1), h, (T, 1024), _km, dqkv, (T, tn), _kn,
                   (None, 1024, tn), lambda m, n, k: (n // per, m, n % per), _sds((N_CHIPS, D, qs), BF))
        dh = mm("att_dh", NT, (T // tm, D // 1024, N_CHIPS), dqkv, (tm, qs), _mk, w_qkv, (None, 1024, qs),
                lambda m, n, k: (k, n, 0), (tm, 1024), _mn, _sds((T, D), F32))
        dxn, dgain = rms_bwd(xin, mix_norm[1][None], dh, dx2, "mix1_rms_bwd")
        return dxn, dgain, dwo, dwqkv, relbias_reduce(dbm)

    def mixer0_bwd(dx2, sv, dep):
        xin, h, zmain, gzp, ug, o, S, u, yc, cat = sv
        dcat = mm_nt("ab_dcat", dx2, w_out, F32, 1024, dep=dep)
        dwout = mm_tn("ab_dwout", cat, dx2, BF, 1024, 1024, dep=dep)
        do, dr, dgn = rowwise("gla_post_bwd", _gla_post_bwd_fn,
                              [("cols", dcat, GLA_H * HV, 0), o, ("cols", zmain, GLA_H * HV, 2)], [gla_norm_g],
                              [(GLA_H * HV, F32), (GLA_H * HV, BF)], [(1, HV)])
        dyc, dlg, dlb = rowwise("conv_ln_bwd", _ln_silu_bwd_fn, [("cols", dcat, 1024, 1), yc], [conv_ln_g, conv_ln_b],
                                [(1024, F32)], [(1, 1024), (1, 1024)])
        du, ddw, ddwb = conv_bwd(dyc, u, conv_w)
        dca, dcb = rowwise("conv_glu_bwd", _glu_bwd_fn, [du, ("cols", zmain, 1024, 3), ("cols", zmain, 1024, 4)], [],
                           [(1024, BF), (1024, BF)])
        dq, dk, dv, dug, dgb = gla_bwd(zmain, ug, gla_gate_b, S, do)
        dgw = mm_tn("gla_dgate_w", gzp, dug, F32, LANES, GLA_H * HK)
        dgzp = mm_nt("gla_dgz", dug, gate_w_p, F32, LANES)
        dzm = jnp.concatenate([dq, dk, dv, dr, dca, dcb], axis=1)
        dwmain = mm_tn("ab_dwmain", h, dzm, BF, 1024, 1024)
        dwgz = mm_tn("ab_dwgz", h, dgzp, BF, 1024, LANES)
        dh = mm_nt("ab_dh_gz", dgzp, w_gz, F32, 1024, res=mm_nt("ab_dh", dzm, w_main, F32, 512))
        dxn, dgain = rms_bwd(xin, mix_norm[0][None], dh, dx2, "mix0_rms_bwd")
        dwin = jnp.concatenate([dwmain[:, :gz0], dwgz[:, :16], dwmain[:, gz0:]], axis=1)
        g_win = jnp.transpose(dwin.reshape(D, N_CHIPS, ws), (1, 0, 2))
        return dxn, dgain, g_win, dwout, (dgn, dlg, dlb, ddw, ddwb, dgw[:16], dgb)

    scattering, exchanging = [], []

    def tags_of(s):
        return [n + "".join(str(i) for i in lead) for n, lead in order[s]]

    def to_chips(tok):
        if not exchanging:
            return tok
        s, ssem, rsem, thru = exchanging.pop()
        n_u = len(order[s])
        landed = split_wait("sibling_wait_%d" % s, ssem, rsem, thru, _sibling_copies, tok)
        hs = [add_sibling(g4, got, sidx, tag) for g4, got, tag in zip(landed[:n_u], landed[n_u:], tags_of(s))]
        land = [lax.empty((3,) + h.shape[1:], h.dtype) for h in hs]
        ssem, rsem, thru, tok = split_start("scatter_start_%d" % s, hs + land, _scatter_copies, 3 * n_u, tok)
        scattering.append((s, ssem, rsem, thru))
        return tok

    def scatter(s, tok):
        g4s = [G[key].reshape(N_CHIPS, 2, G[key].shape[1] // 2, G[key].shape[2]) for key in order[s]]
        land = [lax.empty((N_CHIPS,) + g4.shape[2:], g4.dtype) for g4 in g4s]
        ssem, rsem, thru, tok = split_start("sibling_start_%d" % s, g4s + land, _sibling_copies, N_CHIPS * len(g4s), tok)
        tok = to_chips(tok)
        exchanging.append((s, ssem, rsem, thru))
        return tok

    dpl, dffn, dmix = [None, None], [[None, None], [None, None]], [None, None]
    tok = jnp.zeros(TOKEN, F32)
    dx, dpl[1] = pl_bwd(dx, saved['p1'], 1, tok)
    tok = scatter(7, tok)
    dx, dffn[1][1], tok = ffn_b(dx, 'f11', 1, 1, 6, tok)
    dx, dmix[1], dwo, dwqkv, drb = mixer1_bwd(dx, saved['m1'], tok)
    G['att_w_qkv', (0,)] = dwqkv
    G['att_w_o', (0,)] = dwo.reshape((N_CHIPS,) + att_w_o.shape[1:])
    tok = scatter(5, tok)
    dx, dffn[1][0], tok = ffn_b(dx, 'f10', 1, 0, 4, tok)
    dx, dpl[0] = pl_bwd(dx, saved['p0'], 0, tok)
    tok = scatter(3, tok)
    dx, dffn[0][1], tok = ffn_b(dx, 'f01', 0, 1, 2, tok)
    dx, dmix[0], g_win, dwout, (dgn, dlg, dlb, ddw, ddwb, dgw, dgb) = mixer0_bwd(dx, saved['m0'], tok)
    G['ab_w_in', (0,)] = g_win
    G['ab_w_out', (0,)] = dwout.reshape((N_CHIPS,) + ab_w_out.shape[1:])
    tok = scatter(1, tok)
    dx, dffn[0][0], tok = ffn_b(dx, 'f00', 0, 0, 0, tok)
    grad_x = dx[None]

    grads, outs = {}, {}
    sharing = []

    def update(tok):
        if not sharing:
            return
        s, ssem, rsem, thru = sharing.pop()
        landed = split_wait("share_wait_%d" % s, ssem, rsem, thru, _share_copies, tok)
        for (n, lead), tag, r in zip(order[s], tags_of(s), landed):
            outs[n] = adamw_unit(W[n], M[n], V[n], r.reshape(W[n].shape[-2:]), lead, outs.get(n), "adamw_" + tag)

    after = dx
    for s, ssem, rsem, thru in scattering:
        n_u = len(order[s])
        landed = split_wait("scatter_wait_%d" % s, ssem, rsem, thru, _scatter_copies, after)
        rs = [add_chips(h, got2, sidx, tag) for h, got2, tag in zip(landed[:n_u], landed[n_u:], tags_of(s))]
        ssem, rsem, thru, tok = split_start("share_start_%d" % s, rs, _share_copies, n_u, tok)
        update(tok)
        sharing.append((s, ssem, rsem, thru))
        after = tok
    update(tok)

    small_g['ffn_norm'] = jnp.stack([jnp.stack([dffn[i][j][0] for j in range(2)]) for i in range(2)])
    small_g['mix_norm'] = jnp.concatenate(dmix, axis=0)
    small_g['gla_gate_w'] = dgw[None]
    small_g['gla_gate_b'] = dgb
    small_g['gla_norm_g'] = dgn
    small_g['conv_dw'] = ddw[None]
    small_g['conv_dw_b'] = ddwb
    small_g['conv_ln_g'] = dlg
    small_g['conv_ln_b'] = dlb
    small_g['att_rel_bias'] = drb[None, :, :att_rel_bias.shape[-1]]
    small_g['pl_norm'] = jnp.concatenate(dpl, axis=0)
    small_g['final_norm'] = small_g['final_norm'][0]
    small_names = [n for n in WEIGHTS if n not in BIG]
    small_shapes = [small_g[n].shape for n in small_names]
    rows_g = _rows_for(small_shapes)
    allp = allgather8(_pack([small_g[n] for n in small_names], rows_g), "gather_small_grads")
    summed = rowwise("sum_small_grads", lambda *v: (((v[0] + v[1]) + (v[2] + v[3])) + ((v[4] + v[5]) + (v[6] + v[7]))),
                     [("leads", allp.reshape(8, rows_g, LANES), (d,)) for d in range(8)], [], [(LANES, F32)], tb=rows_g)
    for n, g in zip(small_names, _unpack(summed.reshape(-1), small_shapes)):
        grads[n] = g
    chip = 2 * xc + yc_
    for n, axis in (('ffn_norm', 2), ('gla_gate_w', 2), ('conv_dw', 2)):
        width = W[n].shape[axis]
        grads[n] = lax.dynamic_slice_in_dim(grads[n], chip * width, width, axis)

    for n in small_names:
        outs[n] = adamw(W[n], grads[n], M[n], V[n], "adamw_" + n)
    return (loss, grad_x, *[outs[n][0] for n in WEIGHTS], *[outs[n][1] for n in WEIGHTS],
            *[outs[n][2] for n in WEIGHTS], *[outs[n][3] for n in WEIGHTS])
```

```python
import math

import jax
import jax.numpy as jnp
from jax import lax
from jax.experimental import pallas as pl
from jax.experimental.pallas import tpu as pltpu

F32 = jnp.float32
BF = jnp.bfloat16
MESH = pl.DeviceIdType.MESH
HI = lax.Precision.HIGHEST
V7X_VMEM_LIMIT = 56 * 1024 * 1024
LANES = 128
EPS = 1e-6
NEG = -1e30

CHUNK = 64
LEFT_CHUNKS = 8
QB = 256
KW = QB + LEFT_CHUNKS * CHUNK
PADK = LEFT_CHUNKS * CHUNK
REL_CLIP = 128
REL_PAD = 384
ATT_H = 16
HD = 128
GLA_H = 4
HK = 128
HV = 256
GLA_TAU = 16.0
CONV_W = 31
CONV_PAD = 32
N_CHIPS = 4

ADAM_LR = 0.001
ADAM_B1 = 0.9
ADAM_B2 = 0.999
ADAM_EPS = 1e-08
ADAM_WD = 0.01
ADAM_STEP = 10

NN = ((1,), (0,))
NT = ((1,), (1,))
TN = ((0,), (0,))


def _dot(a, b, dims, prec=None):
    return lax.dot_general(a, b, (dims, ((), ())), preferred_element_type=F32, precision=prec)


def _bdot(a, b, dims):
    return _dot(a.astype(BF), b.astype(BF), dims)


def _cp(*sem):
    return pltpu.CompilerParams(dimension_semantics=sem if sem else None, vmem_limit_bytes=V7X_VMEM_LIMIT)


def _sds(shape, dtype):
    return jax.ShapeDtypeStruct(shape, dtype)


def _sigmoid(x):
    return 1.0 / (1.0 + jnp.exp(-x))


def _silu(x):
    return x * _sigmoid(x)


def _dsilu(x):
    s = _sigmoid(x)
    return s * (1.0 + x * (1.0 - s))


def _f(x):
    return x.astype(F32)


def _row_item(item, tb):
    if not isinstance(item, tuple):
        return item, (tb, item.shape[1]), lambda i, s: (i, 0)
    kind, arr = item[0], item[1]
    if kind == "cols":
        return arr, (tb, item[2]), lambda i, s, blk=item[3]: (i, blk)
    if kind == "leads":
        lead = tuple(item[2])
        return arr, (None,) * len(lead) + (tb, arr.shape[-1]), lambda i, s: lead + (i, 0)
    if kind == "dyn":
        return arr, (None, tb, arr.shape[-1]), lambda i, s, sel=item[2]: (s[sel], i, 0)
    if kind == "dyn4":
        nb = arr.shape[2] // tb
        return arr, (None, None, tb, arr.shape[-1]), lambda i, s, sel=item[2]: (i // nb, s[sel], i % nb, 0)
    raise ValueError(kind)


def _row_block(n, cap):
    for cand in range(min(cap, n) // 16 * 16, 0, -16):
        if n % cand == 0:
            return cand
    return n


def rowwise(name, fn, rows, bcast, outs, accs=(), tb=256, n_rows=None, sidx=None, row_period=None, dep=None):
    if n_rows is None:
        first = rows[0][1] if isinstance(rows[0], tuple) else rows[0]
        n_rows = first.shape[-2]
    tb = _row_block(n_rows if row_period is None else row_period, tb)
    items = [_row_item(it, tb) for it in rows]
    n_r, n_b, n_o = len(rows), len(bcast), len(outs)
    pre = 0 if sidx is None else 1

    def wrap(f):
        return (lambda i: f(i, None)) if sidx is None else (lambda i, s: f(i, s))

    def body(*refs):
        refs = refs[pre:]
        r, b = refs[:n_r], refs[n_r:n_r + n_b]
        refs = refs[n_r + n_b + (dep is not None):]
        o, a = refs[:n_o], refs[n_o:]
        res = fn(*[v[...] for v in r], *[v[...] for v in b])
        if not isinstance(res, tuple):
            res = (res,)
        for ref, val in zip(o, res[:n_o]):
            ref[...] = val.astype(ref.dtype)
        if a:
            @pl.when(pl.program_id(0) == 0)
            def _():
                for ref in a:
                    ref[...] = jnp.zeros_like(ref)
            for ref, val in zip(a, res[n_o:]):
                ref[...] += val

    in_specs = [pl.BlockSpec(bs, wrap(f)) for _, bs, f in items]
    in_specs += [pl.BlockSpec(v.shape, wrap(lambda i, s, nd=v.ndim: (0,) * nd)) for v in bcast]
    out_specs, out_shape = [], []
    for o in outs:
        if o[0] == "dyn":
            _, L, c, dt, sel = o
            out_specs.append(pl.BlockSpec((None, tb, c), wrap(lambda i, s, sel=sel: (s[sel], i, 0))))
            out_shape.append(_sds((L, n_rows, c), dt))
        else:
            c, dt = o
            out_specs.append(pl.BlockSpec((tb, c), wrap(lambda i, s: (i, 0))))
            out_shape.append(_sds((n_rows, c), dt))
    out_specs += [pl.BlockSpec(sh, wrap(lambda i, s: (0, 0))) for sh in accs]
    out_shape += [_sds(sh, F32) for sh in accs]
    operands = [a for a, _, _ in items] + list(bcast)
    if dep is not None:
        operands.append(dep)
        in_specs.append(pl.BlockSpec(TOKEN, wrap(lambda i, s: (0, 0))))
    grid = (n_rows // tb,)
    if sidx is None:
        res = pl.pallas_call(body, grid=grid, in_specs=in_specs, out_specs=out_specs, out_shape=out_shape, name=name,
                             compiler_params=_cp("arbitrary"))(*operands)
    else:
        spec = pltpu.PrefetchScalarGridSpec(num_scalar_prefetch=1, grid=grid, in_specs=in_specs, out_specs=out_specs)
        res = pl.pallas_call(body, grid_spec=spec, out_shape=out_shape, name=name,
                             compiler_params=_cp("arbitrary"))(sidx, *operands)
    return res[0] if len(res) == 1 else tuple(res)


TOKEN = (8, LANES)


def mm(name, dims, grid, a, a_bs, a_im, b, b_bs, b_im, o_bs, o_im, out, scale=1.0, res=None, dep=None):
    nk = grid[2]
    acc_shape = tuple(d for d in o_bs if d is not None)

    def body(*refs):
        a_ref, b_ref = refs[0], refs[1]
        pos = 2
        res_ref = None
        if res is not None:
            res_ref = refs[pos]
            pos += 1
        if dep is not None:
            pos += 1
        o_ref = refs[pos]
        part = _bdot(a_ref[...], b_ref[...], dims)

        def finish(acc):
            v = acc * scale if scale != 1.0 else acc
            if res_ref is not None:
                v = v + _f(res_ref[...])
            o_ref[...] = v.astype(o_ref.dtype)

        if nk == 1:
            finish(part)
        else:
            acc_ref = refs[pos + 1]
            k = pl.program_id(2)

            @pl.when(k == 0)
            def _():
                acc_ref[...] = part

            @pl.when(k > 0)
            def _():
                acc_ref[...] += part

            @pl.when(k == nk - 1)
            def _():
                finish(acc_ref[...])

    operands = [a, b]
    in_specs = [pl.BlockSpec(a_bs, a_im), pl.BlockSpec(b_bs, b_im)]
    if res is not None:
        operands.append(res)
        in_specs.append(pl.BlockSpec(o_bs, o_im))
    if dep is not None:
        operands.append(dep)
        in_specs.append(pl.BlockSpec(TOKEN, lambda m, n, k: (0, 0)))
    scratch = [pltpu.VMEM(acc_shape, F32)] if nk > 1 else []
    return pl.pallas_call(body, grid=grid, in_specs=in_specs, out_specs=pl.BlockSpec(o_bs, o_im), out_shape=out,
                          scratch_shapes=scratch, name=name,
                          compiler_params=_cp("parallel", "parallel", "arbitrary"))(*operands)


def _mk(m, n, k):
    return (m, k)


def _mn(m, n, k):
    return (m, n)


def _km(m, n, k):
    return (k, m)


def _kn(m, n, k):
    return (k, n)


def _nk(m, n, k):
    return (n, k)


def mm_nn(name, a, b, out_dtype, tn, tk=None, scale=1.0, res=None):
    T, K = a.shape
    N = b.shape[1]
    tm, tk, tn = min(T, 1024), K if tk is None else min(tk, K), min(tn, N)
    return mm(name, NN, (T // tm, N // tn, K // tk), a, (tm, tk), _mk, b, (tk, tn), _kn, (tm, tn), _mn,
              _sds((T, N), out_dtype), scale=scale, res=res)


def mm_nt(name, a, b, out_dtype, tn, tk=None, scale=1.0, res=None, dep=None):
    T, K = a.shape
    N = b.shape[0]
    tm, tk, tn = min(T, 1024), K if tk is None else min(tk, K), min(tn, N)
    return mm(name, NT, (T // tm, N // tn, K // tk), a, (tm, tk), _mk, b, (tn, tk), _nk, (tm, tn), _mn,
              _sds((T, N), out_dtype), scale=scale, res=res, dep=dep)


def mm_tn(name, a, b, out_dtype, tm, tn, scale=1.0, dep=None):
    T, M = a.shape
    N = b.shape[1]
    tk, tm, tn = T, min(tm, M), min(tn, N)
    return mm(name, TN, (M // tm, N // tn, T // tk), a, (tk, tm), _km, b, (tk, tn), _kn, (tm, tn), _mn,
              _sds((M, N), out_dtype), scale=scale, dep=dep)


HBM_SPEC = pl.BlockSpec(memory_space=pl.ANY)


def _place():
    x, y, c = lax.axis_index("x"), lax.axis_index("y"), lax.axis_index("c")
    chips = [(1 - x, y), (x, 1 - y), (1 - x, 1 - y)]
    return x, y, c, chips


def allgather8(v, name):
    m_per, n = v.shape

    def body(x_ref, out_ref, send_sems, recv_sems, local_sem):
        x, y, c, chips = _place()
        me, sibling = (x, y, c), (x, y, 1 - c)

        def rows(px, py, pc):
            return out_ref.at[pl.ds((4 * px + 2 * py + pc) * m_per, m_per), :]

        def copy(k, block, to, src=None):
            return pltpu.make_async_remote_copy(
                src_ref=rows(*block) if src is None else src, dst_ref=rows(*block),
                send_sem=send_sems.at[k], recv_sem=recv_sems.at[k], device_id=to, device_id_type=MESH)

        mine = pltpu.make_async_copy(x_ref, rows(*me), local_sem)
        mine.start()
        first = [copy(0, me, sibling, src=x_ref)]
        first += [copy(1 + j, me, (*chip, c), src=x_ref) for j, chip in enumerate(chips)]
        for cp in first:
            cp.start()
        passed = [copy(4 + j, (*chip, c), sibling) for j, chip in enumerate(chips)]
        for j, chip in enumerate(chips):
            copy(1 + j, (*chip, c), me).wait_recv()
            passed[j].start()
        copy(0, sibling, me).wait_recv()
        for j, chip in enumerate(chips):
            copy(4 + j, (*chip, 1 - c), me).wait_recv()
        for cp in first + passed:
            cp.wait_send()
        mine.wait()

    return pl.pallas_call(
        body, out_shape=_sds((8 * m_per, n), v.dtype), in_specs=[pl.BlockSpec(memory_space=pltpu.VMEM)],
        out_specs=pl.BlockSpec(memory_space=pltpu.VMEM), name=name,
        scratch_shapes=[pltpu.SemaphoreType.DMA((7,)), pltpu.SemaphoreType.DMA((7,)), pltpu.SemaphoreType.DMA],
    )(v)


D2D_PIECES = 4


def _pieces(ref, n):
    rows = ref.shape[0] // n
    return [ref.at[pl.ds(q * rows, rows)] for q in range(n)]


def _start_in_pieces(src, dst, ssem, rsem, to, n):
    for s_q, d_q in zip(_pieces(src, n), _pieces(dst, n)):
        pltpu.make_async_remote_copy(src_ref=s_q, dst_ref=d_q, send_sem=ssem, recv_sem=rsem, device_id=to,
                                     device_id_type=MESH).start()


def _whole(src, dst, ssem, rsem, to):
    return pltpu.make_async_remote_copy(src_ref=src, dst_ref=dst, send_sem=ssem, recv_sem=rsem, device_id=to,
                                        device_id_type=MESH)


HBM_ONLY = pl.BlockSpec(memory_space=pltpu.HBM)
SEM_SPEC = pl.BlockSpec(memory_space=pltpu.SEMAPHORE)
DATAFLOW = pltpu.SideEffectType.DATAFLOW_SIDE_EFFECTING


def _gather_copies(bufs):
    x, y, c, chips = _place()
    out = []
    for o_ref in bufs:
        hr = o_ref.shape[1] // 2

        def half(chip, o_ref=o_ref, hr=hr):
            return o_ref.at[2 * chip[0] + chip[1], pl.ds(c * hr, hr)]

        out += [(half((x, y)), half((x, y)), (*chip, c), half(chip)) for chip in chips]
    return out


def _scatter_copies(arrays):
    x, y, c, chips = _place()
    n = len(arrays) // 2
    out = []
    for h_ref, got_ref in zip(arrays[:n], arrays[n:]):
        out += [(h_ref.at[2 * chip[0] + chip[1]], got_ref.at[j], (*chip, c), got_ref.at[j])
                for j, chip in enumerate(chips)]
    return out


def split_start(name, arrays, copies_of, n_copies, after):
    n = len(arrays)

    def body(*refs):
        ssem, rsem = refs[n + 1], refs[n + 2]
        for i, (src, dst, to, _) in enumerate(copies_of(refs[n + 3:2 * n + 3])):
            _whole(src, dst, ssem.at[i], rsem.at[i], to).start()
        refs[2 * n + 3][...] = jnp.zeros(TOKEN, F32)

    res = pl.pallas_call(
        body, name=name,
        out_shape=(pltpu.SemaphoreType.DMA((n_copies,)), pltpu.SemaphoreType.DMA((n_copies,)),
                   *[pltpu.HBM(a.shape, a.dtype) for a in arrays], _sds(TOKEN, F32)),
        in_specs=[HBM_ONLY] * n + [pl.BlockSpec(memory_space=pl.ANY)],
        out_specs=(SEM_SPEC, SEM_SPEC, *[HBM_ONLY] * n, pl.BlockSpec(memory_space=pltpu.VMEM)),
        input_output_aliases={i: 2 + i for i in range(n)},
        compiler_params=pltpu.CompilerParams(has_side_effects=DATAFLOW),
    )(*[pltpu.with_memory_space_constraint(a, pltpu.HBM) for a in arrays], after)
    return res[0], res[1], list(res[2:2 + n]), res[2 + n]


def split_wait(name, ssem, rsem, arrays, copies_of, after):
    n = len(arrays)

    def body(*refs):
        s_ref, r_ref = refs[n], refs[n + 1]
        for i, (src, _, to, mine) in enumerate(copies_of(refs[:n])):
            cp = _whole(src, mine, s_ref.at[i], r_ref.at[i], to)
            cp.wait_send()
            cp.wait_recv()

    res = pl.pallas_call(
        body, name=name, out_shape=tuple(pltpu.HBM(a.shape, a.dtype) for a in arrays),
        in_specs=[HBM_ONLY] * n + [SEM_SPEC, SEM_SPEC, pl.BlockSpec(memory_space=pl.ANY)],
        out_specs=tuple([HBM_ONLY] * n), input_output_aliases={i: i for i in range(n)},
        compiler_params=pltpu.CompilerParams(has_side_effects=DATAFLOW),
    )(*arrays, ssem, rsem, after)
    return list(res)


def gather_pass_on(bufs, name):
    n = len(bufs)

    def body(*refs):
        o_refs, ssem, rsem = refs[n:2 * n], refs[2 * n], refs[2 * n + 1]
        x, y, c, chips = _place()
        sib = (x, y, 1 - c)

        def half(o_ref, chip, h):
            hr = o_ref.shape[1] // 2
            return o_ref.at[2 * chip[0] + chip[1], pl.ds(h * hr, hr)]

        for u, o_ref in enumerate(o_refs):
            for j, chip in enumerate(chips):
                _start_in_pieces(half(o_ref, chip, c), half(o_ref, chip, c), ssem.at[3 * u + j], rsem.at[3 * u + j], sib,
                                 D2D_PIECES)
        for u, o_ref in enumerate(o_refs):
            for j, chip in enumerate(chips):
                _whole(half(o_ref, chip, c), half(o_ref, chip, 1 - c), ssem.at[3 * u + j], rsem.at[3 * u + j], sib).wait()

    res = pl.pallas_call(
        body, out_shape=tuple(_sds(b.shape, b.dtype) for b in bufs), in_specs=[HBM_SPEC] * n,
        out_specs=tuple([HBM_SPEC] * n), name=name, input_output_aliases={i: i for i in range(n)},
        scratch_shapes=[pltpu.SemaphoreType.DMA((3 * n,)), pltpu.SemaphoreType.DMA((3 * n,))],
    )(*bufs)
    return list(res)


def _sibling_copies(arrays):
    x, y, c, _ = _place()
    n = len(arrays) // 2
    out = []
    for g_ref, got_ref in zip(arrays[:n], arrays[n:]):
        out += [(g_ref.at[k, 1 - c], got_ref.at[k], (x, y, 1 - c), got_ref.at[k]) for k in range(N_CHIPS)]
    return out


def _allgather_copies(arrays):
    x, y, c, _ = _place()
    (buf,) = arrays
    mine = buf.at[4 * x + 2 * y + c]
    out = []
    for fx, fy, fc in [(0, 0, 1), (0, 1, 0), (0, 1, 1), (1, 0, 0), (1, 0, 1), (1, 1, 0), (1, 1, 1)]:
        px, py, pc = (1 - x if fx else x), (1 - y if fy else y), (1 - c if fc else c)
        out.append((mine, mine, (px, py, pc), buf.at[4 * px + 2 * py + pc]))
    return out


def _share_copies(arrays):
    x, y, c, _ = _place()
    return [(r_ref.at[c], r_ref.at[c], (x, y, 1 - c), r_ref.at[1 - c]) for r_ref in arrays]


def add_sibling(g4, got, sidx, tag):
    _, _, hr, B = g4.shape
    h = rowwise("rs_add2_" + tag, lambda a, b: _f(a) + _f(b), [("dyn4", g4, 1), got.reshape(N_CHIPS * hr, B)], [],
                [(B, BF)], tb=512, n_rows=N_CHIPS * hr, sidx=sidx, row_period=hr)
    return h.reshape(N_CHIPS, hr, B)


def add_chips(h, got2, sidx, tag):
    _, hr, B = h.shape
    return rowwise("rs_add4_" + tag, lambda a, b, c, d: ((_f(a) + _f(b)) + _f(c)) + _f(d),
                   [("dyn", h, 0), ("leads", got2, (0,)), ("leads", got2, (1,)), ("leads", got2, (2,))], [],
                   [("dyn", 2, B, F32, 1)], tb=512, n_rows=hr, sidx=sidx)


def _rms_parts(x):
    rstd = lax.rsqrt(jnp.mean(x * x, axis=-1, keepdims=True) + EPS)
    return x * rstd, rstd


def rms_fwd(x, g, name, dep=None):
    return rowwise(name, lambda xv, gv: _rms_parts(xv)[0] * gv, [x], [g], [(x.shape[1], BF)], dep=dep)


def _rms_bwd_fn(x, dh, dres, g):
    xh, rstd = _rms_parts(x)
    dxh = _f(dh) * g
    dx = rstd * (dxh - xh * jnp.mean(dxh * xh, axis=-1, keepdims=True)) + dres
    return dx, jnp.sum(_f(dh) * xh, axis=0, keepdims=True)


def rms_bwd(x, g, dh, dres, name):
    D = x.shape[1]
    return rowwise(name, _rms_bwd_fn, [x, dh, dres], [g], [(D, F32)], [(1, D)])


def cast_unit(w, lead, sidx, name):
    R, B = w.shape[-2:]
    return rowwise(name, lambda v: v, [("leads", w, lead)], [], [("dyn", N_CHIPS, B, BF, 0)], n_rows=R, sidx=sidx)


FFN_TM = 512


def ffn_up(h, wg, wu):
    T, D = h.shape
    fs = wg.shape[-1]
    tm = min(T, FFN_TM)

    def body(h_ref, wg_ref, wu_ref, zg_ref, zu_ref, a_ref):
        hb = h_ref[...]
        g = _bdot(hb, wg_ref[...], NN)
        u = _bdot(hb, wu_ref[...], NN)
        zg_ref[...] = g.astype(zg_ref.dtype)
        zu_ref[...] = u.astype(zu_ref.dtype)
        a_ref[...] = (_silu(g) * u).astype(a_ref.dtype)

    w_spec = pl.BlockSpec((None, D, fs), lambda n, m: (n, 0, 0))
    o_spec = pl.BlockSpec((tm, fs), lambda n, m: (m, n))
    return pl.pallas_call(
        body, grid=(N_CHIPS, T // tm), in_specs=[pl.BlockSpec((tm, D), lambda n, m: (m, 0)), w_spec, w_spec],
        out_specs=[o_spec] * 3, out_shape=[_sds((T, N_CHIPS * fs), BF)] * 3, name="ffn_up",
        compiler_params=_cp("parallel", "arbitrary"))(h, wg, wu)


def ffn_dz(dx2, wd2, zg, zu, dep):
    T, D = dx2.shape
    F = wd2.shape[0]
    fs = F // N_CHIPS
    tm = min(T, FFN_TM)

    def body(dx_ref, wd_ref, zg_ref, zu_ref, dep_ref, dzg_ref, dzu_ref):
        da = 0.5 * _bdot(dx_ref[...], wd_ref[...], NT)
        dzg, dzu = _swiglu_bwd_fn(da, zg_ref[...], zu_ref[...])
        dzg_ref[...] = dzg.astype(dzg_ref.dtype)
        dzu_ref[...] = dzu.astype(dzu_ref.dtype)

    z_spec = pl.BlockSpec((tm, fs), lambda n, m: (m, n))
    return pl.pallas_call(
        body, grid=(N_CHIPS, T // tm),
        in_specs=[pl.BlockSpec((tm, D), lambda n, m: (m, 0)), pl.BlockSpec((fs, D), lambda n, m: (n, 0)), z_spec, z_spec,
                  pl.BlockSpec(TOKEN, lambda n, m: (0, 0))],
        out_specs=[z_spec] * 2, out_shape=[_sds((T, F), BF)] * 2, name="ffn_dz",
        compiler_params=_cp("parallel", "arbitrary"))(dx2, wd2, zg, zu, dep)


def ffn_dh(dzg, dzu, wg, wu, dep):
    T, F = dzg.shape
    _, D, fs = wg.shape
    tm, tn = min(T, 1024), 1024

    def body(g_ref, u_ref, wg_ref, wu_ref, dep_ref, o_ref, acc_ref):
        k = pl.program_id(2)
        part = _bdot(g_ref[...], wg_ref[...], NT) + _bdot(u_ref[...], wu_ref[...], NT)

        @pl.when(k == 0)
        def _():
            acc_ref[...] = part

        @pl.when(k > 0)
        def _():
            acc_ref[...] += part

        @pl.when(k == N_CHIPS - 1)
        def _():
            o_ref[...] = acc_ref[...]

    z_spec = pl.BlockSpec((tm, fs), _mk)
    w_spec = pl.BlockSpec((None, tn, fs), lambda m, n, k: (k, n, 0))
    return pl.pallas_call(
        body, grid=(T // tm, D // tn, N_CHIPS),
        in_specs=[z_spec, z_spec, w_spec, w_spec, pl.BlockSpec(TOKEN, lambda m, n, k: (0, 0))],
        out_specs=pl.BlockSpec((tm, tn), _mn), out_shape=_sds((T, D), F32),
        scratch_shapes=[pltpu.VMEM((tm, tn), F32)], name="ffn_dh",
        compiler_params=_cp("parallel", "parallel", "arbitrary"))(dzg, dzu, wg, wu, dep)


def ffn_fwd(xin, gain, wg, wu, wd, dep=None):
    T, D = xin.shape
    F = N_CHIPS * wg.shape[-1]
    h = rms_fwd(xin, gain, "ffn_rms", dep)
    zg, zu, a = ffn_up(h, wg, wu)
    if callable(wd):
        wd = wd(a)
    x2 = mm_nn("ffn_down", a, wd.reshape(F, D), F32, 512, scale=0.5, res=xin)
    return x2, (xin, h, zg, zu, a)


def _swiglu_bwd_fn(da, zg, zu):
    da, zg, zu = _f(da), _f(zg), _f(zu)
    return da * zu * _dsilu(zg), da * _silu(zg)


def ffn_bwd(dx2, saved, gain, wg, wu, wd, dep, sends):
    xin, h, zg, zu, a = saved
    T, D = xin.shape
    fs = wg.shape[-1]
    F = N_CHIPS * fs
    tn = 1024
    dzg, dzu = ffn_dz(dx2, wd.reshape(F, D), zg, zu, dep)
    g_wd = mm_tn("ffn_dwd", a, dx2, BF, fs, 1024, scale=0.5, dep=dep).reshape(N_CHIPS, fs, D)

    def dw(dz, nm):
        return mm(nm, TN, (D // tn, N_CHIPS, 1), h, (T, tn), _km, dz, (T, fs), _kn,
                  (None, tn, fs), lambda m, n, k: (n, m, 0), _sds((N_CHIPS, D, fs), BF))

    g_wg, g_wu = dw(dzg, "ffn_dwg"), dw(dzu, "ffn_dwu")
    dep = sends(g_wg, g_wu, g_wd)
    dh = ffn_dh(dzg, dzu, wg, wu, dep)
    dx, dgain = rms_bwd(xin, gain, dh, dx2, "ffn_rms_bwd")
    return dx, dgain, dep


def _gla_chunk(q_ref, k_ref, v_ref, u_ref, b_ref, rows):
    r = lax.broadcasted_iota(jnp.int32, (CHUNK, CHUNK), 0)
    c = lax.broadcasted_iota(jnp.int32, (CHUNK, CHUNK), 1)
    causal = c <= r
    u = u_ref[rows, :] + b_ref[...]
    g = (jnp.minimum(u, 0.0) - jnp.log(1.0 + jnp.exp(-jnp.abs(u)))) * (1.0 / GLA_TAU)
    b = _dot(causal.astype(F32), g, NN, HI)
    last = lax.broadcasted_iota(jnp.int32, (CHUNK, HK), 0) == CHUNK - 1
    blast = jnp.sum(jnp.where(last, b, 0.0), axis=0, keepdims=True)
    eb = jnp.exp(b)
    qb = q_ref[rows, :] * (HK ** -0.5) * eb
    k = k_ref[rows, :]
    kb = k * jnp.exp(-b)
    kl = k * jnp.exp(blast - b)
    A = jnp.where(causal, _bdot(qb, kb, NT), 0.0)
    return causal, u, b, blast, eb, qb, kb, kl, A


def _gla_in_specs(T):
    return [pl.BlockSpec((T, HK), lambda h: (0, h)), pl.BlockSpec((T, HK), lambda h: (0, GLA_H + h)),
            pl.BlockSpec((T, HV), lambda h: (0, GLA_H + h)), pl.BlockSpec((T, HK), lambda h: (0, h)),
            pl.BlockSpec((1, HK), lambda h: (0, h))]


def gla_fwd(zmain, ug, gate_b):
    T = zmain.shape[0]
    nC = T // CHUNK

    def body(q_ref, k_ref, v_ref, u_ref, b_ref, o_ref, s_ref, st_ref):
        st_ref[...] = jnp.zeros_like(st_ref)

        def step(n, carry):
            rows = pl.ds(pl.multiple_of(n * CHUNK, CHUNK), CHUNK)
            _, _, _, blast, _, qb, _, kl, A = _gla_chunk(q_ref, k_ref, v_ref, u_ref, b_ref, rows)
            v = v_ref[rows, :]
            ST = st_ref[...]
            s_ref[n] = ST
            o_ref[rows, :] = _bdot(qb, ST, NT) + _bdot(A, v, NN)
            st_ref[...] = ST * jnp.exp(blast) + _bdot(v, kl, TN)
            return carry

        lax.fori_loop(0, nC, step, 0)

    return pl.pallas_call(
        body, grid=(GLA_H,), in_specs=_gla_in_specs(T),
        out_specs=[pl.BlockSpec((T, HV), lambda h: (0, h)), pl.BlockSpec((nC, None, HV, HK), lambda h: (0, h, 0, 0))],
        out_shape=[_sds((T, GLA_H * HV), F32), _sds((nC, GLA_H, HV, HK), F32)],
        scratch_shapes=[pltpu.VMEM((HV, HK), F32)], name="gla_fwd", compiler_params=_cp("parallel"),
    )(zmain, zmain, zmain, ug, gate_b)


def gla_bwd(zmain, ug, gate_b, S, do):
    T = zmain.shape[0]
    nC = T // CHUNK

    def body(q_ref, k_ref, v_ref, u_ref, b_ref, s_ref, do_ref, dq_ref, dk_ref, dv_ref, du_ref, dgb_ref, dst_ref):
        dst_ref[...] = jnp.zeros_like(dst_ref)
        dgb_ref[...] = jnp.zeros_like(dgb_ref)

        def step(it, carry):
            n = nC - 1 - it
            rows = pl.ds(pl.multiple_of(n * CHUNK, CHUNK), CHUNK)
            causal, u, b, blast, eb, qb, kb, kl, A = _gla_chunk(q_ref, k_ref, v_ref, u_ref, b_ref, rows)
            v = v_ref[rows, :]
            dout = do_ref[rows, :]
            ST = s_ref[n]
            dST = dst_ref[...]
            elast = jnp.exp(blast)
            dA = jnp.where(causal, _bdot(dout, v, NT), 0.0)
            dv_ref[rows, :] = (_bdot(A, dout, TN) + _bdot(kl, dST, NT)).astype(dv_ref.dtype)
            dqb = _bdot(dout, ST, NN) + _bdot(dA, kb, NN)
            dkb = _bdot(dA, qb, TN)
            dkl = _bdot(v, dST, NN)
            ddec = jnp.sum(ST * dST, axis=0, keepdims=True)
            dst_ref[...] = dST * elast + _bdot(dout, qb, TN)
            dq_ref[rows, :] = (dqb * eb * (HK ** -0.5)).astype(dq_ref.dtype)
            dk_ref[rows, :] = (dkb * jnp.exp(-b) + dkl * jnp.exp(blast - b)).astype(dk_ref.dtype)
            db = dqb * qb - dkb * kb - dkl * kl
            dbl = jnp.sum(dkl * kl, axis=0, keepdims=True) + elast * ddec
            dg = _dot(jnp.logical_not(causal).astype(F32) + jnp.where(
                lax.broadcasted_iota(jnp.int32, (CHUNK, CHUNK), 0) == lax.broadcasted_iota(jnp.int32, (CHUNK, CHUNK), 1),
                1.0, 0.0), db, NN, HI) + dbl
            du = dg * (1.0 / GLA_TAU) / (1.0 + jnp.exp(u))
            du_ref[rows, :] = du
            dgb_ref[...] += jnp.sum(du, axis=0, keepdims=True)
            return carry

        lax.fori_loop(0, nC, step, 0)

    specs = _gla_in_specs(T) + [pl.BlockSpec((nC, None, HV, HK), lambda h: (0, h, 0, 0)),
                                pl.BlockSpec((T, HV), lambda h: (0, h))]
    return pl.pallas_call(
        body, grid=(GLA_H,), in_specs=specs,
        out_specs=[pl.BlockSpec((T, HK), lambda h: (0, h)), pl.BlockSpec((T, HK), lambda h: (0, h)),
                   pl.BlockSpec((T, HV), lambda h: (0, h)), pl.BlockSpec((T, HK), lambda h: (0, h)),
                   pl.BlockSpec((1, HK), lambda h: (0, h))],
        out_shape=[_sds((T, GLA_H * HK), BF), _sds((T, GLA_H * HK), BF), _sds((T, GLA_H * HV), BF),
                   _sds((T, GLA_H * HK), F32), _sds((1, GLA_H * HK), F32)],
        scratch_shapes=[pltpu.VMEM((HV, HK), F32)], name="gla_bwd", compiler_params=_cp("parallel"),
    )(zmain, zmain, zmain, ug, gate_b, S, do)


def _gla_post_fn(o, r, g):
    outs = []
    for h in range(GLA_H):
        on = _rms_parts(o[:, h * HV:(h + 1) * HV])[0] * g
        outs.append(on * _silu(r[:, h * HV:(h + 1) * HV]))
    return jnp.concatenate(outs, axis=1)


def _gla_post_bwd_fn(da, o, r, g):
    dos, drs = [], []
    dg = jnp.zeros((1, HV), F32)
    for h in range(GLA_H):
        sl = slice(h * HV, (h + 1) * HV)
        xh, rstd = _rms_parts(o[:, sl])
        drs.append(da[:, sl] * xh * g * _dsilu(r[:, sl]))
        don = da[:, sl] * _silu(r[:, sl])
        dg = dg + jnp.sum(don * xh, axis=0, keepdims=True)
        dxh = don * g
        dos.append(rstd * (dxh - xh * jnp.mean(dxh * xh, axis=-1, keepdims=True)))
    return jnp.concatenate(dos, axis=1), jnp.concatenate(drs, axis=1), dg


def conv_fwd(u, dw, dwb):
    T, C = u.shape
    TB = min(T, 256)

    def body(u_ref, w_ref, b_ref, y_ref, pad_ref):
        pad_ref[0:CONV_PAD, :] = jnp.zeros((CONV_PAD, LANES), F32)
        pad_ref[CONV_PAD:CONV_PAD + T, :] = u_ref[...]
        off = CONV_PAD - (CONV_W - 1)
        for t0 in range(0, T, TB):
            acc = jnp.zeros((TB, LANES), F32) + b_ref[...]
            for j in range(CONV_W):
                acc = acc + w_ref[j:j + 1, :] * pad_ref[t0 + off + j:t0 + off + j + TB, :]
            y_ref[t0:t0 + TB, :] = acc

    col = lambda i: (0, i)
    return pl.pallas_call(
        body, grid=(C // LANES,),
        in_specs=[pl.BlockSpec((T, LANES), col), pl.BlockSpec((CONV_W, LANES), col), pl.BlockSpec((1, LANES), col)],
        out_specs=pl.BlockSpec((T, LANES), col), out_shape=_sds((T, C), F32),
        scratch_shapes=[pltpu.VMEM((T + CONV_PAD, LANES), F32)], name="conv_fwd", compiler_params=_cp("parallel"),
    )(u, dw, dwb)


def conv_bwd(dy, u, dw):
    T, C = u.shape
    TB = min(T, 256)

    def body(dy_ref, u_ref, w_ref, du_ref, dw_ref, db_ref, upad, dypad):
        upad[0:CONV_PAD, :] = jnp.zeros((CONV_PAD, LANES), F32)
        upad[CONV_PAD:CONV_PAD + T, :] = u_ref[...]
        dypad[0:T, :] = dy_ref[...]
        dypad[T:T + CONV_PAD, :] = jnp.zeros((CONV_PAD, LANES), F32)
        off = CONV_PAD - (CONV_W - 1)
        for t0 in range(0, T, TB):
            acc = jnp.zeros((TB, LANES), F32)
            for j in range(CONV_W):
                s = t0 + (CONV_W - 1) - j
                acc = acc + w_ref[j:j + 1, :] * dypad[s:s + TB, :]
            du_ref[t0:t0 + TB, :] = acc
        for j in range(CONV_W):
            acc = jnp.zeros((TB, LANES), F32)
            for t0 in range(0, T, TB):
                acc = acc + dy_ref[t0:t0 + TB, :] * upad[t0 + off + j:t0 + off + j + TB, :]
            dw_ref[j:j + 1, :] = jnp.sum(acc, axis=0, keepdims=True)
        db_ref[...] = jnp.sum(dy_ref[...], axis=0, keepdims=True)

    col = lambda i: (0, i)
    return pl.pallas_call(
        body, grid=(C // LANES,),
        in_specs=[pl.BlockSpec((T, LANES), col), pl.BlockSpec((T, LANES), col), pl.BlockSpec((CONV_W, LANES), col)],
        out_specs=[pl.BlockSpec((T, LANES), col), pl.BlockSpec((CONV_W, LANES), col), pl.BlockSpec((1, LANES), col)],
        out_shape=[_sds((T, C), F32), _sds((CONV_W, C), F32), _sds((1, C), F32)],
        scratch_shapes=[pltpu.VMEM((T + CONV_PAD, LANES), F32), pltpu.VMEM((T + CONV_PAD, LANES), F32)],
        name="conv_bwd", compiler_params=_cp("parallel"),
    )(dy, u, dw)


def _ln_parts(x):
    mu = jnp.mean(x, axis=-1, keepdims=True)
    xc = x - mu
    rstd = lax.rsqrt(jnp.mean(xc * xc, axis=-1, keepdims=True) + EPS)
    return xc * rstd, rstd


def _ln_silu_fn(x, g, b):
    return _silu(_ln_parts(x)[0] * g + b)


def _ln_silu_bwd_fn(dbo, x, g, b):
    xh, rstd = _ln_parts(x)
    dy = dbo * _dsilu(xh * g + b)
    dyg = dy * g
    dx = rstd * (dyg - jnp.mean(dyg, axis=-1, keepdims=True) - xh * jnp.mean(dyg * xh, axis=-1, keepdims=True))
    return dx, jnp.sum(dy * xh, axis=0, keepdims=True), jnp.sum(dy, axis=0, keepdims=True)


def _glu_bwd_fn(du, ca, cb):
    s = _sigmoid(cb)
    return du * s, du * ca * s * (1.0 - s)


DIAGS = QB + KW


def _onehot_diag():
    j = lax.broadcasted_iota(jnp.int32, (REL_PAD, DIAGS), 1)
    i = lax.broadcasted_iota(jnp.int32, (REL_PAD, DIAGS), 0)
    return (i == jnp.clip(KW - j, -REL_CLIP, REL_CLIP) + REL_CLIP).astype(F32)


def relbias_tile(rbp):
    def body(rb_ref, o_ref, e_ref):
        e_ref[...] = _dot(rb_ref[...], _onehot_diag(), NN, HI)
        tc = lax.shift_right_logical(lax.broadcasted_iota(jnp.int32, (QB, KW), 0), 6)
        wc = lax.shift_right_logical(lax.broadcasted_iota(jnp.int32, (QB, KW), 1), 6)
        ok = jnp.logical_and(wc >= tc, wc <= tc + LEFT_CHUNKS)
        for h in range(ATT_H):
            spread = pltpu.roll(jnp.broadcast_to(e_ref[h:h + 1, :], (QB, DIAGS)), KW, 1, stride=1, stride_axis=0)
            o_ref[h] = jnp.where(ok, spread[:, :KW], NEG)

    return pl.pallas_call(body, out_shape=_sds((ATT_H, QB, KW), F32), name="relbias_tile",
                          scratch_shapes=[pltpu.VMEM((ATT_H, DIAGS), F32)], compiler_params=_cp())(rbp)


def relbias_reduce(dbm):
    def body(d_ref, o_ref, e_ref):
        u = lax.broadcasted_iota(jnp.int32, (QB, QB), 0)
        t = lax.broadcasted_iota(jnp.int32, (QB, QB), 1)
        flip = (u + t == QB - 1).astype(F32)
        for h in range(ATT_H):
            padded = jnp.concatenate([d_ref[h], jnp.zeros((QB, QB), F32)], axis=1)
            lined = pltpu.roll(_dot(flip, padded, NN, HI), 1, 1, stride=1, stride_axis=0)
            e_ref[h:h + 1, :] = jnp.sum(lined, axis=0, keepdims=True)
        o_ref[...] = _dot(e_ref[...], _onehot_diag(), NT, HI)

    return pl.pallas_call(body, out_shape=_sds((ATT_H, REL_PAD), F32), name="relbias_reduce",
                          scratch_shapes=[pltpu.VMEM((ATT_H, DIAGS), F32)], compiler_params=_cp())(dbm)


def _att_scores(q_ref, kp_ref, bm_ref, i):
    q0 = pl.multiple_of(i * QB, QB)
    kw = kp_ref[pl.ds(q0, KW), :]
    s = _bdot(q_ref[...], kw, NT) * (HD ** -0.5) + bm_ref[...]
    w = lax.broadcasted_iota(jnp.int32, (QB, KW), 1)
    return jnp.where(w + q0 >= PADK, s, NEG), kw, q0


def attn_fwd(qkv, kvp, bm):
    T = qkv.shape[0]
    D = ATT_H * HD

    def body(q_ref, kp_ref, vp_ref, bm_ref, o_ref, lse_ref):
        s, _, q0 = _att_scores(q_ref, kp_ref, bm_ref, pl.program_id(1))
        m = jnp.max(s, axis=-1, keepdims=True)
        e = jnp.exp(s - m)
        l = jnp.sum(e, axis=-1, keepdims=True)
        o_ref[...] = _bdot(e * (1.0 / l), vp_ref[pl.ds(q0, KW), :], NN).astype(o_ref.dtype)
        lse_ref[...] = m + jnp.log(l)

    return pl.pallas_call(
        body, grid=(ATT_H, T // QB),
        in_specs=[pl.BlockSpec((QB, HD), lambda h, i: (i, h)), pl.BlockSpec((T + PADK, HD), lambda h, i: (0, h)),
                  pl.BlockSpec((T + PADK, HD), lambda h, i: (0, ATT_H + h)),
                  pl.BlockSpec((None, QB, KW), lambda h, i: (h, 0, 0))],
        out_specs=[pl.BlockSpec((QB, HD), lambda h, i: (i, h)), pl.BlockSpec((None, QB, 1), lambda h, i: (h, i, 0))],
        out_shape=[_sds((T, D), BF), _sds((ATT_H, T, 1), F32)], name="attn_fwd",
        compiler_params=_cp("parallel", "arbitrary"),
    )(qkv, kvp, kvp, bm)


def attn_bwd(qkv, kvp, bm, o, lse, do):
    T = qkv.shape[0]
    D = ATT_H * HD

    def body(q_ref, kp_ref, vp_ref, bm_ref, o_ref, lse_ref, do_ref, dq_ref, dkp_ref, dvp_ref, dbm_ref):
        i = pl.program_id(1)

        @pl.when(i == 0)
        def _():
            dkp_ref[...] = jnp.zeros_like(dkp_ref)
            dvp_ref[...] = jnp.zeros_like(dvp_ref)
            dbm_ref[...] = jnp.zeros_like(dbm_ref)

        s, kw, q0 = _att_scores(q_ref, kp_ref, bm_ref, i)
        p = jnp.exp(s - lse_ref[...])
        dout = do_ref[...]
        dp = _bdot(dout, vp_ref[pl.ds(q0, KW), :], NT)
        delta = jnp.sum(_f(dout) * _f(o_ref[...]), axis=-1, keepdims=True)
        ds = p * (dp - delta)
        dq_ref[...] = (_bdot(ds, kw, NN) * (HD ** -0.5)).astype(dq_ref.dtype)
        dkp_ref[pl.ds(q0, KW), :] += _bdot(ds, q_ref[...], TN) * (HD ** -0.5)
        dvp_ref[pl.ds(q0, KW), :] += _bdot(p, dout, TN)
        dbm_ref[...] += ds

    qspec = pl.BlockSpec((QB, HD), lambda h, i: (i, h))
    kspec = pl.BlockSpec((T + PADK, HD), lambda h, i: (0, h))
    bspec = pl.BlockSpec((None, QB, KW), lambda h, i: (h, 0, 0))
    return pl.pallas_call(
        body, grid=(ATT_H, T // QB),
        in_specs=[qspec, kspec, pl.BlockSpec((T + PADK, HD), lambda h, i: (0, ATT_H + h)), bspec, qspec,
                  pl.BlockSpec((None, QB, 1), lambda h, i: (h, i, 0)), qspec],
        out_specs=[qspec, kspec, kspec, bspec],
        out_shape=[_sds((T, D), BF), _sds((T + PADK, D), F32), _sds((T + PADK, D), F32), _sds((ATT_H, QB, KW), F32)],
        name="attn_bwd", compiler_params=_cp("parallel", "arbitrary"),
    )(qkv, kvp, kvp, bm, o, lse, do)


def _final_fn(x, tgt, g):
    D = x.shape[1]
    xh, rstd = _rms_parts(x)
    diff = xh * g - tgt
    dy = diff * (1.0 / D)
    dxh = dy * g
    dx = rstd * (dxh - xh * jnp.mean(dxh * xh, axis=-1, keepdims=True))
    loss = jnp.sum(jnp.sum(diff * diff, axis=-1, keepdims=True), axis=0, keepdims=True) * (0.5 / D)
    return dx, jnp.sum(dy * xh, axis=0, keepdims=True), jnp.broadcast_to(loss, (1, LANES))


def _adamw_fn(w, g, m, v):
    m = ADAM_B1 * m + (1.0 - ADAM_B1) * g
    v = ADAM_B2 * v + (1.0 - ADAM_B2) * (g * g)
    m_hat = m / (1.0 - ADAM_B1 ** ADAM_STEP)
    v_hat = v / (1.0 - ADAM_B2 ** ADAM_STEP)
    delta = -ADAM_LR * (m_hat / (jnp.sqrt(v_hat) + ADAM_EPS) + ADAM_WD * w)
    return g, delta, m, v


def adamw(w, g, m, v, name):
    shape = w.shape
    C = shape[-1]
    R = w.size // C
    outs = rowwise(name, _adamw_fn, [t.reshape(R, C) for t in (w, g, m, v)], [], [(C, F32)] * 4)
    return tuple(t.reshape(shape) for t in outs)


def adamw_unit(w, m, v, g, lead, outs, name):
    R, B = w.shape[-2:]
    tb = _row_block(R, 256)
    if outs is None:
        outs = [lax.empty(w.shape, F32) for _ in range(4)]

    def body(w_ref, m_ref, v_ref, g_ref, *rest):
        for ref, val in zip(rest[4:], _adamw_fn(w_ref[...], g_ref[...], m_ref[...], v_ref[...])):
            ref[...] = val

    native = pl.BlockSpec((None,) * len(lead) + (tb, B), lambda r: tuple(lead) + (r, 0))
    return pl.pallas_call(
        body, grid=(R // tb,), in_specs=[native] * 3 + [pl.BlockSpec((tb, B), lambda r: (r, 0))] + [HBM_SPEC] * 4,
        out_specs=[native] * 4, out_shape=[_sds(w.shape, F32)] * 4, input_output_aliases={4 + i: i for i in range(4)},
        name=name, compiler_params=_cp("arbitrary"),
    )(w, m, v, g, *outs)


WEIGHTS = ['ffn_norm', 'ffn_w_gate', 'ffn_w_up', 'ffn_w_down', 'mix_norm', 'ab_w_in', 'gla_gate_w', 'gla_gate_b',
           'gla_norm_g', 'conv_dw', 'conv_dw_b', 'conv_ln_g', 'conv_ln_b', 'ab_w_out', 'att_w_qkv', 'att_rel_bias',
           'att_w_o', 'pl_norm', 'pl_w_gate', 'pl_w_proj', 'final_norm']
BIG = ['ffn_w_gate', 'ffn_w_up', 'ffn_w_down', 'ab_w_in', 'ab_w_out', 'att_w_qkv', 'att_w_o', 'pl_w_gate', 'pl_w_proj']


def _pack(parts, rows):
    flat = jnp.concatenate([p.reshape(-1) for p in parts])
    return jnp.pad(flat, (0, rows * LANES - flat.shape[0])).reshape(rows, LANES)


def _unpack(flat, shapes):
    out, pos = [], 0
    for s in shapes:
        n = 1
        for d in s:
            n *= d
        out.append(flat[pos:pos + n].reshape(s))
        pos += n
    return out


def _rows_for(shapes):
    n = sum(math.prod(s) for s in shapes)
    return -(-n // (8 * LANES)) * 8


def kernel(x, p, ffn_norm, ffn_w_gate, ffn_w_up, ffn_w_down, mix_norm, ab_w_in, gla_gate_w, gla_gate_b, gla_norm_g, conv_dw, conv_dw_b, conv_ln_g, conv_ln_b, ab_w_out, att_w_qkv, att_rel_bias, att_w_o, pl_norm, pl_w_gate, pl_w_proj, final_norm, loss_target, m_ffn_norm, m_ffn_w_gate, m_ffn_w_up, m_ffn_w_down, m_mix_norm, m_ab_w_in, m_gla_gate_w, m_gla_gate_b, m_gla_norm_g, m_conv_dw, m_conv_dw_b, m_conv_ln_g, m_conv_ln_b, m_ab_w_out, m_att_w_qkv, m_att_rel_bias, m_att_w_o, m_pl_norm, m_pl_w_gate, m_pl_w_proj, m_final_norm, v_ffn_norm, v_ffn_w_gate, v_ffn_w_up, v_ffn_w_down, v_mix_norm, v_ab_w_in, v_gla_gate_w, v_gla_gate_b, v_gla_norm_g, v_conv_dw, v_conv_dw_b, v_conv_ln_g, v_conv_ln_b, v_ab_w_out, v_att_w_qkv, v_att_rel_bias, v_att_w_o, v_pl_norm, v_pl_w_gate, v_pl_w_proj, v_final_norm):
    env = dict(locals())
    W = {n: env[n] for n in WEIGHTS}
    M = {n: env["m_" + n] for n in WEIGHTS}
    V = {n: env["v_" + n] for n in WEIGHTS}
    xc, yc_, cc = lax.axis_index("x"), lax.axis_index("y"), lax.axis_index("c")

    x0 = x[0]
    tgt = loss_target[0]
    T, D = x0.shape
    fs = ffn_w_gate.shape[-1]
    ws = ab_w_in.shape[-1]
    AB_IN = N_CHIPS * ws
    gz0 = 2 * GLA_H * HK + 2 * GLA_H * HV
    sidx = jnp.stack([2 * xc + yc_, cc, 4 * xc + 2 * yc_ + cc]).astype(jnp.int32)

    def ffn_keys(i, j):
        return [('ffn_w_gate', (i, j)), ('ffn_w_up', (i, j)), ('ffn_w_down', (i, j))]

    order = [ffn_keys(0, 0), [('ab_w_in', (0,)), ('ab_w_out', (0,))], ffn_keys(0, 1),
             [('pl_w_gate', (0,)), ('pl_w_proj', (0,))], ffn_keys(1, 0), [('att_w_qkv', (0,)), ('att_w_o', (0,))],
             ffn_keys(1, 1), [('pl_w_gate', (1,)), ('pl_w_proj', (1,))]]
    arrivals = [order[0][:2], order[0][2:]] + order[1:]
    full, in_flight = {}, []
    tok = jnp.zeros(TOKEN, F32)
    for s, keys in enumerate(arrivals):
        parts = [cast_unit(W[n], lead, sidx, "cast_" + n + "".join(str(i) for i in lead)) for n, lead in keys]
        ssem, rsem, thru, tok = split_start("gather_start_%d" % s, parts, _gather_copies, 3 * len(keys), tok)
        in_flight.append((ssem, rsem, thru))

    def arrive(s, after):
        ssem, rsem, thru = in_flight[s]
        landed = split_wait("gather_wait_%d" % s, ssem, rsem, thru, _gather_copies, after)
        for key, buf in zip(arrivals[s], gather_pass_on(landed, "gather_pass_on_%d" % s)):
            full[key] = buf

    small_sharded = [ffn_norm, gla_gate_w, conv_dw]
    rows_s = _rows_for([t.shape for t in small_sharded])
    got = allgather8(_pack(small_sharded, rows_s), "gather_small").reshape(N_CHIPS, 2, rows_s * LANES)[:, 0]
    per_chip = [_unpack(got[k], [t.shape for t in small_sharded]) for k in range(N_CHIPS)]
    ffn_norm_f, gate_w_f, conv_dw_f = [jnp.concatenate([per_chip[k][t] for k in range(N_CHIPS)], axis=-1)
                                       for t in range(3)]
    gate_w_p = jnp.pad(gate_w_f[0], ((0, LANES - 16), (0, 0)))
    conv_w = conv_dw_f[0]
    rb_p = jnp.pad(att_rel_bias[0], ((0, 0), (0, REL_PAD - att_rel_bias.shape[-1])))

    G = {}
    small_g = {}

    def ffn_w(i, j):
        return full['ffn_w_gate', (i, j)], full['ffn_w_up', (i, j)], full['ffn_w_down', (i, j)]

    saved = {}
    xs = x0
    arrive(0, tok)

    def first_down(after):
        arrive(1, after)
        return full['ffn_w_down', (0, 0)]

    xs, saved['f00'] = ffn_fwd(xs, ffn_norm_f[0, 0][None], full['ffn_w_gate', (0, 0)], full['ffn_w_up', (0, 0)],
                               first_down, dep=tok)
    arrive(2, xs)
    w_in = jnp.transpose(full['ab_w_in', (0,)], (1, 0, 2)).reshape(D, AB_IN)
    w_main = jnp.concatenate([w_in[:, :gz0], w_in[:, gz0 + 16:]], axis=1)
    w_gz = jnp.pad(w_in[:, gz0:gz0 + 16], ((0, 0), (0, LANES - 16)))
    w_out = full['ab_w_out', (0,)].reshape(D, D)

    def mixer0_fwd(xin):
        h = rms_fwd(xin, mix_norm[0][None], "mix0_rms")
        zmain = mm_nn("ab_in", h, w_main, F32, 1024)
        gzp = mm_nn("ab_gz", h, w_gz, F32, LANES)
        ug = mm_nn("gla_gate", gzp, gate_w_p, F32, GLA_H * HK)
        o, S = gla_fwd(zmain, ug, gla_gate_b)
        a_out = rowwise("gla_post", _gla_post_fn, [o, ("cols", zmain, GLA_H * HV, 2)], [gla_norm_g], [(GLA_H * HV, BF)])
        u = rowwise("conv_glu", lambda a, b: a * _sigmoid(b), [("cols", zmain, 1024, 3), ("cols", zmain, 1024, 4)], [],
                    [(1024, F32)])
        yc = conv_fwd(u, conv_w, conv_dw_b)
        b_out = rowwise("conv_ln", _ln_silu_fn, [yc], [conv_ln_g, conv_ln_b], [(1024, BF)])
        cat = jnp.concatenate([a_out, b_out], axis=1)
        x2 = mm_nn("ab_out", cat, w_out, F32, 1024, res=xin)
        return x2, (xin, h, zmain, gzp, ug, o, S, u, yc, cat)

    xs, saved['m0'] = mixer0_fwd(xs)
    arrive(3, xs)
    xs, saved['f01'] = ffn_fwd(xs, ffn_norm_f[0, 1][None], *ffn_w(0, 1))
    ks = D // N_CHIPS
    dp = p.shape[-1]

    def pl_fwd(xin, i):
        h = rms_fwd(xin, pl_norm[i][None], "pl_rms")
        tm = min(T, 1024)
        u = mm_nn("pl_gate", h, full['pl_w_gate', (i,)].reshape(D, D), F32, 1024)
        e = mm("pl_proj", NN, (T // tm, N_CHIPS, 1), p[i, 0], (tm, dp), _mk, full['pl_w_proj', (i,)],
               (None, dp, ks), lambda m, n, k: (n, 0, 0), (tm, ks), _mn, _sds((T, D), F32))
        x2 = rowwise("pl_mix", lambda xv, uv, ev: xv + _sigmoid(uv) * ev, [xin, u, e], [], [(D, F32)])
        return x2, (xin, h, u, e)

    arrive(4, xs)
    xs, saved['p0'] = pl_fwd(xs, 0)
    arrive(5, xs)
    xs, saved['f10'] = ffn_fwd(xs, ffn_norm_f[1, 0][None], *ffn_w(1, 0))
    arrive(6, xs)
    w_qkv = full['att_w_qkv', (0,)]
    w_o = full['att_w_o', (0,)].reshape(D, D)
    qs = w_qkv.shape[-1]

    bm = relbias_tile(rb_p)

    def mixer1_fwd(xin):
        h = rms_fwd(xin, mix_norm[1][None], "mix1_rms")
        tm, tn = min(T, 1024), 512
        per = qs // tn
        qkv = mm("att_qkv", NN, (T // tm, 3 * D // tn, 1), h, (tm, D), _mk, w_qkv, (None, D, tn),
                 lambda m, n, k: (n // per, k, n % per), (tm, tn), _mn, _sds((T, 3 * D), BF))
        kvp = jnp.pad(qkv[:, D:], ((PADK, 0), (0, 0)))
        o, lse = attn_fwd(qkv, kvp, bm)
        x2 = mm_nn("att_o", o, w_o, F32, 1024, res=xin)
        return x2, (xin, h, qkv, kvp, o, lse)

    xs, saved['m1'] = mixer1_fwd(xs)
    arrive(7, xs)
    xs, saved['f11'] = ffn_fwd(xs, ffn_norm_f[1, 1][None], *ffn_w(1, 1))
    arrive(8, xs)
    xs, saved['p1'] = pl_fwd(xs, 1)

    dx, small_g['final_norm'], loss_acc = rowwise("loss_head", _final_fn, [xs, tgt], [final_norm[None]], [(D, F32)],
                                                  [(1, D), (1, LANES)])
    loss = lax.psum(loss_acc[0, 0], ("x", "y", "c"))

    def pl_bwd(dx2, sv, i, dep):
        xin, h, u, e = sv
        tm = min(T, 1024)

        def fn(d, uv, ev):
            s = _sigmoid(uv)
            return d * s, d * ev * s * (1.0 - s)

        de, du = rowwise("pl_mix_bwd", fn, [dx2, u, e], [], [(D, BF), (D, BF)], dep=dep)
        G['pl_w_proj', (i,)] = mm("pl_dproj", TN, (1, N_CHIPS, 1), p[i, 0], (T, dp), _km, de, (T, ks), _kn,
                                  (None, dp, ks), lambda m, n, k: (n, 0, 0), _sds((N_CHIPS, dp, ks), BF))
        G['pl_w_gate', (i,)] = mm_tn("pl_dgate", h, du, BF, ks, 1024).reshape(N_CHIPS, ks, D)
        dh = mm_nt("pl_dh", du, full['pl_w_gate', (i,)].reshape(D, D), F32, 1024)
        return rms_bwd(xin, pl_norm[i][None], dh, dx2, "pl_rms_bwd")

    def ffn_b(dx2, key, i, j, s, dep):
        def sends(g_wg, g_wu, g_wd):
            G['ffn_w_gate', (i, j)], G['ffn_w_up', (i, j)], G['ffn_w_down', (i, j)] = g_wg, g_wu, g_wd
            tok = scatter(s, dep)
            return to_chips(tok) if s == 0 else tok

        return ffn_bwd(dx2, saved[key], ffn_norm_f[i, j][None], *ffn_w(i, j), dep, sends)

    def mixer1_bwd(dx2, sv, dep):
        xin, h, qkv, kvp, o, lse = sv
        do = mm_nt("att_do", dx2, w_o, BF, 1024, dep=dep)
        dwo = mm_tn("att_dwo", o, dx2, BF, 1024, 1024, dep=dep)
        dq, dkp, dvp, dbm = attn_bwd(qkv, kvp, bm, o, lse, do)
        dqkv = jnp.concatenate([dq, dkp[PADK:].astype(BF), dvp[PADK:].astype(BF)], axis=1)
        tm, tn = min(T, 1024), 512
        per = qs // tn
        dwqkv = mm("att_dwqkv", TN, (D // 1024, 3 * D // tn, 1), h, (T, 1024), _km, dqkv, (T, tn), _kn,
                   (None, 1024, tn), lambda m, n, k: (n // per, m, n % per), _sds((N_CHIPS, D, qs), BF))
        dh = mm("att_dh", NT, (T // tm, D // 1024, N_CHIPS), dqkv, (tm, qs), _mk, w_qkv, (None, 1024, qs),
                lambda m, n, k: (k, n, 0), (tm, 1024), _mn, _sds((T, D), F32))
        dxn, dgain = rms_bwd(xin, mix_norm[1][None], dh, dx2, "mix1_rms_bwd")
        return dxn, dgain, dwo, dwqkv, relbias_reduce(dbm)

    def mixer0_bwd(dx2, sv, dep):
        xin, h, zmain, gzp, ug, o, S, u, yc, cat = sv
        dcat = mm_nt("ab_dcat", dx2, w_out, F32, 1024, dep=dep)
        dwout = mm_tn("ab_dwout", cat, dx2, BF, 1024, 1024, dep=dep)
        do, dr, dgn = rowwise("gla_post_bwd", _gla_post_bwd_fn,
                              [("cols", dcat, GLA_H * HV, 0), o, ("cols", zmain, GLA_H * HV, 2)], [gla_norm_g],
                              [(GLA_H * HV, F32), (GLA_H * HV, BF)], [(1, HV)])
        dyc, dlg, dlb = rowwise("conv_ln_bwd", _ln_silu_bwd_fn, [("cols", dcat, 1024, 1), yc], [conv_ln_g, conv_ln_b],
                                [(1024, F32)], [(1, 1024), (1, 1024)])
        du, ddw, ddwb = conv_bwd(dyc, u, conv_w)
        dca, dcb = rowwise("conv_glu_bwd", _glu_bwd_fn, [du, ("cols", zmain, 1024, 3), ("cols", zmain, 1024, 4)], [],
                           [(1024, BF), (1024, BF)])
        dq, dk, dv, dug, dgb = gla_bwd(zmain, ug, gla_gate_b, S, do)
        dgw = mm_tn("gla_dgate_w", gzp, dug, F32, LANES, GLA_H * HK)
        dgzp = mm_nt("gla_dgz", dug, gate_w_p, F32, LANES)
        dzm = jnp.concatenate([dq, dk, dv, dr, dca, dcb], axis=1)
        dwmain = mm_tn("ab_dwmain", h, dzm, BF, 1024, 1024)
        dwgz = mm_tn("ab_dwgz", h, dgzp, BF, 1024, LANES)
        dh = mm_nt("ab_dh_gz", dgzp, w_gz, F32, 1024, res=mm_nt("ab_dh", dzm, w_main, F32, 512))
        dxn, dgain = rms_bwd(xin, mix_norm[0][None], dh, dx2, "mix0_rms_bwd")
        dwin = jnp.concatenate([dwmain[:, :gz0], dwgz[:, :16], dwmain[:, gz0:]], axis=1)
        g_win = jnp.transpose(dwin.reshape(D, N_CHIPS, ws), (1, 0, 2))
        return dxn, dgain, g_win, dwout, (dgn, dlg, dlb, ddw, ddwb, dgw[:16], dgb)

    scattering, exchanging = [], []

    def tags_of(s):
        return [n + "".join(str(i) for i in lead) for n, lead in order[s]]

    def to_chips(tok):
        if not exchanging:
            return tok
        s, ssem, rsem, thru = exchanging.pop()
        n_u = len(order[s])
        landed = split_wait("sibling_wait_%d" % s, ssem, rsem, thru, _sibling_copies, tok)
        hs = [add_sibling(g4, got, sidx, tag) for g4, got, tag in zip(landed[:n_u], landed[n_u:], tags_of(s))]
        land = [lax.empty((3,) + h.shape[1:], h.dtype) for h in hs]
        ssem, rsem, thru, tok = split_start("scatter_start_%d" % s, hs + land, _scatter_copies, 3 * n_u, tok)
        scattering.append((s, ssem, rsem, thru))
        return tok

    def scatter(s, tok):
        g4s = [G[key].reshape(N_CHIPS, 2, G[key].shape[1] // 2, G[key].shape[2]) for key in order[s]]
        land = [lax.empty((N_CHIPS,) + g4.shape[2:], g4.dtype) for g4 in g4s]
        ssem, rsem, thru, tok = split_start("sibling_start_%d" % s, g4s + land, _sibling_copies, N_CHIPS * len(g4s), tok)
        tok = to_chips(tok)
        exchanging.append((s, ssem, rsem, thru))
        return tok

    dpl, dffn, dmix = [None, None], [[None, None], [None, None]], [None, None]
    tok = jnp.zeros(TOKEN, F32)
    dx, dpl[1] = pl_bwd(dx, saved['p1'], 1, tok)
    tok = scatter(7, tok)
    dx, dffn[1][1], tok = ffn_b(dx, 'f11', 1, 1, 6, tok)
    dx, dmix[1], dwo, dwqkv, drb = mixer1_bwd(dx, saved['m1'], tok)
    G['att_w_qkv', (0,)] = dwqkv
    G['att_w_o', (0,)] = dwo.reshape((N_CHIPS,) + att_w_o.shape[1:])
    tok = scatter(5, tok)
    dx, dffn[1][0], tok = ffn_b(dx, 'f10', 1, 0, 4, tok)
    dx, dpl[0] = pl_bwd(dx, saved['p0'], 0, tok)
    tok = scatter(3, tok)
    dx, dffn[0][1], tok = ffn_b(dx, 'f01', 0, 1, 2, tok)
    dx, dmix[0], g_win, dwout, (dgn, dlg, dlb, ddw, ddwb, dgw, dgb) = mixer0_bwd(dx, saved['m0'], tok)
    G['ab_w_in', (0,)] = g_win
    G['ab_w_out', (0,)] = dwout.reshape((N_CHIPS,) + ab_w_out.shape[1:])
    tok = scatter(1, tok)
    dx, dffn[0][0], tok = ffn_b(dx, 'f00', 0, 0, 0, tok)
    grad_x = dx[None]

    small_g['ffn_norm'] = jnp.stack([jnp.stack([dffn[i][j][0] for j in range(2)]) for i in range(2)])
    small_g['mix_norm'] = jnp.concatenate(dmix, axis=0)
    small_g['gla_gate_w'] = dgw[None]
    small_g['gla_gate_b'] = dgb
    small_g['gla_norm_g'] = dgn
    small_g['conv_dw'] = ddw[None]
    small_g['conv_dw_b'] = ddwb
    small_g['conv_ln_g'] = dlg
    small_g['conv_ln_b'] = dlb
    small_g['att_rel_bias'] = drb[None, :, :att_rel_bias.shape[-1]]
    small_g['pl_norm'] = jnp.concatenate(dpl, axis=0)
    small_g['final_norm'] = small_g['final_norm'][0]
    small_names = [n for n in WEIGHTS if n not in BIG]
    small_shapes = [small_g[n].shape for n in small_names]
    rows_g = _rows_for(small_shapes)
    mine = rowwise("place_small_grads", lambda v: v, [_pack([small_g[n] for n in small_names], rows_g)], [],
                   [("dyn", 8, LANES, F32, 2)], tb=rows_g, sidx=sidx)
    small_ssem, small_rsem, small_thru, tok = split_start("small_start", [mine], _allgather_copies, 7, tok)

    grads, outs = {}, {}
    sharing, updated = [], [dx]

    def update(tok):
        if not sharing:
            return
        s, ssem, rsem, thru = sharing.pop()
        landed = split_wait("share_wait_%d" % s, ssem, rsem, thru, _share_copies, tok)
        for (n, lead), tag, r in zip(order[s], tags_of(s), landed):
            outs[n] = adamw_unit(W[n], M[n], V[n], r.reshape(W[n].shape[-2:]), lead, outs.get(n), "adamw_" + tag)
            updated.append(outs[n][0])

    after = dx
    for s, ssem, rsem, thru in scattering:
        n_u = len(order[s])
        landed = split_wait("scatter_wait_%d" % s, ssem, rsem, thru, _scatter_copies, after)
        rs = [add_chips(h, got2, sidx, tag) for h, got2, tag in zip(landed[:n_u], landed[n_u:], tags_of(s))]
        ssem, rsem, thru, tok = split_start("share_start_%d" % s, rs, _share_copies, n_u, tok)
        update(tok)
        sharing.append((s, ssem, rsem, thru))
        after = tok
    update(tok)

    (allp,) = split_wait("small_wait", small_ssem, small_rsem, small_thru, _allgather_copies, updated[-1])
    summed = rowwise("sum_small_grads", lambda *v: (((v[0] + v[1]) + (v[2] + v[3])) + ((v[4] + v[5]) + (v[6] + v[7]))),
                     [("leads", allp, (d,)) for d in range(8)], [], [(LANES, F32)], tb=rows_g)
    for n, g in zip(small_names, _unpack(summed.reshape(-1), small_shapes)):
        grads[n] = g
    chip = 2 * xc + yc_
    for n, axis in (('ffn_norm', 2), ('gla_gate_w', 2), ('conv_dw', 2)):
        width = W[n].shape[axis]
        grads[n] = lax.dynamic_slice_in_dim(grads[n], chip * width, width, axis)

    for n in small_names:
        outs[n] = adamw(W[n], grads[n], M[n], V[n], "adamw_" + n)
    return (loss, grad_x, *[outs[n][0] for n in WEIGHTS], *[outs[n][1] for n in WEIGHTS],
            *[outs[n][2] for n in WEIGHTS], *[outs[n][3] for n in WEIGHTS])
```

```python
import math

import jax
import jax.numpy as jnp
from jax import lax
from jax.experimental import pallas as pl
from jax.experimental.pallas import tpu as pltpu

F32 = jnp.float32
BF = jnp.bfloat16
MESH = pl.DeviceIdType.MESH
HI = lax.Precision.HIGHEST
V7X_VMEM_LIMIT = 56 * 1024 * 1024
LANES = 128
EPS = 1e-6
NEG = -1e30

CHUNK = 64
LEFT_CHUNKS = 8
QB = 256
KW = QB + LEFT_CHUNKS * CHUNK
PADK = LEFT_CHUNKS * CHUNK
REL_CLIP = 128
REL_PAD = 384
ATT_H = 16
HD = 128
GLA_H = 4
HK = 128
HV = 256
GLA_TAU = 16.0
CONV_W = 31
CONV_PAD = 32
N_CHIPS = 4

ADAM_LR = 0.001
ADAM_B1 = 0.9
ADAM_B2 = 0.999
ADAM_EPS = 1e-08
ADAM_WD = 0.01
ADAM_STEP = 10

NN = ((1,), (0,))
NT = ((1,), (1,))
TN = ((0,), (0,))


def _dot(a, b, dims, prec=None):
    return lax.dot_general(a, b, (dims, ((), ())), preferred_element_type=F32, precision=prec)


def _bdot(a, b, dims):
    return _dot(a.astype(BF), b.astype(BF), dims)


def _cp(*sem):
    return pltpu.CompilerParams(dimension_semantics=sem if sem else None, vmem_limit_bytes=V7X_VMEM_LIMIT)


def _sds(shape, dtype):
    return jax.ShapeDtypeStruct(shape, dtype)


def _sigmoid(x):
    return 1.0 / (1.0 + jnp.exp(-x))


def _silu(x):
    return x * _sigmoid(x)


def _dsilu(x):
    s = _sigmoid(x)
    return s * (1.0 + x * (1.0 - s))


def _f(x):
    return x.astype(F32)


def _row_item(item, tb):
    if not isinstance(item, tuple):
        return item, (tb, item.shape[1]), lambda i, s: (i, 0)
    kind, arr = item[0], item[1]
    if kind == "cols":
        return arr, (tb, item[2]), lambda i, s, blk=item[3]: (i, blk)
    if kind == "leads":
        lead = tuple(item[2])
        return arr, (None,) * len(lead) + (tb, arr.shape[-1]), lambda i, s: lead + (i, 0)
    if kind == "dyn":
        return arr, (None, tb, arr.shape[-1]), lambda i, s, sel=item[2]: (s[sel], i, 0)
    if kind == "dyn4":
        nb = arr.shape[2] // tb
        return arr, (None, None, tb, arr.shape[-1]), lambda i, s, sel=item[2]: (i // nb, s[sel], i % nb, 0)
    raise ValueError(kind)


def _row_block(n, cap):
    for cand in range(min(cap, n) // 16 * 16, 0, -16):
        if n % cand == 0:
            return cand
    return n


def rowwise(name, fn, rows, bcast, outs, accs=(), tb=256, n_rows=None, sidx=None, row_period=None, dep=None):
    if n_rows is None:
        first = rows[0][1] if isinstance(rows[0], tuple) else rows[0]
        n_rows = first.shape[-2]
    tb = _row_block(n_rows if row_period is None else row_period, tb)
    items = [_row_item(it, tb) for it in rows]
    n_r, n_b, n_o = len(rows), len(bcast), len(outs)
    pre = 0 if sidx is None else 1

    def wrap(f):
        return (lambda i: f(i, None)) if sidx is None else (lambda i, s: f(i, s))

    def body(*refs):
        refs = refs[pre:]
        r, b = refs[:n_r], refs[n_r:n_r + n_b]
        refs = refs[n_r + n_b + (dep is not None):]
        o, a = refs[:n_o], refs[n_o:]
        res = fn(*[v[...] for v in r], *[v[...] for v in b])
        if not isinstance(res, tuple):
            res = (res,)
        for ref, val in zip(o, res[:n_o]):
            ref[...] = val.astype(ref.dtype)
        if a:
            @pl.when(pl.program_id(0) == 0)
            def _():
                for ref in a:
                    ref[...] = jnp.zeros_like(ref)
            for ref, val in zip(a, res[n_o:]):
                ref[...] += val

    in_specs = [pl.BlockSpec(bs, wrap(f)) for _, bs, f in items]
    in_specs += [pl.BlockSpec(v.shape, wrap(lambda i, s, nd=v.ndim: (0,) * nd)) for v in bcast]
    out_specs, out_shape = [], []
    for o in outs:
        if o[0] == "dyn":
            _, L, c, dt, sel = o
            out_specs.append(pl.BlockSpec((None, tb, c), wrap(lambda i, s, sel=sel: (s[sel], i, 0))))
            out_shape.append(_sds((L, n_rows, c), dt))
        else:
            c, dt = o
            out_specs.append(pl.BlockSpec((tb, c), wrap(lambda i, s: (i, 0))))
            out_shape.append(_sds((n_rows, c), dt))
    out_specs += [pl.BlockSpec(sh, wrap(lambda i, s: (0, 0))) for sh in accs]
    out_shape += [_sds(sh, F32) for sh in accs]
    operands = [a for a, _, _ in items] + list(bcast)
    if dep is not None:
        operands.append(dep)
        in_specs.append(pl.BlockSpec(TOKEN, wrap(lambda i, s: (0, 0))))
    grid = (n_rows // tb,)
    if sidx is None:
        res = pl.pallas_call(body, grid=grid, in_specs=in_specs, out_specs=out_specs, out_shape=out_shape, name=name,
                             compiler_params=_cp("arbitrary"))(*operands)
    else:
        spec = pltpu.PrefetchScalarGridSpec(num_scalar_prefetch=1, grid=grid, in_specs=in_specs, out_specs=out_specs)
        res = pl.pallas_call(body, grid_spec=spec, out_shape=out_shape, name=name,
                             compiler_params=_cp("arbitrary"))(sidx, *operands)
    return res[0] if len(res) == 1 else tuple(res)


TOKEN = (8, LANES)


def mm(name, dims, grid, a, a_bs, a_im, b, b_bs, b_im, o_bs, o_im, out, scale=1.0, res=None, dep=None):
    nk = grid[2]
    acc_shape = tuple(d for d in o_bs if d is not None)

    def body(*refs):
        a_ref, b_ref = refs[0], refs[1]
        pos = 2
        res_ref = None
        if res is not None:
            res_ref = refs[pos]
            pos += 1
        if dep is not None:
            pos += 1
        o_ref = refs[pos]
        part = _bdot(a_ref[...], b_ref[...], dims)

        def finish(acc):
            v = acc * scale if scale != 1.0 else acc
            if res_ref is not None:
                v = v + _f(res_ref[...])
            o_ref[...] = v.astype(o_ref.dtype)

        if nk == 1:
            finish(part)
        else:
            acc_ref = refs[pos + 1]
            k = pl.program_id(2)

            @pl.when(k == 0)
            def _():
                acc_ref[...] = part

            @pl.when(k > 0)
            def _():
                acc_ref[...] += part

            @pl.when(k == nk - 1)
            def _():
                finish(acc_ref[...])

    operands = [a, b]
    in_specs = [pl.BlockSpec(a_bs, a_im), pl.BlockSpec(b_bs, b_im)]
    if res is not None:
        operands.append(res)
        in_specs.append(pl.BlockSpec(o_bs, o_im))
    if dep is not None:
        operands.append(dep)
        in_specs.append(pl.BlockSpec(TOKEN, lambda m, n, k: (0, 0)))
    scratch = [pltpu.VMEM(acc_shape, F32)] if nk > 1 else []
    return pl.pallas_call(body, grid=grid, in_specs=in_specs, out_specs=pl.BlockSpec(o_bs, o_im), out_shape=out,
                          scratch_shapes=scratch, name=name,
                          compiler_params=_cp("parallel", "parallel", "arbitrary"))(*operands)


def _mk(m, n, k):
    return (m, k)


def _mn(m, n, k):
    return (m, n)


def _km(m, n, k):
    return (k, m)


def _kn(m, n, k):
    return (k, n)


def _nk(m, n, k):
    return (n, k)


def mm_nn(name, a, b, out_dtype, tn, tk=None, scale=1.0, res=None):
    T, K = a.shape
    N = b.shape[1]
    tm, tk, tn = min(T, 1024), K if tk is None else min(tk, K), min(tn, N)
    return mm(name, NN, (T // tm, N // tn, K // tk), a, (tm, tk), _mk, b, (tk, tn), _kn, (tm, tn), _mn,
              _sds((T, N), out_dtype), scale=scale, res=res)


def mm_nt(name, a, b, out_dtype, tn, tk=None, scale=1.0, res=None, dep=None):
    T, K = a.shape
    N = b.shape[0]
    tm, tk, tn = min(T, 1024), K if tk is None else min(tk, K), min(tn, N)
    return mm(name, NT, (T // tm, N // tn, K // tk), a, (tm, tk), _mk, b, (tn, tk), _nk, (tm, tn), _mn,
              _sds((T, N), out_dtype), scale=scale, res=res, dep=dep)


def mm_tn(name, a, b, out_dtype, tm, tn, scale=1.0, dep=None):
    T, M = a.shape
    N = b.shape[1]
    tk, tm, tn = T, min(tm, M), min(tn, N)
    return mm(name, TN, (M // tm, N // tn, T // tk), a, (tk, tm), _km, b, (tk, tn), _kn, (tm, tn), _mn,
              _sds((M, N), out_dtype), scale=scale, dep=dep)


HBM_SPEC = pl.BlockSpec(memory_space=pl.ANY)


def _place():
    x, y, c = lax.axis_index("x"), lax.axis_index("y"), lax.axis_index("c")
    chips = [(1 - x, y), (x, 1 - y), (1 - x, 1 - y)]
    return x, y, c, chips


def allgather8(v, name):
    m_per, n = v.shape

    def body(x_ref, out_ref, send_sems, recv_sems, local_sem):
        x, y, c, chips = _place()
        me, sibling = (x, y, c), (x, y, 1 - c)

        def rows(px, py, pc):
            return out_ref.at[pl.ds((4 * px + 2 * py + pc) * m_per, m_per), :]

        def copy(k, block, to, src=None):
            return pltpu.make_async_remote_copy(
                src_ref=rows(*block) if src is None else src, dst_ref=rows(*block),
                send_sem=send_sems.at[k], recv_sem=recv_sems.at[k], device_id=to, device_id_type=MESH)

        mine = pltpu.make_async_copy(x_ref, rows(*me), local_sem)
        mine.start()
        first = [copy(0, me, sibling, src=x_ref)]
        first += [copy(1 + j, me, (*chip, c), src=x_ref) for j, chip in enumerate(chips)]
        for cp in first:
            cp.start()
        passed = [copy(4 + j, (*chip, c), sibling) for j, chip in enumerate(chips)]
        for j, chip in enumerate(chips):
            copy(1 + j, (*chip, c), me).wait_recv()
            passed[j].start()
        copy(0, sibling, me).wait_recv()
        for j, chip in enumerate(chips):
            copy(4 + j, (*chip, 1 - c), me).wait_recv()
        for cp in first + passed:
            cp.wait_send()
        mine.wait()

    return pl.pallas_call(
        body, out_shape=_sds((8 * m_per, n), v.dtype), in_specs=[pl.BlockSpec(memory_space=pltpu.VMEM)],
        out_specs=pl.BlockSpec(memory_space=pltpu.VMEM), name=name,
        scratch_shapes=[pltpu.SemaphoreType.DMA((7,)), pltpu.SemaphoreType.DMA((7,)), pltpu.SemaphoreType.DMA],
    )(v)


D2D_PIECES = 4


def _pieces(ref, n):
    rows = ref.shape[0] // n
    return [ref.at[pl.ds(q * rows, rows)] for q in range(n)]


def _start_in_pieces(src, dst, ssem, rsem, to, n):
    for s_q, d_q in zip(_pieces(src, n), _pieces(dst, n)):
        pltpu.make_async_remote_copy(src_ref=s_q, dst_ref=d_q, send_sem=ssem, recv_sem=rsem, device_id=to,
                                     device_id_type=MESH).start()


def _whole(src, dst, ssem, rsem, to):
    return pltpu.make_async_remote_copy(src_ref=src, dst_ref=dst, send_sem=ssem, recv_sem=rsem, device_id=to,
                                        device_id_type=MESH)


HBM_ONLY = pl.BlockSpec(memory_space=pltpu.HBM)
SEM_SPEC = pl.BlockSpec(memory_space=pltpu.SEMAPHORE)
DATAFLOW = pltpu.SideEffectType.DATAFLOW_SIDE_EFFECTING


def _gather_copies(bufs):
    x, y, c, chips = _place()
    out = []
    for o_ref in bufs:
        hr = o_ref.shape[1] // 2

        def half(chip, o_ref=o_ref, hr=hr):
            return o_ref.at[2 * chip[0] + chip[1], pl.ds(c * hr, hr)]

        out += [(half((x, y)), half((x, y)), (*chip, c), half(chip)) for chip in chips]
    return out


def _scatter_copies(arrays):
    x, y, c, chips = _place()
    n = len(arrays) // 2
    out = []
    for h_ref, got_ref in zip(arrays[:n], arrays[n:]):
        out += [(h_ref.at[2 * chip[0] + chip[1]], got_ref.at[j], (*chip, c), got_ref.at[j])
                for j, chip in enumerate(chips)]
    return out


def split_start(name, arrays, copies_of, n_copies, after):
    n = len(arrays)

    def body(*refs):
        ssem, rsem = refs[n + 1], refs[n + 2]
        for i, (src, dst, to, _) in enumerate(copies_of(refs[n + 3:2 * n + 3])):
            _whole(src, dst, ssem.at[i], rsem.at[i], to).start()
        refs[2 * n + 3][...] = jnp.zeros(TOKEN, F32)

    res = pl.pallas_call(
        body, name=name,
        out_shape=(pltpu.SemaphoreType.DMA((n_copies,)), pltpu.SemaphoreType.DMA((n_copies,)),
                   *[pltpu.HBM(a.shape, a.dtype) for a in arrays], _sds(TOKEN, F32)),
        in_specs=[HBM_ONLY] * n + [pl.BlockSpec(memory_space=pl.ANY)],
        out_specs=(SEM_SPEC, SEM_SPEC, *[HBM_ONLY] * n, pl.BlockSpec(memory_space=pltpu.VMEM)),
        input_output_aliases={i: 2 + i for i in range(n)},
        compiler_params=pltpu.CompilerParams(has_side_effects=DATAFLOW),
    )(*[pltpu.with_memory_space_constraint(a, pltpu.HBM) for a in arrays], after)
    return res[0], res[1], list(res[2:2 + n]), res[2 + n]


def split_wait(name, ssem, rsem, arrays, copies_of, after):
    n = len(arrays)

    def body(*refs):
        s_ref, r_ref = refs[n], refs[n + 1]
        for i, (src, _, to, mine) in enumerate(copies_of(refs[:n])):
            cp = _whole(src, mine, s_ref.at[i], r_ref.at[i], to)
            cp.wait_send()
            cp.wait_recv()

    res = pl.pallas_call(
        body, name=name, out_shape=tuple(pltpu.HBM(a.shape, a.dtype) for a in arrays),
        in_specs=[HBM_ONLY] * n + [SEM_SPEC, SEM_SPEC, pl.BlockSpec(memory_space=pl.ANY)],
        out_specs=tuple([HBM_ONLY] * n), input_output_aliases={i: i for i in range(n)},
        compiler_params=pltpu.CompilerParams(has_side_effects=DATAFLOW),
    )(*arrays, ssem, rsem, after)
    return list(res)


def gather_pass_on(bufs, name):
    n = len(bufs)

    def body(*refs):
        o_refs, ssem, rsem = refs[n:2 * n], refs[2 * n], refs[2 * n + 1]
        x, y, c, chips = _place()
        sib = (x, y, 1 - c)

        def half(o_ref, chip, h):
            hr = o_ref.shape[1] // 2
            return o_ref.at[2 * chip[0] + chip[1], pl.ds(h * hr, hr)]

        for u, o_ref in enumerate(o_refs):
            for j, chip in enumerate(chips):
                _start_in_pieces(half(o_ref, chip, c), half(o_ref, chip, c), ssem.at[3 * u + j], rsem.at[3 * u + j], sib,
                                 D2D_PIECES)
        for u, o_ref in enumerate(o_refs):
            for j, chip in enumerate(chips):
                _whole(half(o_ref, chip, c), half(o_ref, chip, 1 - c), ssem.at[3 * u + j], rsem.at[3 * u + j], sib).wait()

    res = pl.pallas_call(
        body, out_shape=tuple(_sds(b.shape, b.dtype) for b in bufs), in_specs=[HBM_SPEC] * n,
        out_specs=tuple([HBM_SPEC] * n), name=name, input_output_aliases={i: i for i in range(n)},
        scratch_shapes=[pltpu.SemaphoreType.DMA((3 * n,)), pltpu.SemaphoreType.DMA((3 * n,))],
    )(*bufs)
    return list(res)


def _sibling_copies(arrays):
    x, y, c, _ = _place()
    n = len(arrays) // 2
    out = []
    for g_ref, got_ref in zip(arrays[:n], arrays[n:]):
        out += [(g_ref.at[k, 1 - c], got_ref.at[k], (x, y, 1 - c), got_ref.at[k]) for k in range(N_CHIPS)]
    return out


def _allgather_copies(arrays):
    x, y, c, _ = _place()
    (buf,) = arrays
    mine = buf.at[4 * x + 2 * y + c]
    out = []
    for fx, fy, fc in [(0, 0, 1), (0, 1, 0), (0, 1, 1), (1, 0, 0), (1, 0, 1), (1, 1, 0), (1, 1, 1)]:
        px, py, pc = (1 - x if fx else x), (1 - y if fy else y), (1 - c if fc else c)
        out.append((mine, mine, (px, py, pc), buf.at[4 * px + 2 * py + pc]))
    return out


def _share_copies(arrays):
    x, y, c, _ = _place()
    return [(r_ref.at[c], r_ref.at[c], (x, y, 1 - c), r_ref.at[1 - c]) for r_ref in arrays]


def add_sibling(g4, got, sidx, tag):
    _, _, hr, B = g4.shape
    h = rowwise("rs_add2_" + tag, lambda a, b: _f(a) + _f(b), [("dyn4", g4, 1), got.reshape(N_CHIPS * hr, B)], [],
                [(B, BF)], tb=1024, n_rows=N_CHIPS * hr, sidx=sidx, row_period=hr)
    return h.reshape(N_CHIPS, hr, B)


def add_chips(h, got2, sidx, tag):
    _, hr, B = h.shape
    return rowwise("rs_add4_" + tag, lambda a, b, c, d: ((_f(a) + _f(b)) + _f(c)) + _f(d),
                   [("dyn", h, 0), ("leads", got2, (0,)), ("leads", got2, (1,)), ("leads", got2, (2,))], [],
                   [("dyn", 2, B, F32, 1)], tb=512, n_rows=hr, sidx=sidx)


def _rms_parts(x):
    rstd = lax.rsqrt(jnp.mean(x * x, axis=-1, keepdims=True) + EPS)
    return x * rstd, rstd


def rms_fwd(x, g, name, dep=None):
    return rowwise(name, lambda xv, gv: _rms_parts(xv)[0] * gv, [x], [g], [(x.shape[1], BF)], dep=dep)


def _rms_bwd_fn(x, dh, dres, g):
    xh, rstd = _rms_parts(x)
    dxh = _f(dh) * g
    dx = rstd * (dxh - xh * jnp.mean(dxh * xh, axis=-1, keepdims=True)) + dres
    return dx, jnp.sum(_f(dh) * xh, axis=0, keepdims=True)


def rms_bwd(x, g, dh, dres, name):
    D = x.shape[1]
    return rowwise(name, _rms_bwd_fn, [x, dh, dres], [g], [(D, F32)], [(1, D)])


def cast_unit(w, lead, sidx, name):
    R, B = w.shape[-2:]
    return rowwise(name, lambda v: v, [("leads", w, lead)], [], [("dyn", N_CHIPS, B, BF, 0)], tb=1024, n_rows=R,
                   sidx=sidx)


FFN_TM = 512


def ffn_up(h, wg, wu):
    T, D = h.shape
    fs = wg.shape[-1]
    tm = min(T, FFN_TM)

    def body(h_ref, wg_ref, wu_ref, zg_ref, zu_ref, a_ref):
        hb = h_ref[...]
        g = _bdot(hb, wg_ref[...], NN)
        u = _bdot(hb, wu_ref[...], NN)
        zg_ref[...] = g.astype(zg_ref.dtype)
        zu_ref[...] = u.astype(zu_ref.dtype)
        a_ref[...] = (_silu(g) * u).astype(a_ref.dtype)

    w_spec = pl.BlockSpec((None, D, fs), lambda n, m: (n, 0, 0))
    o_spec = pl.BlockSpec((tm, fs), lambda n, m: (m, n))
    return pl.pallas_call(
        body, grid=(N_CHIPS, T // tm), in_specs=[pl.BlockSpec((tm, D), lambda n, m: (m, 0)), w_spec, w_spec],
        out_specs=[o_spec] * 3, out_shape=[_sds((T, N_CHIPS * fs), BF)] * 3, name="ffn_up",
        compiler_params=_cp("parallel", "arbitrary"))(h, wg, wu)


def ffn_dz(dx2, wd2, zg, zu, dep):
    T, D = dx2.shape
    F = wd2.shape[0]
    fs = F // N_CHIPS
    tm = min(T, FFN_TM)

    def body(dx_ref, wd_ref, zg_ref, zu_ref, dep_ref, dzg_ref, dzu_ref):
        da = 0.5 * _bdot(dx_ref[...], wd_ref[...], NT)
        dzg, dzu = _swiglu_bwd_fn(da, zg_ref[...], zu_ref[...])
        dzg_ref[...] = dzg.astype(dzg_ref.dtype)
        dzu_ref[...] = dzu.astype(dzu_ref.dtype)

    z_spec = pl.BlockSpec((tm, fs), lambda n, m: (m, n))
    return pl.pallas_call(
        body, grid=(N_CHIPS, T // tm),
        in_specs=[pl.BlockSpec((tm, D), lambda n, m: (m, 0)), pl.BlockSpec((fs, D), lambda n, m: (n, 0)), z_spec, z_spec,
                  pl.BlockSpec(TOKEN, lambda n, m: (0, 0))],
        out_specs=[z_spec] * 2, out_shape=[_sds((T, F), BF)] * 2, name="ffn_dz",
        compiler_params=_cp("parallel", "arbitrary"))(dx2, wd2, zg, zu, dep)


def ffn_dh(dzg, dzu, wg, wu, dep):
    T, F = dzg.shape
    _, D, fs = wg.shape
    tm, tn = min(T, 1024), 1024

    def body(g_ref, u_ref, wg_ref, wu_ref, dep_ref, o_ref, acc_ref):
        k = pl.program_id(2)
        part = _bdot(g_ref[...], wg_ref[...], NT) + _bdot(u_ref[...], wu_ref[...], NT)

        @pl.when(k == 0)
        def _():
            acc_ref[...] = part

        @pl.when(k > 0)
        def _():
            acc_ref[...] += part

        @pl.when(k == N_CHIPS - 1)
        def _():
            o_ref[...] = acc_ref[...]

    z_spec = pl.BlockSpec((tm, fs), _mk)
    w_spec = pl.BlockSpec((None, tn, fs), lambda m, n, k: (k, n, 0))
    return pl.pallas_call(
        body, grid=(T // tm, D // tn, N_CHIPS),
        in_specs=[z_spec, z_spec, w_spec, w_spec, pl.BlockSpec(TOKEN, lambda m, n, k: (0, 0))],
        out_specs=pl.BlockSpec((tm, tn), _mn), out_shape=_sds((T, D), F32),
        scratch_shapes=[pltpu.VMEM((tm, tn), F32)], name="ffn_dh",
        compiler_params=_cp("parallel", "parallel", "arbitrary"))(dzg, dzu, wg, wu, dep)


def ffn_fwd(xin, gain, wg, wu, wd, dep=None):
    T, D = xin.shape
    F = N_CHIPS * wg.shape[-1]
    h = rms_fwd(xin, gain, "ffn_rms", dep)
    zg, zu, a = ffn_up(h, wg, wu)
    if callable(wd):
        wd = wd(a)
    x2 = mm_nn("ffn_down", a, wd.reshape(F, D), F32, 512, scale=0.5, res=xin)
    return x2, (xin, h, zg, zu, a)


def _swiglu_bwd_fn(da, zg, zu):
    da, zg, zu = _f(da), _f(zg), _f(zu)
    return da * zu * _dsilu(zg), da * _silu(zg)


def ffn_bwd(dx2, saved, gain, wg, wu, wd, dep, sends):
    xin, h, zg, zu, a = saved
    T, D = xin.shape
    fs = wg.shape[-1]
    F = N_CHIPS * fs
    tn = 1024
    g_wd = mm_tn("ffn_dwd", a, dx2, BF, fs, 1024, scale=0.5, dep=dep).reshape(N_CHIPS, fs, D)
    dep = sends('ffn_w_down', g_wd, dep)
    dzg, dzu = ffn_dz(dx2, wd.reshape(F, D), zg, zu, dep)

    def dw(dz, nm):
        return mm(nm, TN, (D // tn, N_CHIPS, 1), h, (T, tn), _km, dz, (T, fs), _kn,
                  (None, tn, fs), lambda m, n, k: (n, m, 0), _sds((N_CHIPS, D, fs), BF))

    dep = sends('ffn_w_gate', dw(dzg, "ffn_dwg"), dep)
    dep = sends('ffn_w_up', dw(dzu, "ffn_dwu"), dep)
    dh = ffn_dh(dzg, dzu, wg, wu, dep)
    dx, dgain = rms_bwd(xin, gain, dh, dx2, "ffn_rms_bwd")
    return dx, dgain, dep


def _gla_chunk(q_ref, k_ref, v_ref, u_ref, b_ref, rows):
    r = lax.broadcasted_iota(jnp.int32, (CHUNK, CHUNK), 0)
    c = lax.broadcasted_iota(jnp.int32, (CHUNK, CHUNK), 1)
    causal = c <= r
    u = u_ref[rows, :] + b_ref[...]
    g = (jnp.minimum(u, 0.0) - jnp.log(1.0 + jnp.exp(-jnp.abs(u)))) * (1.0 / GLA_TAU)
    b = _dot(causal.astype(F32), g, NN, HI)
    last = lax.broadcasted_iota(jnp.int32, (CHUNK, HK), 0) == CHUNK - 1
    blast = jnp.sum(jnp.where(last, b, 0.0), axis=0, keepdims=True)
    eb = jnp.exp(b)
    qb = q_ref[rows, :] * (HK ** -0.5) * eb
    k = k_ref[rows, :]
    kb = k * jnp.exp(-b)
    kl = k * jnp.exp(blast - b)
    A = jnp.where(causal, _bdot(qb, kb, NT), 0.0)
    return causal, u, b, blast, eb, qb, kb, kl, A


def _gla_in_specs(T):
    return [pl.BlockSpec((T, HK), lambda h: (0, h)), pl.BlockSpec((T, HK), lambda h: (0, GLA_H + h)),
            pl.BlockSpec((T, HV), lambda h: (0, GLA_H + h)), pl.BlockSpec((T, HK), lambda h: (0, h)),
            pl.BlockSpec((1, HK), lambda h: (0, h))]


def gla_fwd(zmain, ug, gate_b):
    T = zmain.shape[0]
    nC = T // CHUNK

    def body(q_ref, k_ref, v_ref, u_ref, b_ref, o_ref, s_ref, st_ref):
        st_ref[...] = jnp.zeros_like(st_ref)

        def step(n, carry):
            rows = pl.ds(pl.multiple_of(n * CHUNK, CHUNK), CHUNK)
            _, _, _, blast, _, qb, _, kl, A = _gla_chunk(q_ref, k_ref, v_ref, u_ref, b_ref, rows)
            v = v_ref[rows, :]
            ST = st_ref[...]
            s_ref[n] = ST
            o_ref[rows, :] = _bdot(qb, ST, NT) + _bdot(A, v, NN)
            st_ref[...] = ST * jnp.exp(blast) + _bdot(v, kl, TN)
            return carry

        lax.fori_loop(0, nC, step, 0)

    return pl.pallas_call(
        body, grid=(GLA_H,), in_specs=_gla_in_specs(T),
        out_specs=[pl.BlockSpec((T, HV), lambda h: (0, h)), pl.BlockSpec((nC, None, HV, HK), lambda h: (0, h, 0, 0))],
        out_shape=[_sds((T, GLA_H * HV), F32), _sds((nC, GLA_H, HV, HK), F32)],
        scratch_shapes=[pltpu.VMEM((HV, HK), F32)], name="gla_fwd", compiler_params=_cp("parallel"),
    )(zmain, zmain, zmain, ug, gate_b)


def gla_bwd(zmain, ug, gate_b, S, do):
    T = zmain.shape[0]
    nC = T // CHUNK

    def body(q_ref, k_ref, v_ref, u_ref, b_ref, s_ref, do_ref, dq_ref, dk_ref, dv_ref, du_ref, dgb_ref, dst_ref):
        dst_ref[...] = jnp.zeros_like(dst_ref)
        dgb_ref[...] = jnp.zeros_like(dgb_ref)

        def step(it, carry):
            n = nC - 1 - it
            rows = pl.ds(pl.multiple_of(n * CHUNK, CHUNK), CHUNK)
            causal, u, b, blast, eb, qb, kb, kl, A = _gla_chunk(q_ref, k_ref, v_ref, u_ref, b_ref, rows)
            v = v_ref[rows, :]
            dout = do_ref[rows, :]
            ST = s_ref[n]
            dST = dst_ref[...]
            elast = jnp.exp(blast)
            dA = jnp.where(causal, _bdot(dout, v, NT), 0.0)
            dv_ref[rows, :] = (_bdot(A, dout, TN) + _bdot(kl, dST, NT)).astype(dv_ref.dtype)
            dqb = _bdot(dout, ST, NN) + _bdot(dA, kb, NN)
            dkb = _bdot(dA, qb, TN)
            dkl = _bdot(v, dST, NN)
            ddec = jnp.sum(ST * dST, axis=0, keepdims=True)
            dst_ref[...] = dST * elast + _bdot(dout, qb, TN)
            dq_ref[rows, :] = (dqb * eb * (HK ** -0.5)).astype(dq_ref.dtype)
            dk_ref[rows, :] = (dkb * jnp.exp(-b) + dkl * jnp.exp(blast - b)).astype(dk_ref.dtype)
            db = dqb * qb - dkb * kb - dkl * kl
            dbl = jnp.sum(dkl * kl, axis=0, keepdims=True) + elast * ddec
            dg = _dot(jnp.logical_not(causal).astype(F32) + jnp.where(
                lax.broadcasted_iota(jnp.int32, (CHUNK, CHUNK), 0) == lax.broadcasted_iota(jnp.int32, (CHUNK, CHUNK), 1),
                1.0, 0.0), db, NN, HI) + dbl
            du = dg * (1.0 / GLA_TAU) / (1.0 + jnp.exp(u))
            du_ref[rows, :] = du
            dgb_ref[...] += jnp.sum(du, axis=0, keepdims=True)
            return carry

        lax.fori_loop(0, nC, step, 0)

    specs = _gla_in_specs(T) + [pl.BlockSpec((nC, None, HV, HK), lambda h: (0, h, 0, 0)),
                                pl.BlockSpec((T, HV), lambda h: (0, h))]
    return pl.pallas_call(
        body, grid=(GLA_H,), in_specs=specs,
        out_specs=[pl.BlockSpec((T, HK), lambda h: (0, h)), pl.BlockSpec((T, HK), lambda h: (0, h)),
                   pl.BlockSpec((T, HV), lambda h: (0, h)), pl.BlockSpec((T, HK), lambda h: (0, h)),
                   pl.BlockSpec((1, HK), lambda h: (0, h))],
        out_shape=[_sds((T, GLA_H * HK), BF), _sds((T, GLA_H * HK), BF), _sds((T, GLA_H * HV), BF),
                   _sds((T, GLA_H * HK), F32), _sds((1, GLA_H * HK), F32)],
        scratch_shapes=[pltpu.VMEM((HV, HK), F32)], name="gla_bwd", compiler_params=_cp("parallel"),
    )(zmain, zmain, zmain, ug, gate_b, S, do)


def _gla_post_fn(o, r, g):
    outs = []
    for h in range(GLA_H):
        on = _rms_parts(o[:, h * HV:(h + 1) * HV])[0] * g
        outs.append(on * _silu(r[:, h * HV:(h + 1) * HV]))
    return jnp.concatenate(outs, axis=1)


def _gla_post_bwd_fn(da, o, r, g):
    dos, drs = [], []
    dg = jnp.zeros((1, HV), F32)
    for h in range(GLA_H):
        sl = slice(h * HV, (h + 1) * HV)
        xh, rstd = _rms_parts(o[:, sl])
        drs.append(da[:, sl] * xh * g * _dsilu(r[:, sl]))
        don = da[:, sl] * _silu(r[:, sl])
        dg = dg + jnp.sum(don * xh, axis=0, keepdims=True)
        dxh = don * g
        dos.append(rstd * (dxh - xh * jnp.mean(dxh * xh, axis=-1, keepdims=True)))
    return jnp.concatenate(dos, axis=1), jnp.concatenate(drs, axis=1), dg


def conv_fwd(u, dw, dwb):
    T, C = u.shape
    TB = min(T, 256)

    def body(u_ref, w_ref, b_ref, y_ref, pad_ref):
        pad_ref[0:CONV_PAD, :] = jnp.zeros((CONV_PAD, LANES), F32)
        pad_ref[CONV_PAD:CONV_PAD + T, :] = u_ref[...]
        off = CONV_PAD - (CONV_W - 1)
        for t0 in range(0, T, TB):
            acc = jnp.zeros((TB, LANES), F32) + b_ref[...]
            for j in range(CONV_W):
                acc = acc + w_ref[j:j + 1, :] * pad_ref[t0 + off + j:t0 + off + j + TB, :]
            y_ref[t0:t0 + TB, :] = acc

    col = lambda i: (0, i)
    return pl.pallas_call(
        body, grid=(C // LANES,),
        in_specs=[pl.BlockSpec((T, LANES), col), pl.BlockSpec((CONV_W, LANES), col), pl.BlockSpec((1, LANES), col)],
        out_specs=pl.BlockSpec((T, LANES), col), out_shape=_sds((T, C), F32),
        scratch_shapes=[pltpu.VMEM((T + CONV_PAD, LANES), F32)], name="conv_fwd", compiler_params=_cp("parallel"),
    )(u, dw, dwb)


def conv_bwd(dy, u, dw):
    T, C = u.shape
    TB = min(T, 256)

    def body(dy_ref, u_ref, w_ref, du_ref, dw_ref, db_ref, upad, dypad):
        upad[0:CONV_PAD, :] = jnp.zeros((CONV_PAD, LANES), F32)
        upad[CONV_PAD:CONV_PAD + T, :] = u_ref[...]
        dypad[0:T, :] = dy_ref[...]
        dypad[T:T + CONV_PAD, :] = jnp.zeros((CONV_PAD, LANES), F32)
        off = CONV_PAD - (CONV_W - 1)
        for t0 in range(0, T, TB):
            acc = jnp.zeros((TB, LANES), F32)
            for j in range(CONV_W):
                s = t0 + (CONV_W - 1) - j
                acc = acc + w_ref[j:j + 1, :] * dypad[s:s + TB, :]
            du_ref[t0:t0 + TB, :] = acc
        for j in range(CONV_W):
            acc = jnp.zeros((TB, LANES), F32)
            for t0 in range(0, T, TB):
                acc = acc + dy_ref[t0:t0 + TB, :] * upad[t0 + off + j:t0 + off + j + TB, :]
            dw_ref[j:j + 1, :] = jnp.sum(acc, axis=0, keepdims=True)
        db_ref[...] = jnp.sum(dy_ref[...], axis=0, keepdims=True)

    col = lambda i: (0, i)
    return pl.pallas_call(
        body, grid=(C // LANES,),
        in_specs=[pl.BlockSpec((T, LANES), col), pl.BlockSpec((T, LANES), col), pl.BlockSpec((CONV_W, LANES), col)],
        out_specs=[pl.BlockSpec((T, LANES), col), pl.BlockSpec((CONV_W, LANES), col), pl.BlockSpec((1, LANES), col)],
        out_shape=[_sds((T, C), F32), _sds((CONV_W, C), F32), _sds((1, C), F32)],
        scratch_shapes=[pltpu.VMEM((T + CONV_PAD, LANES), F32), pltpu.VMEM((T + CONV_PAD, LANES), F32)],
        name="conv_bwd", compiler_params=_cp("parallel"),
    )(dy, u, dw)


def _ln_parts(x):
    mu = jnp.mean(x, axis=-1, keepdims=True)
    xc = x - mu
    rstd = lax.rsqrt(jnp.mean(xc * xc, axis=-1, keepdims=True) + EPS)
    return xc * rstd, rstd


def _ln_silu_fn(x, g, b):
    return _silu(_ln_parts(x)[0] * g + b)


def _ln_silu_bwd_fn(dbo, x, g, b):
    xh, rstd = _ln_parts(x)
    dy = dbo * _dsilu(xh * g + b)
    dyg = dy * g
    dx = rstd * (dyg - jnp.mean(dyg, axis=-1, keepdims=True) - xh * jnp.mean(dyg * xh, axis=-1, keepdims=True))
    return dx, jnp.sum(dy * xh, axis=0, keepdims=True), jnp.sum(dy, axis=0, keepdims=True)


def _glu_bwd_fn(du, ca, cb):
    s = _sigmoid(cb)
    return du * s, du * ca * s * (1.0 - s)


DIAGS = QB + KW


def _onehot_diag():
    j = lax.broadcasted_iota(jnp.int32, (REL_PAD, DIAGS), 1)
    i = lax.broadcasted_iota(jnp.int32, (REL_PAD, DIAGS), 0)
    return (i == jnp.clip(KW - j, -REL_CLIP, REL_CLIP) + REL_CLIP).astype(F32)


def relbias_tile(rbp):
    def body(rb_ref, o_ref, e_ref):
        e_ref[...] = _dot(rb_ref[...], _onehot_diag(), NN, HI)
        tc = lax.shift_right_logical(lax.broadcasted_iota(jnp.int32, (QB, KW), 0), 6)
        wc = lax.shift_right_logical(lax.broadcasted_iota(jnp.int32, (QB, KW), 1), 6)
        ok = jnp.logical_and(wc >= tc, wc <= tc + LEFT_CHUNKS)
        for h in range(ATT_H):
            spread = pltpu.roll(jnp.broadcast_to(e_ref[h:h + 1, :], (QB, DIAGS)), KW, 1, stride=1, stride_axis=0)
            o_ref[h] = jnp.where(ok, spread[:, :KW], NEG)

    return pl.pallas_call(body, out_shape=_sds((ATT_H, QB, KW), F32), name="relbias_tile",
                          scratch_shapes=[pltpu.VMEM((ATT_H, DIAGS), F32)], compiler_params=_cp())(rbp)


def relbias_reduce(dbm):
    def body(d_ref, o_ref, e_ref):
        u = lax.broadcasted_iota(jnp.int32, (QB, QB), 0)
        t = lax.broadcasted_iota(jnp.int32, (QB, QB), 1)
        flip = (u + t == QB - 1).astype(F32)
        for h in range(ATT_H):
            padded = jnp.concatenate([d_ref[h], jnp.zeros((QB, QB), F32)], axis=1)
            lined = pltpu.roll(_dot(flip, padded, NN, HI), 1, 1, stride=1, stride_axis=0)
            e_ref[h:h + 1, :] = jnp.sum(lined, axis=0, keepdims=True)
        o_ref[...] = _dot(e_ref[...], _onehot_diag(), NT, HI)

    return pl.pallas_call(body, out_shape=_sds((ATT_H, REL_PAD), F32), name="relbias_reduce",
                          scratch_shapes=[pltpu.VMEM((ATT_H, DIAGS), F32)], compiler_params=_cp())(dbm)


def _att_scores(q_ref, kp_ref, bm_ref, i):
    q0 = pl.multiple_of(i * QB, QB)
    kw = kp_ref[pl.ds(q0, KW), :]
    s = _bdot(q_ref[...], kw, NT) * (HD ** -0.5) + bm_ref[...]
    w = lax.broadcasted_iota(jnp.int32, (QB, KW), 1)
    return jnp.where(w + q0 >= PADK, s, NEG), kw, q0


def attn_fwd(qkv, kvp, bm):
    T = qkv.shape[0]
    D = ATT_H * HD

    def body(q_ref, kp_ref, vp_ref, bm_ref, o_ref, lse_ref):
        s, _, q0 = _att_scores(q_ref, kp_ref, bm_ref, pl.program_id(1))
        m = jnp.max(s, axis=-1, keepdims=True)
        e = jnp.exp(s - m)
        l = jnp.sum(e, axis=-1, keepdims=True)
        o_ref[...] = _bdot(e * (1.0 / l), vp_ref[pl.ds(q0, KW), :], NN).astype(o_ref.dtype)
        lse_ref[...] = m + jnp.log(l)

    return pl.pallas_call(
        body, grid=(ATT_H, T // QB),
        in_specs=[pl.BlockSpec((QB, HD), lambda h, i: (i, h)), pl.BlockSpec((T + PADK, HD), lambda h, i: (0, h)),
                  pl.BlockSpec((T + PADK, HD), lambda h, i: (0, ATT_H + h)),
                  pl.BlockSpec((None, QB, KW), lambda h, i: (h, 0, 0))],
        out_specs=[pl.BlockSpec((QB, HD), lambda h, i: (i, h)), pl.BlockSpec((None, QB, 1), lambda h, i: (h, i, 0))],
        out_shape=[_sds((T, D), BF), _sds((ATT_H, T, 1), F32)], name="attn_fwd",
        compiler_params=_cp("parallel", "arbitrary"),
    )(qkv, kvp, kvp, bm)


def attn_bwd(qkv, kvp, bm, o, lse, do):
    T = qkv.shape[0]
    D = ATT_H * HD

    def body(q_ref, kp_ref, vp_ref, bm_ref, o_ref, lse_ref, do_ref, dq_ref, dkp_ref, dvp_ref, dbm_ref):
        i = pl.program_id(1)

        @pl.when(i == 0)
        def _():
            dkp_ref[...] = jnp.zeros_like(dkp_ref)
            dvp_ref[...] = jnp.zeros_like(dvp_ref)
            dbm_ref[...] = jnp.zeros_like(dbm_ref)

        s, kw, q0 = _att_scores(q_ref, kp_ref, bm_ref, i)
        p = jnp.exp(s - lse_ref[...])
        dout = do_ref[...]
        dp = _bdot(dout, vp_ref[pl.ds(q0, KW), :], NT)
        delta = jnp.sum(_f(dout) * _f(o_ref[...]), axis=-1, keepdims=True)
        ds = p * (dp - delta)
        dq_ref[...] = (_bdot(ds, kw, NN) * (HD ** -0.5)).astype(dq_ref.dtype)
        dkp_ref[pl.ds(q0, KW), :] += _bdot(ds, q_ref[...], TN) * (HD ** -0.5)
        dvp_ref[pl.ds(q0, KW), :] += _bdot(p, dout, TN)
        dbm_ref[...] += ds

    qspec = pl.BlockSpec((QB, HD), lambda h, i: (i, h))
    kspec = pl.BlockSpec((T + PADK, HD), lambda h, i: (0, h))
    bspec = pl.BlockSpec((None, QB, KW), lambda h, i: (h, 0, 0))
    return pl.pallas_call(
        body, grid=(ATT_H, T // QB),
        in_specs=[qspec, kspec, pl.BlockSpec((T + PADK, HD), lambda h, i: (0, ATT_H + h)), bspec, qspec,
                  pl.BlockSpec((None, QB, 1), lambda h, i: (h, i, 0)), qspec],
        out_specs=[qspec, kspec, kspec, bspec],
        out_shape=[_sds((T, D), BF), _sds((T + PADK, D), F32), _sds((T + PADK, D), F32), _sds((ATT_H, QB, KW), F32)],
        name="attn_bwd", compiler_params=_cp("parallel", "arbitrary"),
    )(qkv, kvp, kvp, bm, o, lse, do)


def _final_fn(x, tgt, g):
    D = x.shape[1]
    xh, rstd = _rms_parts(x)
    diff = xh * g - tgt
    dy = diff * (1.0 / D)
    dxh = dy * g
    dx = rstd * (dxh - xh * jnp.mean(dxh * xh, axis=-1, keepdims=True))
    loss = jnp.sum(jnp.sum(diff * diff, axis=-1, keepdims=True), axis=0, keepdims=True) * (0.5 / D)
    return dx, jnp.sum(dy * xh, axis=0, keepdims=True), jnp.broadcast_to(loss, (1, LANES))


def _adamw_fn(w, g, m, v):
    m = ADAM_B1 * m + (1.0 - ADAM_B1) * g
    v = ADAM_B2 * v + (1.0 - ADAM_B2) * (g * g)
    m_hat = m / (1.0 - ADAM_B1 ** ADAM_STEP)
    v_hat = v / (1.0 - ADAM_B2 ** ADAM_STEP)
    delta = -ADAM_LR * (m_hat / (jnp.sqrt(v_hat) + ADAM_EPS) + ADAM_WD * w)
    return g, delta, m, v


def adamw(w, g, m, v, name):
    shape = w.shape
    C = shape[-1]
    R = w.size // C
    outs = rowwise(name, _adamw_fn, [t.reshape(R, C) for t in (w, g, m, v)], [], [(C, F32)] * 4)
    return tuple(t.reshape(shape) for t in outs)


def adamw_unit(w, m, v, g, lead, outs, name):
    R, B = w.shape[-2:]
    tb = _row_block(R, 256)
    if outs is None:
        outs = [lax.empty(w.shape, F32) for _ in range(4)]

    def body(w_ref, m_ref, v_ref, g_ref, *rest):
        for ref, val in zip(rest[4:], _adamw_fn(w_ref[...], g_ref[...], m_ref[...], v_ref[...])):
            ref[...] = val

    native = pl.BlockSpec((None,) * len(lead) + (tb, B), lambda r: tuple(lead) + (r, 0))
    return pl.pallas_call(
        body, grid=(R // tb,), in_specs=[native] * 3 + [pl.BlockSpec((tb, B), lambda r: (r, 0))] + [HBM_SPEC] * 4,
        out_specs=[native] * 4, out_shape=[_sds(w.shape, F32)] * 4, input_output_aliases={4 + i: i for i in range(4)},
        name=name, compiler_params=_cp("arbitrary"),
    )(w, m, v, g, *outs)


WEIGHTS = ['ffn_norm', 'ffn_w_gate', 'ffn_w_up', 'ffn_w_down', 'mix_norm', 'ab_w_in', 'gla_gate_w', 'gla_gate_b',
           'gla_norm_g', 'conv_dw', 'conv_dw_b', 'conv_ln_g', 'conv_ln_b', 'ab_w_out', 'att_w_qkv', 'att_rel_bias',
           'att_w_o', 'pl_norm', 'pl_w_gate', 'pl_w_proj', 'final_norm']
BIG = ['ffn_w_gate', 'ffn_w_up', 'ffn_w_down', 'ab_w_in', 'ab_w_out', 'att_w_qkv', 'att_w_o', 'pl_w_gate', 'pl_w_proj']


def _pack(parts, rows):
    flat = jnp.concatenate([p.reshape(-1) for p in parts])
    return jnp.pad(flat, (0, rows * LANES - flat.shape[0])).reshape(rows, LANES)


def _unpack(flat, shapes):
    out, pos = [], 0
    for s in shapes:
        n = 1
        for d in s:
            n *= d
        out.append(flat[pos:pos + n].reshape(s))
        pos += n
    return out


def _rows_for(shapes):
    n = sum(math.prod(s) for s in shapes)
    return -(-n // (8 * LANES)) * 8


def kernel(x, p, ffn_norm, ffn_w_gate, ffn_w_up, ffn_w_down, mix_norm, ab_w_in, gla_gate_w, gla_gate_b, gla_norm_g, conv_dw, conv_dw_b, conv_ln_g, conv_ln_b, ab_w_out, att_w_qkv, att_rel_bias, att_w_o, pl_norm, pl_w_gate, pl_w_proj, final_norm, loss_target, m_ffn_norm, m_ffn_w_gate, m_ffn_w_up, m_ffn_w_down, m_mix_norm, m_ab_w_in, m_gla_gate_w, m_gla_gate_b, m_gla_norm_g, m_conv_dw, m_conv_dw_b, m_conv_ln_g, m_conv_ln_b, m_ab_w_out, m_att_w_qkv, m_att_rel_bias, m_att_w_o, m_pl_norm, m_pl_w_gate, m_pl_w_proj, m_final_norm, v_ffn_norm, v_ffn_w_gate, v_ffn_w_up, v_ffn_w_down, v_mix_norm, v_ab_w_in, v_gla_gate_w, v_gla_gate_b, v_gla_norm_g, v_conv_dw, v_conv_dw_b, v_conv_ln_g, v_conv_ln_b, v_ab_w_out, v_att_w_qkv, v_att_rel_bias, v_att_w_o, v_pl_norm, v_pl_w_gate, v_pl_w_proj, v_final_norm):
    env = dict(locals())
    W = {n: env[n] for n in WEIGHTS}
    M = {n: env["m_" + n] for n in WEIGHTS}
    V = {n: env["v_" + n] for n in WEIGHTS}
    xc, yc_, cc = lax.axis_index("x"), lax.axis_index("y"), lax.axis_index("c")

    x0 = x[0]
    tgt = loss_target[0]
    T, D = x0.shape
    fs = ffn_w_gate.shape[-1]
    ws = ab_w_in.shape[-1]
    AB_IN = N_CHIPS * ws
    gz0 = 2 * GLA_H * HK + 2 * GLA_H * HV
    sidx = jnp.stack([2 * xc + yc_, cc, 4 * xc + 2 * yc_ + cc]).astype(jnp.int32)

    def ffn_keys(i, j):
        return [('ffn_w_gate', (i, j)), ('ffn_w_up', (i, j)), ('ffn_w_down', (i, j))]

    later = [[('ab_w_in', (0,)), ('ab_w_out', (0,))], ffn_keys(0, 1), [('pl_w_gate', (0,)), ('pl_w_proj', (0,))],
             ffn_keys(1, 0), [('att_w_qkv', (0,)), ('att_w_o', (0,))], ffn_keys(1, 1),
             [('pl_w_gate', (1,)), ('pl_w_proj', (1,))]]
    arrivals = [ffn_keys(0, 0)[:2], ffn_keys(0, 0)[2:]] + later
    order = [[k] for k in ffn_keys(0, 0)] + later
    full, in_flight = {}, []
    tok = jnp.zeros(TOKEN, F32)
    for s, keys in enumerate(arrivals):
        parts = [cast_unit(W[n], lead, sidx, "cast_" + n + "".join(str(i) for i in lead)) for n, lead in keys]
        ssem, rsem, thru, tok = split_start("gather_start_%d" % s, parts, _gather_copies, 3 * len(keys), tok)
        in_flight.append((ssem, rsem, thru))

    def arrive(s, after):
        ssem, rsem, thru = in_flight[s]
        landed = split_wait("gather_wait_%d" % s, ssem, rsem, thru, _gather_copies, after)
        for key, buf in zip(arrivals[s], gather_pass_on(landed, "gather_pass_on_%d" % s)):
            full[key] = buf

    small_sharded = [ffn_norm, gla_gate_w, conv_dw]
    rows_s = _rows_for([t.shape for t in small_sharded])
    got = allgather8(_pack(small_sharded, rows_s), "gather_small").reshape(N_CHIPS, 2, rows_s * LANES)[:, 0]
    per_chip = [_unpack(got[k], [t.shape for t in small_sharded]) for k in range(N_CHIPS)]
    ffn_norm_f, gate_w_f, conv_dw_f = [jnp.concatenate([per_chip[k][t] for k in range(N_CHIPS)], axis=-1)
                                       for t in range(3)]
    gate_w_p = jnp.pad(gate_w_f[0], ((0, LANES - 16), (0, 0)))
    conv_w = conv_dw_f[0]
    rb_p = jnp.pad(att_rel_bias[0], ((0, 0), (0, REL_PAD - att_rel_bias.shape[-1])))

    G = {}
    small_g = {}

    def ffn_w(i, j):
        return full['ffn_w_gate', (i, j)], full['ffn_w_up', (i, j)], full['ffn_w_down', (i, j)]

    saved = {}
    xs = x0
    arrive(0, tok)

    def first_down(after):
        arrive(1, after)
        return full['ffn_w_down', (0, 0)]

    xs, saved['f00'] = ffn_fwd(xs, ffn_norm_f[0, 0][None], full['ffn_w_gate', (0, 0)], full['ffn_w_up', (0, 0)],
                               first_down, dep=tok)
    arrive(2, xs)
    w_in = jnp.transpose(full['ab_w_in', (0,)], (1, 0, 2)).reshape(D, AB_IN)
    w_main = jnp.concatenate([w_in[:, :gz0], w_in[:, gz0 + 16:]], axis=1)
    w_gz = jnp.pad(w_in[:, gz0:gz0 + 16], ((0, 0), (0, LANES - 16)))
    w_out = full['ab_w_out', (0,)].reshape(D, D)

    def mixer0_fwd(xin):
        h = rms_fwd(xin, mix_norm[0][None], "mix0_rms")
        zmain = mm_nn("ab_in", h, w_main, F32, 1024)
        gzp = mm_nn("ab_gz", h, w_gz, F32, LANES)
        ug = mm_nn("gla_gate", gzp, gate_w_p, F32, GLA_H * HK)
        o, S = gla_fwd(zmain, ug, gla_gate_b)
        a_out = rowwise("gla_post", _gla_post_fn, [o, ("cols", zmain, GLA_H * HV, 2)], [gla_norm_g], [(GLA_H * HV, BF)])
        u = rowwise("conv_glu", lambda a, b: a * _sigmoid(b), [("cols", zmain, 1024, 3), ("cols", zmain, 1024, 4)], [],
                    [(1024, F32)])
        yc = conv_fwd(u, conv_w, conv_dw_b)
        b_out = rowwise("conv_ln", _ln_silu_fn, [yc], [conv_ln_g, conv_ln_b], [(1024, BF)])
        cat = jnp.concatenate([a_out, b_out], axis=1)
        x2 = mm_nn("ab_out", cat, w_out, F32, 1024, res=xin)
        return x2, (xin, h, zmain, gzp, ug, o, S, u, yc, cat)

    xs, saved['m0'] = mixer0_fwd(xs)
    arrive(3, xs)
    xs, saved['f01'] = ffn_fwd(xs, ffn_norm_f[0, 1][None], *ffn_w(0, 1))
    ks = D // N_CHIPS
    dp = p.shape[-1]

    def pl_fwd(xin, i):
        h = rms_fwd(xin, pl_norm[i][None], "pl_rms")
        tm = min(T, 1024)
        u = mm_nn("pl_gate", h, full['pl_w_gate', (i,)].reshape(D, D), F32, 1024)
        e = mm("pl_proj", NN, (T // tm, N_CHIPS, 1), p[i, 0], (tm, dp), _mk, full['pl_w_proj', (i,)],
               (None, dp, ks), lambda m, n, k: (n, 0, 0), (tm, ks), _mn, _sds((T, D), F32))
        x2 = rowwise("pl_mix", lambda xv, uv, ev: xv + _sigmoid(uv) * ev, [xin, u, e], [], [(D, F32)])
        return x2, (xin, h, u, e)

    arrive(4, xs)
    xs, saved['p0'] = pl_fwd(xs, 0)
    arrive(5, xs)
    xs, saved['f10'] = ffn_fwd(xs, ffn_norm_f[1, 0][None], *ffn_w(1, 0))
    arrive(6, xs)
    w_qkv = full['att_w_qkv', (0,)]
    w_o = full['att_w_o', (0,)].reshape(D, D)
    qs = w_qkv.shape[-1]

    bm = relbias_tile(rb_p)

    def mixer1_fwd(xin):
        h = rms_fwd(xin, mix_norm[1][None], "mix1_rms")
        tm, tn = min(T, 1024), 512
        per = qs // tn
        qkv = mm("att_qkv", NN, (T // tm, 3 * D // tn, 1), h, (tm, D), _mk, w_qkv, (None, D, tn),
                 lambda m, n, k: (n // per, k, n % per), (tm, tn), _mn, _sds((T, 3 * D), BF))
        kvp = jnp.pad(qkv[:, D:], ((PADK, 0), (0, 0)))
        o, lse = attn_fwd(qkv, kvp, bm)
        x2 = mm_nn("att_o", o, w_o, F32, 1024, res=xin)
        return x2, (xin, h, qkv, kvp, o, lse)

    xs, saved['m1'] = mixer1_fwd(xs)
    arrive(7, xs)
    xs, saved['f11'] = ffn_fwd(xs, ffn_norm_f[1, 1][None], *ffn_w(1, 1))
    arrive(8, xs)
    xs, saved['p1'] = pl_fwd(xs, 1)

    dx, small_g['final_norm'], loss_acc = rowwise("loss_head", _final_fn, [xs, tgt], [final_norm[None]], [(D, F32)],
                                                  [(1, D), (1, LANES)])
    loss = lax.psum(loss_acc[0, 0], ("x", "y", "c"))

    def pl_bwd(dx2, sv, i, dep):
        xin, h, u, e = sv
        tm = min(T, 1024)

        def fn(d, uv, ev):
            s = _sigmoid(uv)
            return d * s, d * ev * s * (1.0 - s)

        de, du = rowwise("pl_mix_bwd", fn, [dx2, u, e], [], [(D, BF), (D, BF)], dep=dep)
        G['pl_w_proj', (i,)] = mm("pl_dproj", TN, (1, N_CHIPS, 1), p[i, 0], (T, dp), _km, de, (T, ks), _kn,
                                  (None, dp, ks), lambda m, n, k: (n, 0, 0), _sds((N_CHIPS, dp, ks), BF))
        G['pl_w_gate', (i,)] = mm_tn("pl_dgate", h, du, BF, ks, 1024).reshape(N_CHIPS, ks, D)
        dh = mm_nt("pl_dh", du, full['pl_w_gate', (i,)].reshape(D, D), F32, 1024)
        return rms_bwd(xin, pl_norm[i][None], dh, dx2, "pl_rms_bwd")

    def ffn_b(dx2, key, i, j, dep):
        def sends(name, g, dep):
            G[name, (i, j)] = g
            s = [t for t, keys in enumerate(order) if (name, (i, j)) in keys][0]
            if not all(k in G for k in order[s]):
                return dep
            tok = scatter(s, dep)
            return to_chips(tok) if len(order[s]) == 1 else tok

        return ffn_bwd(dx2, saved[key], ffn_norm_f[i, j][None], *ffn_w(i, j), dep, sends)

    def mixer1_bwd(dx2, sv, dep):
        xin, h, qkv, kvp, o, lse = sv
        do = mm_nt("att_do", dx2, w_o, BF, 1024, dep=dep)
        dwo = mm_tn("att_dwo", o, dx2, BF, 1024, 1024, dep=dep)
        dq, dkp, dvp, dbm = attn_bwd(qkv, kvp, bm, o, lse, do)
        dqkv = jnp.concatenate([dq, dkp[PADK:].astype(BF), dvp[PADK:].astype(BF)], axis=1)
        tm, tn = min(T, 1024), 512
        per = qs // tn
        dwqkv = mm("att_dwqkv", TN, (D // 1024, 3 * D // tn, 1), h, (T, 1024), _km, dqkv, (T, tn), _kn,
                   (None, 1024, tn), lambda m, n, k: (n // per, m, n % per), _sds((N_CHIPS, D, qs), BF))
        dh = mm("att_dh", NT, (T // tm, D // 1024, N_CHIPS), dqkv, (tm, qs), _mk, w_qkv, (None, 1024, qs),
                lambda m, n, k: (k, n, 0), (tm, 1024), _mn, _sds((T, D), F32))
        dxn, dgain = rms_bwd(xin, mix_norm[1][None], dh, dx2, "mix1_rms_bwd")
        return dxn, dgain, dwo, dwqkv, relbias_reduce(dbm)

    def mixer0_bwd(dx2, sv, dep):
        xin, h, zmain, gzp, ug, o, S, u, yc, cat = sv
        dcat = mm_nt("ab_dcat", dx2, w_out, F32, 1024, dep=dep)
        dwout = mm_tn("ab_dwout", cat, dx2, BF, 1024, 1024, dep=dep)
        do, dr, dgn = rowwise("gla_post_bwd", _gla_post_bwd_fn,
                              [("cols", dcat, GLA_H * HV, 0), o, ("cols", zmain, GLA_H * HV, 2)], [gla_norm_g],
                              [(GLA_H * HV, F32), (GLA_H * HV, BF)], [(1, HV)])
        dyc, dlg, dlb = rowwise("conv_ln_bwd", _ln_silu_bwd_fn, [("cols", dcat, 1024, 1), yc], [conv_ln_g, conv_ln_b],
                                [(1024, F32)], [(1, 1024), (1, 1024)])
        du, ddw, ddwb = conv_bwd(dyc, u, conv_w)
        dca, dcb = rowwise("conv_glu_bwd", _glu_bwd_fn, [du, ("cols", zmain, 1024, 3), ("cols", zmain, 1024, 4)], [],
                           [(1024, BF), (1024, BF)])
        dq, dk, dv, dug, dgb = gla_bwd(zmain, ug, gla_gate_b, S, do)
        dgw = mm_tn("gla_dgate_w", gzp, dug, F32, LANES, GLA_H * HK)
        dgzp = mm_nt("gla_dgz", dug, gate_w_p, F32, LANES)
        dzm = jnp.concatenate([dq, dk, dv, dr, dca, dcb], axis=1)
        dwmain = mm_tn("ab_dwmain", h, dzm, BF, 1024, 1024)
        dwgz = mm_tn("ab_dwgz", h, dgzp, BF, 1024, LANES)
        dh = mm_nt("ab_dh_gz", dgzp, w_gz, F32, 1024, res=mm_nt("ab_dh", dzm, w_main, F32, 512))
        dxn, dgain = rms_bwd(xin, mix_norm[0][None], dh, dx2, "mix0_rms_bwd")
        dwin = jnp.concatenate([dwmain[:, :gz0], dwgz[:, :16], dwmain[:, gz0:]], axis=1)
        g_win = jnp.transpose(dwin.reshape(D, N_CHIPS, ws), (1, 0, 2))
        return dxn, dgain, g_win, dwout, (dgn, dlg, dlb, ddw, ddwb, dgw[:16], dgb)

    scattering, exchanging = [], []

    def tags_of(s):
        return [n + "".join(str(i) for i in lead) for n, lead in order[s]]

    def to_chips(tok):
        if not exchanging:
            return tok
        s, ssem, rsem, thru = exchanging.pop()
        n_u = len(order[s])
        landed = split_wait("sibling_wait_%d" % s, ssem, rsem, thru, _sibling_copies, tok)
        hs = [add_sibling(g4, got, sidx, tag) for g4, got, tag in zip(landed[:n_u], landed[n_u:], tags_of(s))]
        land = [lax.empty((3,) + h.shape[1:], h.dtype) for h in hs]
        ssem, rsem, thru, tok = split_start("scatter_start_%d" % s, hs + land, _scatter_copies, 3 * n_u, tok)
        scattering.append((s, ssem, rsem, thru))
        return tok

    def scatter(s, tok):
        g4s = [G[key].reshape(N_CHIPS, 2, G[key].shape[1] // 2, G[key].shape[2]) for key in order[s]]
        land = [lax.empty((N_CHIPS,) + g4.shape[2:], g4.dtype) for g4 in g4s]
        ssem, rsem, thru, tok = split_start("sibling_start_%d" % s, g4s + land, _sibling_copies, N_CHIPS * len(g4s), tok)
        tok = to_chips(tok)
        exchanging.append((s, ssem, rsem, thru))
        return tok

    dpl, dffn, dmix = [None, None], [[None, None], [None, None]], [None, None]
    tok = jnp.zeros(TOKEN, F32)
    dx, dpl[1] = pl_bwd(dx, saved['p1'], 1, tok)
    tok = scatter(9, tok)
    dx, dffn[1][1], tok = ffn_b(dx, 'f11', 1, 1, tok)
    dx, dmix[1], dwo, dwqkv, drb = mixer1_bwd(dx, saved['m1'], tok)
    G['att_w_qkv', (0,)] = dwqkv
    G['att_w_o', (0,)] = dwo.reshape((N_CHIPS,) + att_w_o.shape[1:])
    tok = scatter(7, tok)
    dx, dffn[1][0], tok = ffn_b(dx, 'f10', 1, 0, tok)
    dx, dpl[0] = pl_bwd(dx, saved['p0'], 0, tok)
    tok = scatter(5, tok)
    dx, dffn[0][1], tok = ffn_b(dx, 'f01', 0, 1, tok)
    dx, dmix[0], g_win, dwout, (dgn, dlg, dlb, ddw, ddwb, dgw, dgb) = mixer0_bwd(dx, saved['m0'], tok)
    G['ab_w_in', (0,)] = g_win
    G['ab_w_out', (0,)] = dwout.reshape((N_CHIPS,) + ab_w_out.shape[1:])
    tok = scatter(3, tok)
    dx, dffn[0][0], tok = ffn_b(dx, 'f00', 0, 0, tok)
    grad_x = dx[None]

    small_g['ffn_norm'] = jnp.stack([jnp.stack([dffn[i][j][0] for j in range(2)]) for i in range(2)])
    small_g['mix_norm'] = jnp.concatenate(dmix, axis=0)
    small_g['gla_gate_w'] = dgw[None]
    small_g['gla_gate_b'] = dgb
    small_g['gla_norm_g'] = dgn
    small_g['conv_dw'] = ddw[None]
    small_g['conv_dw_b'] = ddwb
    small_g['conv_ln_g'] = dlg
    small_g['conv_ln_b'] = dlb
    small_g['att_rel_bias'] = drb[None, :, :att_rel_bias.shape[-1]]
    small_g['pl_norm'] = jnp.concatenate(dpl, axis=0)
    small_g['final_norm'] = small_g['final_norm'][0]
    small_names = [n for n in WEIGHTS if n not in BIG]
    small_shapes = [small_g[n].shape for n in small_names]
    rows_g = _rows_for(small_shapes)
    mine = rowwise("place_small_grads", lambda v: v, [_pack([small_g[n] for n in small_names], rows_g)], [],
                   [("dyn", 8, LANES, F32, 2)], tb=rows_g, sidx=sidx)
    small_ssem, small_rsem, small_thru, tok = split_start("small_start", [mine], _allgather_copies, 7, tok)

    grads, outs = {}, {}
    sharing, updated = [], [dx]

    def update(tok):
        if not sharing:
            return
        s, ssem, rsem, thru = sharing.pop()
        landed = split_wait("share_wait_%d" % s, ssem, rsem, thru, _share_copies, tok)
        for (n, lead), tag, r in zip(order[s], tags_of(s), landed):
            outs[n] = adamw_unit(W[n], M[n], V[n], r.reshape(W[n].shape[-2:]), lead, outs.get(n), "adamw_" + tag)
            updated.append(outs[n][0])

    after = dx
    for s, ssem, rsem, thru in scattering:
        n_u = len(order[s])
        landed = split_wait("scatter_wait_%d" % s, ssem, rsem, thru, _scatter_copies, after)
        rs = [add_chips(h, got2, sidx, tag) for h, got2, tag in zip(landed[:n_u], landed[n_u:], tags_of(s))]
        ssem, rsem, thru, tok = split_start("share_start_%d" % s, rs, _share_copies, n_u, tok)
        update(tok)
        sharing.append((s, ssem, rsem, thru))
        after = tok
    update(tok)

    (allp,) = split_wait("small_wait", small_ssem, small_rsem, small_thru, _allgather_copies, updated[-1])
    summed = rowwise("sum_small_grads", lambda *v: (((v[0] + v[1]) + (v[2] + v[3])) + ((v[4] + v[5]) + (v[6] + v[7]))),
                     [("leads", allp, (d,)) for d in range(8)], [], [(LANES, F32)], tb=rows_g)
    for n, g in zip(small_names, _unpack(summed.reshape(-1), small_shapes)):
        grads[n] = g
    chip = 2 * xc + yc_
    for n, axis in (('ffn_norm', 2), ('gla_gate_w', 2), ('conv_dw', 2)):
        width = W[n].shape[axis]
        grads[n] = lax.dynamic_slice_in_dim(grads[n], chip * width, width, axis)

    for n in small_names:
        outs[n] = adamw(W[n], grads[n], M[n], V[n], "adamw_" + n)
    return (loss, grad_x, *[outs[n][0] for n in WEIGHTS], *[outs[n][1] for n in WEIGHTS],
            *[outs[n][2] for n in WEIGHTS], *[outs[n][3] for n in WEIGHTS])
```

```python
import math

import jax
import jax.numpy as jnp
from jax import lax
from jax.experimental import pallas as pl
from jax.experimental.pallas import tpu as pltpu

F32 = jnp.float32
BF = jnp.bfloat16
MESH = pl.DeviceIdType.MESH
HI = lax.Precision.HIGHEST
V7X_VMEM_LIMIT = 56 * 1024 * 1024
LANES = 128
EPS = 1e-6
NEG = -1e30

CHUNK = 64
LEFT_CHUNKS = 8
QB = 256
KW = QB + LEFT_CHUNKS * CHUNK
PADK = LEFT_CHUNKS * CHUNK
REL_CLIP = 128
REL_PAD = 384
ATT_H = 16
HD = 128
GLA_H = 4
HK = 128
HV = 256
GLA_TAU = 16.0
CONV_W = 31
CONV_PAD = 32
N_CHIPS = 4

ADAM_LR = 0.001
ADAM_B1 = 0.9
ADAM_B2 = 0.999
ADAM_EPS = 1e-08
ADAM_WD = 0.01
ADAM_STEP = 10

NN = ((1,), (0,))
NT = ((1,), (1,))
TN = ((0,), (0,))


def _dot(a, b, dims, prec=None):
    return lax.dot_general(a, b, (dims, ((), ())), preferred_element_type=F32, precision=prec)


def _bdot(a, b, dims):
    return _dot(a.astype(BF), b.astype(BF), dims)


def _cp(*sem):
    return pltpu.CompilerParams(dimension_semantics=sem if sem else None, vmem_limit_bytes=V7X_VMEM_LIMIT)


def _sds(shape, dtype):
    return jax.ShapeDtypeStruct(shape, dtype)


def _sigmoid(x):
    return 1.0 / (1.0 + jnp.exp(-x))


def _silu(x):
    return x * _sigmoid(x)


def _dsilu(x):
    s = _sigmoid(x)
    return s * (1.0 + x * (1.0 - s))


def _f(x):
    return x.astype(F32)


def _row_item(item, tb):
    if not isinstance(item, tuple):
        return item, (tb, item.shape[1]), lambda i, s: (i, 0)
    kind, arr = item[0], item[1]
    if kind == "cols":
        return arr, (tb, item[2]), lambda i, s, blk=item[3]: (i, blk)
    if kind == "leads":
        lead = tuple(item[2])
        return arr, (None,) * len(lead) + (tb, arr.shape[-1]), lambda i, s: lead + (i, 0)
    if kind == "dyn":
        return arr, (None, tb, arr.shape[-1]), lambda i, s, sel=item[2]: (s[sel], i, 0)
    if kind == "dyn4":
        nb = arr.shape[2] // tb
        return arr, (None, None, tb, arr.shape[-1]), lambda i, s, sel=item[2]: (i // nb, s[sel], i % nb, 0)
    raise ValueError(kind)


def _row_block(n, cap):
    for cand in range(min(cap, n) // 16 * 16, 0, -16):
        if n % cand == 0:
            return cand
    return n


def rowwise(name, fn, rows, bcast, outs, accs=(), tb=256, n_rows=None, sidx=None, row_period=None, dep=None):
    if n_rows is None:
        first = rows[0][1] if isinstance(rows[0], tuple) else rows[0]
        n_rows = first.shape[-2]
    tb = _row_block(n_rows if row_period is None else row_period, tb)
    items = [_row_item(it, tb) for it in rows]
    n_r, n_b, n_o = len(rows), len(bcast), len(outs)
    pre = 0 if sidx is None else 1

    def wrap(f):
        return (lambda i: f(i, None)) if sidx is None else (lambda i, s: f(i, s))

    def body(*refs):
        refs = refs[pre:]
        r, b = refs[:n_r], refs[n_r:n_r + n_b]
        refs = refs[n_r + n_b + (dep is not None):]
        o, a = refs[:n_o], refs[n_o:]
        res = fn(*[v[...] for v in r], *[v[...] for v in b])
        if not isinstance(res, tuple):
            res = (res,)
        for ref, val in zip(o, res[:n_o]):
            ref[...] = val.astype(ref.dtype)
        if a:
            @pl.when(pl.program_id(0) == 0)
            def _():
                for ref in a:
                    ref[...] = jnp.zeros_like(ref)
            for ref, val in zip(a, res[n_o:]):
                ref[...] += val

    in_specs = [pl.BlockSpec(bs, wrap(f)) for _, bs, f in items]
    in_specs += [pl.BlockSpec(v.shape, wrap(lambda i, s, nd=v.ndim: (0,) * nd)) for v in bcast]
    out_specs, out_shape = [], []
    for o in outs:
        if o[0] == "dyn":
            _, L, c, dt, sel = o
            out_specs.append(pl.BlockSpec((None, tb, c), wrap(lambda i, s, sel=sel: (s[sel], i, 0))))
            out_shape.append(_sds((L, n_rows, c), dt))
        else:
            c, dt = o
            out_specs.append(pl.BlockSpec((tb, c), wrap(lambda i, s: (i, 0))))
            out_shape.append(_sds((n_rows, c), dt))
    out_specs += [pl.BlockSpec(sh, wrap(lambda i, s: (0, 0))) for sh in accs]
    out_shape += [_sds(sh, F32) for sh in accs]
    operands = [a for a, _, _ in items] + list(bcast)
    if dep is not None:
        operands.append(dep)
        in_specs.append(pl.BlockSpec(TOKEN, wrap(lambda i, s: (0, 0))))
    grid = (n_rows // tb,)
    if sidx is None:
        res = pl.pallas_call(body, grid=grid, in_specs=in_specs, out_specs=out_specs, out_shape=out_shape, name=name,
                             compiler_params=_cp("arbitrary"))(*operands)
    else:
        spec = pltpu.PrefetchScalarGridSpec(num_scalar_prefetch=1, grid=grid, in_specs=in_specs, out_specs=out_specs)
        res = pl.pallas_call(body, grid_spec=spec, out_shape=out_shape, name=name,
                             compiler_params=_cp("arbitrary"))(sidx, *operands)
    return res[0] if len(res) == 1 else tuple(res)


TOKEN = (8, LANES)


def mm(name, dims, grid, a, a_bs, a_im, b, b_bs, b_im, o_bs, o_im, out, scale=1.0, res=None, dep=None):
    nk = grid[2]
    acc_shape = tuple(d for d in o_bs if d is not None)

    def body(*refs):
        a_ref, b_ref = refs[0], refs[1]
        pos = 2
        res_ref = None
        if res is not None:
            res_ref = refs[pos]
            pos += 1
        if dep is not None:
            pos += 1
        o_ref = refs[pos]
        part = _bdot(a_ref[...], b_ref[...], dims)

        def finish(acc):
            v = acc * scale if scale != 1.0 else acc
            if res_ref is not None:
                v = v + _f(res_ref[...])
            o_ref[...] = v.astype(o_ref.dtype)

        if nk == 1:
            finish(part)
        else:
            acc_ref = refs[pos + 1]
            k = pl.program_id(2)

            @pl.when(k == 0)
            def _():
                acc_ref[...] = part

            @pl.when(k > 0)
            def _():
                acc_ref[...] += part

            @pl.when(k == nk - 1)
            def _():
                finish(acc_ref[...])

    operands = [a, b]
    in_specs = [pl.BlockSpec(a_bs, a_im), pl.BlockSpec(b_bs, b_im)]
    if res is not None:
        operands.append(res)
        in_specs.append(pl.BlockSpec(o_bs, o_im))
    if dep is not None:
        operands.append(dep)
        in_specs.append(pl.BlockSpec(TOKEN, lambda m, n, k: (0, 0)))
    scratch = [pltpu.VMEM(acc_shape, F32)] if nk > 1 else []
    return pl.pallas_call(body, grid=grid, in_specs=in_specs, out_specs=pl.BlockSpec(o_bs, o_im), out_shape=out,
                          scratch_shapes=scratch, name=name,
                          compiler_params=_cp("parallel", "parallel", "arbitrary"))(*operands)


def _mk(m, n, k):
    return (m, k)


def _mn(m, n, k):
    return (m, n)


def _km(m, n, k):
    return (k, m)


def _kn(m, n, k):
    return (k, n)


def _nk(m, n, k):
    return (n, k)


def mm_nn(name, a, b, out_dtype, tn, tk=None, scale=1.0, res=None):
    T, K = a.shape
    N = b.shape[1]
    tm, tk, tn = min(T, 1024), K if tk is None else min(tk, K), min(tn, N)
    return mm(name, NN, (T // tm, N // tn, K // tk), a, (tm, tk), _mk, b, (tk, tn), _kn, (tm, tn), _mn,
              _sds((T, N), out_dtype), scale=scale, res=res)


def mm_nt(name, a, b, out_dtype, tn, tk=None, scale=1.0, res=None, dep=None):
    T, K = a.shape
    N = b.shape[0]
    tm, tk, tn = min(T, 1024), K if tk is None else min(tk, K), min(tn, N)
    return mm(name, NT, (T // tm, N // tn, K // tk), a, (tm, tk), _mk, b, (tn, tk), _nk, (tm, tn), _mn,
              _sds((T, N), out_dtype), scale=scale, res=res, dep=dep)


def mm_tn(name, a, b, out_dtype, tm, tn, scale=1.0, dep=None):
    T, M = a.shape
    N = b.shape[1]
    tk, tm, tn = T, min(tm, M), min(tn, N)
    return mm(name, TN, (M // tm, N // tn, T // tk), a, (tk, tm), _km, b, (tk, tn), _kn, (tm, tn), _mn,
              _sds((M, N), out_dtype), scale=scale, dep=dep)


HBM_SPEC = pl.BlockSpec(memory_space=pl.ANY)


def _place():
    x, y, c = lax.axis_index("x"), lax.axis_index("y"), lax.axis_index("c")
    chips = [(1 - x, y), (x, 1 - y), (1 - x, 1 - y)]
    return x, y, c, chips


def allgather8(v, name):
    m_per, n = v.shape

    def body(x_ref, out_ref, send_sems, recv_sems, local_sem):
        x, y, c, chips = _place()
        me, sibling = (x, y, c), (x, y, 1 - c)

        def rows(px, py, pc):
            return out_ref.at[pl.ds((4 * px + 2 * py + pc) * m_per, m_per), :]

        def copy(k, block, to, src=None):
            return pltpu.make_async_remote_copy(
                src_ref=rows(*block) if src is None else src, dst_ref=rows(*block),
                send_sem=send_sems.at[k], recv_sem=recv_sems.at[k], device_id=to, device_id_type=MESH)

        mine = pltpu.make_async_copy(x_ref, rows(*me), local_sem)
        mine.start()
        first = [copy(0, me, sibling, src=x_ref)]
        first += [copy(1 + j, me, (*chip, c), src=x_ref) for j, chip in enumerate(chips)]
        for cp in first:
            cp.start()
        passed = [copy(4 + j, (*chip, c), sibling) for j, chip in enumerate(chips)]
        for j, chip in enumerate(chips):
            copy(1 + j, (*chip, c), me).wait_recv()
            passed[j].start()
        copy(0, sibling, me).wait_recv()
        for j, chip in enumerate(chips):
            copy(4 + j, (*chip, 1 - c), me).wait_recv()
        for cp in first + passed:
            cp.wait_send()
        mine.wait()

    return pl.pallas_call(
        body, out_shape=_sds((8 * m_per, n), v.dtype), in_specs=[pl.BlockSpec(memory_space=pltpu.VMEM)],
        out_specs=pl.BlockSpec(memory_space=pltpu.VMEM), name=name,
        scratch_shapes=[pltpu.SemaphoreType.DMA((7,)), pltpu.SemaphoreType.DMA((7,)), pltpu.SemaphoreType.DMA],
    )(v)


D2D_PIECES = 4


def _pieces(ref, n):
    rows = ref.shape[0] // n
    return [ref.at[pl.ds(q * rows, rows)] for q in range(n)]


def _start_in_pieces(src, dst, ssem, rsem, to, n):
    for s_q, d_q in zip(_pieces(src, n), _pieces(dst, n)):
        pltpu.make_async_remote_copy(src_ref=s_q, dst_ref=d_q, send_sem=ssem, recv_sem=rsem, device_id=to,
                                     device_id_type=MESH).start()


def _whole(src, dst, ssem, rsem, to):
    return pltpu.make_async_remote_copy(src_ref=src, dst_ref=dst, send_sem=ssem, recv_sem=rsem, device_id=to,
                                        device_id_type=MESH)


HBM_ONLY = pl.BlockSpec(memory_space=pltpu.HBM)
SEM_SPEC = pl.BlockSpec(memory_space=pltpu.SEMAPHORE)
DATAFLOW = pltpu.SideEffectType.DATAFLOW_SIDE_EFFECTING


def _gather_copies(bufs):
    x, y, c, chips = _place()
    out = []
    for o_ref in bufs:
        hr = o_ref.shape[1] // 2

        def half(chip, o_ref=o_ref, hr=hr):
            return o_ref.at[2 * chip[0] + chip[1], pl.ds(c * hr, hr)]

        out += [(half((x, y)), half((x, y)), (*chip, c), half(chip)) for chip in chips]
    return out


def _scatter_copies(arrays):
    x, y, c, chips = _place()
    n = len(arrays) // 2
    out = []
    for h_ref, got_ref in zip(arrays[:n], arrays[n:]):
        out += [(h_ref.at[2 * chip[0] + chip[1]], got_ref.at[j], (*chip, c), got_ref.at[j])
                for j, chip in enumerate(chips)]
    return out


def split_start(name, arrays, copies_of, n_copies, after):
    n = len(arrays)

    def body(*refs):
        ssem, rsem = refs[n + 1], refs[n + 2]
        for i, (src, dst, to, _) in enumerate(copies_of(refs[n + 3:2 * n + 3])):
            _whole(src, dst, ssem.at[i], rsem.at[i], to).start()
        refs[2 * n + 3][...] = jnp.zeros(TOKEN, F32)

    res = pl.pallas_call(
        body, name=name,
        out_shape=(pltpu.SemaphoreType.DMA((n_copies,)), pltpu.SemaphoreType.DMA((n_copies,)),
                   *[pltpu.HBM(a.shape, a.dtype) for a in arrays], _sds(TOKEN, F32)),
        in_specs=[HBM_ONLY] * n + [pl.BlockSpec(memory_space=pl.ANY)],
        out_specs=(SEM_SPEC, SEM_SPEC, *[HBM_ONLY] * n, pl.BlockSpec(memory_space=pltpu.VMEM)),
        input_output_aliases={i: 2 + i for i in range(n)},
        compiler_params=pltpu.CompilerParams(has_side_effects=DATAFLOW),
    )(*[pltpu.with_memory_space_constraint(a, pltpu.HBM) for a in arrays], after)
    return res[0], res[1], list(res[2:2 + n]), res[2 + n]


def split_wait(name, ssem, rsem, arrays, copies_of, after):
    n = len(arrays)

    def body(*refs):
        s_ref, r_ref = refs[n], refs[n + 1]
        for i, (src, _, to, mine) in enumerate(copies_of(refs[:n])):
            cp = _whole(src, mine, s_ref.at[i], r_ref.at[i], to)
            cp.wait_send()
            cp.wait_recv()

    res = pl.pallas_call(
        body, name=name, out_shape=tuple(pltpu.HBM(a.shape, a.dtype) for a in arrays),
        in_specs=[HBM_ONLY] * n + [SEM_SPEC, SEM_SPEC, pl.BlockSpec(memory_space=pl.ANY)],
        out_specs=tuple([HBM_ONLY] * n), input_output_aliases={i: i for i in range(n)},
        compiler_params=pltpu.CompilerParams(has_side_effects=DATAFLOW),
    )(*arrays, ssem, rsem, after)
    return list(res)


def gather_pass_on(bufs, name):
    n = len(bufs)

    def body(*refs):
        o_refs, ssem, rsem = refs[n:2 * n], refs[2 * n], refs[2 * n + 1]
        x, y, c, chips = _place()
        sib = (x, y, 1 - c)

        def half(o_ref, chip, h):
            hr = o_ref.shape[1] // 2
            return o_ref.at[2 * chip[0] + chip[1], pl.ds(h * hr, hr)]

        for u, o_ref in enumerate(o_refs):
            for j, chip in enumerate(chips):
                _start_in_pieces(half(o_ref, chip, c), half(o_ref, chip, c), ssem.at[3 * u + j], rsem.at[3 * u + j], sib,
                                 D2D_PIECES)
        for u, o_ref in enumerate(o_refs):
            for j, chip in enumerate(chips):
                _whole(half(o_ref, chip, c), half(o_ref, chip, 1 - c), ssem.at[3 * u + j], rsem.at[3 * u + j], sib).wait()

    res = pl.pallas_call(
        body, out_shape=tuple(_sds(b.shape, b.dtype) for b in bufs), in_specs=[HBM_SPEC] * n,
        out_specs=tuple([HBM_SPEC] * n), name=name, input_output_aliases={i: i for i in range(n)},
        scratch_shapes=[pltpu.SemaphoreType.DMA((3 * n,)), pltpu.SemaphoreType.DMA((3 * n,))],
    )(*bufs)
    return list(res)


def _sibling_copies(arrays):
    x, y, c, _ = _place()
    n = len(arrays) // 2
    out = []
    for g_ref, got_ref in zip(arrays[:n], arrays[n:]):
        out += [(g_ref.at[k, 1 - c], got_ref.at[k], (x, y, 1 - c), got_ref.at[k]) for k in range(N_CHIPS)]
    return out


def _allgather_copies(arrays):
    x, y, c, _ = _place()
    (buf,) = arrays
    mine = buf.at[4 * x + 2 * y + c]
    out = []
    for fx, fy, fc in [(0, 0, 1), (0, 1, 0), (0, 1, 1), (1, 0, 0), (1, 0, 1), (1, 1, 0), (1, 1, 1)]:
        px, py, pc = (1 - x if fx else x), (1 - y if fy else y), (1 - c if fc else c)
        out.append((mine, mine, (px, py, pc), buf.at[4 * px + 2 * py + pc]))
    return out


def _share_copies(arrays):
    x, y, c, _ = _place()
    return [(r_ref.at[c], r_ref.at[c], (x, y, 1 - c), r_ref.at[1 - c]) for r_ref in arrays]


def add_sibling(g4, got, sidx, tag):
    _, _, hr, B = g4.shape
    h = rowwise("rs_add2_" + tag, lambda a, b: _f(a) + _f(b), [("dyn4", g4, 1), got.reshape(N_CHIPS * hr, B)], [],
                [(B, BF)], tb=1024, n_rows=N_CHIPS * hr, sidx=sidx, row_period=hr)
    return h.reshape(N_CHIPS, hr, B)


def add_chips(h, got2, sidx, tag):
    _, hr, B = h.shape
    return rowwise("rs_add4_" + tag, lambda a, b, c, d: ((_f(a) + _f(b)) + _f(c)) + _f(d),
                   [("dyn", h, 0), ("leads", got2, (0,)), ("leads", got2, (1,)), ("leads", got2, (2,))], [],
                   [("dyn", 2, B, F32, 1)], tb=512, n_rows=hr, sidx=sidx)


def _rms_parts(x):
    rstd = lax.rsqrt(jnp.mean(x * x, axis=-1, keepdims=True) + EPS)
    return x * rstd, rstd


def rms_fwd(x, g, name, dep=None):
    return rowwise(name, lambda xv, gv: _rms_parts(xv)[0] * gv, [x], [g], [(x.shape[1], BF)], dep=dep)


def _rms_bwd_fn(x, dh, dres, g):
    xh, rstd = _rms_parts(x)
    dxh = _f(dh) * g
    dx = rstd * (dxh - xh * jnp.mean(dxh * xh, axis=-1, keepdims=True)) + dres
    return dx, jnp.sum(_f(dh) * xh, axis=0, keepdims=True)


def rms_bwd(x, g, dh, dres, name):
    D = x.shape[1]
    return rowwise(name, _rms_bwd_fn, [x, dh, dres], [g], [(D, F32)], [(1, D)])


def cast_unit(w, lead, sidx, name):
    R, B = w.shape[-2:]
    return rowwise(name, lambda v: v, [("leads", w, lead)], [], [("dyn", N_CHIPS, B, BF, 0)], tb=1024, n_rows=R,
                   sidx=sidx)


FFN_TM = 512


def ffn_up(h, wg, wu):
    T, D = h.shape
    fs = wg.shape[-1]
    tm = min(T, FFN_TM)

    def body(h_ref, wg_ref, wu_ref, zg_ref, zu_ref, a_ref):
        hb = h_ref[...]
        g = _bdot(hb, wg_ref[...], NN)
        u = _bdot(hb, wu_ref[...], NN)
        zg_ref[...] = g.astype(zg_ref.dtype)
        zu_ref[...] = u.astype(zu_ref.dtype)
        a_ref[...] = (_silu(g) * u).astype(a_ref.dtype)

    w_spec = pl.BlockSpec((None, D, fs), lambda n, m: (n, 0, 0))
    o_spec = pl.BlockSpec((tm, fs), lambda n, m: (m, n))
    return pl.pallas_call(
        body, grid=(N_CHIPS, T // tm), in_specs=[pl.BlockSpec((tm, D), lambda n, m: (m, 0)), w_spec, w_spec],
        out_specs=[o_spec] * 3, out_shape=[_sds((T, N_CHIPS * fs), BF)] * 3, name="ffn_up",
        compiler_params=_cp("parallel", "arbitrary"))(h, wg, wu)


def ffn_dz(dx2, wd2, zg, zu, dep):
    T, D = dx2.shape
    F = wd2.shape[0]
    fs = F // N_CHIPS
    tm = min(T, FFN_TM)

    def body(dx_ref, wd_ref, zg_ref, zu_ref, dep_ref, dzg_ref, dzu_ref):
        da = 0.5 * _bdot(dx_ref[...], wd_ref[...], NT)
        dzg, dzu = _swiglu_bwd_fn(da, zg_ref[...], zu_ref[...])
        dzg_ref[...] = dzg.astype(dzg_ref.dtype)
        dzu_ref[...] = dzu.astype(dzu_ref.dtype)

    z_spec = pl.BlockSpec((tm, fs), lambda n, m: (m, n))
    return pl.pallas_call(
        body, grid=(N_CHIPS, T // tm),
        in_specs=[pl.BlockSpec((tm, D), lambda n, m: (m, 0)), pl.BlockSpec((fs, D), lambda n, m: (n, 0)), z_spec, z_spec,
                  pl.BlockSpec(TOKEN, lambda n, m: (0, 0))],
        out_specs=[z_spec] * 2, out_shape=[_sds((T, F), BF)] * 2, name="ffn_dz",
        compiler_params=_cp("parallel", "arbitrary"))(dx2, wd2, zg, zu, dep)


def ffn_dh(dzg, dzu, wg, wu, dep):
    T, F = dzg.shape
    _, D, fs = wg.shape
    tm, tn = min(T, 1024), 1024

    def body(g_ref, u_ref, wg_ref, wu_ref, dep_ref, o_ref, acc_ref):
        k = pl.program_id(2)
        part = _bdot(g_ref[...], wg_ref[...], NT) + _bdot(u_ref[...], wu_ref[...], NT)

        @pl.when(k == 0)
        def _():
            acc_ref[...] = part

        @pl.when(k > 0)
        def _():
            acc_ref[...] += part

        @pl.when(k == N_CHIPS - 1)
        def _():
            o_ref[...] = acc_ref[...]

    z_spec = pl.BlockSpec((tm, fs), _mk)
    w_spec = pl.BlockSpec((None, tn, fs), lambda m, n, k: (k, n, 0))
    return pl.pallas_call(
        body, grid=(T // tm, D // tn, N_CHIPS),
        in_specs=[z_spec, z_spec, w_spec, w_spec, pl.BlockSpec(TOKEN, lambda m, n, k: (0, 0))],
        out_specs=pl.BlockSpec((tm, tn), _mn), out_shape=_sds((T, D), F32),
        scratch_shapes=[pltpu.VMEM((tm, tn), F32)], name="ffn_dh",
        compiler_params=_cp("parallel", "parallel", "arbitrary"))(dzg, dzu, wg, wu, dep)


def ffn_fwd(xin, gain, wg, wu, wd, dep=None):
    T, D = xin.shape
    F = N_CHIPS * wg.shape[-1]
    h = rms_fwd(xin, gain, "ffn_rms", dep)
    zg, zu, a = ffn_up(h, wg, wu)
    if callable(wd):
        wd = wd(a)
    x2 = mm_nn("ffn_down", a, wd.reshape(F, D), F32, 512, scale=0.5, res=xin)
    return x2, (xin, h, zg, zu, a)


def _swiglu_bwd_fn(da, zg, zu):
    da, zg, zu = _f(da), _f(zg), _f(zu)
    s = _sigmoid(zg)
    return da * zu * (s * (1.0 + zg * (1.0 - s))), da * (zg * s)


def ffn_bwd(dx2, saved, gain, wg, wu, wd, dep, sends):
    xin, h, zg, zu, a = saved
    T, D = xin.shape
    fs = wg.shape[-1]
    F = N_CHIPS * fs
    tn = 1024
    g_wd = mm_tn("ffn_dwd", a, dx2, BF, fs, 1024, scale=0.5, dep=dep).reshape(N_CHIPS, fs, D)
    dep = sends('ffn_w_down', g_wd, dep)
    dzg, dzu = ffn_dz(dx2, wd.reshape(F, D), zg, zu, dep)

    def dw(dz, nm):
        return mm(nm, TN, (D // tn, N_CHIPS, 1), h, (T, tn), _km, dz, (T, fs), _kn,
                  (None, tn, fs), lambda m, n, k: (n, m, 0), _sds((N_CHIPS, D, fs), BF))

    dep = sends('ffn_w_gate', dw(dzg, "ffn_dwg"), dep)
    dep = sends('ffn_w_up', dw(dzu, "ffn_dwu"), dep)
    dh = ffn_dh(dzg, dzu, wg, wu, dep)
    dx, dgain = rms_bwd(xin, gain, dh, dx2, "ffn_rms_bwd")
    return dx, dgain, dep


def _gla_chunk(q_ref, k_ref, v_ref, u_ref, b_ref, rows):
    r = lax.broadcasted_iota(jnp.int32, (CHUNK, CHUNK), 0)
    c = lax.broadcasted_iota(jnp.int32, (CHUNK, CHUNK), 1)
    causal = c <= r
    u = u_ref[rows, :] + b_ref[...]
    g = (jnp.minimum(u, 0.0) - jnp.log(1.0 + jnp.exp(-jnp.abs(u)))) * (1.0 / GLA_TAU)
    b = _dot(causal.astype(F32), g, NN, HI)
    last = lax.broadcasted_iota(jnp.int32, (CHUNK, HK), 0) == CHUNK - 1
    blast = jnp.sum(jnp.where(last, b, 0.0), axis=0, keepdims=True)
    eb = jnp.exp(b)
    qb = q_ref[rows, :] * (HK ** -0.5) * eb
    k = k_ref[rows, :]
    kb = k * jnp.exp(-b)
    kl = k * jnp.exp(blast - b)
    A = jnp.where(causal, _bdot(qb, kb, NT), 0.0)
    return causal, u, b, blast, eb, qb, kb, kl, A


def _gla_in_specs(T):
    return [pl.BlockSpec((T, HK), lambda h: (0, h)), pl.BlockSpec((T, HK), lambda h: (0, GLA_H + h)),
            pl.BlockSpec((T, HV), lambda h: (0, GLA_H + h)), pl.BlockSpec((T, HK), lambda h: (0, h)),
            pl.BlockSpec((1, HK), lambda h: (0, h))]


def gla_fwd(zmain, ug, gate_b):
    T = zmain.shape[0]
    nC = T // CHUNK

    def body(q_ref, k_ref, v_ref, u_ref, b_ref, o_ref, s_ref, st_ref):
        st_ref[...] = jnp.zeros_like(st_ref)

        def step(n, carry):
            rows = pl.ds(pl.multiple_of(n * CHUNK, CHUNK), CHUNK)
            _, _, _, blast, _, qb, _, kl, A = _gla_chunk(q_ref, k_ref, v_ref, u_ref, b_ref, rows)
            v = v_ref[rows, :]
            ST = st_ref[...]
            s_ref[n] = ST
            o_ref[rows, :] = _bdot(qb, ST, NT) + _bdot(A, v, NN)
            st_ref[...] = ST * jnp.exp(blast) + _bdot(v, kl, TN)
            return carry

        lax.fori_loop(0, nC, step, 0)

    return pl.pallas_call(
        body, grid=(GLA_H,), in_specs=_gla_in_specs(T),
        out_specs=[pl.BlockSpec((T, HV), lambda h: (0, h)), pl.BlockSpec((nC, None, HV, HK), lambda h: (0, h, 0, 0))],
        out_shape=[_sds((T, GLA_H * HV), F32), _sds((nC, GLA_H, HV, HK), F32)],
        scratch_shapes=[pltpu.VMEM((HV, HK), F32)], name="gla_fwd", compiler_params=_cp("parallel"),
    )(zmain, zmain, zmain, ug, gate_b)


def gla_bwd(zmain, ug, gate_b, S, do):
    T = zmain.shape[0]
    nC = T // CHUNK

    def body(q_ref, k_ref, v_ref, u_ref, b_ref, s_ref, do_ref, dq_ref, dk_ref, dv_ref, du_ref, dgb_ref, dst_ref):
        dst_ref[...] = jnp.zeros_like(dst_ref)
        dgb_ref[...] = jnp.zeros_like(dgb_ref)

        def step(it, carry):
            n = nC - 1 - it
            rows = pl.ds(pl.multiple_of(n * CHUNK, CHUNK), CHUNK)
            causal, u, b, blast, eb, qb, kb, kl, A = _gla_chunk(q_ref, k_ref, v_ref, u_ref, b_ref, rows)
            v = v_ref[rows, :]
            dout = do_ref[rows, :]
            ST = s_ref[n]
            dST = dst_ref[...]
            elast = jnp.exp(blast)
            dA = jnp.where(causal, _bdot(dout, v, NT), 0.0)
            dv_ref[rows, :] = (_bdot(A, dout, TN) + _bdot(kl, dST, NT)).astype(dv_ref.dtype)
            dqb = _bdot(dout, ST, NN) + _bdot(dA, kb, NN)
            dkb = _bdot(dA, qb, TN)
            dkl = _bdot(v, dST, NN)
            ddec = jnp.sum(ST * dST, axis=0, keepdims=True)
            dst_ref[...] = dST * elast + _bdot(dout, qb, TN)
            dq_ref[rows, :] = (dqb * eb * (HK ** -0.5)).astype(dq_ref.dtype)
            dk_ref[rows, :] = (dkb * jnp.exp(-b) + dkl * jnp.exp(blast - b)).astype(dk_ref.dtype)
            db = dqb * qb - dkb * kb - dkl * kl
            dbl = jnp.sum(dkl * kl, axis=0, keepdims=True) + elast * ddec
            dg = _dot(jnp.logical_not(causal).astype(F32) + jnp.where(
                lax.broadcasted_iota(jnp.int32, (CHUNK, CHUNK), 0) == lax.broadcasted_iota(jnp.int32, (CHUNK, CHUNK), 1),
                1.0, 0.0), db, NN, HI) + dbl
            du = dg * (1.0 / GLA_TAU) / (1.0 + jnp.exp(u))
            du_ref[rows, :] = du
            dgb_ref[...] += jnp.sum(du, axis=0, keepdims=True)
            return carry

        lax.fori_loop(0, nC, step, 0)

    specs = _gla_in_specs(T) + [pl.BlockSpec((nC, None, HV, HK), lambda h: (0, h, 0, 0)),
                                pl.BlockSpec((T, HV), lambda h: (0, h))]
    return pl.pallas_call(
        body, grid=(GLA_H,), in_specs=specs,
        out_specs=[pl.BlockSpec((T, HK), lambda h: (0, h)), pl.BlockSpec((T, HK), lambda h: (0, h)),
                   pl.BlockSpec((T, HV), lambda h: (0, h)), pl.BlockSpec((T, HK), lambda h: (0, h)),
                   pl.BlockSpec((1, HK), lambda h: (0, h))],
        out_shape=[_sds((T, GLA_H * HK), BF), _sds((T, GLA_H * HK), BF), _sds((T, GLA_H * HV), BF),
                   _sds((T, GLA_H * HK), F32), _sds((1, GLA_H * HK), F32)],
        scratch_shapes=[pltpu.VMEM((HV, HK), F32)], name="gla_bwd", compiler_params=_cp("parallel"),
    )(zmain, zmain, zmain, ug, gate_b, S, do)


def _gla_post_fn(o, r, g):
    outs = []
    for h in range(GLA_H):
        on = _rms_parts(o[:, h * HV:(h + 1) * HV])[0] * g
        outs.append(on * _silu(r[:, h * HV:(h + 1) * HV]))
    return jnp.concatenate(outs, axis=1)


def _gla_post_bwd_fn(da, o, r, g):
    dos, drs = [], []
    dg = jnp.zeros((1, HV), F32)
    for h in range(GLA_H):
        sl = slice(h * HV, (h + 1) * HV)
        xh, rstd = _rms_parts(o[:, sl])
        drs.append(da[:, sl] * xh * g * _dsilu(r[:, sl]))
        don = da[:, sl] * _silu(r[:, sl])
        dg = dg + jnp.sum(don * xh, axis=0, keepdims=True)
        dxh = don * g
        dos.append(rstd * (dxh - xh * jnp.mean(dxh * xh, axis=-1, keepdims=True)))
    return jnp.concatenate(dos, axis=1), jnp.concatenate(drs, axis=1), dg


def conv_fwd(u, dw, dwb):
    T, C = u.shape
    TB = min(T, 256)

    def body(u_ref, w_ref, b_ref, y_ref, pad_ref):
        pad_ref[0:CONV_PAD, :] = jnp.zeros((CONV_PAD, LANES), F32)
        pad_ref[CONV_PAD:CONV_PAD + T, :] = u_ref[...]
        off = CONV_PAD - (CONV_W - 1)
        for t0 in range(0, T, TB):
            acc = jnp.zeros((TB, LANES), F32) + b_ref[...]
            for j in range(CONV_W):
                acc = acc + w_ref[j:j + 1, :] * pad_ref[t0 + off + j:t0 + off + j + TB, :]
            y_ref[t0:t0 + TB, :] = acc

    col = lambda i: (0, i)
    return pl.pallas_call(
        body, grid=(C // LANES,),
        in_specs=[pl.BlockSpec((T, LANES), col), pl.BlockSpec((CONV_W, LANES), col), pl.BlockSpec((1, LANES), col)],
        out_specs=pl.BlockSpec((T, LANES), col), out_shape=_sds((T, C), F32),
        scratch_shapes=[pltpu.VMEM((T + CONV_PAD, LANES), F32)], name="conv_fwd", compiler_params=_cp("parallel"),
    )(u, dw, dwb)


def conv_bwd(dy, u, dw):
    T, C = u.shape
    TB = min(T, 256)

    def body(dy_ref, u_ref, w_ref, du_ref, dw_ref, db_ref, upad, dypad):
        upad[0:CONV_PAD, :] = jnp.zeros((CONV_PAD, LANES), F32)
        upad[CONV_PAD:CONV_PAD + T, :] = u_ref[...]
        dypad[0:T, :] = dy_ref[...]
        dypad[T:T + CONV_PAD, :] = jnp.zeros((CONV_PAD, LANES), F32)
        off = CONV_PAD - (CONV_W - 1)
        for t0 in range(0, T, TB):
            acc = jnp.zeros((TB, LANES), F32)
            for j in range(CONV_W):
                s = t0 + (CONV_W - 1) - j
                acc = acc + w_ref[j:j + 1, :] * dypad[s:s + TB, :]
            du_ref[t0:t0 + TB, :] = acc
        for j in range(CONV_W):
            acc = jnp.zeros((TB, LANES), F32)
            for t0 in range(0, T, TB):
                acc = acc + dy_ref[t0:t0 + TB, :] * upad[t0 + off + j:t0 + off + j + TB, :]
            dw_ref[j:j + 1, :] = jnp.sum(acc, axis=0, keepdims=True)
        db_ref[...] = jnp.sum(dy_ref[...], axis=0, keepdims=True)

    col = lambda i: (0, i)
    return pl.pallas_call(
        body, grid=(C // LANES,),
        in_specs=[pl.BlockSpec((T, LANES), col), pl.BlockSpec((T, LANES), col), pl.BlockSpec((CONV_W, LANES), col)],
        out_specs=[pl.BlockSpec((T, LANES), col), pl.BlockSpec((CONV_W, LANES), col), pl.BlockSpec((1, LANES), col)],
        out_shape=[_sds((T, C), F32), _sds((CONV_W, C), F32), _sds((1, C), F32)],
        scratch_shapes=[pltpu.VMEM((T + CONV_PAD, LANES), F32), pltpu.VMEM((T + CONV_PAD, LANES), F32)],
        name="conv_bwd", compiler_params=_cp("parallel"),
    )(dy, u, dw)


def _ln_parts(x):
    mu = jnp.mean(x, axis=-1, keepdims=True)
    xc = x - mu
    rstd = lax.rsqrt(jnp.mean(xc * xc, axis=-1, keepdims=True) + EPS)
    return xc * rstd, rstd


def _ln_silu_fn(x, g, b):
    return _silu(_ln_parts(x)[0] * g + b)


def _ln_silu_bwd_fn(dbo, x, g, b):
    xh, rstd = _ln_parts(x)
    dy = dbo * _dsilu(xh * g + b)
    dyg = dy * g
    dx = rstd * (dyg - jnp.mean(dyg, axis=-1, keepdims=True) - xh * jnp.mean(dyg * xh, axis=-1, keepdims=True))
    return dx, jnp.sum(dy * xh, axis=0, keepdims=True), jnp.sum(dy, axis=0, keepdims=True)


def _glu_bwd_fn(du, ca, cb):
    s = _sigmoid(cb)
    return du * s, du * ca * s * (1.0 - s)


DIAGS = QB + KW


def _onehot_diag():
    j = lax.broadcasted_iota(jnp.int32, (REL_PAD, DIAGS), 1)
    i = lax.broadcasted_iota(jnp.int32, (REL_PAD, DIAGS), 0)
    return (i == jnp.clip(KW - j, -REL_CLIP, REL_CLIP) + REL_CLIP).astype(F32)


def relbias_tile(rbp):
    def body(rb_ref, o_ref, e_ref):
        e_ref[...] = _dot(rb_ref[...], _onehot_diag(), NN, HI)
        tc = lax.shift_right_logical(lax.broadcasted_iota(jnp.int32, (QB, KW), 0), 6)
        wc = lax.shift_right_logical(lax.broadcasted_iota(jnp.int32, (QB, KW), 1), 6)
        ok = jnp.logical_and(wc >= tc, wc <= tc + LEFT_CHUNKS)
        for h in range(ATT_H):
            spread = pltpu.roll(jnp.broadcast_to(e_ref[h:h + 1, :], (QB, DIAGS)), KW, 1, stride=1, stride_axis=0)
            o_ref[h] = jnp.where(ok, spread[:, :KW], NEG)

    return pl.pallas_call(body, out_shape=_sds((ATT_H, QB, KW), F32), name="relbias_tile",
                          scratch_shapes=[pltpu.VMEM((ATT_H, DIAGS), F32)], compiler_params=_cp())(rbp)


def relbias_reduce(dbm):
    def body(d_ref, o_ref, e_ref):
        u = lax.broadcasted_iota(jnp.int32, (QB, QB), 0)
        t = lax.broadcasted_iota(jnp.int32, (QB, QB), 1)
        flip = (u + t == QB - 1).astype(F32)
        for h in range(ATT_H):
            padded = jnp.concatenate([d_ref[h], jnp.zeros((QB, QB), F32)], axis=1)
            lined = pltpu.roll(_dot(flip, padded, NN, HI), 1, 1, stride=1, stride_axis=0)
            e_ref[h:h + 1, :] = jnp.sum(lined, axis=0, keepdims=True)
        o_ref[...] = _dot(e_ref[...], _onehot_diag(), NT, HI)

    return pl.pallas_call(body, out_shape=_sds((ATT_H, REL_PAD), F32), name="relbias_reduce",
                          scratch_shapes=[pltpu.VMEM((ATT_H, DIAGS), F32)], compiler_params=_cp())(dbm)


def _att_scores(q_ref, kp_ref, bm_ref, i):
    q0 = pl.multiple_of(i * QB, QB)
    kw = kp_ref[pl.ds(q0, KW), :]
    s = _bdot(q_ref[...], kw, NT) * (HD ** -0.5) + bm_ref[...]
    w = lax.broadcasted_iota(jnp.int32, (QB, KW), 1)
    return jnp.where(w + q0 >= PADK, s, NEG), kw, q0


def attn_fwd(qkv, kvp, bm):
    T = qkv.shape[0]
    D = ATT_H * HD

    def body(q_ref, kp_ref, vp_ref, bm_ref, o_ref, lse_ref):
        s, _, q0 = _att_scores(q_ref, kp_ref, bm_ref, pl.program_id(1))
        m = jnp.max(s, axis=-1, keepdims=True)
        e = jnp.exp(s - m)
        l = jnp.sum(e, axis=-1, keepdims=True)
        o_ref[...] = _bdot(e * (1.0 / l), vp_ref[pl.ds(q0, KW), :], NN).astype(o_ref.dtype)
        lse_ref[...] = m + jnp.log(l)

    return pl.pallas_call(
        body, grid=(ATT_H, T // QB),
        in_specs=[pl.BlockSpec((QB, HD), lambda h, i: (i, h)), pl.BlockSpec((T + PADK, HD), lambda h, i: (0, h)),
                  pl.BlockSpec((T + PADK, HD), lambda h, i: (0, ATT_H + h)),
                  pl.BlockSpec((None, QB, KW), lambda h, i: (h, 0, 0))],
        out_specs=[pl.BlockSpec((QB, HD), lambda h, i: (i, h)), pl.BlockSpec((None, QB, 1), lambda h, i: (h, i, 0))],
        out_shape=[_sds((T, D), BF), _sds((ATT_H, T, 1), F32)], name="attn_fwd",
        compiler_params=_cp("parallel", "arbitrary"),
    )(qkv, kvp, kvp, bm)


def attn_bwd(qkv, kvp, bm, o, lse, do):
    T = qkv.shape[0]
    D = ATT_H * HD

    def body(q_ref, kp_ref, vp_ref, bm_ref, o_ref, lse_ref, do_ref, dq_ref, dkp_ref, dvp_ref, dbm_ref):
        i = pl.program_id(1)

        @pl.when(i == 0)
        def _():
            dkp_ref[...] = jnp.zeros_like(dkp_ref)
            dvp_ref[...] = jnp.zeros_like(dvp_ref)
            dbm_ref[...] = jnp.zeros_like(dbm_ref)

        s, kw, q0 = _att_scores(q_ref, kp_ref, bm_ref, i)
        p = jnp.exp(s - lse_ref[...])
        dout = do_ref[...]
        dp = _bdot(dout, vp_ref[pl.ds(q0, KW), :], NT)
        delta = jnp.sum(_f(dout) * _f(o_ref[...]), axis=-1, keepdims=True)
        ds = p * (dp - delta)
        dq_ref[...] = (_bdot(ds, kw, NN) * (HD ** -0.5)).astype(dq_ref.dtype)
        dkp_ref[pl.ds(q0, KW), :] += _bdot(ds, q_ref[...], TN) * (HD ** -0.5)
        dvp_ref[pl.ds(q0, KW), :] += _bdot(p, dout, TN)
        dbm_ref[...] += ds

    qspec = pl.BlockSpec((QB, HD), lambda h, i: (i, h))
    kspec = pl.BlockSpec((T + PADK, HD), lambda h, i: (0, h))
    bspec = pl.BlockSpec((None, QB, KW), lambda h, i: (h, 0, 0))
    return pl.pallas_call(
        body, grid=(ATT_H, T // QB),
        in_specs=[qspec, kspec, pl.BlockSpec((T + PADK, HD), lambda h, i: (0, ATT_H + h)), bspec, qspec,
                  pl.BlockSpec((None, QB, 1), lambda h, i: (h, i, 0)), qspec],
        out_specs=[qspec, kspec, kspec, bspec],
        out_shape=[_sds((T, D), BF), _sds((T + PADK, D), F32), _sds((T + PADK, D), F32), _sds((ATT_H, QB, KW), F32)],
        name="attn_bwd", compiler_params=_cp("parallel", "arbitrary"),
    )(qkv, kvp, kvp, bm, o, lse, do)


def _final_fn(x, tgt, g):
    D = x.shape[1]
    xh, rstd = _rms_parts(x)
    diff = xh * g - tgt
    dy = diff * (1.0 / D)
    dxh = dy * g
    dx = rstd * (dxh - xh * jnp.mean(dxh * xh, axis=-1, keepdims=True))
    loss = jnp.sum(jnp.sum(diff * diff, axis=-1, keepdims=True), axis=0, keepdims=True) * (0.5 / D)
    return dx, jnp.sum(dy * xh, axis=0, keepdims=True), jnp.broadcast_to(loss, (1, LANES))


def _adamw_fn(w, g, m, v):
    m = ADAM_B1 * m + (1.0 - ADAM_B1) * g
    v = ADAM_B2 * v + (1.0 - ADAM_B2) * (g * g)
    m_hat = m / (1.0 - ADAM_B1 ** ADAM_STEP)
    v_hat = v / (1.0 - ADAM_B2 ** ADAM_STEP)
    delta = -ADAM_LR * (m_hat / (jnp.sqrt(v_hat) + ADAM_EPS) + ADAM_WD * w)
    return g, delta, m, v


def adamw(w, g, m, v, name):
    shape = w.shape
    C = shape[-1]
    R = w.size // C
    outs = rowwise(name, _adamw_fn, [t.reshape(R, C) for t in (w, g, m, v)], [], [(C, F32)] * 4)
    return tuple(t.reshape(shape) for t in outs)


def adamw_unit(w, m, v, g, lead, outs, name):
    R, B = w.shape[-2:]
    tb = _row_block(R, 256)
    if outs is None:
        outs = [lax.empty(w.shape, F32) for _ in range(4)]

    def body(w_ref, m_ref, v_ref, g_ref, *rest):
        for ref, val in zip(rest[4:], _adamw_fn(w_ref[...], g_ref[...], m_ref[...], v_ref[...])):
            ref[...] = val

    native = pl.BlockSpec((None,) * len(lead) + (tb, B), lambda r: tuple(lead) + (r, 0))
    return pl.pallas_call(
        body, grid=(R // tb,), in_specs=[native] * 3 + [pl.BlockSpec((tb, B), lambda r: (r, 0))] + [HBM_SPEC] * 4,
        out_specs=[native] * 4, out_shape=[_sds(w.shape, F32)] * 4, input_output_aliases={4 + i: i for i in range(4)},
        name=name, compiler_params=_cp("arbitrary"),
    )(w, m, v, g, *outs)


WEIGHTS = ['ffn_norm', 'ffn_w_gate', 'ffn_w_up', 'ffn_w_down', 'mix_norm', 'ab_w_in', 'gla_gate_w', 'gla_gate_b',
           'gla_norm_g', 'conv_dw', 'conv_dw_b', 'conv_ln_g', 'conv_ln_b', 'ab_w_out', 'att_w_qkv', 'att_rel_bias',
           'att_w_o', 'pl_norm', 'pl_w_gate', 'pl_w_proj', 'final_norm']
BIG = ['ffn_w_gate', 'ffn_w_up', 'ffn_w_down', 'ab_w_in', 'ab_w_out', 'att_w_qkv', 'att_w_o', 'pl_w_gate', 'pl_w_proj']


def _pack(parts, rows):
    flat = jnp.concatenate([p.reshape(-1) for p in parts])
    return jnp.pad(flat, (0, rows * LANES - flat.shape[0])).reshape(rows, LANES)


def _unpack(flat, shapes):
    out, pos = [], 0
    for s in shapes:
        n = 1
        for d in s:
            n *= d
        out.append(flat[pos:pos + n].reshape(s))
        pos += n
    return out


def _rows_for(shapes):
    n = sum(math.prod(s) for s in shapes)
    return -(-n // (8 * LANES)) * 8


def kernel(x, p, ffn_norm, ffn_w_gate, ffn_w_up, ffn_w_down, mix_norm, ab_w_in, gla_gate_w, gla_gate_b, gla_norm_g, conv_dw, conv_dw_b, conv_ln_g, conv_ln_b, ab_w_out, att_w_qkv, att_rel_bias, att_w_o, pl_norm, pl_w_gate, pl_w_proj, final_norm, loss_target, m_ffn_norm, m_ffn_w_gate, m_ffn_w_up, m_ffn_w_down, m_mix_norm, m_ab_w_in, m_gla_gate_w, m_gla_gate_b, m_gla_norm_g, m_conv_dw, m_conv_dw_b, m_conv_ln_g, m_conv_ln_b, m_ab_w_out, m_att_w_qkv, m_att_rel_bias, m_att_w_o, m_pl_norm, m_pl_w_gate, m_pl_w_proj, m_final_norm, v_ffn_norm, v_ffn_w_gate, v_ffn_w_up, v_ffn_w_down, v_mix_norm, v_ab_w_in, v_gla_gate_w, v_gla_gate_b, v_gla_norm_g, v_conv_dw, v_conv_dw_b, v_conv_ln_g, v_conv_ln_b, v_ab_w_out, v_att_w_qkv, v_att_rel_bias, v_att_w_o, v_pl_norm, v_pl_w_gate, v_pl_w_proj, v_final_norm):
    env = dict(locals())
    W = {n: env[n] for n in WEIGHTS}
    M = {n: env["m_" + n] for n in WEIGHTS}
    V = {n: env["v_" + n] for n in WEIGHTS}
    xc, yc_, cc = lax.axis_index("x"), lax.axis_index("y"), lax.axis_index("c")

    x0 = x[0]
    tgt = loss_target[0]
    T, D = x0.shape
    fs = ffn_w_gate.shape[-1]
    ws = ab_w_in.shape[-1]
    AB_IN = N_CHIPS * ws
    gz0 = 2 * GLA_H * HK + 2 * GLA_H * HV
    sidx = jnp.stack([2 * xc + yc_, cc, 4 * xc + 2 * yc_ + cc]).astype(jnp.int32)

    def ffn_keys(i, j):
        return [('ffn_w_gate', (i, j)), ('ffn_w_up', (i, j)), ('ffn_w_down', (i, j))]

    later = [[('ab_w_in', (0,)), ('ab_w_out', (0,))], ffn_keys(0, 1), [('pl_w_gate', (0,)), ('pl_w_proj', (0,))],
             ffn_keys(1, 0), [('att_w_qkv', (0,)), ('att_w_o', (0,))], ffn_keys(1, 1),
             [('pl_w_gate', (1,)), ('pl_w_proj', (1,))]]
    arrivals = [ffn_keys(0, 0)[:2], ffn_keys(0, 0)[2:]] + later
    order = [[k] for k in ffn_keys(0, 0)] + later
    full, in_flight = {}, []
    tok = jnp.zeros(TOKEN, F32)
    for s, keys in enumerate(arrivals):
        parts = [cast_unit(W[n], lead, sidx, "cast_" + n + "".join(str(i) for i in lead)) for n, lead in keys]
        ssem, rsem, thru, tok = split_start("gather_start_%d" % s, parts, _gather_copies, 3 * len(keys), tok)
        in_flight.append((ssem, rsem, thru))

    def arrive(s, after):
        ssem, rsem, thru = in_flight[s]
        landed = split_wait("gather_wait_%d" % s, ssem, rsem, thru, _gather_copies, after)
        for key, buf in zip(arrivals[s], gather_pass_on(landed, "gather_pass_on_%d" % s)):
            full[key] = buf

    small_sharded = [ffn_norm, gla_gate_w, conv_dw]
    rows_s = _rows_for([t.shape for t in small_sharded])
    got = allgather8(_pack(small_sharded, rows_s), "gather_small").reshape(N_CHIPS, 2, rows_s * LANES)[:, 0]
    per_chip = [_unpack(got[k], [t.shape for t in small_sharded]) for k in range(N_CHIPS)]
    ffn_norm_f, gate_w_f, conv_dw_f = [jnp.concatenate([per_chip[k][t] for k in range(N_CHIPS)], axis=-1)
                                       for t in range(3)]
    gate_w_p = jnp.pad(gate_w_f[0], ((0, LANES - 16), (0, 0)))
    conv_w = conv_dw_f[0]
    rb_p = jnp.pad(att_rel_bias[0], ((0, 0), (0, REL_PAD - att_rel_bias.shape[-1])))

    G = {}
    small_g = {}

    def ffn_w(i, j):
        return full['ffn_w_gate', (i, j)], full['ffn_w_up', (i, j)], full['ffn_w_down', (i, j)]

    saved = {}
    xs = x0
    arrive(0, tok)

    def first_down(after):
        arrive(1, after)
        return full['ffn_w_down', (0, 0)]

    xs, saved['f00'] = ffn_fwd(xs, ffn_norm_f[0, 0][None], full['ffn_w_gate', (0, 0)], full['ffn_w_up', (0, 0)],
                               first_down, dep=tok)
    arrive(2, xs)
    w_in = jnp.transpose(full['ab_w_in', (0,)], (1, 0, 2)).reshape(D, AB_IN)
    w_main = jnp.concatenate([w_in[:, :gz0], w_in[:, gz0 + 16:]], axis=1)
    w_gz = jnp.pad(w_in[:, gz0:gz0 + 16], ((0, 0), (0, LANES - 16)))
    w_out = full['ab_w_out', (0,)].reshape(D, D)

    def mixer0_fwd(xin):
        h = rms_fwd(xin, mix_norm[0][None], "mix0_rms")
        zmain = mm_nn("ab_in", h, w_main, F32, 1024)
        gzp = mm_nn("ab_gz", h, w_gz, F32, LANES)
        ug = mm_nn("gla_gate", gzp, gate_w_p, F32, GLA_H * HK)
        o, S = gla_fwd(zmain, ug, gla_gate_b)
        a_out = rowwise("gla_post", _gla_post_fn, [o, ("cols", zmain, GLA_H * HV, 2)], [gla_norm_g], [(GLA_H * HV, BF)])
        u = rowwise("conv_glu", lambda a, b: a * _sigmoid(b), [("cols", zmain, 1024, 3), ("cols", zmain, 1024, 4)], [],
                    [(1024, F32)])
        yc = conv_fwd(u, conv_w, conv_dw_b)
        b_out = rowwise("conv_ln", _ln_silu_fn, [yc], [conv_ln_g, conv_ln_b], [(1024, BF)])
        cat = jnp.concatenate([a_out, b_out], axis=1)
        x2 = mm_nn("ab_out", cat, w_out, F32, 1024, res=xin)
        return x2, (xin, h, zmain, gzp, ug, o, S, u, yc, cat)

    xs, saved['m0'] = mixer0_fwd(xs)
    arrive(3, xs)
    xs, saved['f01'] = ffn_fwd(xs, ffn_norm_f[0, 1][None], *ffn_w(0, 1))
    ks = D // N_CHIPS
    dp = p.shape[-1]

    def pl_fwd(xin, i):
        h = rms_fwd(xin, pl_norm[i][None], "pl_rms")
        tm = min(T, 1024)
        u = mm_nn("pl_gate", h, full['pl_w_gate', (i,)].reshape(D, D), F32, 1024)
        e = mm("pl_proj", NN, (T // tm, N_CHIPS, 1), p[i, 0], (tm, dp), _mk, full['pl_w_proj', (i,)],
               (None, dp, ks), lambda m, n, k: (n, 0, 0), (tm, ks), _mn, _sds((T, D), F32))
        x2 = rowwise("pl_mix", lambda xv, uv, ev: xv + _sigmoid(uv) * ev, [xin, u, e], [], [(D, F32)])
        return x2, (xin, h, u, e)

    arrive(4, xs)
    xs, saved['p0'] = pl_fwd(xs, 0)
    arrive(5, xs)
    xs, saved['f10'] = ffn_fwd(xs, ffn_norm_f[1, 0][None], *ffn_w(1, 0))
    arrive(6, xs)
    w_qkv = full['att_w_qkv', (0,)]
    w_o = full['att_w_o', (0,)].reshape(D, D)
    qs = w_qkv.shape[-1]

    bm = relbias_tile(rb_p)

    def mixer1_fwd(xin):
        h = rms_fwd(xin, mix_norm[1][None], "mix1_rms")
        tm, tn = min(T, 1024), 512
        per = qs // tn
        qkv = mm("att_qkv", NN, (T // tm, 3 * D // tn, 1), h, (tm, D), _mk, w_qkv, (None, D, tn),
                 lambda m, n, k: (n // per, k, n % per), (tm, tn), _mn, _sds((T, 3 * D), BF))
        kvp = jnp.pad(qkv[:, D:], ((PADK, 0), (0, 0)))
        o, lse = attn_fwd(qkv, kvp, bm)
        x2 = mm_nn("att_o", o, w_o, F32, 1024, res=xin)
        return x2, (xin, h, qkv, kvp, o, lse)

    xs, saved['m1'] = mixer1_fwd(xs)
    arrive(7, xs)
    xs, saved['f11'] = ffn_fwd(xs, ffn_norm_f[1, 1][None], *ffn_w(1, 1))
    arrive(8, xs)
    xs, saved['p1'] = pl_fwd(xs, 1)

    dx, small_g['final_norm'], loss_acc = rowwise("loss_head", _final_fn, [xs, tgt], [final_norm[None]], [(D, F32)],
                                                  [(1, D), (1, LANES)])
    loss = lax.psum(loss_acc[0, 0], ("x", "y", "c"))

    def pl_bwd(dx2, sv, i, dep):
        xin, h, u, e = sv
        tm = min(T, 1024)

        def fn(d, uv, ev):
            s = _sigmoid(uv)
            return d * s, d * ev * s * (1.0 - s)

        de, du = rowwise("pl_mix_bwd", fn, [dx2, u, e], [], [(D, BF), (D, BF)], dep=dep)
        G['pl_w_proj', (i,)] = mm("pl_dproj", TN, (1, N_CHIPS, 1), p[i, 0], (T, dp), _km, de, (T, ks), _kn,
                                  (None, dp, ks), lambda m, n, k: (n, 0, 0), _sds((N_CHIPS, dp, ks), BF))
        G['pl_w_gate', (i,)] = mm_tn("pl_dgate", h, du, BF, ks, 1024).reshape(N_CHIPS, ks, D)
        dh = mm_nt("pl_dh", du, full['pl_w_gate', (i,)].reshape(D, D), F32, 1024)
        return rms_bwd(xin, pl_norm[i][None], dh, dx2, "pl_rms_bwd")

    def ffn_b(dx2, key, i, j, dep):
        def sends(name, g, dep):
            G[name, (i, j)] = g
            s = [t for t, keys in enumerate(order) if (name, (i, j)) in keys][0]
            if not all(k in G for k in order[s]):
                return dep
            tok = scatter(s, dep)
            return to_chips(tok) if len(order[s]) == 1 else tok

        return ffn_bwd(dx2, saved[key], ffn_norm_f[i, j][None], *ffn_w(i, j), dep, sends)

    def mixer1_bwd(dx2, sv, dep):
        xin, h, qkv, kvp, o, lse = sv
        do = mm_nt("att_do", dx2, w_o, BF, 1024, dep=dep)
        dwo = mm_tn("att_dwo", o, dx2, BF, 1024, 1024, dep=dep)
        dq, dkp, dvp, dbm = attn_bwd(qkv, kvp, bm, o, lse, do)
        dqkv = jnp.concatenate([dq, dkp[PADK:].astype(BF), dvp[PADK:].astype(BF)], axis=1)
        tm, tn = min(T, 1024), 512
        per = qs // tn
        dwqkv = mm("att_dwqkv", TN, (D // 1024, 3 * D // tn, 1), h, (T, 1024), _km, dqkv, (T, tn), _kn,
                   (None, 1024, tn), lambda m, n, k: (n // per, m, n % per), _sds((N_CHIPS, D, qs), BF))
        dh = mm("att_dh", NT, (T // tm, D // 1024, N_CHIPS), dqkv, (tm, qs), _mk, w_qkv, (None, 1024, qs),
                lambda m, n, k: (k, n, 0), (tm, 1024), _mn, _sds((T, D), F32))
        dxn, dgain = rms_bwd(xin, mix_norm[1][None], dh, dx2, "mix1_rms_bwd")
        return dxn, dgain, dwo, dwqkv, relbias_reduce(dbm)

    def mixer0_bwd(dx2, sv, dep):
        xin, h, zmain, gzp, ug, o, S, u, yc, cat = sv
        dcat = mm_nt("ab_dcat", dx2, w_out, F32, 1024, dep=dep)
        dwout = mm_tn("ab_dwout", cat, dx2, BF, 1024, 1024, dep=dep)
        do, dr, dgn = rowwise("gla_post_bwd", _gla_post_bwd_fn,
                              [("cols", dcat, GLA_H * HV, 0), o, ("cols", zmain, GLA_H * HV, 2)], [gla_norm_g],
                              [(GLA_H * HV, F32), (GLA_H * HV, BF)], [(1, HV)])
        dyc, dlg, dlb = rowwise("conv_ln_bwd", _ln_silu_bwd_fn, [("cols", dcat, 1024, 1), yc], [conv_ln_g, conv_ln_b],
                                [(1024, F32)], [(1, 1024), (1, 1024)])
        du, ddw, ddwb = conv_bwd(dyc, u, conv_w)
        dca, dcb = rowwise("conv_glu_bwd", _glu_bwd_fn, [du, ("cols", zmain, 1024, 3), ("cols", zmain, 1024, 4)], [],
                           [(1024, BF), (1024, BF)])
        dq, dk, dv, dug, dgb = gla_bwd(zmain, ug, gla_gate_b, S, do)
        dgw = mm_tn("gla_dgate_w", gzp, dug, F32, LANES, GLA_H * HK)
        dgzp = mm_nt("gla_dgz", dug, gate_w_p, F32, LANES)
        dzm = jnp.concatenate([dq, dk, dv, dr, dca, dcb], axis=1)
        dwmain = mm_tn("ab_dwmain", h, dzm, BF, 1024, 1024)
        dwgz = mm_tn("ab_dwgz", h, dgzp, BF, 1024, LANES)
        dh = mm_nt("ab_dh_gz", dgzp, w_gz, F32, 1024, res=mm_nt("ab_dh", dzm, w_main, F32, 512))
        dxn, dgain = rms_bwd(xin, mix_norm[0][None], dh, dx2, "mix0_rms_bwd")
        dwin = jnp.concatenate([dwmain[:, :gz0], dwgz[:, :16], dwmain[:, gz0:]], axis=1)
        g_win = jnp.transpose(dwin.reshape(D, N_CHIPS, ws), (1, 0, 2))
        return dxn, dgain, g_win, dwout, (dgn, dlg, dlb, ddw, ddwb, dgw[:16], dgb)

    scattering, exchanging = [], []

    def tags_of(s):
        return [n + "".join(str(i) for i in lead) for n, lead in order[s]]

    def to_chips(tok):
        if not exchanging:
            return tok
        s, ssem, rsem, thru = exchanging.pop()
        n_u = len(order[s])
        landed = split_wait("sibling_wait_%d" % s, ssem, rsem, thru, _sibling_copies, tok)
        hs = [add_sibling(g4, got, sidx, tag) for g4, got, tag in zip(landed[:n_u], landed[n_u:], tags_of(s))]
        land = [lax.empty((3,) + h.shape[1:], h.dtype) for h in hs]
        ssem, rsem, thru, tok = split_start("scatter_start_%d" % s, hs + land, _scatter_copies, 3 * n_u, tok)
        scattering.append((s, ssem, rsem, thru))
        return tok

    def scatter(s, tok):
        g4s = [G[key].reshape(N_CHIPS, 2, G[key].shape[1] // 2, G[key].shape[2]) for key in order[s]]
        land = [lax.empty((N_CHIPS,) + g4.shape[2:], g4.dtype) for g4 in g4s]
        ssem, rsem, thru, tok = split_start("sibling_start_%d" % s, g4s + land, _sibling_copies, N_CHIPS * len(g4s), tok)
        tok = to_chips(tok)
        exchanging.append((s, ssem, rsem, thru))
        return tok

    dpl, dffn, dmix = [None, None], [[None, None], [None, None]], [None, None]
    tok = jnp.zeros(TOKEN, F32)
    dx, dpl[1] = pl_bwd(dx, saved['p1'], 1, tok)
    tok = scatter(9, tok)
    dx, dffn[1][1], tok = ffn_b(dx, 'f11', 1, 1, tok)
    dx, dmix[1], dwo, dwqkv, drb = mixer1_bwd(dx, saved['m1'], tok)
    G['att_w_qkv', (0,)] = dwqkv
    G['att_w_o', (0,)] = dwo.reshape((N_CHIPS,) + att_w_o.shape[1:])
    tok = scatter(7, tok)
    dx, dffn[1][0], tok = ffn_b(dx, 'f10', 1, 0, tok)
    dx, dpl[0] = pl_bwd(dx, saved['p0'], 0, tok)
    tok = scatter(5, tok)
    dx, dffn[0][1], tok = ffn_b(dx, 'f01', 0, 1, tok)
    dx, dmix[0], g_win, dwout, (dgn, dlg, dlb, ddw, ddwb, dgw, dgb) = mixer0_bwd(dx, saved['m0'], tok)
    G['ab_w_in', (0,)] = g_win
    G['ab_w_out', (0,)] = dwout.reshape((N_CHIPS,) + ab_w_out.shape[1:])
    tok = scatter(3, tok)
    dx, dffn[0][0], tok = ffn_b(dx, 'f00', 0, 0, tok)
    grad_x = dx[None]

    small_g['ffn_norm'] = jnp.stack([jnp.stack([dffn[i][j][0] for j in range(2)]) for i in range(2)])
    small_g['mix_norm'] = jnp.concatenate(dmix, axis=0)
    small_g['gla_gate_w'] = dgw[None]
    small_g['gla_gate_b'] = dgb
    small_g['gla_norm_g'] = dgn
    small_g['conv_dw'] = ddw[None]
    small_g['conv_dw_b'] = ddwb
    small_g['conv_ln_g'] = dlg
    small_g['conv_ln_b'] = dlb
    small_g['att_rel_bias'] = drb[None, :, :att_rel_bias.shape[-1]]
    small_g['pl_norm'] = jnp.concatenate(dpl, axis=0)
    small_g['final_norm'] = small_g['final_norm'][0]
    small_names = [n for n in WEIGHTS if n not in BIG]
    small_shapes = [small_g[n].shape for n in small_names]
    rows_g = _rows_for(small_shapes)
    mine = rowwise("place_small_grads", lambda v: v, [_pack([small_g[n] for n in small_names], rows_g)], [],
                   [("dyn", 8, LANES, F32, 2)], tb=rows_g, sidx=sidx)
    small_ssem, small_rsem, small_thru, tok = split_start("small_start", [mine], _allgather_copies, 7, tok)

    grads, outs = {}, {}
    sharing, updated = [], [dx]

    def update(tok):
        if not sharing:
            return
        s, ssem, rsem, thru = sharing.pop()
        landed = split_wait("share_wait_%d" % s, ssem, rsem, thru, _share_copies, tok)
        for (n, lead), tag, r in zip(order[s], tags_of(s), landed):
            outs[n] = adamw_unit(W[n], M[n], V[n], r.reshape(W[n].shape[-2:]), lead, outs.get(n), "adamw_" + tag)
            updated.append(outs[n][0])

    after = dx
    for s, ssem, rsem, thru in scattering:
        n_u = len(order[s])
        landed = split_wait("scatter_wait_%d" % s, ssem, rsem, thru, _scatter_copies, after)
        rs = [add_chips(h, got2, sidx, tag) for h, got2, tag in zip(landed[:n_u], landed[n_u:], tags_of(s))]
        ssem, rsem, thru, tok = split_start("share_start_%d" % s, rs, _share_copies, n_u, tok)
        update(tok)
        sharing.append((s, ssem, rsem, thru))
        after = tok
    update(tok)

    (allp,) = split_wait("small_wait", small_ssem, small_rsem, small_thru, _allgather_copies, updated[-1])
    summed = rowwise("sum_small_grads", lambda *v: (((v[0] + v[1]) + (v[2] + v[3])) + ((v[4] + v[5]) + (v[6] + v[7]))),
                     [("leads", allp, (d,)) for d in range(8)], [], [(LANES, F32)], tb=rows_g)
    for n, g in zip(small_names, _unpack(summed.reshape(-1), small_shapes)):
        grads[n] = g
    chip = 2 * xc + yc_
    for n, axis in (('ffn_norm', 2), ('gla_gate_w', 2), ('conv_dw', 2)):
        width = W[n].shape[axis]
        grads[n] = lax.dynamic_slice_in_dim(grads[n], chip * width, width, axis)

    for n in small_names:
        outs[n] = adamw(W[n], grads[n], M[n], V[n], "adamw_" + n)
    return (loss, grad_x, *[outs[n][0] for n in WEIGHTS], *[outs[n][1] for n in WEIGHTS],
            *[outs[n][2] for n in WEIGHTS], *[outs[n][3] for n in WEIGHTS])
```

```python
import math

import jax
import jax.numpy as jnp
from jax import lax
from jax.experimental import pallas as pl
from jax.experimental.pallas import tpu as pltpu

F32 = jnp.float32
BF = jnp.bfloat16
MESH = pl.DeviceIdType.MESH
HI = lax.Precision.HIGHEST
V7X_VMEM_LIMIT = 56 * 1024 * 1024
LANES = 128
EPS = 1e-6
NEG = -1e30

CHUNK = 64
LEFT_CHUNKS = 8
QB = 256
KW = QB + LEFT_CHUNKS * CHUNK
PADK = LEFT_CHUNKS * CHUNK
REL_CLIP = 128
REL_PAD = 384
ATT_H = 16
HD = 128
GLA_H = 4
HK = 128
HV = 256
GLA_TAU = 16.0
CONV_W = 31
CONV_PAD = 32
N_CHIPS = 4

ADAM_LR = 0.001
ADAM_B1 = 0.9
ADAM_B2 = 0.999
ADAM_EPS = 1e-08
ADAM_WD = 0.01
ADAM_STEP = 10

NN = ((1,), (0,))
NT = ((1,), (1,))
TN = ((0,), (0,))


def _dot(a, b, dims, prec=None):
    return lax.dot_general(a, b, (dims, ((), ())), preferred_element_type=F32, precision=prec)


def _bdot(a, b, dims):
    return _dot(a.astype(BF), b.astype(BF), dims)


def _cp(*sem):
    return pltpu.CompilerParams(dimension_semantics=sem if sem else None, vmem_limit_bytes=V7X_VMEM_LIMIT)


def _sds(shape, dtype):
    return jax.ShapeDtypeStruct(shape, dtype)


def _sigmoid(x):
    return 1.0 / (1.0 + jnp.exp(-x))


def _silu(x):
    return x * _sigmoid(x)


def _dsilu(x):
    s = _sigmoid(x)
    return s * (1.0 + x * (1.0 - s))


def _f(x):
    return x.astype(F32)


def _row_item(item, tb):
    if not isinstance(item, tuple):
        return item, (tb, item.shape[1]), lambda i, s: (i, 0)
    kind, arr = item[0], item[1]
    if kind == "cols":
        return arr, (tb, item[2]), lambda i, s, blk=item[3]: (i, blk)
    if kind == "leads":
        lead = tuple(item[2])
        return arr, (None,) * len(lead) + (tb, arr.shape[-1]), lambda i, s: lead + (i, 0)
    if kind == "dyn":
        return arr, (None, tb, arr.shape[-1]), lambda i, s, sel=item[2]: (s[sel], i, 0)
    if kind == "dyn4":
        nb = arr.shape[2] // tb
        return arr, (None, None, tb, arr.shape[-1]), lambda i, s, sel=item[2]: (i // nb, s[sel], i % nb, 0)
    raise ValueError(kind)


def _row_block(n, cap):
    for cand in range(min(cap, n) // 16 * 16, 0, -16):
        if n % cand == 0:
            return cand
    return n


def rowwise(name, fn, rows, bcast, outs, accs=(), tb=256, n_rows=None, sidx=None, row_period=None, dep=None):
    if n_rows is None:
        first = rows[0][1] if isinstance(rows[0], tuple) else rows[0]
        n_rows = first.shape[-2]
    tb = _row_block(n_rows if row_period is None else row_period, tb)
    items = [_row_item(it, tb) for it in rows]
    n_r, n_b, n_o = len(rows), len(bcast), len(outs)
    pre = 0 if sidx is None else 1

    def wrap(f):
        return (lambda i: f(i, None)) if sidx is None else (lambda i, s: f(i, s))

    def body(*refs):
        refs = refs[pre:]
        r, b = refs[:n_r], refs[n_r:n_r + n_b]
        refs = refs[n_r + n_b + (dep is not None):]
        o, a = refs[:n_o], refs[n_o:]
        res = fn(*[v[...] for v in r], *[v[...] for v in b])
        if not isinstance(res, tuple):
            res = (res,)
        for ref, val in zip(o, res[:n_o]):
            ref[...] = val.astype(ref.dtype)
        if a:
            @pl.when(pl.program_id(0) == 0)
            def _():
                for ref in a:
                    ref[...] = jnp.zeros_like(ref)
            for ref, val in zip(a, res[n_o:]):
                ref[...] += val

    in_specs = [pl.BlockSpec(bs, wrap(f)) for _, bs, f in items]
    in_specs += [pl.BlockSpec(v.shape, wrap(lambda i, s, nd=v.ndim: (0,) * nd)) for v in bcast]
    out_specs, out_shape = [], []
    for o in outs:
        if o[0] == "dyn":
            _, L, c, dt, sel = o
            out_specs.append(pl.BlockSpec((None, tb, c), wrap(lambda i, s, sel=sel: (s[sel], i, 0))))
            out_shape.append(_sds((L, n_rows, c), dt))
        else:
            c, dt = o
            out_specs.append(pl.BlockSpec((tb, c), wrap(lambda i, s: (i, 0))))
            out_shape.append(_sds((n_rows, c), dt))
    out_specs += [pl.BlockSpec(sh, wrap(lambda i, s: (0, 0))) for sh in accs]
    out_shape += [_sds(sh, F32) for sh in accs]
    operands = [a for a, _, _ in items] + list(bcast)
    if dep is not None:
        operands.append(dep)
        in_specs.append(pl.BlockSpec(TOKEN, wrap(lambda i, s: (0, 0))))
    grid = (n_rows // tb,)
    if sidx is None:
        res = pl.pallas_call(body, grid=grid, in_specs=in_specs, out_specs=out_specs, out_shape=out_shape, name=name,
                             compiler_params=_cp("arbitrary"))(*operands)
    else:
        spec = pltpu.PrefetchScalarGridSpec(num_scalar_prefetch=1, grid=grid, in_specs=in_specs, out_specs=out_specs)
        res = pl.pallas_call(body, grid_spec=spec, out_shape=out_shape, name=name,
                             compiler_params=_cp("arbitrary"))(sidx, *operands)
    return res[0] if len(res) == 1 else tuple(res)


TOKEN = (8, LANES)


def mm(name, dims, grid, a, a_bs, a_im, b, b_bs, b_im, o_bs, o_im, out, scale=1.0, res=None, dep=None):
    nk = grid[2]
    acc_shape = tuple(d for d in o_bs if d is not None)

    def body(*refs):
        a_ref, b_ref = refs[0], refs[1]
        pos = 2
        res_ref = None
        if res is not None:
            res_ref = refs[pos]
            pos += 1
        if dep is not None:
            pos += 1
        o_ref = refs[pos]
        part = _bdot(a_ref[...], b_ref[...], dims)

        def finish(acc):
            v = acc * scale if scale != 1.0 else acc
            if res_ref is not None:
                v = v + _f(res_ref[...])
            o_ref[...] = v.astype(o_ref.dtype)

        if nk == 1:
            finish(part)
        else:
            acc_ref = refs[pos + 1]
            k = pl.program_id(2)

            @pl.when(k == 0)
            def _():
                acc_ref[...] = part

            @pl.when(k > 0)
            def _():
                acc_ref[...] += part

            @pl.when(k == nk - 1)
            def _():
                finish(acc_ref[...])

    operands = [a, b]
    in_specs = [pl.BlockSpec(a_bs, a_im), pl.BlockSpec(b_bs, b_im)]
    if res is not None:
        operands.append(res)
        in_specs.append(pl.BlockSpec(o_bs, o_im))
    if dep is not None:
        operands.append(dep)
        in_specs.append(pl.BlockSpec(TOKEN, lambda m, n, k: (0, 0)))
    scratch = [pltpu.VMEM(acc_shape, F32)] if nk > 1 else []
    return pl.pallas_call(body, grid=grid, in_specs=in_specs, out_specs=pl.BlockSpec(o_bs, o_im), out_shape=out,
                          scratch_shapes=scratch, name=name,
                          compiler_params=_cp("parallel", "parallel", "arbitrary"))(*operands)


def _mk(m, n, k):
    return (m, k)


def _mn(m, n, k):
    return (m, n)


def _km(m, n, k):
    return (k, m)


def _kn(m, n, k):
    return (k, n)


def _nk(m, n, k):
    return (n, k)


def mm_nn(name, a, b, out_dtype, tn, tk=None, scale=1.0, res=None):
    T, K = a.shape
    N = b.shape[1]
    tm, tk, tn = min(T, 1024), K if tk is None else min(tk, K), min(tn, N)
    return mm(name, NN, (T // tm, N // tn, K // tk), a, (tm, tk), _mk, b, (tk, tn), _kn, (tm, tn), _mn,
              _sds((T, N), out_dtype), scale=scale, res=res)


def mm_nt(name, a, b, out_dtype, tn, tk=None, scale=1.0, res=None, dep=None):
    T, K = a.shape
    N = b.shape[0]
    tm, tk, tn = min(T, 1024), K if tk is None else min(tk, K), min(tn, N)
    return mm(name, NT, (T // tm, N // tn, K // tk), a, (tm, tk), _mk, b, (tn, tk), _nk, (tm, tn), _mn,
              _sds((T, N), out_dtype), scale=scale, res=res, dep=dep)


def mm_tn(name, a, b, out_dtype, tm, tn, scale=1.0, dep=None):
    T, M = a.shape
    N = b.shape[1]
    tk, tm, tn = T, min(tm, M), min(tn, N)
    return mm(name, TN, (M // tm, N // tn, T // tk), a, (tk, tm), _km, b, (tk, tn), _kn, (tm, tn), _mn,
              _sds((M, N), out_dtype), scale=scale, dep=dep)


HBM_SPEC = pl.BlockSpec(memory_space=pl.ANY)


def _place():
    x, y, c = lax.axis_index("x"), lax.axis_index("y"), lax.axis_index("c")
    chips = [(1 - x, y), (x, 1 - y), (1 - x, 1 - y)]
    return x, y, c, chips


def allgather8(v, name):
    m_per, n = v.shape

    def body(x_ref, out_ref, send_sems, recv_sems, local_sem):
        x, y, c, chips = _place()
        me, sibling = (x, y, c), (x, y, 1 - c)

        def rows(px, py, pc):
            return out_ref.at[pl.ds((4 * px + 2 * py + pc) * m_per, m_per), :]

        def copy(k, block, to, src=None):
            return pltpu.make_async_remote_copy(
                src_ref=rows(*block) if src is None else src, dst_ref=rows(*block),
                send_sem=send_sems.at[k], recv_sem=recv_sems.at[k], device_id=to, device_id_type=MESH)

        mine = pltpu.make_async_copy(x_ref, rows(*me), local_sem)
        mine.start()
        first = [copy(0, me, sibling, src=x_ref)]
        first += [copy(1 + j, me, (*chip, c), src=x_ref) for j, chip in enumerate(chips)]
        for cp in first:
            cp.start()
        passed = [copy(4 + j, (*chip, c), sibling) for j, chip in enumerate(chips)]
        for j, chip in enumerate(chips):
            copy(1 + j, (*chip, c), me).wait_recv()
            passed[j].start()
        copy(0, sibling, me).wait_recv()
        for j, chip in enumerate(chips):
            copy(4 + j, (*chip, 1 - c), me).wait_recv()
        for cp in first + passed:
            cp.wait_send()
        mine.wait()

    return pl.pallas_call(
        body, out_shape=_sds((8 * m_per, n), v.dtype), in_specs=[pl.BlockSpec(memory_space=pltpu.VMEM)],
        out_specs=pl.BlockSpec(memory_space=pltpu.VMEM), name=name,
        scratch_shapes=[pltpu.SemaphoreType.DMA((7,)), pltpu.SemaphoreType.DMA((7,)), pltpu.SemaphoreType.DMA],
    )(v)


D2D_PIECES = 4


def _pieces(ref, n):
    rows = ref.shape[0] // n
    return [ref.at[pl.ds(q * rows, rows)] for q in range(n)]


def _start_in_pieces(src, dst, ssem, rsem, to, n):
    for s_q, d_q in zip(_pieces(src, n), _pieces(dst, n)):
        pltpu.make_async_remote_copy(src_ref=s_q, dst_ref=d_q, send_sem=ssem, recv_sem=rsem, device_id=to,
                                     device_id_type=MESH).start()


def _whole(src, dst, ssem, rsem, to):
    return pltpu.make_async_remote_copy(src_ref=src, dst_ref=dst, send_sem=ssem, recv_sem=rsem, device_id=to,
                                        device_id_type=MESH)


HBM_ONLY = pl.BlockSpec(memory_space=pltpu.HBM)
SEM_SPEC = pl.BlockSpec(memory_space=pltpu.SEMAPHORE)
DATAFLOW = pltpu.SideEffectType.DATAFLOW_SIDE_EFFECTING


def _gather_copies(bufs):
    x, y, c, chips = _place()
    out = []
    for o_ref in bufs:
        hr = o_ref.shape[1] // 2

        def half(chip, o_ref=o_ref, hr=hr):
            return o_ref.at[2 * chip[0] + chip[1], pl.ds(c * hr, hr)]

        out += [(half((x, y)), half((x, y)), (*chip, c), half(chip)) for chip in chips]
    return out


def _scatter_copies(arrays):
    x, y, c, chips = _place()
    n = len(arrays) // 2
    out = []
    for h_ref, got_ref in zip(arrays[:n], arrays[n:]):
        out += [(h_ref.at[2 * chip[0] + chip[1]], got_ref.at[j], (*chip, c), got_ref.at[j])
                for j, chip in enumerate(chips)]
    return out


def split_start(name, arrays, copies_of, n_copies, after):
    n = len(arrays)

    def body(*refs):
        ssem, rsem = refs[n + 1], refs[n + 2]
        for i, (src, dst, to, _) in enumerate(copies_of(refs[n + 3:2 * n + 3])):
            _whole(src, dst, ssem.at[i], rsem.at[i], to).start()
        refs[2 * n + 3][...] = jnp.zeros(TOKEN, F32)

    res = pl.pallas_call(
        body, name=name,
        out_shape=(pltpu.SemaphoreType.DMA((n_copies,)), pltpu.SemaphoreType.DMA((n_copies,)),
                   *[pltpu.HBM(a.shape, a.dtype) for a in arrays], _sds(TOKEN, F32)),
        in_specs=[HBM_ONLY] * n + [pl.BlockSpec(memory_space=pl.ANY)],
        out_specs=(SEM_SPEC, SEM_SPEC, *[HBM_ONLY] * n, pl.BlockSpec(memory_space=pltpu.VMEM)),
        input_output_aliases={i: 2 + i for i in range(n)},
        compiler_params=pltpu.CompilerParams(has_side_effects=DATAFLOW),
    )(*[pltpu.with_memory_space_constraint(a, pltpu.HBM) for a in arrays], after)
    return res[0], res[1], list(res[2:2 + n]), res[2 + n]


def split_wait(name, ssem, rsem, arrays, copies_of, after):
    n = len(arrays)

    def body(*refs):
        s_ref, r_ref = refs[n], refs[n + 1]
        for i, (src, _, to, mine) in enumerate(copies_of(refs[:n])):
            cp = _whole(src, mine, s_ref.at[i], r_ref.at[i], to)
            cp.wait_send()
            cp.wait_recv()

    res = pl.pallas_call(
        body, name=name, out_shape=tuple(pltpu.HBM(a.shape, a.dtype) for a in arrays),
        in_specs=[HBM_ONLY] * n + [SEM_SPEC, SEM_SPEC, pl.BlockSpec(memory_space=pl.ANY)],
        out_specs=tuple([HBM_ONLY] * n), input_output_aliases={i: i for i in range(n)},
        compiler_params=pltpu.CompilerParams(has_side_effects=DATAFLOW),
    )(*arrays, ssem, rsem, after)
    return list(res)


def gather_pass_on(bufs, name):
    n = len(bufs)

    def body(*refs):
        o_refs, ssem, rsem = refs[n:2 * n], refs[2 * n], refs[2 * n + 1]
        x, y, c, chips = _place()
        sib = (x, y, 1 - c)

        def half(o_ref, chip, h):
            hr = o_ref.shape[1] // 2
            return o_ref.at[2 * chip[0] + chip[1], pl.ds(h * hr, hr)]

        for u, o_ref in enumerate(o_refs):
            for j, chip in enumerate(chips):
                _start_in_pieces(half(o_ref, chip, c), half(o_ref, chip, c), ssem.at[3 * u + j], rsem.at[3 * u + j], sib,
                                 D2D_PIECES)
        for u, o_ref in enumerate(o_refs):
            for j, chip in enumerate(chips):
                _whole(half(o_ref, chip, c), half(o_ref, chip, 1 - c), ssem.at[3 * u + j], rsem.at[3 * u + j], sib).wait()

    res = pl.pallas_call(
        body, out_shape=tuple(_sds(b.shape, b.dtype) for b in bufs), in_specs=[HBM_SPEC] * n,
        out_specs=tuple([HBM_SPEC] * n), name=name, input_output_aliases={i: i for i in range(n)},
        scratch_shapes=[pltpu.SemaphoreType.DMA((3 * n,)), pltpu.SemaphoreType.DMA((3 * n,))],
    )(*bufs)
    return list(res)


def _sibling_copies(arrays):
    x, y, c, _ = _place()
    n = len(arrays) // 2
    out = []
    for g_ref, got_ref in zip(arrays[:n], arrays[n:]):
        out += [(g_ref.at[k, 1 - c], got_ref.at[k], (x, y, 1 - c), got_ref.at[k]) for k in range(N_CHIPS)]
    return out


def _allgather_copies(arrays):
    x, y, c, _ = _place()
    (buf,) = arrays
    mine = buf.at[4 * x + 2 * y + c]
    out = []
    for fx, fy, fc in [(0, 0, 1), (0, 1, 0), (0, 1, 1), (1, 0, 0), (1, 0, 1), (1, 1, 0), (1, 1, 1)]:
        px, py, pc = (1 - x if fx else x), (1 - y if fy else y), (1 - c if fc else c)
        out.append((mine, mine, (px, py, pc), buf.at[4 * px + 2 * py + pc]))
    return out


def _share_copies(arrays):
    x, y, c, _ = _place()
    return [(r_ref.at[c], r_ref.at[c], (x, y, 1 - c), r_ref.at[1 - c]) for r_ref in arrays]


def add_sibling(g4, got, sidx, tag):
    _, _, hr, B = g4.shape
    h = rowwise("rs_add2_" + tag, lambda a, b: _f(a) + _f(b), [("dyn4", g4, 1), got.reshape(N_CHIPS * hr, B)], [],
                [(B, BF)], tb=1024, n_rows=N_CHIPS * hr, sidx=sidx, row_period=hr)
    return h.reshape(N_CHIPS, hr, B)


def add_chips(h, got2, sidx, tag):
    _, hr, B = h.shape
    return rowwise("rs_add4_" + tag, lambda a, b, c, d: ((_f(a) + _f(b)) + _f(c)) + _f(d),
                   [("dyn", h, 0), ("leads", got2, (0,)), ("leads", got2, (1,)), ("leads", got2, (2,))], [],
                   [("dyn", 2, B, F32, 1)], tb=512, n_rows=hr, sidx=sidx)


def _rms_parts(x):
    rstd = lax.rsqrt(jnp.mean(x * x, axis=-1, keepdims=True) + EPS)
    return x * rstd, rstd


def rms_fwd(x, g, name, dep=None):
    return rowwise(name, lambda xv, gv: _rms_parts(xv)[0] * gv, [x], [g], [(x.shape[1], BF)], dep=dep)


def _rms_bwd_fn(x, dh, dres, g):
    xh, rstd = _rms_parts(x)
    dxh = _f(dh) * g
    dx = rstd * (dxh - xh * jnp.mean(dxh * xh, axis=-1, keepdims=True)) + dres
    return dx, dx, jnp.sum(_f(dh) * xh, axis=0, keepdims=True)


def rms_bwd(x, g, dh, dres, name):
    D = x.shape[1]
    dx, dxb, dg = rowwise(name, _rms_bwd_fn, [x, dh, dres], [g], [(D, F32), (D, BF)], [(1, D)])
    return (dx, dxb), dg


def cast_unit(w, lead, sidx, name):
    R, B = w.shape[-2:]
    return rowwise(name, lambda v: v, [("leads", w, lead)], [], [("dyn", N_CHIPS, B, BF, 0)], tb=1024, n_rows=R,
                   sidx=sidx)


FFN_TM = 512


def ffn_up(h, wg, wu):
    T, D = h.shape
    fs = wg.shape[-1]
    tm = min(T, FFN_TM)

    def body(h_ref, wg_ref, wu_ref, zg_ref, zu_ref, a_ref):
        hb = h_ref[...]
        g = _bdot(hb, wg_ref[...], NN)
        u = _bdot(hb, wu_ref[...], NN)
        zg_ref[...] = g.astype(zg_ref.dtype)
        zu_ref[...] = u.astype(zu_ref.dtype)
        a_ref[...] = (_silu(g) * u).astype(a_ref.dtype)

    w_spec = pl.BlockSpec((None, D, fs), lambda n, m: (n, 0, 0))
    o_spec = pl.BlockSpec((tm, fs), lambda n, m: (m, n))
    return pl.pallas_call(
        body, grid=(N_CHIPS, T // tm), in_specs=[pl.BlockSpec((tm, D), lambda n, m: (m, 0)), w_spec, w_spec],
        out_specs=[o_spec] * 3, out_shape=[_sds((T, N_CHIPS * fs), BF)] * 3, name="ffn_up",
        compiler_params=_cp("parallel", "arbitrary"))(h, wg, wu)


def ffn_dz(dx2, wd2, zg, zu, dep):
    T, D = dx2.shape
    F = wd2.shape[0]
    fs = F // N_CHIPS
    tm = min(T, FFN_TM)

    def body(dx_ref, wd_ref, zg_ref, zu_ref, dep_ref, dzg_ref, dzu_ref):
        da = 0.5 * _bdot(dx_ref[...], wd_ref[...], NT)
        dzg, dzu = _swiglu_bwd_fn(da, zg_ref[...], zu_ref[...])
        dzg_ref[...] = dzg.astype(dzg_ref.dtype)
        dzu_ref[...] = dzu.astype(dzu_ref.dtype)

    z_spec = pl.BlockSpec((tm, fs), lambda n, m: (m, n))
    return pl.pallas_call(
        body, grid=(N_CHIPS, T // tm),
        in_specs=[pl.BlockSpec((tm, D), lambda n, m: (m, 0)), pl.BlockSpec((fs, D), lambda n, m: (n, 0)), z_spec, z_spec,
                  pl.BlockSpec(TOKEN, lambda n, m: (0, 0))],
        out_specs=[z_spec] * 2, out_shape=[_sds((T, F), BF)] * 2, name="ffn_dz",
        compiler_params=_cp("parallel", "arbitrary"))(dx2, wd2, zg, zu, dep)


def ffn_dh(dzg, dzu, wg, wu, dep):
    T, F = dzg.shape
    _, D, fs = wg.shape
    tm, tn = min(T, 1024), 1024

    def body(g_ref, u_ref, wg_ref, wu_ref, dep_ref, o_ref, acc_ref):
        k = pl.program_id(2)
        part = _bdot(g_ref[...], wg_ref[...], NT) + _bdot(u_ref[...], wu_ref[...], NT)

        @pl.when(k == 0)
        def _():
            acc_ref[...] = part

        @pl.when(k > 0)
        def _():
            acc_ref[...] += part

        @pl.when(k == N_CHIPS - 1)
        def _():
            o_ref[...] = acc_ref[...]

    z_spec = pl.BlockSpec((tm, fs), _mk)
    w_spec = pl.BlockSpec((None, tn, fs), lambda m, n, k: (k, n, 0))
    return pl.pallas_call(
        body, grid=(T // tm, D // tn, N_CHIPS),
        in_specs=[z_spec, z_spec, w_spec, w_spec, pl.BlockSpec(TOKEN, lambda m, n, k: (0, 0))],
        out_specs=pl.BlockSpec((tm, tn), _mn), out_shape=_sds((T, D), F32),
        scratch_shapes=[pltpu.VMEM((tm, tn), F32)], name="ffn_dh",
        compiler_params=_cp("parallel", "parallel", "arbitrary"))(dzg, dzu, wg, wu, dep)


def ffn_fwd(xin, gain, wg, wu, wd, dep=None):
    T, D = xin.shape
    F = N_CHIPS * wg.shape[-1]
    h = rms_fwd(xin, gain, "ffn_rms", dep)
    zg, zu, a = ffn_up(h, wg, wu)
    if callable(wd):
        wd = wd(a)
    x2 = mm_nn("ffn_down", a, wd.reshape(F, D), F32, 512, scale=0.5, res=xin)
    return x2, (xin, h, zg, zu, a)


def _swiglu_bwd_fn(da, zg, zu):
    da, zg, zu = _f(da), _f(zg), _f(zu)
    return da * zu * _dsilu(zg), da * _silu(zg)


def ffn_bwd(dx2, saved, gain, wg, wu, wd, dep, sends):
    xin, h, zg, zu, a = saved
    T, D = xin.shape
    fs = wg.shape[-1]
    F = N_CHIPS * fs
    tn = 1024
    dx2, dx2b = dx2
    g_wd = mm_tn("ffn_dwd", a, dx2b, BF, fs, 1024, scale=0.5, dep=dep).reshape(N_CHIPS, fs, D)
    dep = sends('ffn_w_down', g_wd, dep)
    dzg, dzu = ffn_dz(dx2b, wd.reshape(F, D), zg, zu, dep)

    def dw(dz, nm):
        return mm(nm, TN, (D // tn, N_CHIPS, 1), h, (T, tn), _km, dz, (T, fs), _kn,
                  (None, tn, fs), lambda m, n, k: (n, m, 0), _sds((N_CHIPS, D, fs), BF))

    dep = sends('ffn_w_gate', dw(dzg, "ffn_dwg"), dep)
    dep = sends('ffn_w_up', dw(dzu, "ffn_dwu"), dep)
    dh = ffn_dh(dzg, dzu, wg, wu, dep)
    dx, dgain = rms_bwd(xin, gain, dh, dx2, "ffn_rms_bwd")
    return dx, dgain, dep


def _gla_chunk(q_ref, k_ref, v_ref, u_ref, b_ref, rows):
    r = lax.broadcasted_iota(jnp.int32, (CHUNK, CHUNK), 0)
    c = lax.broadcasted_iota(jnp.int32, (CHUNK, CHUNK), 1)
    causal = c <= r
    u = u_ref[rows, :] + b_ref[...]
    g = (jnp.minimum(u, 0.0) - jnp.log(1.0 + jnp.exp(-jnp.abs(u)))) * (1.0 / GLA_TAU)
    b = _dot(causal.astype(F32), g, NN, HI)
    last = lax.broadcasted_iota(jnp.int32, (CHUNK, HK), 0) == CHUNK - 1
    blast = jnp.sum(jnp.where(last, b, 0.0), axis=0, keepdims=True)
    eb = jnp.exp(b)
    qb = q_ref[rows, :] * (HK ** -0.5) * eb
    k = k_ref[rows, :]
    kb = k * jnp.exp(-b)
    kl = k * jnp.exp(blast - b)
    A = jnp.where(causal, _bdot(qb, kb, NT), 0.0)
    return causal, u, b, blast, eb, qb, kb, kl, A


def _gla_in_specs(T):
    return [pl.BlockSpec((T, HK), lambda h: (0, h)), pl.BlockSpec((T, HK), lambda h: (0, GLA_H + h)),
            pl.BlockSpec((T, HV), lambda h: (0, GLA_H + h)), pl.BlockSpec((T, HK), lambda h: (0, h)),
            pl.BlockSpec((1, HK), lambda h: (0, h))]


def gla_fwd(zmain, ug, gate_b):
    T = zmain.shape[0]
    nC = T // CHUNK

    def body(q_ref, k_ref, v_ref, u_ref, b_ref, o_ref, s_ref, st_ref):
        st_ref[...] = jnp.zeros_like(st_ref)

        def step(n, carry):
            rows = pl.ds(pl.multiple_of(n * CHUNK, CHUNK), CHUNK)
            _, _, _, blast, _, qb, _, kl, A = _gla_chunk(q_ref, k_ref, v_ref, u_ref, b_ref, rows)
            v = v_ref[rows, :]
            ST = st_ref[...]
            s_ref[n] = ST
            o_ref[rows, :] = _bdot(qb, ST, NT) + _bdot(A, v, NN)
            st_ref[...] = ST * jnp.exp(blast) + _bdot(v, kl, TN)
            return carry

        lax.fori_loop(0, nC, step, 0)

    return pl.pallas_call(
        body, grid=(GLA_H,), in_specs=_gla_in_specs(T),
        out_specs=[pl.BlockSpec((T, HV), lambda h: (0, h)), pl.BlockSpec((nC, None, HV, HK), lambda h: (0, h, 0, 0))],
        out_shape=[_sds((T, GLA_H * HV), F32), _sds((nC, GLA_H, HV, HK), F32)],
        scratch_shapes=[pltpu.VMEM((HV, HK), F32)], name="gla_fwd", compiler_params=_cp("parallel"),
    )(zmain, zmain, zmain, ug, gate_b)


def gla_bwd(zmain, ug, gate_b, S, do):
    T = zmain.shape[0]
    nC = T // CHUNK

    def body(q_ref, k_ref, v_ref, u_ref, b_ref, s_ref, do_ref, dq_ref, dk_ref, dv_ref, du_ref, dgb_ref, dst_ref):
        dst_ref[...] = jnp.zeros_like(dst_ref)
        dgb_ref[...] = jnp.zeros_like(dgb_ref)

        def step(it, carry):
            n = nC - 1 - it
            rows = pl.ds(pl.multiple_of(n * CHUNK, CHUNK), CHUNK)
            causal, u, b, blast, eb, qb, kb, kl, A = _gla_chunk(q_ref, k_ref, v_ref, u_ref, b_ref, rows)
            v = v_ref[rows, :]
            dout = do_ref[rows, :]
            ST = s_ref[n]
            dST = dst_ref[...]
            elast = jnp.exp(blast)
            dA = jnp.where(causal, _bdot(dout, v, NT), 0.0)
            dv_ref[rows, :] = (_bdot(A, dout, TN) + _bdot(kl, dST, NT)).astype(dv_ref.dtype)
            dqb = _bdot(dout, ST, NN) + _bdot(dA, kb, NN)
            dkb = _bdot(dA, qb, TN)
            dkl = _bdot(v, dST, NN)
            ddec = jnp.sum(ST * dST, axis=0, keepdims=True)
            dst_ref[...] = dST * elast + _bdot(dout, qb, TN)
            dq_ref[rows, :] = (dqb * eb * (HK ** -0.5)).astype(dq_ref.dtype)
            dk_ref[rows, :] = (dkb * jnp.exp(-b) + dkl * jnp.exp(blast - b)).astype(dk_ref.dtype)
            db = dqb * qb - dkb * kb - dkl * kl
            dbl = jnp.sum(dkl * kl, axis=0, keepdims=True) + elast * ddec
            dg = _dot(jnp.logical_not(causal).astype(F32) + jnp.where(
                lax.broadcasted_iota(jnp.int32, (CHUNK, CHUNK), 0) == lax.broadcasted_iota(jnp.int32, (CHUNK, CHUNK), 1),
                1.0, 0.0), db, NN, HI) + dbl
            du = dg * (1.0 / GLA_TAU) / (1.0 + jnp.exp(u))
            du_ref[rows, :] = du
            dgb_ref[...] += jnp.sum(du, axis=0, keepdims=True)
            return carry

        lax.fori_loop(0, nC, step, 0)

    specs = _gla_in_specs(T) + [pl.BlockSpec((nC, None, HV, HK), lambda h: (0, h, 0, 0)),
                                pl.BlockSpec((T, HV), lambda h: (0, h))]
    return pl.pallas_call(
        body, grid=(GLA_H,), in_specs=specs,
        out_specs=[pl.BlockSpec((T, HK), lambda h: (0, h)), pl.BlockSpec((T, HK), lambda h: (0, h)),
                   pl.BlockSpec((T, HV), lambda h: (0, h)), pl.BlockSpec((T, HK), lambda h: (0, h)),
                   pl.BlockSpec((1, HK), lambda h: (0, h))],
        out_shape=[_sds((T, GLA_H * HK), BF), _sds((T, GLA_H * HK), BF), _sds((T, GLA_H * HV), BF),
                   _sds((T, GLA_H * HK), F32), _sds((1, GLA_H * HK), F32)],
        scratch_shapes=[pltpu.VMEM((HV, HK), F32)], name="gla_bwd", compiler_params=_cp("parallel"),
    )(zmain, zmain, zmain, ug, gate_b, S, do)


def _gla_post_fn(o, r, g):
    outs = []
    for h in range(GLA_H):
        on = _rms_parts(o[:, h * HV:(h + 1) * HV])[0] * g
        outs.append(on * _silu(r[:, h * HV:(h + 1) * HV]))
    return jnp.concatenate(outs, axis=1)


def _gla_post_bwd_fn(da, o, r, g):
    dos, drs = [], []
    dg = jnp.zeros((1, HV), F32)
    for h in range(GLA_H):
        sl = slice(h * HV, (h + 1) * HV)
        xh, rstd = _rms_parts(o[:, sl])
        drs.append(da[:, sl] * xh * g * _dsilu(r[:, sl]))
        don = da[:, sl] * _silu(r[:, sl])
        dg = dg + jnp.sum(don * xh, axis=0, keepdims=True)
        dxh = don * g
        dos.append(rstd * (dxh - xh * jnp.mean(dxh * xh, axis=-1, keepdims=True)))
    return jnp.concatenate(dos, axis=1), jnp.concatenate(drs, axis=1), dg


def conv_fwd(u, dw, dwb):
    T, C = u.shape
    TB = min(T, 256)

    def body(u_ref, w_ref, b_ref, y_ref, pad_ref):
        pad_ref[0:CONV_PAD, :] = jnp.zeros((CONV_PAD, LANES), F32)
        pad_ref[CONV_PAD:CONV_PAD + T, :] = u_ref[...]
        off = CONV_PAD - (CONV_W - 1)
        for t0 in range(0, T, TB):
            acc = jnp.zeros((TB, LANES), F32) + b_ref[...]
            for j in range(CONV_W):
                acc = acc + w_ref[j:j + 1, :] * pad_ref[t0 + off + j:t0 + off + j + TB, :]
            y_ref[t0:t0 + TB, :] = acc

    col = lambda i: (0, i)
    return pl.pallas_call(
        body, grid=(C // LANES,),
        in_specs=[pl.BlockSpec((T, LANES), col), pl.BlockSpec((CONV_W, LANES), col), pl.BlockSpec((1, LANES), col)],
        out_specs=pl.BlockSpec((T, LANES), col), out_shape=_sds((T, C), F32),
        scratch_shapes=[pltpu.VMEM((T + CONV_PAD, LANES), F32)], name="conv_fwd", compiler_params=_cp("parallel"),
    )(u, dw, dwb)


def conv_bwd(dy, u, dw):
    T, C = u.shape
    TB = min(T, 256)

    def body(dy_ref, u_ref, w_ref, du_ref, dw_ref, db_ref, upad, dypad):
        upad[0:CONV_PAD, :] = jnp.zeros((CONV_PAD, LANES), F32)
        upad[CONV_PAD:CONV_PAD + T, :] = u_ref[...]
        dypad[0:T, :] = dy_ref[...]
        dypad[T:T + CONV_PAD, :] = jnp.zeros((CONV_PAD, LANES), F32)
        off = CONV_PAD - (CONV_W - 1)
        for t0 in range(0, T, TB):
            acc = jnp.zeros((TB, LANES), F32)
            for j in range(CONV_W):
                s = t0 + (CONV_W - 1) - j
                acc = acc + w_ref[j:j + 1, :] * dypad[s:s + TB, :]
            du_ref[t0:t0 + TB, :] = acc
        for j in range(CONV_W):
            acc = jnp.zeros((TB, LANES), F32)
            for t0 in range(0, T, TB):
                acc = acc + dy_ref[t0:t0 + TB, :] * upad[t0 + off + j:t0 + off + j + TB, :]
            dw_ref[j:j + 1, :] = jnp.sum(acc, axis=0, keepdims=True)
        db_ref[...] = jnp.sum(dy_ref[...], axis=0, keepdims=True)

    col = lambda i: (0, i)
    return pl.pallas_call(
        body, grid=(C // LANES,),
        in_specs=[pl.BlockSpec((T, LANES), col), pl.BlockSpec((T, LANES), col), pl.BlockSpec((CONV_W, LANES), col)],
        out_specs=[pl.BlockSpec((T, LANES), col), pl.BlockSpec((CONV_W, LANES), col), pl.BlockSpec((1, LANES), col)],
        out_shape=[_sds((T, C), F32), _sds((CONV_W, C), F32), _sds((1, C), F32)],
        scratch_shapes=[pltpu.VMEM((T + CONV_PAD, LANES), F32), pltpu.VMEM((T + CONV_PAD, LANES), F32)],
        name="conv_bwd", compiler_params=_cp("parallel"),
    )(dy, u, dw)


def _ln_parts(x):
    mu = jnp.mean(x, axis=-1, keepdims=True)
    xc = x - mu
    rstd = lax.rsqrt(jnp.mean(xc * xc, axis=-1, keepdims=True) + EPS)
    return xc * rstd, rstd


def _ln_silu_fn(x, g, b):
    return _silu(_ln_parts(x)[0] * g + b)


def _ln_silu_bwd_fn(dbo, x, g, b):
    xh, rstd = _ln_parts(x)
    dy = dbo * _dsilu(xh * g + b)
    dyg = dy * g
    dx = rstd * (dyg - jnp.mean(dyg, axis=-1, keepdims=True) - xh * jnp.mean(dyg * xh, axis=-1, keepdims=True))
    return dx, jnp.sum(dy * xh, axis=0, keepdims=True), jnp.sum(dy, axis=0, keepdims=True)


def _glu_bwd_fn(du, ca, cb):
    s = _sigmoid(cb)
    return du * s, du * ca * s * (1.0 - s)


DIAGS = QB + KW


def _onehot_diag():
    j = lax.broadcasted_iota(jnp.int32, (REL_PAD, DIAGS), 1)
    i = lax.broadcasted_iota(jnp.int32, (REL_PAD, DIAGS), 0)
    return (i == jnp.clip(KW - j, -REL_CLIP, REL_CLIP) + REL_CLIP).astype(F32)


def relbias_tile(rbp):
    def body(rb_ref, o_ref, e_ref):
        e_ref[...] = _dot(rb_ref[...], _onehot_diag(), NN, HI)
        tc = lax.shift_right_logical(lax.broadcasted_iota(jnp.int32, (QB, KW), 0), 6)
        wc = lax.shift_right_logical(lax.broadcasted_iota(jnp.int32, (QB, KW), 1), 6)
        ok = jnp.logical_and(wc >= tc, wc <= tc + LEFT_CHUNKS)
        for h in range(ATT_H):
            spread = pltpu.roll(jnp.broadcast_to(e_ref[h:h + 1, :], (QB, DIAGS)), KW, 1, stride=1, stride_axis=0)
            o_ref[h] = jnp.where(ok, spread[:, :KW], NEG)

    return pl.pallas_call(body, out_shape=_sds((ATT_H, QB, KW), F32), name="relbias_tile",
                          scratch_shapes=[pltpu.VMEM((ATT_H, DIAGS), F32)], compiler_params=_cp())(rbp)


def relbias_reduce(dbm):
    def body(d_ref, o_ref, e_ref):
        u = lax.broadcasted_iota(jnp.int32, (QB, QB), 0)
        t = lax.broadcasted_iota(jnp.int32, (QB, QB), 1)
        flip = (u + t == QB - 1).astype(F32)
        for h in range(ATT_H):
            padded = jnp.concatenate([d_ref[h], jnp.zeros((QB, QB), F32)], axis=1)
            lined = pltpu.roll(_dot(flip, padded, NN, HI), 1, 1, stride=1, stride_axis=0)
            e_ref[h:h + 1, :] = jnp.sum(lined, axis=0, keepdims=True)
        o_ref[...] = _dot(e_ref[...], _onehot_diag(), NT, HI)

    return pl.pallas_call(body, out_shape=_sds((ATT_H, REL_PAD), F32), name="relbias_reduce",
                          scratch_shapes=[pltpu.VMEM((ATT_H, DIAGS), F32)], compiler_params=_cp())(dbm)


def _att_scores(q_ref, kp_ref, bm_ref, i):
    q0 = pl.multiple_of(i * QB, QB)
    kw = kp_ref[pl.ds(q0, KW), :]
    s = _bdot(q_ref[...], kw, NT) * (HD ** -0.5) + bm_ref[...]
    w = lax.broadcasted_iota(jnp.int32, (QB, KW), 1)
    return jnp.where(w + q0 >= PADK, s, NEG), kw, q0


def attn_fwd(qkv, kvp, bm):
    T = qkv.shape[0]
    D = ATT_H * HD

    def body(q_ref, kp_ref, vp_ref, bm_ref, o_ref, lse_ref):
        s, _, q0 = _att_scores(q_ref, kp_ref, bm_ref, pl.program_id(1))
        m = jnp.max(s, axis=-1, keepdims=True)
        e = jnp.exp(s - m)
        l = jnp.sum(e, axis=-1, keepdims=True)
        o_ref[...] = _bdot(e * (1.0 / l), vp_ref[pl.ds(q0, KW), :], NN).astype(o_ref.dtype)
        lse_ref[...] = m + jnp.log(l)

    return pl.pallas_call(
        body, grid=(ATT_H, T // QB),
        in_specs=[pl.BlockSpec((QB, HD), lambda h, i: (i, h)), pl.BlockSpec((T + PADK, HD), lambda h, i: (0, h)),
                  pl.BlockSpec((T + PADK, HD), lambda h, i: (0, ATT_H + h)),
                  pl.BlockSpec((None, QB, KW), lambda h, i: (h, 0, 0))],
        out_specs=[pl.BlockSpec((QB, HD), lambda h, i: (i, h)), pl.BlockSpec((None, QB, 1), lambda h, i: (h, i, 0))],
        out_shape=[_sds((T, D), BF), _sds((ATT_H, T, 1), F32)], name="attn_fwd",
        compiler_params=_cp("parallel", "arbitrary"),
    )(qkv, kvp, kvp, bm)


def attn_bwd(qkv, kvp, bm, o, lse, do):
    T = qkv.shape[0]
    D = ATT_H * HD

    def body(q_ref, kp_ref, vp_ref, bm_ref, o_ref, lse_ref, do_ref, dq_ref, dkp_ref, dvp_ref, dbm_ref):
        i = pl.program_id(1)

        @pl.when(i == 0)
        def _():
            dkp_ref[...] = jnp.zeros_like(dkp_ref)
            dvp_ref[...] = jnp.zeros_like(dvp_ref)
            dbm_ref[...] = jnp.zeros_like(dbm_ref)

        s, kw, q0 = _att_scores(q_ref, kp_ref, bm_ref, i)
        p = jnp.exp(s - lse_ref[...])
        dout = do_ref[...]
        dp = _bdot(dout, vp_ref[pl.ds(q0, KW), :], NT)
        delta = jnp.sum(_f(dout) * _f(o_ref[...]), axis=-1, keepdims=True)
        ds = p * (dp - delta)
        dq_ref[...] = (_bdot(ds, kw, NN) * (HD ** -0.5)).astype(dq_ref.dtype)
        dkp_ref[pl.ds(q0, KW), :] += _bdot(ds, q_ref[...], TN) * (HD ** -0.5)
        dvp_ref[pl.ds(q0, KW), :] += _bdot(p, dout, TN)
        dbm_ref[...] += ds

    qspec = pl.BlockSpec((QB, HD), lambda h, i: (i, h))
    kspec = pl.BlockSpec((T + PADK, HD), lambda h, i: (0, h))
    bspec = pl.BlockSpec((None, QB, KW), lambda h, i: (h, 0, 0))
    return pl.pallas_call(
        body, grid=(ATT_H, T // QB),
        in_specs=[qspec, kspec, pl.BlockSpec((T + PADK, HD), lambda h, i: (0, ATT_H + h)), bspec, qspec,
                  pl.BlockSpec((None, QB, 1), lambda h, i: (h, i, 0)), qspec],
        out_specs=[qspec, kspec, kspec, bspec],
        out_shape=[_sds((T, D), BF), _sds((T + PADK, D), F32), _sds((T + PADK, D), F32), _sds((ATT_H, QB, KW), F32)],
        name="attn_bwd", compiler_params=_cp("parallel", "arbitrary"),
    )(qkv, kvp, kvp, bm, o, lse, do)


def _final_fn(x, tgt, g):
    D = x.shape[1]
    xh, rstd = _rms_parts(x)
    diff = xh * g - tgt
    dy = diff * (1.0 / D)
    dxh = dy * g
    dx = rstd * (dxh - xh * jnp.mean(dxh * xh, axis=-1, keepdims=True))
    loss = jnp.sum(jnp.sum(diff * diff, axis=-1, keepdims=True), axis=0, keepdims=True) * (0.5 / D)
    return dx, dx, jnp.sum(dy * xh, axis=0, keepdims=True), jnp.broadcast_to(loss, (1, LANES))


def _adamw_fn(w, g, m, v):
    m = ADAM_B1 * m + (1.0 - ADAM_B1) * g
    v = ADAM_B2 * v + (1.0 - ADAM_B2) * (g * g)
    m_hat = m / (1.0 - ADAM_B1 ** ADAM_STEP)
    v_hat = v / (1.0 - ADAM_B2 ** ADAM_STEP)
    delta = -ADAM_LR * (m_hat / (jnp.sqrt(v_hat) + ADAM_EPS) + ADAM_WD * w)
    return g, delta, m, v


def adamw(w, g, m, v, name):
    shape = w.shape
    C = shape[-1]
    R = w.size // C
    outs = rowwise(name, _adamw_fn, [t.reshape(R, C) for t in (w, g, m, v)], [], [(C, F32)] * 4)
    return tuple(t.reshape(shape) for t in outs)


def adamw_unit(w, m, v, g, lead, outs, name):
    R, B = w.shape[-2:]
    tb = _row_block(R, 256)
    if outs is None:
        outs = [lax.empty(w.shape, F32) for _ in range(4)]

    def body(w_ref, m_ref, v_ref, g_ref, *rest):
        for ref, val in zip(rest[4:], _adamw_fn(w_ref[...], g_ref[...], m_ref[...], v_ref[...])):
            ref[...] = val

    native = pl.BlockSpec((None,) * len(lead) + (tb, B), lambda r: tuple(lead) + (r, 0))
    return pl.pallas_call(
        body, grid=(R // tb,), in_specs=[native] * 3 + [pl.BlockSpec((tb, B), lambda r: (r, 0))] + [HBM_SPEC] * 4,
        out_specs=[native] * 4, out_shape=[_sds(w.shape, F32)] * 4, input_output_aliases={4 + i: i for i in range(4)},
        name=name, compiler_params=_cp("arbitrary"),
    )(w, m, v, g, *outs)


WEIGHTS = ['ffn_norm', 'ffn_w_gate', 'ffn_w_up', 'ffn_w_down', 'mix_norm', 'ab_w_in', 'gla_gate_w', 'gla_gate_b',
           'gla_norm_g', 'conv_dw', 'conv_dw_b', 'conv_ln_g', 'conv_ln_b', 'ab_w_out', 'att_w_qkv', 'att_rel_bias',
           'att_w_o', 'pl_norm', 'pl_w_gate', 'pl_w_proj', 'final_norm']
BIG = ['ffn_w_gate', 'ffn_w_up', 'ffn_w_down', 'ab_w_in', 'ab_w_out', 'att_w_qkv', 'att_w_o', 'pl_w_gate', 'pl_w_proj']


def _pack(parts, rows):
    flat = jnp.concatenate([p.reshape(-1) for p in parts])
    return jnp.pad(flat, (0, rows * LANES - flat.shape[0])).reshape(rows, LANES)


def _unpack(flat, shapes):
    out, pos = [], 0
    for s in shapes:
        n = 1
        for d in s:
            n *= d
        out.append(flat[pos:pos + n].reshape(s))
        pos += n
    return out


def _rows_for(shapes):
    n = sum(math.prod(s) for s in shapes)
    return -(-n // (8 * LANES)) * 8


def kernel(x, p, ffn_norm, ffn_w_gate, ffn_w_up, ffn_w_down, mix_norm, ab_w_in, gla_gate_w, gla_gate_b, gla_norm_g, conv_dw, conv_dw_b, conv_ln_g, conv_ln_b, ab_w_out, att_w_qkv, att_rel_bias, att_w_o, pl_norm, pl_w_gate, pl_w_proj, final_norm, loss_target, m_ffn_norm, m_ffn_w_gate, m_ffn_w_up, m_ffn_w_down, m_mix_norm, m_ab_w_in, m_gla_gate_w, m_gla_gate_b, m_gla_norm_g, m_conv_dw, m_conv_dw_b, m_conv_ln_g, m_conv_ln_b, m_ab_w_out, m_att_w_qkv, m_att_rel_bias, m_att_w_o, m_pl_norm, m_pl_w_gate, m_pl_w_proj, m_final_norm, v_ffn_norm, v_ffn_w_gate, v_ffn_w_up, v_ffn_w_down, v_mix_norm, v_ab_w_in, v_gla_gate_w, v_gla_gate_b, v_gla_norm_g, v_conv_dw, v_conv_dw_b, v_conv_ln_g, v_conv_ln_b, v_ab_w_out, v_att_w_qkv, v_att_rel_bias, v_att_w_o, v_pl_norm, v_pl_w_gate, v_pl_w_proj, v_final_norm):
    env = dict(locals())
    W = {n: env[n] for n in WEIGHTS}
    M = {n: env["m_" + n] for n in WEIGHTS}
    V = {n: env["v_" + n] for n in WEIGHTS}
    xc, yc_, cc = lax.axis_index("x"), lax.axis_index("y"), lax.axis_index("c")

    x0 = x[0]
    tgt = loss_target[0]
    T, D = x0.shape
    fs = ffn_w_gate.shape[-1]
    ws = ab_w_in.shape[-1]
    AB_IN = N_CHIPS * ws
    gz0 = 2 * GLA_H * HK + 2 * GLA_H * HV
    sidx = jnp.stack([2 * xc + yc_, cc, 4 * xc + 2 * yc_ + cc]).astype(jnp.int32)

    def ffn_keys(i, j):
        return [('ffn_w_gate', (i, j)), ('ffn_w_up', (i, j)), ('ffn_w_down', (i, j))]

    later = [[('ab_w_in', (0,)), ('ab_w_out', (0,))], ffn_keys(0, 1), [('pl_w_gate', (0,)), ('pl_w_proj', (0,))],
             ffn_keys(1, 0), [('att_w_qkv', (0,)), ('att_w_o', (0,))], ffn_keys(1, 1),
             [('pl_w_gate', (1,)), ('pl_w_proj', (1,))]]
    arrivals = [ffn_keys(0, 0)[:2], ffn_keys(0, 0)[2:]] + later
    order = [[k] for k in ffn_keys(0, 0)] + later
    full, in_flight = {}, []
    tok = jnp.zeros(TOKEN, F32)
    for s, keys in enumerate(arrivals):
        parts = [cast_unit(W[n], lead, sidx, "cast_" + n + "".join(str(i) for i in lead)) for n, lead in keys]
        ssem, rsem, thru, tok = split_start("gather_start_%d" % s, parts, _gather_copies, 3 * len(keys), tok)
        in_flight.append((ssem, rsem, thru))

    def arrive(s, after):
        ssem, rsem, thru = in_flight[s]
        landed = split_wait("gather_wait_%d" % s, ssem, rsem, thru, _gather_copies, after)
        for key, buf in zip(arrivals[s], gather_pass_on(landed, "gather_pass_on_%d" % s)):
            full[key] = buf

    small_sharded = [ffn_norm, gla_gate_w, conv_dw]
    rows_s = _rows_for([t.shape for t in small_sharded])
    got = allgather8(_pack(small_sharded, rows_s), "gather_small").reshape(N_CHIPS, 2, rows_s * LANES)[:, 0]
    per_chip = [_unpack(got[k], [t.shape for t in small_sharded]) for k in range(N_CHIPS)]
    ffn_norm_f, gate_w_f, conv_dw_f = [jnp.concatenate([per_chip[k][t] for k in range(N_CHIPS)], axis=-1)
                                       for t in range(3)]
    gate_w_p = jnp.pad(gate_w_f[0], ((0, LANES - 16), (0, 0)))
    conv_w = conv_dw_f[0]
    rb_p = jnp.pad(att_rel_bias[0], ((0, 0), (0, REL_PAD - att_rel_bias.shape[-1])))

    G = {}
    small_g = {}

    def ffn_w(i, j):
        return full['ffn_w_gate', (i, j)], full['ffn_w_up', (i, j)], full['ffn_w_down', (i, j)]

    saved = {}
    xs = x0
    arrive(0, tok)

    def first_down(after):
        arrive(1, after)
        return full['ffn_w_down', (0, 0)]

    xs, saved['f00'] = ffn_fwd(xs, ffn_norm_f[0, 0][None], full['ffn_w_gate', (0, 0)], full['ffn_w_up', (0, 0)],
                               first_down, dep=tok)
    arrive(2, xs)
    w_in = jnp.transpose(full['ab_w_in', (0,)], (1, 0, 2)).reshape(D, AB_IN)
    w_main = jnp.concatenate([w_in[:, :gz0], w_in[:, gz0 + 16:]], axis=1)
    w_gz = jnp.pad(w_in[:, gz0:gz0 + 16], ((0, 0), (0, LANES - 16)))
    w_out = full['ab_w_out', (0,)].reshape(D, D)

    def mixer0_fwd(xin):
        h = rms_fwd(xin, mix_norm[0][None], "mix0_rms")
        zmain = mm_nn("ab_in", h, w_main, F32, 1024)
        gzp = mm_nn("ab_gz", h, w_gz, F32, LANES)
        ug = mm_nn("gla_gate", gzp, gate_w_p, F32, GLA_H * HK)
        o, S = gla_fwd(zmain, ug, gla_gate_b)
        a_out = rowwise("gla_post", _gla_post_fn, [o, ("cols", zmain, GLA_H * HV, 2)], [gla_norm_g], [(GLA_H * HV, BF)])
        u = rowwise("conv_glu", lambda a, b: a * _sigmoid(b), [("cols", zmain, 1024, 3), ("cols", zmain, 1024, 4)], [],
                    [(1024, F32)])
        yc = conv_fwd(u, conv_w, conv_dw_b)
        b_out = rowwise("conv_ln", _ln_silu_fn, [yc], [conv_ln_g, conv_ln_b], [(1024, BF)])
        cat = jnp.concatenate([a_out, b_out], axis=1)
        x2 = mm_nn("ab_out", cat, w_out, F32, 1024, res=xin)
        return x2, (xin, h, zmain, gzp, ug, o, S, u, yc, cat)

    xs, saved['m0'] = mixer0_fwd(xs)
    arrive(3, xs)
    xs, saved['f01'] = ffn_fwd(xs, ffn_norm_f[0, 1][None], *ffn_w(0, 1))
    ks = D // N_CHIPS
    dp = p.shape[-1]

    def pl_fwd(xin, i):
        h = rms_fwd(xin, pl_norm[i][None], "pl_rms")
        tm = min(T, 1024)
        u = mm_nn("pl_gate", h, full['pl_w_gate', (i,)].reshape(D, D), F32, 1024)
        e = mm("pl_proj", NN, (T // tm, N_CHIPS, 1), p[i, 0], (tm, dp), _mk, full['pl_w_proj', (i,)],
               (None, dp, ks), lambda m, n, k: (n, 0, 0), (tm, ks), _mn, _sds((T, D), F32))
        x2 = rowwise("pl_mix", lambda xv, uv, ev: xv + _sigmoid(uv) * ev, [xin, u, e], [], [(D, F32)])
        return x2, (xin, h, u, e)

    arrive(4, xs)
    xs, saved['p0'] = pl_fwd(xs, 0)
    arrive(5, xs)
    xs, saved['f10'] = ffn_fwd(xs, ffn_norm_f[1, 0][None], *ffn_w(1, 0))
    arrive(6, xs)
    w_qkv = full['att_w_qkv', (0,)]
    w_o = full['att_w_o', (0,)].reshape(D, D)
    qs = w_qkv.shape[-1]

    bm = relbias_tile(rb_p)

    def mixer1_fwd(xin):
        h = rms_fwd(xin, mix_norm[1][None], "mix1_rms")
        tm, tn = min(T, 1024), 512
        per = qs // tn
        qkv = mm("att_qkv", NN, (T // tm, 3 * D // tn, 1), h, (tm, D), _mk, w_qkv, (None, D, tn),
                 lambda m, n, k: (n // per, k, n % per), (tm, tn), _mn, _sds((T, 3 * D), BF))
        kvp = jnp.pad(qkv[:, D:], ((PADK, 0), (0, 0)))
        o, lse = attn_fwd(qkv, kvp, bm)
        x2 = mm_nn("att_o", o, w_o, F32, 1024, res=xin)
        return x2, (xin, h, qkv, kvp, o, lse)

    xs, saved['m1'] = mixer1_fwd(xs)
    arrive(7, xs)
    xs, saved['f11'] = ffn_fwd(xs, ffn_norm_f[1, 1][None], *ffn_w(1, 1))
    arrive(8, xs)
    xs, saved['p1'] = pl_fwd(xs, 1)

    dx, dxb, small_g['final_norm'], loss_acc = rowwise("loss_head", _final_fn, [xs, tgt], [final_norm[None]], [(D, F32), (D, BF)],
                                                  [(1, D), (1, LANES)])
    loss = lax.psum(loss_acc[0, 0], ("x", "y", "c"))
    dx = (dx, dxb)

    def pl_bwd(dx2, sv, i, dep):
        xin, h, u, e = sv
        tm = min(T, 1024)

        def fn(d, uv, ev):
            s = _sigmoid(uv)
            return d * s, d * ev * s * (1.0 - s)

        dx2, _ = dx2
        de, du = rowwise("pl_mix_bwd", fn, [dx2, u, e], [], [(D, BF), (D, BF)], dep=dep)
        G['pl_w_proj', (i,)] = mm("pl_dproj", TN, (1, N_CHIPS, 1), p[i, 0], (T, dp), _km, de, (T, ks), _kn,
                                  (None, dp, ks), lambda m, n, k: (n, 0, 0), _sds((N_CHIPS, dp, ks), BF))
        G['pl_w_gate', (i,)] = mm_tn("pl_dgate", h, du, BF, ks, 1024).reshape(N_CHIPS, ks, D)
        dh = mm_nt("pl_dh", du, full['pl_w_gate', (i,)].reshape(D, D), F32, 1024)
        return rms_bwd(xin, pl_norm[i][None], dh, dx2, "pl_rms_bwd")

    def ffn_b(dx2, key, i, j, dep):
        def sends(name, g, dep):
            G[name, (i, j)] = g
            s = [t for t, keys in enumerate(order) if (name, (i, j)) in keys][0]
            if not all(k in G for k in order[s]):
                return dep
            tok = scatter(s, dep)
            return to_chips(tok) if len(order[s]) == 1 else tok

        return ffn_bwd(dx2, saved[key], ffn_norm_f[i, j][None], *ffn_w(i, j), dep, sends)

    def mixer1_bwd(dx2, sv, dep):
        xin, h, qkv, kvp, o, lse = sv
        dx2, dx2b = dx2
        do = mm_nt("att_do", dx2b, w_o, BF, 1024, dep=dep)
        dwo = mm_tn("att_dwo", o, dx2b, BF, 1024, 1024, dep=dep)
        dq, dkp, dvp, dbm = attn_bwd(qkv, kvp, bm, o, lse, do)
        dqkv = jnp.concatenate([dq, dkp[PADK:].astype(BF), dvp[PADK:].astype(BF)], axis=1)
        tm, tn = min(T, 1024), 512
        per = qs // tn
        dwqkv = mm("att_dwqkv", TN, (D // 1024, 3 * D // tn, 1), h, (T, 1024), _km, dqkv, (T, tn), _kn,
                   (None, 1024, tn), lambda m, n, k: (n // per, m, n % per), _sds((N_CHIPS, D, qs), BF))
        dh = mm("att_dh", NT, (T // tm, D // 1024, N_CHIPS), dqkv, (tm, qs), _mk, w_qkv, (None, 1024, qs),
                lambda m, n, k: (k, n, 0), (tm, 1024), _mn, _sds((T, D), F32))
        dxn, dgain = rms_bwd(xin, mix_norm[1][None], dh, dx2, "mix1_rms_bwd")
        return dxn, dgain, dwo, dwqkv, relbias_reduce(dbm)

    def mixer0_bwd(dx2, sv, dep):
        xin, h, zmain, gzp, ug, o, S, u, yc, cat = sv
        dx2, dx2b = dx2
        dcat = mm_nt("ab_dcat", dx2b, w_out, F32, 1024, dep=dep)
        dwout = mm_tn("ab_dwout", cat, dx2b, BF, 1024, 1024, dep=dep)
        do, dr, dgn = rowwise("gla_post_bwd", _gla_post_bwd_fn,
                              [("cols", dcat, GLA_H * HV, 0), o, ("cols", zmain, GLA_H * HV, 2)], [gla_norm_g],
                              [(GLA_H * HV, F32), (GLA_H * HV, BF)], [(1, HV)])
        dyc, dlg, dlb = rowwise("conv_ln_bwd", _ln_silu_bwd_fn, [("cols", dcat, 1024, 1), yc], [conv_ln_g, conv_ln_b],
                                [(1024, F32)], [(1, 1024), (1, 1024)])
        du, ddw, ddwb = conv_bwd(dyc, u, conv_w)
        dca, dcb = rowwise("conv_glu_bwd", _glu_bwd_fn, [du, ("cols", zmain, 1024, 3), ("cols", zmain, 1024, 4)], [],
                           [(1024, BF), (1024, BF)])
        dq, dk, dv, dug, dgb = gla_bwd(zmain, ug, gla_gate_b, S, do)
        dgw = mm_tn("gla_dgate_w", gzp, dug, F32, LANES, GLA_H * HK)
        dgzp = mm_nt("gla_dgz", dug, gate_w_p, F32, LANES)
        dzm = jnp.concatenate([dq, dk, dv, dr, dca, dcb], axis=1)
        dwmain = mm_tn("ab_dwmain", h, dzm, BF, 1024, 1024)
        dwgz = mm_tn("ab_dwgz", h, dgzp, BF, 1024, LANES)
        dh = mm_nt("ab_dh_gz", dgzp, w_gz, F32, 1024, res=mm_nt("ab_dh", dzm, w_main, F32, 512))
        dxn, dgain = rms_bwd(xin, mix_norm[0][None], dh, dx2, "mix0_rms_bwd")
        dwin = jnp.concatenate([dwmain[:, :gz0], dwgz[:, :16], dwmain[:, gz0:]], axis=1)
        g_win = jnp.transpose(dwin.reshape(D, N_CHIPS, ws), (1, 0, 2))
        return dxn, dgain, g_win, dwout, (dgn, dlg, dlb, ddw, ddwb, dgw[:16], dgb)

    scattering, exchanging = [], []

    def tags_of(s):
        return [n + "".join(str(i) for i in lead) for n, lead in order[s]]

    def to_chips(tok):
        if not exchanging:
            return tok
        s, ssem, rsem, thru = exchanging.pop()
        n_u = len(order[s])
        landed = split_wait("sibling_wait_%d" % s, ssem, rsem, thru, _sibling_copies, tok)
        hs = [add_sibling(g4, got, sidx, tag) for g4, got, tag in zip(landed[:n_u], landed[n_u:], tags_of(s))]
        land = [lax.empty((3,) + h.shape[1:], h.dtype) for h in hs]
        ssem, rsem, thru, tok = split_start("scatter_start_%d" % s, hs + land, _scatter_copies, 3 * n_u, tok)
        scattering.append((s, ssem, rsem, thru))
        return tok

    def scatter(s, tok):
        g4s = [G[key].reshape(N_CHIPS, 2, G[key].shape[1] // 2, G[key].shape[2]) for key in order[s]]
        land = [lax.empty((N_CHIPS,) + g4.shape[2:], g4.dtype) for g4 in g4s]
        ssem, rsem, thru, tok = split_start("sibling_start_%d" % s, g4s + land, _sibling_copies, N_CHIPS * len(g4s), tok)
        tok = to_chips(tok)
        exchanging.append((s, ssem, rsem, thru))
        return tok

    dpl, dffn, dmix = [None, None], [[None, None], [None, None]], [None, None]
    tok = jnp.zeros(TOKEN, F32)
    dx, dpl[1] = pl_bwd(dx, saved['p1'], 1, tok)
    tok = scatter(9, tok)
    dx, dffn[1][1], tok = ffn_b(dx, 'f11', 1, 1, tok)
    dx, dmix[1], dwo, dwqkv, drb = mixer1_bwd(dx, saved['m1'], tok)
    G['att_w_qkv', (0,)] = dwqkv
    G['att_w_o', (0,)] = dwo.reshape((N_CHIPS,) + att_w_o.shape[1:])
    tok = scatter(7, tok)
    dx, dffn[1][0], tok = ffn_b(dx, 'f10', 1, 0, tok)
    dx, dpl[0] = pl_bwd(dx, saved['p0'], 0, tok)
    tok = scatter(5, tok)
    dx, dffn[0][1], tok = ffn_b(dx, 'f01', 0, 1, tok)
    dx, dmix[0], g_win, dwout, (dgn, dlg, dlb, ddw, ddwb, dgw, dgb) = mixer0_bwd(dx, saved['m0'], tok)
    G['ab_w_in', (0,)] = g_win
    G['ab_w_out', (0,)] = dwout.reshape((N_CHIPS,) + ab_w_out.shape[1:])
    tok = scatter(3, tok)
    dx, dffn[0][0], tok = ffn_b(dx, 'f00', 0, 0, tok)
    dx = dx[0]
    grad_x = dx[None]

    small_g['ffn_norm'] = jnp.stack([jnp.stack([dffn[i][j][0] for j in range(2)]) for i in range(2)])
    small_g['mix_norm'] = jnp.concatenate(dmix, axis=0)
    small_g['gla_gate_w'] = dgw[None]
    small_g['gla_gate_b'] = dgb
    small_g['gla_norm_g'] = dgn
    small_g['conv_dw'] = ddw[None]
    small_g['conv_dw_b'] = ddwb
    small_g['conv_ln_g'] = dlg
    small_g['conv_ln_b'] = dlb
    small_g['att_rel_bias'] = drb[None, :, :att_rel_bias.shape[-1]]
    small_g['pl_norm'] = jnp.concatenate(dpl, axis=0)
    small_g['final_norm'] = small_g['final_norm'][0]
    small_names = [n for n in WEIGHTS if n not in BIG]
    small_shapes = [small_g[n].shape for n in small_names]
    rows_g = _rows_for(small_shapes)
    mine = rowwise("place_small_grads", lambda v: v, [_pack([small_g[n] for n in small_names], rows_g)], [],
                   [("dyn", 8, LANES, F32, 2)], tb=rows_g, sidx=sidx)
    small_ssem, small_rsem, small_thru, tok = split_start("small_start", [mine], _allgather_copies, 7, tok)

    grads, outs = {}, {}
    sharing, updated = [], [dx]

    def update(tok):
        if not sharing:
            return
        s, ssem, rsem, thru = sharing.pop()
        landed = split_wait("share_wait_%d" % s, ssem, rsem, thru, _share_copies, tok)
        for (n, lead), tag, r in zip(order[s], tags_of(s), landed):
            outs[n] = adamw_unit(W[n], M[n], V[n], r.reshape(W[n].shape[-2:]), lead, outs.get(n), "adamw_" + tag)
            updated.append(outs[n][0])

    after = dx
    for s, ssem, rsem, thru in scattering:
        n_u = len(order[s])
        landed = split_wait("scatter_wait_%d" % s, ssem, rsem, thru, _scatter_copies, after)
        rs = [add_chips(h, got2, sidx, tag) for h, got2, tag in zip(landed[:n_u], landed[n_u:], tags_of(s))]
        ssem, rsem, thru, tok = split_start("share_start_%d" % s, rs, _share_copies, n_u, tok)
        update(tok)
        sharing.append((s, ssem, rsem, thru))
        after = tok
    update(tok)

    (allp,) = split_wait("small_wait", small_ssem, small_rsem, small_thru, _allgather_copies, updated[-1])
    summed = rowwise("sum_small_grads", lambda *v: (((v[0] + v[1]) + (v[2] + v[3])) + ((v[4] + v[5]) + (v[6] + v[7]))),
                     [("leads", allp, (d,)) for d in range(8)], [], [(LANES, F32)], tb=rows_g)
    for n, g in zip(small_names, _unpack(summed.reshape(-1), small_shapes)):
        grads[n] = g
    chip = 2 * xc + yc_
    for n, axis in (('ffn_norm', 2), ('gla_gate_w', 2), ('conv_dw', 2)):
        width = W[n].shape[axis]
        grads[n] = lax.dynamic_slice_in_dim(grads[n], chip * width, width, axis)

    for n in small_names:
        outs[n] = adamw(W[n], grads[n], M[n], V[n], "adamw_" + n)
    return (loss, grad_x, *[outs[n][0] for n in WEIGHTS], *[outs[n][1] for n in WEIGHTS],
            *[outs[n][2] for n in WEIGHTS], *[outs[n][3] for n in WEIGHTS])
```
